```python
import jax, jax.numpy as jnp
from jax import lax
import numpy as np

D_MODEL = 1024
BATCH = 4
SEQ = 8192
DEPTH = 2
DEC_BATCH = 32
DEC_SEQ = 8
PAST_LEN = 16384
PAGE_SIZE = 128

CONV_CH = D_MODEL // 2
CONV_WIDTH = 3
ATT_HEADS = 8
KV_HEADS = 2
HEAD_DIM = (D_MODEL // 2) // ATT_HEADS
IDX_HEADS = 4
IDX_DIM = 64
TOPK_MAX = 256
ROPE_THETA = 10000.0
RET_HEADS = 4
RET_DK = D_MODEL // RET_HEADS
RET_DV = 2 * RET_DK
RET_CHUNK = 128
Q_BLOCK = 128
D_FF = 2816
N_EXPERTS = 8
TOP_K = 2
EXPERT_FF = 3584
PLE_DIM = 256
EPS = 1e-6
N_ATTN_LAYERS = (DEPTH + 1) // 2
N_RET_LAYERS = DEPTH // 2
AB_SPLITS = (CONV_CH, CONV_CH, CONV_CH, ATT_HEADS * HEAD_DIM, KV_HEADS * HEAD_DIM, KV_HEADS * HEAD_DIM, IDX_HEADS * IDX_DIM, IDX_DIM, IDX_HEADS)
AB_IN = sum(AB_SPLITS)
AB_OUT = CONV_CH + ATT_HEADS * HEAD_DIM
RET_SPLITS = (RET_HEADS * RET_DK, RET_HEADS * RET_DK, RET_HEADS * RET_DV, RET_HEADS * RET_DV)
RET_IN = sum(RET_SPLITS)

kernel_name = 'conv_dsa_retention_hybrid_step'

F32 = jnp.float32


def _split(z, sizes):
    offs = np.cumsum(sizes)[:-1].tolist()
    return jnp.split(z, offs, axis=-1)


def rmsnorm(x, g):
    xf = x.astype(F32)
    y = xf * lax.rsqrt(jnp.mean(xf * xf, axis=-1, keepdims=True) + EPS) * g.astype(F32)
    return y.astype(x.dtype)


def rope_freqs(d):
    return ROPE_THETA ** (-jnp.arange(0, d, 2, dtype=F32) / d)


def retnet_freqs(d):
    return ROPE_THETA ** (-jnp.linspace(0.0, 1.0, d // 2, dtype=F32))


def rotate(x, pos, inv_freq):
    ang = pos.astype(F32)[:, None] * inv_freq[None, :]
    cos = jnp.cos(ang)[:, None, :]
    sin = jnp.sin(ang)[:, None, :]
    half = x.shape[-1] // 2
    x1 = x[..., :half].astype(F32)
    x2 = x[..., half:].astype(F32)
    return jnp.concatenate([x1 * cos - x2 * sin, x2 * cos + x1 * sin], axis=-1).astype(x.dtype)


def swiglu(h, w_gate, w_up, w_down):
    return (jax.nn.silu(h @ w_gate) * (h @ w_up)) @ w_down


def dsa_attend(q, qi, wi, k_all, v_all, ki_all, q_pos):
    B, T = q.shape[0], q.shape[1]
    L = k_all.shape[1]
    topk = min(TOPK_MAX, L // 4)
    qb = Q_BLOCK if T % Q_BLOCK == 0 else T
    nb = T // qb
    k_pos = jnp.arange(L, dtype=jnp.int32)
    b_idx = jnp.arange(B)[:, None, None]
    group = ATT_HEADS // KV_HEADS
    scale = HEAD_DIM ** -0.5
    ki_f = ki_all.astype(F32)

    def blocks(a):
        return a.reshape((B, nb, qb) + a.shape[2:]).swapaxes(0, 1)

    def one_block(args):
        q_b, qi_b, wi_b, pos_b = args
        s = jnp.einsum('bqhd,bsd->bqhs', qi_b.astype(F32), ki_f) * (IDX_DIM ** -0.5)
        score = jnp.einsum('bqh,bqhs->bqs', wi_b.astype(F32) * (IDX_HEADS ** -0.5), jax.nn.relu(s))
        score = jnp.where(k_pos[None, None, :] <= pos_b[None, :, None], score, -jnp.inf)
        _, sel = lax.top_k(score, topk)
        valid = sel <= pos_b[None, :, None]
        k_sel = k_all[b_idx, sel]
        v_sel = v_all[b_idx, sel]
        qg = q_b.reshape(B, qb, KV_HEADS, group, HEAD_DIM)
        logits = jnp.einsum('bqkgd,bqnkd->bqkgn', qg, k_sel).astype(F32) * scale
        logits = jnp.where(valid[:, :, None, None, :], logits, -jnp.inf)
        p = jax.nn.softmax(logits, axis=-1).astype(v_sel.dtype)
        o = jnp.einsum('bqkgn,bqnkd->bqkgd', p, v_sel)
        return o.reshape(B, qb, ATT_HEADS, HEAD_DIM)

    out = lax.map(one_block, (blocks(q), blocks(qi), blocks(wi), q_pos.reshape(nb, qb)))
    return out.swapaxes(0, 1).reshape(B, T, ATT_HEADS, HEAD_DIM)


def mixer_conv_dsa(h, pos, conv_buf, past_k, past_v, past_ki, w_in, conv_w, w_out):
    B, T, _ = h.shape
    z = h @ w_in
    b_g, c_g, hv, q, k, v, qi, ki, wi = _split(z, AB_SPLITS)
    u = c_g * hv
    u_ext = jnp.concatenate([conv_buf.astype(u.dtype), u], axis=1)
    conv = conv_w[0] * u_ext[:, 0:T]
    for j in range(1, CONV_WIDTH):
        conv = conv + conv_w[j] * u_ext[:, j:j + T]
    y_a = b_g * conv
    fq = rope_freqs(HEAD_DIM)
    fi = rope_freqs(IDX_DIM)
    q = rotate(q.reshape(B, T, ATT_HEADS, HEAD_DIM), pos, fq)
    k = rotate(k.reshape(B, T, KV_HEADS, HEAD_DIM), pos, fq)
    v = v.reshape(B, T, KV_HEADS, HEAD_DIM)
    qi = rotate(qi.reshape(B, T, IDX_HEADS, IDX_DIM), pos, fi)
    ki = rotate(ki.reshape(B, T, 1, IDX_DIM), pos, fi)[:, :, 0]
    k_all = jnp.concatenate([past_k.astype(k.dtype), k], axis=1)
    v_all = jnp.concatenate([past_v.astype(v.dtype), v], axis=1)
    ki_all = jnp.concatenate([past_ki.astype(ki.dtype), ki], axis=1)
    y_b = dsa_attend(q, qi, wi, k_all, v_all, ki_all, pos).reshape(B, T, ATT_HEADS * HEAD_DIM)
    y = jnp.concatenate([y_a, y_b], axis=-1) @ w_out
    return y, k, v, ki, u_ext[:, -(CONV_WIDTH - 1):]


def retention_chunked(q, k, v, s0, log_gamma):
    B, T, H, _ = q.shape
    dv = v.shape[-1]
    C = RET_CHUNK if T % RET_CHUNK == 0 else T
    n = T // C
    i = jnp.arange(C, dtype=F32)
    rel = i[:, None] - i[None, :]
    decay_in = jnp.where(rel[None] >= 0, jnp.exp(log_gamma[:, None, None] * jnp.maximum(rel, 0.0)[None]), 0.0)
    decay_q = jnp.exp(log_gamma[None, :] * (i[:, None] + 1.0))
    decay_k = jnp.exp(log_gamma[None, :] * (C - 1.0 - i)[:, None])
    decay_c = jnp.exp(log_gamma * C)

    def chunks(a):
        return a.astype(F32).reshape((B, n, C) + a.shape[2:]).swapaxes(0, 1)

    def step(S, inp):
        qc, kc, vc = inp
        att = jnp.einsum('bihd,bjhd->bhij', qc, kc) * decay_in[None]
        o = jnp.einsum('bhij,bjhe->bihe', att, vc) + jnp.einsum('bihd,bhde->bihe', qc, S) * decay_q[None, :, :, None]
        S = S * decay_c[None, :, None, None] + jnp.einsum('bjhd,bjhe->bhde', kc * decay_k[None, :, :, None], vc)
        return S, o

    S, o = lax.scan(step, s0.astype(F32), (chunks(q), chunks(k), chunks(v)))
    return o.swapaxes(0, 1).reshape(B, T, H, dv), S


def mixer_retention(h, pos, s0, w_in, w_out):
    B, T, _ = h.shape
    q, k, v, g = _split(h @ w_in, RET_SPLITS)
    fr = retnet_freqs(RET_DK)
    q = rotate(q.reshape(B, T, RET_HEADS, RET_DK), pos, fr)
    k = rotate(k.reshape(B, T, RET_HEADS, RET_DK), pos, fr) * (RET_DK ** -0.5)
    v = v.reshape(B, T, RET_HEADS, RET_DV)
    log_gamma = jnp.log(1.0 - 2.0 ** (-5.0 - jnp.arange(RET_HEADS, dtype=F32)))
    o, S = retention_chunked(q, k, v, s0, log_gamma)
    mu = jnp.mean(o, axis=-1, keepdims=True)
    var = jnp.mean(jnp.square(o - mu), axis=-1, keepdims=True)
    o = ((o - mu) * lax.rsqrt(var + EPS)).reshape(B, T, RET_HEADS * RET_DV).astype(h.dtype)
    y = (jax.nn.silu(g) * o) @ w_out
    return y, S.astype(s0.dtype)


def moe_swiglu(h, router, w_gate, w_up, w_down):
    logits = (h @ router).astype(F32)
    vals, idx = lax.top_k(logits, TOP_K)
    wts = jax.nn.softmax(vals, axis=-1)
    gate = jnp.sum(jax.nn.one_hot(idx, N_EXPERTS, dtype=F32) * wts[..., None], axis=-2).astype(h.dtype)
    out = jnp.zeros_like(h)
    for e in range(N_EXPERTS):
        out = out + gate[..., e:e + 1] * swiglu(h, w_gate[e], w_up[e], w_down[e])
    return out


def trunk(x, p, pos, past_k, past_v, past_ki, conv_buf, ret_state, prm):
    ks, vs, kis, cbs, rss = [], [], [], [], []
    for i in range(DEPTH):
        j = i // 2
        h = rmsnorm(x, prm['norm_mix'][i])
        if i % 2 == 0:
            y, k, v, ki, cb = mixer_conv_dsa(h, pos, conv_buf[j], past_k[j], past_v[j], past_ki[j],
                                             prm['ab_w_in'][j], prm['ab_conv_w'][j], prm['ab_w_out'][j])
            ks.append(k); vs.append(v); kis.append(ki); cbs.append(cb)
            x = x + y
            h = rmsnorm(x, prm['norm_ffn'][i])
            x = x + swiglu(h, prm['ffn_w_gate'][j], prm['ffn_w_up'][j], prm['ffn_w_down'][j])
        else:
            y, S = mixer_retention(h, pos, ret_state[j], prm['ret_w_in'][j], prm['ret_w_out'][j])
            rss.append(S)
            x = x + y
            h = rmsnorm(x, prm['norm_ffn'][i])
            x = x + moe_swiglu(h, prm['moe_router'][j], prm['moe_w_gate'][j], prm['moe_w_up'][j], prm['moe_w_down'][j])
        hp = rmsnorm(x, prm['norm_ple'][i])
        x = x + (p[i] @ prm['ple_w'][i]) * jax.nn.sigmoid(hp @ prm['ple_gate_w'][i])
    x = rmsnorm(x, prm['norm_final'])
    return x, jnp.stack(ks), jnp.stack(vs), jnp.stack(kis), jnp.stack(cbs), jnp.stack(rss)


def setup_inputs(seed: int = 0) -> dict:
    key = jax.random.key(seed)
    ks = iter(jax.random.split(key, 40))

    def nrm(shape, scale):
        return jax.random.normal(next(ks), shape, F32) * scale

    n_pages = PAST_LEN // PAGE_SIZE
    n_pool = (DEC_BATCH * n_pages * 5) // 4
    perm = jax.random.permutation(next(ks), n_pool)
    page_table = perm[:DEC_BATCH * n_pages].reshape(DEC_BATCH, n_pages).astype(jnp.int32)
    return {
        'x_prompt': nrm((BATCH, SEQ, D_MODEL), 1.0),
        'x_sample': nrm((DEC_BATCH, DEC_SEQ, D_MODEL), 1.0),
        'cache_k': nrm((N_ATTN_LAYERS, n_pool, PAGE_SIZE, KV_HEADS, HEAD_DIM), 1.0),
        'cache_v': nrm((N_ATTN_LAYERS, n_pool, PAGE_SIZE, KV_HEADS, HEAD_DIM), 1.0),
        'cache_kidx': nrm((N_ATTN_LAYERS, n_pool, PAGE_SIZE, IDX_DIM), 1.0),
        'state_conv': nrm((N_ATTN_LAYERS, DEC_BATCH, CONV_WIDTH - 1, CONV_CH), 1.0),
        'state_ret': nrm((N_RET_LAYERS, DEC_BATCH, RET_HEADS, RET_DK, RET_DV), 0.25),
        'page_table': page_table,
        'p_prompt': nrm((DEPTH, BATCH, SEQ, PLE_DIM), 1.0),
        'p_sample': nrm((DEPTH, DEC_BATCH, DEC_SEQ, PLE_DIM), 1.0),
        'norm_mix': 1.0 + nrm((DEPTH, D_MODEL), 0.01),
        'norm_ffn': 1.0 + nrm((DEPTH, D_MODEL), 0.01),
        'norm_ple': 1.0 + nrm((DEPTH, D_MODEL), 0.01),
        'norm_final': 1.0 + nrm((D_MODEL,), 0.01),
        'ab_w_in': nrm((N_ATTN_LAYERS, D_MODEL, AB_IN), D_MODEL ** -0.5),
        'ab_conv_w': nrm((N_ATTN_LAYERS, CONV_WIDTH, CONV_CH), CONV_WIDTH ** -0.5),
        'ab_w_out': nrm((N_ATTN_LAYERS, AB_OUT, D_MODEL), AB_OUT ** -0.5),
        'ffn_w_gate': nrm((N_ATTN_LAYERS, D_MODEL, D_FF), D_MODEL ** -0.5),
        'ffn_w_up': nrm((N_ATTN_LAYERS, D_MODEL, D_FF), D_MODEL ** -0.5),
        'ffn_w_down': nrm((N_ATTN_LAYERS, D_FF, D_MODEL), D_FF ** -0.5),
        'ret_w_in': nrm((N_RET_LAYERS, D_MODEL, RET_IN), D_MODEL ** -0.5),
        'ret_w_out': nrm((N_RET_LAYERS, RET_HEADS * RET_DV, D_MODEL), (RET_HEADS * RET_DV) ** -0.5),
        'moe_router': nrm((N_RET_LAYERS, D_MODEL, N_EXPERTS), D_MODEL ** -0.5),
        'moe_w_gate': nrm((N_RET_LAYERS, N_EXPERTS, D_MODEL, EXPERT_FF), D_MODEL ** -0.5),
        'moe_w_up': nrm((N_RET_LAYERS, N_EXPERTS, D_MODEL, EXPERT_FF), D_MODEL ** -0.5),
        'moe_w_down': nrm((N_RET_LAYERS, N_EXPERTS, EXPERT_FF, D_MODEL), EXPERT_FF ** -0.5),
        'ple_w': nrm((DEPTH, PLE_DIM, D_MODEL), PLE_DIM ** -0.5),
        'ple_gate_w': nrm((DEPTH, D_MODEL, D_MODEL), D_MODEL ** -0.5),
    }


def reference(x_prompt, x_sample, cache_k, cache_v, cache_kidx, state_conv, state_ret, page_table,
              p_prompt, p_sample, norm_mix, norm_ffn, norm_ple, norm_final, ab_w_in, ab_conv_w, ab_w_out,
              ffn_w_gate, ffn_w_up, ffn_w_down, ret_w_in, ret_w_out, moe_router, moe_w_gate, moe_w_up,
              moe_w_down, ple_w, ple_gate_w):
    prm = {'norm_mix': norm_mix, 'norm_ffn': norm_ffn, 'norm_ple': norm_ple, 'norm_final': norm_final,
           'ab_w_in': ab_w_in, 'ab_conv_w': ab_conv_w, 'ab_w_out': ab_w_out,
           'ffn_w_gate': ffn_w_gate, 'ffn_w_up': ffn_w_up, 'ffn_w_down': ffn_w_down,
           'ret_w_in': ret_w_in, 'ret_w_out': ret_w_out, 'moe_router': moe_router,
           'moe_w_gate': moe_w_gate, 'moe_w_up': moe_w_up, 'moe_w_down': moe_w_down,
           'ple_w': ple_w, 'ple_gate_w': ple_gate_w}
    B, T = x_prompt.shape[0], x_prompt.shape[1]
    DB, TS = x_sample.shape[0], x_sample.shape[1]
    n_pages = page_table.shape[1]
    past_len = n_pages * PAGE_SIZE
    dt = x_prompt.dtype

    pos_p = jnp.arange(T, dtype=jnp.int32)
    empty_k = [jnp.zeros((B, 0, KV_HEADS, HEAD_DIM), dt) for _ in range(N_ATTN_LAYERS)]
    empty_ki = [jnp.zeros((B, 0, IDX_DIM), dt) for _ in range(N_ATTN_LAYERS)]
    conv0 = jnp.zeros((N_ATTN_LAYERS, B, CONV_WIDTH - 1, CONV_CH), dt)
    ret0 = jnp.zeros((N_RET_LAYERS, B, RET_HEADS, RET_DK, RET_DV), dt)
    y_p, k_p, v_p, ki_p, cb_p, rs_p = trunk(x_prompt, p_prompt, pos_p, empty_k, empty_k, empty_ki, conv0, ret0, prm)

    pos_s = past_len + jnp.arange(TS, dtype=jnp.int32)
    past_k = [cache_k[j, page_table].reshape(DB, past_len, KV_HEADS, HEAD_DIM) for j in range(N_ATTN_LAYERS)]
    past_v = [cache_v[j, page_table].reshape(DB, past_len, KV_HEADS, HEAD_DIM) for j in range(N_ATTN_LAYERS)]
    past_ki = [cache_kidx[j, page_table].reshape(DB, past_len, IDX_DIM) for j in range(N_ATTN_LAYERS)]
    y_s, k_s, v_s, ki_s, cb_s, rs_s = trunk(x_sample, p_sample, pos_s, past_k, past_v, past_ki, state_conv, state_ret, prm)

    return (y_p, y_s, k_p, v_p, ki_p, cb_p, rs_p, k_s, v_s, ki_s, cb_s, rs_s)
```

```python
import functools
import math

import jax
import jax.numpy as jnp
import numpy as np
from jax import lax
from jax.experimental import pallas as pl
from jax.experimental.pallas import tpu as pltpu

F32 = jnp.float32
BF16 = jnp.bfloat16
I32 = jnp.int32

D_MODEL = 1024
PAGE_SIZE = 128
CONV_CH = D_MODEL // 2
CONV_WIDTH = 3
ATT_HEADS = 8
KV_HEADS = 2
HEAD_DIM = 64
IDX_HEADS = 4
IDX_DIM = 64
TOPK_MAX = 256
ROPE_THETA = 10000.0
RET_HEADS = 4
RET_DK = D_MODEL // RET_HEADS
RET_DV = 2 * RET_DK
RET_CHUNK = 128
D_FF = 2816
N_EXPERTS = 8
EXPERT_FF = 3584
PLE_DIM = 256
EPS = 1e-6
AB_SPLITS = (CONV_CH, CONV_CH, CONV_CH, ATT_HEADS * HEAD_DIM, KV_HEADS * HEAD_DIM, KV_HEADS * HEAD_DIM,
             IDX_HEADS * IDX_DIM, IDX_DIM, IDX_HEADS)

LANES = 128
VMEM_LIMIT = 48 * 1024 * 1024
INT_MIN = -2 ** 31
NEG_BIG = -1e30

C_BG, C_CG, C_HV, C_Q, C_K, C_V, C_QI, C_KIW, C_END = 0, 512, 1024, 1536, 2560, 2688, 2816, 3328, 3456
KEY_CHUNK = 512
TOKEN_TILE = 512


def _cparams(sem):
    return pltpu.CompilerParams(dimension_semantics=sem, vmem_limit_bytes=VMEM_LIMIT)


def _rms(x, g):
    return x * lax.rsqrt(jnp.mean(x * x, axis=-1, keepdims=True) + EPS) * g


def _dot(a, b):
    return jnp.dot(a, b, preferred_element_type=F32)


def _dot_nt(a, b):
    return lax.dot_general(a, b, (((1,), (1,)), ((), ())), preferred_element_type=F32)


def _dot_tn(a, b):
    return lax.dot_general(a, b, (((0,), (0,)), ((), ())), preferred_element_type=F32)


def _swap_halves64(x):
    lane = lax.broadcasted_iota(I32, x.shape, 1)
    from_above = pltpu.roll(x, LANES - 32, 1)
    from_below = pltpu.roll(x, 32, 1)
    return jnp.where((lane & 63) < 32, from_above, from_below)


def _l0_in_kernel(x_ref, g_ref, w_ref, cos_ref, sin_ref, cosb_ref, sinb_ref,
                  bg_ref, u_ref, q_ref, k_ref, v_ref, qi_ref, kiw_ref, kbf_ref, vbf_ref, kiwbf_ref):
    h = _rms(x_ref[...], g_ref[...]).astype(BF16)
    cos, sin = cos_ref[...], sin_ref[...]

    def rot(z, c, s):
        return z * c + _swap_halves64(z) * s

    bg_ref[...] = _dot(h, w_ref[:, C_BG:C_CG])
    u_ref[...] = _dot(h, w_ref[:, C_CG:C_HV]) * _dot(h, w_ref[:, C_HV:C_Q])
    zq = _dot(h, w_ref[:, C_Q:C_K])
    for hd in range(ATT_HEADS):
        sl = slice(hd * LANES, (hd + 1) * LANES)
        q_ref[:, sl] = (rot(zq[:, sl], cos, sin) * (HEAD_DIM ** -0.5)).astype(BF16)
    k = rot(_dot(h, w_ref[:, C_K:C_V]), cos, sin)
    k_ref[...] = k
    kbf_ref[...] = k.astype(BF16)
    v = _dot(h, w_ref[:, C_V:C_QI])
    v_ref[...] = v
    vbf_ref[...] = v.astype(BF16)
    zqi = _dot(h, w_ref[:, C_QI:C_KIW])
    for hd in range(IDX_HEADS):
        sl = slice(hd * LANES, (hd + 1) * LANES)
        qi_ref[:, sl] = rot(zqi[:, sl], cos, sin).astype(BF16)
    kiw = rot(_dot(h, w_ref[:, C_KIW:C_END]), cosb_ref[...], sinb_ref[...])
    kiw_ref[...] = kiw
    kiwbf_ref[...] = kiw.astype(BF16)


def _l0_in_proj(x2, g, w, tabs, tm, n_tab_blocks):
    m = x2.shape[0]
    row = lambda i: (i, 0)
    const = lambda i: (0, 0)
    tab = lambda i: (i % n_tab_blocks, 0)
    widths = (512, 512, 1024, 128, 128, 512, 128, 128, 128, 128)
    dtypes = (F32, F32, BF16, F32, F32, BF16, F32, BF16, BF16, BF16)
    return pl.pallas_call(
        _l0_in_kernel,
        grid=(m // tm,),
        in_specs=[pl.BlockSpec((tm, D_MODEL), row), pl.BlockSpec((1, D_MODEL), const),
                  pl.BlockSpec((D_MODEL, C_END), const)] + [pl.BlockSpec((tm, LANES), tab)] * 4,
        out_specs=[pl.BlockSpec((tm, wd), row) for wd in widths],
        out_shape=[jax.ShapeDtypeStruct((m, wd), dt) for wd, dt in zip(widths, dtypes)],
        compiler_params=_cparams(("parallel",)),
        name="l0_in_proj",
    )(x2, g, w, *tabs)


def _score_keys(score):
    bits = pltpu.bitcast(score, I32)
    return bits ^ ((bits >> 31) & jnp.int32(0x7FFFFFFF))


def _lane_fold(m):
    acc = m[:, 0:LANES]
    for c in range(1, m.shape[1] // LANES):
        acc = acc + m[:, c * LANES:(c + 1) * LANES]
    return acc


def _count(s_ref, n_chunks, pred):
    rows = s_ref.shape[0]

    def body(c, acc):
        k0 = pl.multiple_of(c * KEY_CHUNK, KEY_CHUNK)
        blk = s_ref[:, pl.ds(k0, KEY_CHUNK)]
        idx = k0 + lax.broadcasted_iota(I32, blk.shape, 1)
        return acc + _lane_fold(jnp.where(pred(blk, idx), 1, 0).astype(I32))

    acc = lax.fori_loop(0, n_chunks, body, jnp.zeros((rows, LANES), I32))
    return jnp.sum(acc.astype(F32), axis=1, keepdims=True).astype(I32)


def _select_to_bias(s_ref, n_chunks, topk):
    rows = s_ref.shape[0]

    def bit_step(i, thr):
        cand = thr + lax.shift_left(jnp.int32(1), 31 - i)
        cnt = _count(s_ref, n_chunks, lambda blk, idx: blk >= cand)
        return jnp.where(cnt >= topk, cand, thr)

    thr = lax.fori_loop(0, 32, bit_step, jnp.full((rows, 1), INT_MIN, I32))
    cnt_gt = _count(s_ref, n_chunks, lambda blk, idx: blk > thr)
    cnt_ge = _count(s_ref, n_chunks, lambda blk, idx: blk >= thr)
    has_thr = thr > INT_MIN
    excess = has_thr & (cnt_ge > topk)
    need = topk - cnt_gt
    n_bits = (s_ref.shape[1] - 1).bit_length()

    def tie_search(_):
        def step(i, x):
            cand = x | lax.shift_left(jnp.int32(1), n_bits - 1 - i)
            below = _count(s_ref, n_chunks, lambda blk, idx: (blk == thr) & (idx < cand))
            return jnp.where(below < need, cand, x)
        return lax.fori_loop(0, n_bits, step, jnp.zeros((rows, 1), I32))

    any_excess = jnp.max(jnp.where(excess, 1.0, 0.0)) > 0.0
    cut = lax.cond(any_excess, tie_search, lambda _: jnp.zeros((rows, 1), I32), 0)
    cut = jnp.where(excess, cut, jnp.where(has_thr, jnp.int32(2 ** 31 - 1), jnp.int32(-1)))

    def to_bias(c, _):
        k0 = pl.multiple_of(c * KEY_CHUNK, KEY_CHUNK)
        blk = s_ref[:, pl.ds(k0, KEY_CHUNK)]
        idx = k0 + lax.broadcasted_iota(I32, blk.shape, 1)
        sel = (blk > thr) | ((blk == thr) & (idx <= cut))
        s_ref[:, pl.ds(k0, KEY_CHUNK)] = pltpu.bitcast(jnp.where(sel, 0.0, NEG_BIG).astype(F32), I32)
        return 0

    lax.fori_loop(0, n_chunks, to_bias, 0)


def _attend_group(qg, segments, bias_ref, rep):
    n = qg.shape[0]

    def step(load_kv, c, carry):
        m, l, acc = carry
        k0 = pl.multiple_of(c * KEY_CHUNK, KEY_CHUNK)
        kc, vc = load_kv(k0)
        bias = pltpu.bitcast(bias_ref[:, pl.ds(k0, KEY_CHUNK)], F32)
        s = _dot_nt(qg, kc) + jnp.concatenate([bias] * rep, axis=0)
        m_new = jnp.maximum(m, jnp.max(s, axis=1, keepdims=True))
        alpha = jnp.exp(m - m_new)
        p = jnp.exp(s - m_new)
        l = alpha * l + jnp.sum(p, axis=1, keepdims=True)
        acc = alpha * acc + _dot(p.astype(BF16), vc)
        return m_new, l, acc

    carry = (jnp.full((n, 1), NEG_BIG, F32), jnp.zeros((n, 1), F32), jnp.zeros((n, LANES), F32))
    for first, end, load_kv in segments:
        carry = lax.fori_loop(first, end, functools.partial(step, load_kv), carry)
    _, l, acc = carry
    return acc / l


def _stack_heads(x, first, count, width=LANES):
    x = x.astype(F32)
    return jnp.concatenate([x[:, (first + hd) * LANES:(first + hd) * LANES + width] for hd in range(count)],
                           axis=0).astype(BF16)


def _attend_all_heads(q, segments, bias_ref, o_ref):
    rows = q.shape[0]
    group = ATT_HEADS // KV_HEADS
    for g in range(KV_HEADS):
        out = _attend_group(_stack_heads(q, g * group, group), segments, bias_ref, group)
        out = jnp.where(_group_lane_mask(out.shape, g), out, 0.0)
        for hd in range(group):
            sl = slice((g * group + hd) * LANES, (g * group + hd + 1) * LANES)
            o_ref[0, :, sl] = out[hd * rows:(hd + 1) * rows].astype(o_ref.dtype)


def _indexer_weights(kiw_q):
    return kiw_q[:, IDX_DIM:IDX_DIM + IDX_HEADS] * (IDX_HEADS ** -0.5 * IDX_DIM ** -0.5)


def _group_lane_mask(shape, g):
    lane = lax.broadcasted_iota(I32, shape, 1)
    return (lane >= g * HEAD_DIM) & (lane < (g + 1) * HEAD_DIM)


def _dsa_prompt_kernel(q_ref, qi_ref, kiwq_ref, k_ref, v_ref, kiw_ref, o_ref, s_ref, *, qb, topk):
    j = pl.program_id(1)
    n_chunks = (j * qb + qb + KEY_CHUNK - 1) // KEY_CHUNK
    qpos = j * qb + lax.broadcasted_iota(I32, (qb, 1), 0)
    wq = _indexer_weights(kiwq_ref[0])
    qi = qi_ref[0]

    def scores(c, _):
        k0 = pl.multiple_of(c * KEY_CHUNK, KEY_CHUNK)
        kc = kiw_ref[0, pl.ds(k0, KEY_CHUNK), :]
        acc = jnp.zeros((qb, KEY_CHUNK), F32)
        for hd in range(IDX_HEADS):
            s = _dot_nt(qi[:, hd * LANES:(hd + 1) * LANES], kc)
            acc = acc + jnp.maximum(s, 0.0) * wq[:, hd:hd + 1]
        kpos = k0 + lax.broadcasted_iota(I32, (qb, KEY_CHUNK), 1)
        s_ref[:, pl.ds(k0, KEY_CHUNK)] = jnp.where(kpos <= qpos, _score_keys(acc), INT_MIN)
        return 0

    lax.fori_loop(0, n_chunks, scores, 0)
    _select_to_bias(s_ref, n_chunks, topk)

    def load_kv(k0):
        return k_ref[0, pl.ds(k0, KEY_CHUNK), :], v_ref[0, pl.ds(k0, KEY_CHUNK), :]

    _attend_all_heads(q_ref[0], [(0, n_chunks, load_kv)], s_ref, o_ref)


def _dsa_prompt(q, qi, kiw, kbf, vbf, kiwbf, qb):
    b, t, _ = q.shape
    topk = min(TOPK_MAX, t // 4)
    qblk = lambda width: pl.BlockSpec((1, qb, width), lambda bi, j: (bi, j, 0))
    full = pl.BlockSpec((1, t, LANES), lambda bi, j: (bi, 0, 0))
    return pl.pallas_call(
        functools.partial(_dsa_prompt_kernel, qb=qb, topk=topk),
        grid=(b, t // qb),
        in_specs=[qblk(ATT_HEADS * LANES), qblk(IDX_HEADS * LANES), qblk(LANES), full, full, full],
        out_specs=qblk(ATT_HEADS * LANES),
        out_shape=jax.ShapeDtypeStruct((b, t, ATT_HEADS * LANES), BF16),
        scratch_shapes=[pltpu.VMEM((qb, pl.cdiv(t, KEY_CHUNK) * KEY_CHUNK), I32)],
        compiler_params=_cparams(("parallel", "arbitrary")),
        name="dsa_prompt",
    )(q, qi, kiw, kbf, vbf, kiwbf)


def _dsa_sample_kernel(pt_ref, q_ref, qi_ref, kiwq_ref, kn_ref, vn_ref, kiwn_ref, ck_hbm, cv_hbm, cki_hbm,
                       o_ref, kbuf, vbuf, kibuf, s_ref, sems, *, ts, n_pages, topk):
    b = pl.program_id(0)
    past = n_pages * PAGE_SIZE

    streams = ((cki_hbm, kibuf), (ck_hbm, kbuf), (cv_hbm, vbuf))

    def page_copy(p, which):
        src, dst = streams[which]
        rows = pl.ds(pl.multiple_of(p * PAGE_SIZE, PAGE_SIZE), PAGE_SIZE)
        return pltpu.make_async_copy(src.at[pt_ref[b, p]], dst.at[rows], sems.at[which])

    def start_page(p, _):
        for which in range(len(streams)):
            page_copy(p, which).start()
        return 0

    lax.fori_loop(0, n_pages, start_page, 0)

    def wait_pages(which):
        def body(p, _):
            page_copy(p, which).wait()
            return 0
        lax.fori_loop(0, n_pages, body, 0)

    n_past_chunks = past // KEY_CHUNK
    n_chunks = n_past_chunks + 1
    qpos = lax.broadcasted_iota(I32, (ts, 1), 0)
    wq = _indexer_weights(kiwq_ref[0])
    qi_stack = _stack_heads(qi_ref[0], 0, IDX_HEADS, IDX_DIM)
    w_stack = jnp.concatenate([wq[:, hd:hd + 1] for hd in range(IDX_HEADS)], axis=0)

    def head_sum(x):
        acc = x[0:ts]
        for hd in range(1, IDX_HEADS):
            acc = acc + x[hd * ts:(hd + 1) * ts]
        return acc

    wait_pages(0)

    def past_scores(c, _):
        k0 = pl.multiple_of(c * KEY_CHUNK, KEY_CHUNK)
        kc = kibuf[pl.ds(k0, KEY_CHUNK), :].astype(BF16)
        acc = head_sum(jnp.maximum(_dot_nt(qi_stack, kc), 0.0) * w_stack)
        s_ref[:, pl.ds(k0, KEY_CHUNK)] = _score_keys(acc + 0.0)
        return 0

    lax.fori_loop(0, n_past_chunks, past_scores, 0)
    kin = kiwn_ref[0][:, 0:IDX_DIM]
    acc = head_sum(jnp.maximum(_dot_nt(qi_stack, kin), 0.0) * w_stack)
    kpos = lax.broadcasted_iota(I32, (ts, KEY_CHUNK), 1)
    s_ref[:, pl.ds(past, KEY_CHUNK)] = jnp.where(kpos <= qpos, _score_keys(acc + 0.0), INT_MIN)
    _select_to_bias(s_ref, n_chunks, topk)

    wait_pages(1)
    wait_pages(2)

    def load_past(k0):
        return kbuf[pl.ds(k0, KEY_CHUNK), :].astype(BF16), vbuf[pl.ds(k0, KEY_CHUNK), :].astype(BF16)

    def load_new(k0):
        return kn_ref[0], vn_ref[0]

    segments = [(0, n_past_chunks, load_past), (n_past_chunks, n_chunks, load_new)]
    _attend_all_heads(q_ref[0], segments, s_ref, o_ref)


def _dsa_sample(q, qi, kiw, kn, vn, kiwn, cache_k, cache_v, cache_ki, page_table):
    b, ts, _ = q.shape
    n_pages = page_table.shape[1]
    past = n_pages * PAGE_SIZE
    topk = min(TOPK_MAX, (past + ts) // 4)
    blk = lambda rows, width: pl.BlockSpec((1, rows, width), lambda bi, pt: (bi, 0, 0))
    hbm = pl.BlockSpec(memory_space=pl.ANY)
    grid_spec = pltpu.PrefetchScalarGridSpec(
        num_scalar_prefetch=1,
        grid=(b,),
        in_specs=[blk(ts, ATT_HEADS * LANES), blk(ts, IDX_HEADS * LANES), blk(ts, LANES),
                  blk(KEY_CHUNK, LANES), blk(KEY_CHUNK, LANES), blk(KEY_CHUNK, LANES), hbm, hbm, hbm],
        out_specs=blk(ts, ATT_HEADS * LANES),
        scratch_shapes=[pltpu.VMEM((past, LANES), F32), pltpu.VMEM((past, LANES), F32),
                        pltpu.VMEM((past, IDX_DIM), F32), pltpu.VMEM((ts, past + KEY_CHUNK), I32),
                        pltpu.SemaphoreType.DMA((3,))],
    )
    return pl.pallas_call(
        functools.partial(_dsa_sample_kernel, ts=ts, n_pages=n_pages, topk=topk),
        grid_spec=grid_spec,
        out_shape=jax.ShapeDtypeStruct((b, ts, ATT_HEADS * LANES), F32),
        compiler_params=_cparams(("arbitrary",)),
        name="dsa_sample",
    )(page_table, q, qi, kiw, kn, vn, kiwn, cache_k, cache_v, cache_ki)


def _conv_kernel(u_ref, halo_ref, buf_ref, bg_ref, w_ref, ya_ref):
    i = pl.program_id(1)
    u = u_ref[0]
    halo, buf = halo_ref[0], buf_ref[0]
    first = i == 0
    prev1 = jnp.where(first, buf[1:2], halo[7:8])
    prev2 = jnp.where(first, buf[0:1], halo[6:7])
    row = lax.broadcasted_iota(I32, u.shape, 0)
    um1 = jnp.where(row == 0, prev1, pltpu.roll(u, 1, 0))
    um2 = jnp.where(row == 0, prev2, jnp.where(row == 1, prev1, pltpu.roll(u, 2, 0)))
    w = w_ref[...]
    conv = w[0:1] * um2 + w[1:2] * um1 + w[2:3] * u
    ya_ref[0] = (bg_ref[0] * conv).astype(BF16)


def _conv(u, bg, buf, w, tt):
    b, t, c = u.shape
    halo_rows = 8
    tile = pl.BlockSpec((1, tt, c), lambda bi, i: (bi, i, 0))
    halo = pl.BlockSpec((1, halo_rows, c), lambda bi, i: (bi, jnp.maximum(i * (tt // halo_rows) - 1, 0), 0))
    return pl.pallas_call(
        _conv_kernel,
        grid=(b, t // tt),
        in_specs=[tile, halo, pl.BlockSpec((1, CONV_WIDTH - 1, c), lambda bi, i: (bi, 0, 0)), tile,
                  pl.BlockSpec((CONV_WIDTH, c), lambda bi, i: (0, 0))],
        out_specs=tile,
        out_shape=jax.ShapeDtypeStruct((b, t, c), BF16),
        compiler_params=_cparams(("parallel", "parallel")),
        name="short_conv",
    )(u, u, buf, bg, w)


def _mm_res_kernel(*refs, n_in):
    a_refs, w_ref, x_ref, o_ref = refs[:n_in], refs[n_in], refs[n_in + 1], refs[n_in + 2]
    a = jnp.concatenate([r[...] for r in a_refs], axis=1) if n_in > 1 else a_refs[0][...]
    o_ref[...] = x_ref[...] + _dot(a, w_ref[...])


def _mm_res(a_list, w, x, tm):
    m = x.shape[0]
    row = lambda i: (i, 0)
    return pl.pallas_call(
        functools.partial(_mm_res_kernel, n_in=len(a_list)),
        grid=(m // tm,),
        in_specs=[pl.BlockSpec((tm, a.shape[1]), row) for a in a_list]
        + [pl.BlockSpec(w.shape, lambda i: (0, 0)), pl.BlockSpec((tm, D_MODEL), row)],
        out_specs=pl.BlockSpec((tm, D_MODEL), row),
        out_shape=jax.ShapeDtypeStruct((m, D_MODEL), F32),
        compiler_params=_cparams(("parallel",)),
        name="matmul_residual",
    )(*a_list, w, x)


def _norm_mm_kernel(x_ref, g_ref, w_ref, o_ref, h_scr):
    @pl.when(pl.program_id(1) == 0)
    def _():
        h_scr[...] = _rms(x_ref[...], g_ref[...]).astype(BF16)

    o_ref[...] = _dot(h_scr[...], w_ref[...])


def _norm_mm(x, g, w, tm, tn):
    m, n = x.shape[0], w.shape[1]
    return pl.pallas_call(
        _norm_mm_kernel,
        grid=(m // tm, n // tn),
        in_specs=[pl.BlockSpec((tm, D_MODEL), lambda i, j: (i, 0)), pl.BlockSpec((1, D_MODEL), lambda i, j: (0, 0)),
                  pl.BlockSpec((D_MODEL, tn), lambda i, j: (0, j))],
        out_specs=pl.BlockSpec((tm, tn), lambda i, j: (i, j)),
        out_shape=jax.ShapeDtypeStruct((m, n), F32),
        scratch_shapes=[pltpu.VMEM((tm, D_MODEL), BF16)],
        compiler_params=_cparams(("parallel", "arbitrary")),
        name="norm_matmul",
    )(x, g, w)


def _ffn_kernel(x_ref, g_ref, wg_ref, wu_ref, wd_ref, o_ref, h_scr, acc):
    f = pl.program_id(1)

    @pl.when(f == 0)
    def _():
        x = x_ref[...]
        h_scr[...] = _rms(x, g_ref[...]).astype(BF16)
        acc[...] = x

    h = h_scr[...]
    a = jax.nn.silu(_dot(h, wg_ref[...])) * _dot(h, wu_ref[...])
    acc[...] += _dot(a.astype(BF16), wd_ref[...])

    @pl.when(f == pl.num_programs(1) - 1)
    def _():
        o_ref[...] = acc[...]


def _ffn(x, g, wg, wu, wd, tm, tf):
    m, ff = x.shape[0], wg.shape[1]
    row = lambda i, f: (i, 0)
    return pl.pallas_call(
        _ffn_kernel,
        grid=(m // tm, ff // tf),
        in_specs=[pl.BlockSpec((tm, D_MODEL), row), pl.BlockSpec((1, D_MODEL), lambda i, f: (0, 0)),
                  pl.BlockSpec((D_MODEL, tf), lambda i, f: (0, f)), pl.BlockSpec((D_MODEL, tf), lambda i, f: (0, f)),
                  pl.BlockSpec((tf, D_MODEL), lambda i, f: (f, 0))],
        out_specs=pl.BlockSpec((tm, D_MODEL), row),
        out_shape=jax.ShapeDtypeStruct((m, D_MODEL), F32),
        scratch_shapes=[pltpu.VMEM((tm, D_MODEL), BF16), pltpu.VMEM((tm, D_MODEL), F32)],
        compiler_params=_cparams(("parallel", "arbitrary")),
        name="dense_swiglu",
    )(x, g, wg, wu, wd)


def _top2_gates(logits):
    lane = lax.broadcasted_iota(I32, logits.shape, 1).astype(F32)
    neg = jnp.float32(-jnp.inf)
    l1 = jnp.where(lane < N_EXPERTS, logits, neg)
    m1 = jnp.max(l1, axis=1, keepdims=True)
    i1 = jnp.min(jnp.where(l1 == m1, lane, float(LANES)), axis=1, keepdims=True)
    l2 = jnp.where(lane == i1, neg, l1)
    m2 = jnp.max(l2, axis=1, keepdims=True)
    i2 = jnp.min(jnp.where(l2 == m2, lane, float(LANES)), axis=1, keepdims=True)
    e = jnp.exp(m2 - m1)
    w1 = 1.0 / (1.0 + e)
    w2 = e / (1.0 + e)
    return jnp.where(lane == i1, w1, jnp.where(lane == i2, w2, 0.0))


def _moe_kernel(x_ref, g_ref, rhi_ref, rlo_ref, wg_ref, wu_ref, wd_ref, o_ref, h_scr, gate_scr, acc_e, acc_o):
    e, f = pl.program_id(1), pl.program_id(2)
    last_f = pl.num_programs(2) - 1

    @pl.when((e == 0) & (f == 0))
    def _():
        x = x_ref[...]
        hn = _rms(x, g_ref[...])
        h_hi = hn.astype(BF16)
        h_lo = (hn - h_hi.astype(F32)).astype(BF16)
        logits = _dot(h_hi, rhi_ref[...]) + (_dot(h_lo, rhi_ref[...]) + _dot(h_hi, rlo_ref[...]))
        gate_scr[...] = _top2_gates(logits)
        h_scr[...] = h_hi
        acc_o[...] = x

    @pl.when(f == 0)
    def _():
        acc_e[...] = jnp.zeros_like(acc_e)

    h = h_scr[...]
    a = jax.nn.silu(_dot(h, wg_ref[0])) * _dot(h, wu_ref[0])
    acc_e[...] += _dot(a.astype(BF16), wd_ref[0])

    @pl.when(f == last_f)
    def _():
        gate = gate_scr[...]
        lane = lax.broadcasted_iota(I32, gate.shape, 1)
        ge = jnp.sum(jnp.where(lane == e, gate, 0.0), axis=1, keepdims=True)
        acc_o[...] += ge * acc_e[...]

    @pl.when((e == pl.num_programs(1) - 1) & (f == last_f))
    def _():
        o_ref[...] = acc_o[...]


def _moe(x, g, r_hi, r_lo, wg, wu, wd, tm, tf):
    m = x.shape[0]
    row = lambda i, e, f: (i, 0)
    const = lambda i, e, f: (0, 0)
    return pl.pallas_call(
        _moe_kernel,
        grid=(m // tm, N_EXPERTS, EXPERT_FF // tf),
        in_specs=[pl.BlockSpec((tm, D_MODEL), row), pl.BlockSpec((1, D_MODEL), const),
                  pl.BlockSpec((D_MODEL, LANES), const), pl.BlockSpec((D_MODEL, LANES), const),
                  pl.BlockSpec((1, D_MODEL, tf), lambda i, e, f: (e, 0, f)),
                  pl.BlockSpec((1, D_MODEL, tf), lambda i, e, f: (e, 0, f)),
                  pl.BlockSpec((1, tf, D_MODEL), lambda i, e, f: (e, f, 0))],
        out_specs=pl.BlockSpec((tm, D_MODEL), row),
        out_shape=jax.ShapeDtypeStruct((m, D_MODEL), F32),
        scratch_shapes=[pltpu.VMEM((tm, D_MODEL), BF16), pltpu.VMEM((tm, LANES), F32),
                        pltpu.VMEM((tm, D_MODEL), F32), pltpu.VMEM((tm, D_MODEL), F32)],
        compiler_params=_cparams(("parallel", "arbitrary", "arbitrary")),
        name="moe_swiglu",
    )(x, g, r_hi, r_lo, wg, wu, wd)


def _ple_kernel(x_ref, g_ref, p_ref, wp_ref, wgate_ref, gf_ref, o_ref, *, final_norm):
    x = x_ref[...]
    hp = _rms(x, g_ref[...]).astype(BF16)
    gate = jax.nn.sigmoid(_dot(hp, wgate_ref[...]))
    y = x + _dot(p_ref[...].astype(BF16), wp_ref[...]) * gate
    if final_norm:
        y = _rms(y, gf_ref[...])
    o_ref[...] = y


def _ple(x, g, p, wp, wgate, g_final, final_norm, tm):
    m = x.shape[0]
    row = lambda i: (i, 0)
    const = lambda i: (0, 0)
    return pl.pallas_call(
        functools.partial(_ple_kernel, final_norm=final_norm),
        grid=(m // tm,),
        in_specs=[pl.BlockSpec((tm, D_MODEL), row), pl.BlockSpec((1, D_MODEL), const), pl.BlockSpec((tm, PLE_DIM), row),
                  pl.BlockSpec((PLE_DIM, D_MODEL), const), pl.BlockSpec((D_MODEL, D_MODEL), const),
                  pl.BlockSpec((1, D_MODEL), const)],
        out_specs=pl.BlockSpec((tm, D_MODEL), row),
        out_shape=jax.ShapeDtypeStruct((m, D_MODEL), F32),
        compiler_params=_cparams(("parallel",)),
        name="per_layer_embedding",
    )(x, g, p, wp, wgate, g_final)


def _ret_kernel(q_ref, k_ref, v_ref, gate_ref, cos_ref, sin_ref, s0_ref, o_ref, sout_ref, state, *, chunk, chunk_rows):
    i = pl.program_id(1)

    @pl.when(i == 0)
    def _():
        state[...] = s0_ref[0]

    cos, sin = cos_ref[...], sin_ref[...]
    tt = q_ref.shape[1]
    r = chunk_rows
    ii = lax.broadcasted_iota(I32, (r, r), 0).astype(F32)
    jj = lax.broadcasted_iota(I32, (r, r), 1).astype(F32)
    rel = ii - jj
    icol = lax.broadcasted_iota(I32, (r, 1), 0).astype(F32)
    half = RET_DK // 2

    def rot(ref, hd):
        x1 = ref[0, :, hd * RET_DK:hd * RET_DK + half]
        x2 = ref[0, :, hd * RET_DK + half:(hd + 1) * RET_DK]
        return jnp.concatenate([x1 * cos - x2 * sin, x2 * cos + x1 * sin], axis=1)

    for hd in range(RET_HEADS):
        lg = math.log(1.0 - 2.0 ** (-5.0 - hd))
        d_in = jnp.where(rel >= 0, jnp.exp(lg * jnp.maximum(rel, 0.0)), 0.0)
        d_q = jnp.exp(lg * (icol + 1.0))
        d_k = jnp.exp(lg * (chunk - 1.0 - icol)) * (RET_DK ** -0.5)
        d_c = math.exp(lg * chunk)
        qr = rot(q_ref, hd)
        kr = rot(k_ref, hd)
        vsl = slice(hd * RET_DV, (hd + 1) * RET_DV)
        for c in range(tt // r):
            rows = slice(c * r, (c + 1) * r)
            qc = qr[rows].astype(BF16)
            kc = kr[rows]
            vc = v_ref[0, rows, vsl].astype(BF16)
            s_prev = state[hd]
            att = _dot_nt(qc, (kc * (RET_DK ** -0.5)).astype(BF16)) * d_in
            o = _dot(att.astype(BF16), vc) + _dot(qc, s_prev.astype(BF16)) * d_q
            state[hd] = s_prev * d_c + _dot_tn((kc * d_k).astype(BF16), vc)
            mu = jnp.mean(o, axis=-1, keepdims=True)
            var = jnp.mean(jnp.square(o - mu), axis=-1, keepdims=True)
            on = (o - mu) * lax.rsqrt(var + EPS)
            o_ref[0, rows, vsl] = (jax.nn.silu(gate_ref[0, rows, vsl]) * on).astype(BF16)

    @pl.when(i == pl.num_programs(1) - 1)
    def _():
        sout_ref[0] = state[...]


def _retention(z, s0, cos, sin, tt, chunk, chunk_rows):
    b, t, _ = z.shape
    hk, hv = RET_HEADS * RET_DK, RET_HEADS * RET_DV
    half = RET_DK // 2
    state_spec = pl.BlockSpec((1, RET_HEADS, RET_DK, RET_DV), lambda bi, i: (bi, 0, 0, 0))
    tab = pl.BlockSpec((tt, half), lambda bi, i: (i, 0))
    return pl.pallas_call(
        functools.partial(_ret_kernel, chunk=chunk, chunk_rows=chunk_rows),
        grid=(b, t // tt),
        in_specs=[pl.BlockSpec((1, tt, hk), lambda bi, i: (bi, i, 0)), pl.BlockSpec((1, tt, hk), lambda bi, i: (bi, i, 1)),
                  pl.BlockSpec((1, tt, hv), lambda bi, i: (bi, i, 1)), pl.BlockSpec((1, tt, hv), lambda bi, i: (bi, i, 2)),
                  tab, tab, state_spec],
        out_specs=[pl.BlockSpec((1, tt, hv), lambda bi, i: (bi, i, 0)), state_spec],
        out_shape=[jax.ShapeDtypeStruct((b, t, hv), BF16), jax.ShapeDtypeStruct(s0.shape, F32)],
        scratch_shapes=[pltpu.VMEM((RET_HEADS, RET_DK, RET_DV), F32)],
        compiler_params=_cparams(("parallel", "arbitrary")),
        name="retention",
    )(z, z, z, z, cos, sin, s0)


def _pack_l0_w_in(w):
    offs = np.cumsum((0,) + AB_SPLITS)
    bg, cg, hv, q, k, v, qi, ki, wi = [w[:, offs[n]:offs[n + 1]] for n in range(len(AB_SPLITS))]
    group = ATT_HEADS // KV_HEADS
    q4 = q.reshape(D_MODEL, ATT_HEADS, HEAD_DIM)
    zq = jnp.zeros_like(q4)
    q_pad = jnp.concatenate([jnp.concatenate([q4[:, :group], zq[:, :group]], axis=-1),
                             jnp.concatenate([zq[:, group:], q4[:, group:]], axis=-1)], axis=1).reshape(D_MODEL, -1)
    qi4 = qi.reshape(D_MODEL, IDX_HEADS, IDX_DIM)
    qi_pad = jnp.concatenate([qi4, jnp.zeros_like(qi4)], axis=-1).reshape(D_MODEL, -1)
    kiw = jnp.concatenate([ki, wi, jnp.zeros((D_MODEL, LANES - IDX_DIM - IDX_HEADS), w.dtype)], axis=1)
    return jnp.concatenate([bg, cg, hv, q_pad, k, v, qi_pad, kiw], axis=1).astype(BF16)


def _pack_l0_w_out(w):
    group = ATT_HEADS // KV_HEADS
    wa, wb = w[:CONV_CH], w[CONV_CH:].reshape(ATT_HEADS, HEAD_DIM, D_MODEL)
    zb = jnp.zeros_like(wb)
    wb_pad = jnp.concatenate([jnp.concatenate([wb[:group], zb[:group]], axis=1),
                              jnp.concatenate([zb[group:], wb[group:]], axis=1)], axis=0).reshape(-1, D_MODEL)
    return jnp.concatenate([wa, wb_pad], axis=0).astype(BF16)


def _rope_tables(pos, reps):
    inv = ROPE_THETA ** (-jnp.arange(0, HEAD_DIM, 2, dtype=F32) / HEAD_DIM)
    ang = pos.astype(F32)[:, None] * inv[None, :]
    cos, sin = jnp.cos(ang), jnp.sin(ang)
    cos64 = jnp.concatenate([cos, cos], axis=1)
    sin64 = jnp.concatenate([-sin, sin], axis=1)
    one, zero = jnp.ones_like(cos64), jnp.zeros_like(cos64)
    tabs = (jnp.concatenate([cos64, cos64], 1), jnp.concatenate([sin64, sin64], 1),
            jnp.concatenate([cos64, one], 1), jnp.concatenate([sin64, zero], 1))
    return tuple(jnp.tile(tb, (reps, 1)) for tb in tabs)


def _ret_tables(pos):
    inv = ROPE_THETA ** (-jnp.linspace(0.0, 1.0, RET_DK // 2, dtype=F32))
    ang = pos.astype(F32)[:, None] * inv[None, :]
    return jnp.cos(ang), jnp.sin(ang)


def _pack_params(prm):
    r = prm['moe_router'][0]
    r_pad = jnp.concatenate([r, jnp.zeros((D_MODEL, LANES - N_EXPERTS), F32)], axis=1)
    r_hi = r_pad.astype(BF16)
    bf = lambda a: a.astype(BF16)
    return dict(
        l0_w_in=_pack_l0_w_in(prm['ab_w_in'][0]), l0_w_out=_pack_l0_w_out(prm['ab_w_out'][0]),
        conv_w=prm['ab_conv_w'][0],
        ffn=(bf(prm['ffn_w_gate'][0]), bf(prm['ffn_w_up'][0]), bf(prm['ffn_w_down'][0])),
        ret_w_in=bf(prm['ret_w_in'][0]), ret_w_out=bf(prm['ret_w_out'][0]),
        r_hi=r_hi, r_lo=(r_pad - r_hi.astype(F32)).astype(BF16),
        moe=(bf(prm['moe_w_gate'][0]), bf(prm['moe_w_up'][0]), bf(prm['moe_w_down'][0])),
        ple_w=bf(prm['ple_w']), ple_gate_w=bf(prm['ple_gate_w']),
        norm_mix=prm['norm_mix'][:, None, :], norm_ffn=prm['norm_ffn'][:, None, :],
        norm_ple=prm['norm_ple'][:, None, :], norm_final=prm['norm_final'][None, :],
    )


def _trunk(x, p, pos, conv_buf, ret_state, pk, paged):
    b, t, _ = x.shape
    m = b * t
    tm = min(m, TOKEN_TILE)
    x2 = x.reshape(m, D_MODEL)

    reps = max(1, tm // t)
    tabs = _rope_tables(pos, reps)
    bg, u, q, k, v, qi, kiw, kbf, vbf, kiwbf = _l0_in_proj(x2, pk['norm_mix'][0], pk['l0_w_in'], tabs, tm,
                                                          tabs[0].shape[0] // tm)
    seq = lambda a: a.reshape(b, t, a.shape[-1])
    if paged is None:
        yb = _dsa_prompt(seq(q), seq(qi), seq(kiw), seq(kbf), seq(vbf), seq(kiwbf), qb=min(t, 128))
    else:
        cache_k, cache_v, cache_ki, page_table = paged
        pad = lambda a: jnp.pad(seq(a), ((0, 0), (0, KEY_CHUNK - t), (0, 0)))
        n_pool = cache_k.shape[0]
        yb = _dsa_sample(seq(q).astype(F32), seq(qi).astype(F32), seq(kiw), pad(kbf), pad(vbf), pad(kiwbf),
                         cache_k.reshape(n_pool, PAGE_SIZE, LANES), cache_v.reshape(n_pool, PAGE_SIZE, LANES),
                         cache_ki, page_table).astype(BF16)
    u3 = seq(u)
    ya = _conv(u3, seq(bg), conv_buf, pk['conv_w'], tt=min(t, TOKEN_TILE))
    x2 = _mm_res([ya.reshape(m, CONV_CH), yb.reshape(m, ATT_HEADS * LANES)], pk['l0_w_out'], x2, tm)
    x2 = _ffn(x2, pk['norm_ffn'][0], *pk['ffn'], tm=tm, tf=D_FF // 2)
    x2 = _ple(x2, pk['norm_ple'][0], p[0].reshape(m, PLE_DIM), pk['ple_w'][0], pk['ple_gate_w'][0],
              pk['norm_final'], False, tm)
    new_k = k.reshape(1, b, t, KV_HEADS, HEAD_DIM)
    new_v = v.reshape(1, b, t, KV_HEADS, HEAD_DIM)
    new_ki = seq(kiw)[None, :, :, :IDX_DIM]
    new_conv = jnp.concatenate([conv_buf, u3], axis=1)[None, :, -(CONV_WIDTH - 1):]

    z = _norm_mm(x2, pk['norm_mix'][1], pk['ret_w_in'], tm, 1024).reshape(b, t, -1)
    cos_r, sin_r = _ret_tables(pos)
    if t % RET_CHUNK == 0:
        og, s_new = _retention(z, ret_state, cos_r, sin_r, tt=2 * RET_CHUNK, chunk=RET_CHUNK, chunk_rows=RET_CHUNK)
    else:
        rows = 16
        padt = lambda a: jnp.pad(a, ((0, 0),) * (a.ndim - 2) + ((0, rows - t), (0, 0)))
        og, s_new = _retention(padt(z), ret_state, padt(cos_r), padt(sin_r), tt=rows, chunk=t, chunk_rows=rows)
        og = og[:, :t]
    x2 = _mm_res([og.reshape(m, RET_HEADS * RET_DV)], pk['ret_w_out'], x2, tm)
    x2 = _moe(x2, pk['norm_ffn'][1], pk['r_hi'], pk['r_lo'], *pk['moe'], tm=tm, tf=EXPERT_FF // 4)
    x2 = _ple(x2, pk['norm_ple'][1], p[1].reshape(m, PLE_DIM), pk['ple_w'][1], pk['ple_gate_w'][1],
              pk['norm_final'], True, tm)
    return x2.reshape(b, t, D_MODEL), new_k, new_v, new_ki, new_conv, s_new[None]


def kernel(x_prompt, x_sample, cache_k, cache_v, cache_kidx, state_conv, state_ret, page_table, p_prompt, p_sample,
           norm_mix, norm_ffn, norm_ple, norm_final, ab_w_in, ab_conv_w, ab_w_out, ffn_w_gate, ffn_w_up, ffn_w_down,
           ret_w_in, ret_w_out, moe_router, moe_w_gate, moe_w_up, moe_w_down, ple_w, ple_gate_w):
    prm = dict(norm_mix=norm_mix, norm_ffn=norm_ffn, norm_ple=norm_ple, norm_final=norm_final, ab_w_in=ab_w_in,
               ab_conv_w=ab_conv_w, ab_w_out=ab_w_out, ffn_w_gate=ffn_w_gate, ffn_w_up=ffn_w_up, ffn_w_down=ffn_w_down,
               ret_w_in=ret_w_in, ret_w_out=ret_w_out, moe_router=moe_router, moe_w_gate=moe_w_gate,
               moe_w_up=moe_w_up, moe_w_down=moe_w_down, ple_w=ple_w, ple_gate_w=ple_gate_w)
    pk = _pack_params(prm)
    b, t = x_prompt.shape[0], x_prompt.shape[1]
    db, ts = x_sample.shape[0], x_sample.shape[1]
    past_len = page_table.shape[1] * PAGE_SIZE
    dt = x_prompt.dtype

    conv0 = jnp.zeros((b, CONV_WIDTH - 1, CONV_CH), dt)
    ret0 = jnp.zeros((b, RET_HEADS, RET_DK, RET_DV), dt)
    y_p, k_p, v_p, ki_p, cb_p, rs_p = _trunk(x_prompt, p_prompt, jnp.arange(t, dtype=I32), conv0, ret0, pk, None)

    pos_s = past_len + jnp.arange(ts, dtype=I32)
    paged = (cache_k[0], cache_v[0], cache_kidx[0], page_table)
    y_s, k_s, v_s, ki_s, cb_s, rs_s = _trunk(x_sample, p_sample, pos_s, state_conv[0], state_ret[0], pk, paged)
    return (y_p, y_s, k_p, v_p, ki_p, cb_p, rs_p, k_s, v_s, ki_s, cb_s, rs_s)
```

```python
import functools
import math

import jax
import jax.numpy as jnp
import numpy as np
from jax import lax
from jax.experimental import pallas as pl
from jax.experimental.pallas import tpu as pltpu

F32 = jnp.float32
BF16 = jnp.bfloat16
I32 = jnp.int32

D_MODEL = 1024
PAGE_SIZE = 128
CONV_CH = D_MODEL // 2
CONV_WIDTH = 3
ATT_HEADS = 8
KV_HEADS = 2
HEAD_DIM = 64
IDX_HEADS = 4
IDX_DIM = 64
TOPK_MAX = 256
ROPE_THETA = 10000.0
RET_HEADS = 4
RET_DK = D_MODEL // RET_HEADS
RET_DV = 2 * RET_DK
RET_CHUNK = 128
D_FF = 2816
N_EXPERTS = 8
EXPERT_FF = 3584
PLE_DIM = 256
EPS = 1e-6
AB_SPLITS = (CONV_CH, CONV_CH, CONV_CH, ATT_HEADS * HEAD_DIM, KV_HEADS * HEAD_DIM, KV_HEADS * HEAD_DIM,
             IDX_HEADS * IDX_DIM, IDX_DIM, IDX_HEADS)

LANES = 128
VMEM_LIMIT = 48 * 1024 * 1024
INT_MIN = -2 ** 31
NEG_BIG = -1e30

C_BG, C_CG, C_HV, C_Q, C_K, C_V, C_QI, C_KIW, C_END = 0, 512, 1024, 1536, 2560, 2688, 2816, 3328, 3456
KEY_CHUNK = 512
TOKEN_TILE = 512


def _cparams(sem):
    return pltpu.CompilerParams(dimension_semantics=sem, vmem_limit_bytes=VMEM_LIMIT)


def _rms(x, g):
    return x * lax.rsqrt(jnp.mean(x * x, axis=-1, keepdims=True) + EPS) * g


def _dot(a, b):
    return jnp.dot(a, b, preferred_element_type=F32)


def _dot_nt(a, b):
    return lax.dot_general(a, b, (((1,), (1,)), ((), ())), preferred_element_type=F32)


def _dot_tn(a, b):
    return lax.dot_general(a, b, (((0,), (0,)), ((), ())), preferred_element_type=F32)


def _swap_halves64(x):
    lane = lax.broadcasted_iota(I32, x.shape, 1)
    from_above = pltpu.roll(x, LANES - 32, 1)
    from_below = pltpu.roll(x, 32, 1)
    return jnp.where((lane & 63) < 32, from_above, from_below)


def _l0_in_kernel(x_ref, g_ref, w_ref, cos_ref, sin_ref, cosb_ref, sinb_ref,
                  bg_ref, u_ref, q_ref, k_ref, v_ref, qi_ref, kiw_ref, kbf_ref, vbf_ref, kiwbf_ref):
    h = _rms(x_ref[...], g_ref[...]).astype(BF16)
    cos, sin = cos_ref[...], sin_ref[...]

    def rot(z, c, s):
        return z * c + _swap_halves64(z) * s

    bg_ref[...] = _dot(h, w_ref[:, C_BG:C_CG])
    u_ref[...] = _dot(h, w_ref[:, C_CG:C_HV]) * _dot(h, w_ref[:, C_HV:C_Q])
    zq = _dot(h, w_ref[:, C_Q:C_K])
    for hd in range(ATT_HEADS):
        sl = slice(hd * LANES, (hd + 1) * LANES)
        q_ref[:, sl] = (rot(zq[:, sl], cos, sin) * (HEAD_DIM ** -0.5)).astype(BF16)
    k = rot(_dot(h, w_ref[:, C_K:C_V]), cos, sin)
    k_ref[...] = k
    kbf_ref[...] = k.astype(BF16)
    v = _dot(h, w_ref[:, C_V:C_QI])
    v_ref[...] = v
    vbf_ref[...] = v.astype(BF16)
    zqi = _dot(h, w_ref[:, C_QI:C_KIW])
    for hd in range(IDX_HEADS):
        sl = slice(hd * LANES, (hd + 1) * LANES)
        qi_ref[:, sl] = rot(zqi[:, sl], cos, sin).astype(BF16)
    kiw = rot(_dot(h, w_ref[:, C_KIW:C_END]), cosb_ref[...], sinb_ref[...])
    kiw_ref[...] = kiw
    kiwbf_ref[...] = kiw.astype(BF16)


def _l0_in_proj(x2, g, w, tabs, tm, n_tab_blocks):
    m = x2.shape[0]
    row = lambda i: (i, 0)
    const = lambda i: (0, 0)
    tab = lambda i: (i % n_tab_blocks, 0)
    widths = (512, 512, 1024, 128, 128, 512, 128, 128, 128, 128)
    dtypes = (F32, F32, BF16, F32, F32, BF16, F32, BF16, BF16, BF16)
    return pl.pallas_call(
        _l0_in_kernel,
        grid=(m // tm,),
        in_specs=[pl.BlockSpec((tm, D_MODEL), row), pl.BlockSpec((1, D_MODEL), const),
                  pl.BlockSpec((D_MODEL, C_END), const)] + [pl.BlockSpec((tm, LANES), tab)] * 4,
        out_specs=[pl.BlockSpec((tm, wd), row) for wd in widths],
        out_shape=[jax.ShapeDtypeStruct((m, wd), dt) for wd, dt in zip(widths, dtypes)],
        compiler_params=_cparams(("parallel",)),
        name="l0_in_proj",
    )(x2, g, w, *tabs)


def _score_keys(score):
    bits = pltpu.bitcast(score, I32)
    return bits ^ ((bits >> 31) & jnp.int32(0x7FFFFFFF))


def _lane_fold(m):
    acc = m[:, 0:LANES]
    for c in range(1, m.shape[1] // LANES):
        acc = acc + m[:, c * LANES:(c + 1) * LANES]
    return acc


def _count(s_ref, n_chunks, pred):
    rows = s_ref.shape[0]

    def body(c, acc):
        k0 = pl.multiple_of(c * KEY_CHUNK, KEY_CHUNK)
        blk = s_ref[:, pl.ds(k0, KEY_CHUNK)]
        idx = k0 + lax.broadcasted_iota(I32, blk.shape, 1)
        return acc + _lane_fold(jnp.where(pred(blk, idx), 1, 0).astype(I32))

    acc = lax.fori_loop(0, n_chunks, body, jnp.zeros((rows, LANES), I32))
    return jnp.sum(acc.astype(F32), axis=1, keepdims=True).astype(I32)


def _select_to_bias(s_ref, n_chunks, topk):
    rows = s_ref.shape[0]

    def bit_step(i, thr):
        cand = thr + lax.shift_left(jnp.int32(1), 31 - i)
        cnt = _count(s_ref, n_chunks, lambda blk, idx: blk >= cand)
        return jnp.where(cnt >= topk, cand, thr)

    thr = lax.fori_loop(0, 32, bit_step, jnp.full((rows, 1), INT_MIN, I32))
    cnt_gt = _count(s_ref, n_chunks, lambda blk, idx: blk > thr)
    cnt_ge = _count(s_ref, n_chunks, lambda blk, idx: blk >= thr)
    has_thr = thr > INT_MIN
    excess = has_thr & (cnt_ge > topk)
    need = topk - cnt_gt
    n_bits = (s_ref.shape[1] - 1).bit_length()

    def tie_search(_):
        def step(i, x):
            cand = x | lax.shift_left(jnp.int32(1), n_bits - 1 - i)
            below = _count(s_ref, n_chunks, lambda blk, idx: (blk == thr) & (idx < cand))
            return jnp.where(below < need, cand, x)
        return lax.fori_loop(0, n_bits, step, jnp.zeros((rows, 1), I32))

    any_excess = jnp.max(jnp.where(excess, 1.0, 0.0)) > 0.0
    cut = lax.cond(any_excess, tie_search, lambda _: jnp.zeros((rows, 1), I32), 0)
    cut = jnp.where(excess, cut, jnp.where(has_thr, jnp.int32(2 ** 31 - 1), jnp.int32(-1)))

    def to_bias(c, _):
        k0 = pl.multiple_of(c * KEY_CHUNK, KEY_CHUNK)
        blk = s_ref[:, pl.ds(k0, KEY_CHUNK)]
        idx = k0 + lax.broadcasted_iota(I32, blk.shape, 1)
        sel = (blk > thr) | ((blk == thr) & (idx <= cut))
        s_ref[:, pl.ds(k0, KEY_CHUNK)] = pltpu.bitcast(jnp.where(sel, 0.0, NEG_BIG).astype(F32), I32)
        return 0

    lax.fori_loop(0, n_chunks, to_bias, 0)


def _attend_group(qg, segments, bias_ref, rep):
    n = qg.shape[0]

    def step(load_kv, c, carry):
        m, l, acc = carry
        k0 = pl.multiple_of(c * KEY_CHUNK, KEY_CHUNK)
        kc, vc = load_kv(k0)
        bias = pltpu.bitcast(bias_ref[:, pl.ds(k0, KEY_CHUNK)], F32)
        s = _dot_nt(qg, kc) + jnp.concatenate([bias] * rep, axis=0)
        m_new = jnp.maximum(m, jnp.max(s, axis=1, keepdims=True))
        alpha = jnp.exp(m - m_new)
        p = jnp.exp(s - m_new)
        l = alpha * l + jnp.sum(p, axis=1, keepdims=True)
        acc = alpha * acc + _dot(p.astype(BF16), vc)
        return m_new, l, acc

    carry = (jnp.full((n, 1), NEG_BIG, F32), jnp.zeros((n, 1), F32), jnp.zeros((n, LANES), F32))
    for first, end, load_kv in segments:
        carry = lax.fori_loop(first, end, functools.partial(step, load_kv), carry)
    _, l, acc = carry
    return acc / l


def _stack_heads(x, first, count, width=LANES):
    x = x.astype(F32)
    return jnp.concatenate([x[:, (first + hd) * LANES:(first + hd) * LANES + width] for hd in range(count)],
                           axis=0).astype(BF16)


def _attend_all_heads(q, segments, bias_ref, o_ref):
    rows = q.shape[0]
    group = ATT_HEADS // KV_HEADS
    for g in range(KV_HEADS):
        out = _attend_group(_stack_heads(q, g * group, group), segments, bias_ref, group)
        out = jnp.where(_group_lane_mask(out.shape, g), out, 0.0)
        for hd in range(group):
            sl = slice((g * group + hd) * LANES, (g * group + hd + 1) * LANES)
            o_ref[0, :, sl] = out[hd * rows:(hd + 1) * rows].astype(o_ref.dtype)


def _indexer_weights(kiw_q):
    return kiw_q[:, IDX_DIM:IDX_DIM + IDX_HEADS] * (IDX_HEADS ** -0.5 * IDX_DIM ** -0.5)


def _group_lane_mask(shape, g):
    lane = lax.broadcasted_iota(I32, shape, 1)
    return (lane >= g * HEAD_DIM) & (lane < (g + 1) * HEAD_DIM)


def _dsa_prompt_kernel(q_ref, qi_ref, kiwq_ref, k_ref, v_ref, kiw_ref, o_ref, s_ref, *, qb, topk):
    j = pl.program_id(1)
    n_chunks = (j * qb + qb + KEY_CHUNK - 1) // KEY_CHUNK
    qpos = j * qb + lax.broadcasted_iota(I32, (qb, 1), 0)
    wq = _indexer_weights(kiwq_ref[0])
    qi = qi_ref[0]

    def scores(c, _):
        k0 = pl.multiple_of(c * KEY_CHUNK, KEY_CHUNK)
        kc = kiw_ref[0, pl.ds(k0, KEY_CHUNK), :]
        acc = jnp.zeros((qb, KEY_CHUNK), F32)
        for hd in range(IDX_HEADS):
            s = _dot_nt(qi[:, hd * LANES:(hd + 1) * LANES], kc)
            acc = acc + jnp.maximum(s, 0.0) * wq[:, hd:hd + 1]
        kpos = k0 + lax.broadcasted_iota(I32, (qb, KEY_CHUNK), 1)
        s_ref[:, pl.ds(k0, KEY_CHUNK)] = jnp.where(kpos <= qpos, _score_keys(acc), INT_MIN)
        return 0

    lax.fori_loop(0, n_chunks, scores, 0)
    _select_to_bias(s_ref, n_chunks, topk)

    def load_kv(k0):
        return k_ref[0, pl.ds(k0, KEY_CHUNK), :], v_ref[0, pl.ds(k0, KEY_CHUNK), :]

    _attend_all_heads(q_ref[0], [(0, n_chunks, load_kv)], s_ref, o_ref)


def _dsa_prompt(q, qi, kiw, kbf, vbf, kiwbf, qb):
    b, t, _ = q.shape
    topk = min(TOPK_MAX, t // 4)
    qblk = lambda width: pl.BlockSpec((1, qb, width), lambda bi, j: (bi, j, 0))
    full = pl.BlockSpec((1, t, LANES), lambda bi, j: (bi, 0, 0))
    return pl.pallas_call(
        functools.partial(_dsa_prompt_kernel, qb=qb, topk=topk),
        grid=(b, t // qb),
        in_specs=[qblk(ATT_HEADS * LANES), qblk(IDX_HEADS * LANES), qblk(LANES), full, full, full],
        out_specs=qblk(ATT_HEADS * LANES),
        out_shape=jax.ShapeDtypeStruct((b, t, ATT_HEADS * LANES), BF16),
        scratch_shapes=[pltpu.VMEM((qb, pl.cdiv(t, KEY_CHUNK) * KEY_CHUNK), I32)],
        compiler_params=_cparams(("parallel", "arbitrary")),
        name="dsa_prompt",
    )(q, qi, kiw, kbf, vbf, kiwbf)


def _dsa_sample_kernel(pt_ref, q_ref, qi_ref, kiwq_ref, kn_ref, vn_ref, kiwn_ref, ck_hbm, cv_hbm, cki_hbm,
                       o_ref, kbuf, vbuf, kibuf, s_ref, sems, *, ts, n_pages, topk):
    b = pl.program_id(0)
    past = n_pages * PAGE_SIZE

    streams = ((cki_hbm, kibuf), (ck_hbm, kbuf), (cv_hbm, vbuf))

    def page_copy(p, which):
        src, dst = streams[which]
        rows = pl.ds(pl.multiple_of(p * PAGE_SIZE, PAGE_SIZE), PAGE_SIZE)
        return pltpu.make_async_copy(src.at[pt_ref[b, p]], dst.at[rows], sems.at[which])

    def start_page(p, _):
        for which in range(len(streams)):
            page_copy(p, which).start()
        return 0

    lax.fori_loop(0, n_pages, start_page, 0)

    def wait_pages(which):
        def body(p, _):
            page_copy(p, which).wait()
            return 0
        lax.fori_loop(0, n_pages, body, 0)

    n_past_chunks = past // KEY_CHUNK
    n_chunks = n_past_chunks + 1
    qpos = lax.broadcasted_iota(I32, (ts, 1), 0)
    wq = _indexer_weights(kiwq_ref[0])
    qi_stack = _stack_heads(qi_ref[0], 0, IDX_HEADS, IDX_DIM)
    w_stack = jnp.concatenate([wq[:, hd:hd + 1] for hd in range(IDX_HEADS)], axis=0)

    def head_sum(x):
        acc = x[0:ts]
        for hd in range(1, IDX_HEADS):
            acc = acc + x[hd * ts:(hd + 1) * ts]
        return acc

    wait_pages(0)

    def past_scores(c, _):
        k0 = pl.multiple_of(c * KEY_CHUNK, KEY_CHUNK)
        kc = kibuf[pl.ds(k0, KEY_CHUNK), :].astype(BF16)
        acc = head_sum(jnp.maximum(_dot_nt(qi_stack, kc), 0.0) * w_stack)
        s_ref[:, pl.ds(k0, KEY_CHUNK)] = _score_keys(acc + 0.0)
        return 0

    lax.fori_loop(0, n_past_chunks, past_scores, 0)
    kin = kiwn_ref[0][:, 0:IDX_DIM]
    acc = head_sum(jnp.maximum(_dot_nt(qi_stack, kin), 0.0) * w_stack)
    kpos = lax.broadcasted_iota(I32, (ts, KEY_CHUNK), 1)
    s_ref[:, pl.ds(past, KEY_CHUNK)] = jnp.where(kpos <= qpos, _score_keys(acc + 0.0), INT_MIN)
    _select_to_bias(s_ref, n_chunks, topk)

    wait_pages(1)
    wait_pages(2)

    def load_past(k0):
        return kbuf[pl.ds(k0, KEY_CHUNK), :].astype(BF16), vbuf[pl.ds(k0, KEY_CHUNK), :].astype(BF16)

    def load_new(k0):
        return kn_ref[0], vn_ref[0]

    segments = [(0, n_past_chunks, load_past), (n_past_chunks, n_chunks, load_new)]
    _attend_all_heads(q_ref[0], segments, s_ref, o_ref)


def _dsa_sample(q, qi, kiw, kn, vn, kiwn, cache_k, cache_v, cache_ki, page_table):
    b, ts, _ = q.shape
    n_pages = page_table.shape[1]
    past = n_pages * PAGE_SIZE
    topk = min(TOPK_MAX, (past + ts) // 4)
    blk = lambda rows, width: pl.BlockSpec((1, rows, width), lambda bi, pt: (bi, 0, 0))
    hbm = pl.BlockSpec(memory_space=pl.ANY)
    grid_spec = pltpu.PrefetchScalarGridSpec(
        num_scalar_prefetch=1,
        grid=(b,),
        in_specs=[blk(ts, ATT_HEADS * LANES), blk(ts, IDX_HEADS * LANES), blk(ts, LANES),
                  blk(KEY_CHUNK, LANES), blk(KEY_CHUNK, LANES), blk(KEY_CHUNK, LANES), hbm, hbm, hbm],
        out_specs=blk(ts, ATT_HEADS * LANES),
        scratch_shapes=[pltpu.VMEM((past, LANES), F32), pltpu.VMEM((past, LANES), F32),
                        pltpu.VMEM((past, IDX_DIM), F32), pltpu.VMEM((ts, past + KEY_CHUNK), I32),
                        pltpu.SemaphoreType.DMA((3,))],
    )
    return pl.pallas_call(
        functools.partial(_dsa_sample_kernel, ts=ts, n_pages=n_pages, topk=topk),
        grid_spec=grid_spec,
        out_shape=jax.ShapeDtypeStruct((b, ts, ATT_HEADS * LANES), F32),
        compiler_params=_cparams(("arbitrary",)),
        name="dsa_sample",
    )(page_table, q, qi, kiw, kn, vn, kiwn, cache_k, cache_v, cache_ki)


def _conv_kernel(u_ref, halo_ref, buf_ref, bg_ref, w_ref, ya_ref):
    i = pl.program_id(1)
    u = u_ref[0]
    halo, buf = halo_ref[0], buf_ref[0]
    first = i == 0
    prev1 = jnp.where(first, buf[1:2], halo[7:8])
    prev2 = jnp.where(first, buf[0:1], halo[6:7])
    row = lax.broadcasted_iota(I32, u.shape, 0)
    um1 = jnp.where(row == 0, prev1, pltpu.roll(u, 1, 0))
    um2 = jnp.where(row == 0, prev2, jnp.where(row == 1, prev1, pltpu.roll(u, 2, 0)))
    w = w_ref[...]
    conv = w[0:1] * um2 + w[1:2] * um1 + w[2:3] * u
    ya_ref[0] = (bg_ref[0] * conv).astype(BF16)


def _conv(u, bg, buf, w, tt):
    b, t, c = u.shape
    halo_rows = 8
    tile = pl.BlockSpec((1, tt, c), lambda bi, i: (bi, i, 0))
    halo = pl.BlockSpec((1, halo_rows, c), lambda bi, i: (bi, jnp.maximum(i * (tt // halo_rows) - 1, 0), 0))
    return pl.pallas_call(
        _conv_kernel,
        grid=(b, t // tt),
        in_specs=[tile, halo, pl.BlockSpec((1, CONV_WIDTH - 1, c), lambda bi, i: (bi, 0, 0)), tile,
                  pl.BlockSpec((CONV_WIDTH, c), lambda bi, i: (0, 0))],
        out_specs=tile,
        out_shape=jax.ShapeDtypeStruct((b, t, c), BF16),
        compiler_params=_cparams(("parallel", "parallel")),
        name="short_conv",
    )(u, u, buf, bg, w)


def _mm_res_kernel(*refs, n_in):
    a_refs, w_ref, x_ref, o_ref = refs[:n_in], refs[n_in], refs[n_in + 1], refs[n_in + 2]
    a = jnp.concatenate([r[...] for r in a_refs], axis=1) if n_in > 1 else a_refs[0][...]
    o_ref[...] = x_ref[...] + _dot(a, w_ref[...])


def _mm_res(a_list, w, x, tm):
    m = x.shape[0]
    row = lambda i: (i, 0)
    return pl.pallas_call(
        functools.partial(_mm_res_kernel, n_in=len(a_list)),
        grid=(m // tm,),
        in_specs=[pl.BlockSpec((tm, a.shape[1]), row) for a in a_list]
        + [pl.BlockSpec(w.shape, lambda i: (0, 0)), pl.BlockSpec((tm, D_MODEL), row)],
        out_specs=pl.BlockSpec((tm, D_MODEL), row),
        out_shape=jax.ShapeDtypeStruct((m, D_MODEL), F32),
        compiler_params=_cparams(("parallel",)),
        name="matmul_residual",
    )(*a_list, w, x)


def _norm_mm_kernel(x_ref, g_ref, w_ref, o_ref, h_scr):
    @pl.when(pl.program_id(1) == 0)
    def _():
        h_scr[...] = _rms(x_ref[...], g_ref[...]).astype(BF16)

    o_ref[...] = _dot(h_scr[...], w_ref[...])


def _norm_mm(x, g, w, tm, tn):
    m, n = x.shape[0], w.shape[1]
    return pl.pallas_call(
        _norm_mm_kernel,
        grid=(m // tm, n // tn),
        in_specs=[pl.BlockSpec((tm, D_MODEL), lambda i, j: (i, 0)), pl.BlockSpec((1, D_MODEL), lambda i, j: (0, 0)),
                  pl.BlockSpec((D_MODEL, tn), lambda i, j: (0, j))],
        out_specs=pl.BlockSpec((tm, tn), lambda i, j: (i, j)),
        out_shape=jax.ShapeDtypeStruct((m, n), F32),
        scratch_shapes=[pltpu.VMEM((tm, D_MODEL), BF16)],
        compiler_params=_cparams(("parallel", "arbitrary")),
        name="norm_matmul",
    )(x, g, w)


def _ffn_kernel(x_ref, g_ref, wg_ref, wu_ref, wd_ref, o_ref, h_scr, acc):
    f = pl.program_id(1)

    @pl.when(f == 0)
    def _():
        x = x_ref[...]
        h_scr[...] = _rms(x, g_ref[...]).astype(BF16)
        acc[...] = x

    h = h_scr[...]
    a = jax.nn.silu(_dot(h, wg_ref[...])) * _dot(h, wu_ref[...])
    acc[...] += _dot(a.astype(BF16), wd_ref[...])

    @pl.when(f == pl.num_programs(1) - 1)
    def _():
        o_ref[...] = acc[...]


def _ffn(x, g, wg, wu, wd, tm, tf):
    m, ff = x.shape[0], wg.shape[1]
    row = lambda i, f: (i, 0)
    return pl.pallas_call(
        _ffn_kernel,
        grid=(m // tm, ff // tf),
        in_specs=[pl.BlockSpec((tm, D_MODEL), row), pl.BlockSpec((1, D_MODEL), lambda i, f: (0, 0)),
                  pl.BlockSpec((D_MODEL, tf), lambda i, f: (0, f)), pl.BlockSpec((D_MODEL, tf), lambda i, f: (0, f)),
                  pl.BlockSpec((tf, D_MODEL), lambda i, f: (f, 0))],
        out_specs=pl.BlockSpec((tm, D_MODEL), row),
        out_shape=jax.ShapeDtypeStruct((m, D_MODEL), F32),
        scratch_shapes=[pltpu.VMEM((tm, D_MODEL), BF16), pltpu.VMEM((tm, D_MODEL), F32)],
        compiler_params=_cparams(("parallel", "arbitrary")),
        name="dense_swiglu",
    )(x, g, wg, wu, wd)


def _top2_gates(logits):
    lane = lax.broadcasted_iota(I32, logits.shape, 1).astype(F32)
    neg = jnp.float32(-jnp.inf)
    l1 = jnp.where(lane < N_EXPERTS, logits, neg)
    m1 = jnp.max(l1, axis=1, keepdims=True)
    i1 = jnp.min(jnp.where(l1 == m1, lane, float(LANES)), axis=1, keepdims=True)
    l2 = jnp.where(lane == i1, neg, l1)
    m2 = jnp.max(l2, axis=1, keepdims=True)
    i2 = jnp.min(jnp.where(l2 == m2, lane, float(LANES)), axis=1, keepdims=True)
    e = jnp.exp(m2 - m1)
    w1 = 1.0 / (1.0 + e)
    w2 = e / (1.0 + e)
    first, second = lane == i1, lane == i2
    return jnp.where(first, w1, jnp.where(second, w2, 0.0)), jnp.where(first | second, 1.0, 0.0)


MOE_SUB = 128
MOE_TOKEN_TILE = 1024


def _moe_route_kernel(x_ref, g_ref, rhi_ref, rlo_ref, h_ref, gate_ref, posc_ref, posr_ref, cnt_ref):
    tm = x_ref.shape[0]
    hn = _rms(x_ref[...], g_ref[...])
    h_hi = hn.astype(BF16)
    h_lo = (hn - h_hi.astype(F32)).astype(BF16)
    logits = _dot(h_hi, rhi_ref[...]) + (_dot(h_lo, rhi_ref[...]) + _dot(h_hi, rlo_ref[...]))
    gate, routed = _top2_gates(logits)
    h_ref[...] = h_hi
    gate_ref[...] = gate
    earlier = (lax.broadcasted_iota(I32, (tm, tm), 0) > lax.broadcasted_iota(I32, (tm, tm), 1))
    slot = _dot(jnp.where(earlier, 1.0, 0.0).astype(BF16), routed.astype(BF16))
    posc = jnp.where(routed > 0.0, slot, -1.0)
    posc_ref[...] = posc
    posr_ref[0] = posc.T[0:N_EXPERTS]
    cnt_ref[0] = jnp.broadcast_to(jnp.sum(routed, axis=0, keepdims=True), (8, LANES))


def _moe_expert_kernel(cnt_ref, x_ref, h_ref, gate_ref, posc_ref, posr_ref, wg_ref, wu_ref, wd_ref, o_ref, xg, yacc):
    i, e, f = pl.program_id(0), pl.program_id(1), pl.program_id(2)
    tm = x_ref.shape[0]
    n_sub = (cnt_ref[i * N_EXPERTS + e] + MOE_SUB - 1) // MOE_SUB

    @pl.when((e == 0) & (f == 0))
    def _():
        o_ref[...] = x_ref[...]

    @pl.when(f == 0)
    def _():
        posr = posr_ref[0]
        h = h_ref[...]

        def gather(s, _):
            base = pl.multiple_of(s * MOE_SUB, MOE_SUB)
            slot = (base + lax.broadcasted_iota(I32, (MOE_SUB, tm), 0)).astype(F32)
            onehot = jnp.where(posr == slot, 1.0, 0.0).astype(BF16)
            xg[pl.ds(base, MOE_SUB), :] = _dot(onehot, h).astype(BF16)
            yacc[pl.ds(base, MOE_SUB), :] = jnp.zeros((MOE_SUB, D_MODEL), F32)
            return 0

        lax.fori_loop(0, n_sub, gather, 0)

    def expert(s, _):
        rows = pl.ds(pl.multiple_of(s * MOE_SUB, MOE_SUB), MOE_SUB)
        xs = xg[rows, :]
        a = jax.nn.silu(_dot(xs, wg_ref[0])) * _dot(xs, wu_ref[0])
        yacc[rows, :] += _dot(a.astype(BF16), wd_ref[0])
        return 0

    lax.fori_loop(0, n_sub, expert, 0)

    @pl.when(f == pl.num_programs(2) - 1)
    def _():
        lane = lax.broadcasted_iota(I32, (tm, LANES), 1)
        mine = lane == e
        posc = jnp.sum(jnp.where(mine, posc_ref[...], 0.0), axis=1, keepdims=True)
        gate = jnp.sum(jnp.where(mine, gate_ref[...], 0.0), axis=1, keepdims=True)

        def scatter(s, _):
            base = pl.multiple_of(s * MOE_SUB, MOE_SUB)
            slot = (base + lax.broadcasted_iota(I32, (tm, MOE_SUB), 1)).astype(F32)
            onehot = jnp.where(posc == slot, 1.0, 0.0).astype(BF16)
            o_ref[...] += gate * _dot(onehot, yacc[pl.ds(base, MOE_SUB), :].astype(BF16))
            return 0

        lax.fori_loop(0, n_sub, scatter, 0)


def _moe(x, g, r_hi, r_lo, wg, wu, wd, tm, tf):
    m = x.shape[0]
    nt = m // tm
    row = lambda i: (i, 0)
    const = lambda i: (0, 0)
    h, gate, posc, posr, cnt = pl.pallas_call(
        _moe_route_kernel,
        grid=(nt,),
        in_specs=[pl.BlockSpec((tm, D_MODEL), row), pl.BlockSpec((1, D_MODEL), const),
                  pl.BlockSpec((D_MODEL, LANES), const), pl.BlockSpec((D_MODEL, LANES), const)],
        out_specs=[pl.BlockSpec((tm, D_MODEL), row), pl.BlockSpec((tm, LANES), row), pl.BlockSpec((tm, LANES), row),
                   pl.BlockSpec((1, N_EXPERTS, tm), lambda i: (i, 0, 0)), pl.BlockSpec((1, 8, LANES), lambda i: (i, 0, 0))],
        out_shape=[jax.ShapeDtypeStruct((m, D_MODEL), BF16), jax.ShapeDtypeStruct((m, LANES), F32),
                   jax.ShapeDtypeStruct((m, LANES), F32), jax.ShapeDtypeStruct((nt, N_EXPERTS, tm), F32),
                   jax.ShapeDtypeStruct((nt, 8, LANES), F32)],
        compiler_params=_cparams(("parallel",)),
        name="moe_route",
    )(x, g, r_hi, r_lo)
    counts = cnt[:, 0, :N_EXPERTS].astype(I32).reshape(nt * N_EXPERTS)
    posr = posr.reshape(nt * N_EXPERTS, 1, tm)
    row3 = lambda i, e, f, c: (i, 0)
    slot_rows = pl.cdiv(tm, MOE_SUB) * MOE_SUB
    grid_spec = pltpu.PrefetchScalarGridSpec(
        num_scalar_prefetch=1,
        grid=(nt, N_EXPERTS, EXPERT_FF // tf),
        in_specs=[pl.BlockSpec((tm, D_MODEL), row3), pl.BlockSpec((tm, D_MODEL), row3),
                  pl.BlockSpec((tm, LANES), row3), pl.BlockSpec((tm, LANES), row3),
                  pl.BlockSpec((1, 1, tm), lambda i, e, f, c: (i * N_EXPERTS + e, 0, 0)),
                  pl.BlockSpec((1, D_MODEL, tf), lambda i, e, f, c: (e, 0, f)),
                  pl.BlockSpec((1, D_MODEL, tf), lambda i, e, f, c: (e, 0, f)),
                  pl.BlockSpec((1, tf, D_MODEL), lambda i, e, f, c: (e, f, 0))],
        out_specs=pl.BlockSpec((tm, D_MODEL), row3),
        scratch_shapes=[pltpu.VMEM((slot_rows, D_MODEL), BF16), pltpu.VMEM((slot_rows, D_MODEL), F32)],
    )
    return pl.pallas_call(
        _moe_expert_kernel,
        grid_spec=grid_spec,
        out_shape=jax.ShapeDtypeStruct((m, D_MODEL), F32),
        compiler_params=_cparams(("parallel", "arbitrary", "arbitrary")),
        name="moe_experts",
    )(counts, x, h, gate, posc, posr, wg, wu, wd)


def _ple_kernel(x_ref, g_ref, p_ref, wp_ref, wgate_ref, gf_ref, o_ref, *, final_norm):
    x = x_ref[...]
    hp = _rms(x, g_ref[...]).astype(BF16)
    gate = jax.nn.sigmoid(_dot(hp, wgate_ref[...]))
    y = x + _dot(p_ref[...].astype(BF16), wp_ref[...]) * gate
    if final_norm:
        y = _rms(y, gf_ref[...])
    o_ref[...] = y


def _ple(x, g, p, wp, wgate, g_final, final_norm, tm):
    m = x.shape[0]
    row = lambda i: (i, 0)
    const = lambda i: (0, 0)
    return pl.pallas_call(
        functools.partial(_ple_kernel, final_norm=final_norm),
        grid=(m // tm,),
        in_specs=[pl.BlockSpec((tm, D_MODEL), row), pl.BlockSpec((1, D_MODEL), const), pl.BlockSpec((tm, PLE_DIM), row),
                  pl.BlockSpec((PLE_DIM, D_MODEL), const), pl.BlockSpec((D_MODEL, D_MODEL), const),
                  pl.BlockSpec((1, D_MODEL), const)],
        out_specs=pl.BlockSpec((tm, D_MODEL), row),
        out_shape=jax.ShapeDtypeStruct((m, D_MODEL), F32),
        compiler_params=_cparams(("parallel",)),
        name="per_layer_embedding",
    )(x, g, p, wp, wgate, g_final)


def _ret_kernel(q_ref, k_ref, v_ref, gate_ref, cos_ref, sin_ref, s0_ref, o_ref, sout_ref, state, *, chunk, chunk_rows):
    i = pl.program_id(1)

    @pl.when(i == 0)
    def _():
        state[...] = s0_ref[0]

    cos, sin = cos_ref[...], sin_ref[...]
    tt = q_ref.shape[1]
    r = chunk_rows
    ii = lax.broadcasted_iota(I32, (r, r), 0).astype(F32)
    jj = lax.broadcasted_iota(I32, (r, r), 1).astype(F32)
    rel = ii - jj
    icol = lax.broadcasted_iota(I32, (r, 1), 0).astype(F32)
    half = RET_DK // 2

    def rot(ref, hd):
        x1 = ref[0, :, hd * RET_DK:hd * RET_DK + half]
        x2 = ref[0, :, hd * RET_DK + half:(hd + 1) * RET_DK]
        return jnp.concatenate([x1 * cos - x2 * sin, x2 * cos + x1 * sin], axis=1)

    for hd in range(RET_HEADS):
        lg = math.log(1.0 - 2.0 ** (-5.0 - hd))
        d_in = jnp.where(rel >= 0, jnp.exp(lg * jnp.maximum(rel, 0.0)), 0.0)
        d_q = jnp.exp(lg * (icol + 1.0))
        d_k = jnp.exp(lg * (chunk - 1.0 - icol)) * (RET_DK ** -0.5)
        d_c = math.exp(lg * chunk)
        qr = rot(q_ref, hd)
        kr = rot(k_ref, hd)
        vsl = slice(hd * RET_DV, (hd + 1) * RET_DV)
        for c in range(tt // r):
            rows = slice(c * r, (c + 1) * r)
            qc = qr[rows].astype(BF16)
            kc = kr[rows]
            vc = v_ref[0, rows, vsl].astype(BF16)
            s_prev = state[hd]
            att = _dot_nt(qc, (kc * (RET_DK ** -0.5)).astype(BF16)) * d_in
            o = _dot(att.astype(BF16), vc) + _dot(qc, s_prev.astype(BF16)) * d_q
            state[hd] = s_prev * d_c + _dot_tn((kc * d_k).astype(BF16), vc)
            mu = jnp.mean(o, axis=-1, keepdims=True)
            var = jnp.mean(jnp.square(o - mu), axis=-1, keepdims=True)
            on = (o - mu) * lax.rsqrt(var + EPS)
            o_ref[0, rows, vsl] = (jax.nn.silu(gate_ref[0, rows, vsl]) * on).astype(BF16)

    @pl.when(i == pl.num_programs(1) - 1)
    def _():
        sout_ref[0] = state[...]


def _retention(z, s0, cos, sin, tt, chunk, chunk_rows):
    b, t, _ = z.shape
    hk, hv = RET_HEADS * RET_DK, RET_HEADS * RET_DV
    half = RET_DK // 2
    state_spec = pl.BlockSpec((1, RET_HEADS, RET_DK, RET_DV), lambda bi, i: (bi, 0, 0, 0))
    tab = pl.BlockSpec((tt, half), lambda bi, i: (i, 0))
    return pl.pallas_call(
        functools.partial(_ret_kernel, chunk=chunk, chunk_rows=chunk_rows),
        grid=(b, t // tt),
        in_specs=[pl.BlockSpec((1, tt, hk), lambda bi, i: (bi, i, 0)), pl.BlockSpec((1, tt, hk), lambda bi, i: (bi, i, 1)),
                  pl.BlockSpec((1, tt, hv), lambda bi, i: (bi, i, 1)), pl.BlockSpec((1, tt, hv), lambda bi, i: (bi, i, 2)),
                  tab, tab, state_spec],
        out_specs=[pl.BlockSpec((1, tt, hv), lambda bi, i: (bi, i, 0)), state_spec],
        out_shape=[jax.ShapeDtypeStruct((b, t, hv), BF16), jax.ShapeDtypeStruct(s0.shape, F32)],
        scratch_shapes=[pltpu.VMEM((RET_HEADS, RET_DK, RET_DV), F32)],
        compiler_params=_cparams(("parallel", "arbitrary")),
        name="retention",
    )(z, z, z, z, cos, sin, s0)


def _pack_l0_w_in(w):
    offs = np.cumsum((0,) + AB_SPLITS)
    bg, cg, hv, q, k, v, qi, ki, wi = [w[:, offs[n]:offs[n + 1]] for n in range(len(AB_SPLITS))]
    group = ATT_HEADS // KV_HEADS
    q4 = q.reshape(D_MODEL, ATT_HEADS, HEAD_DIM)
    zq = jnp.zeros_like(q4)
    q_pad = jnp.concatenate([jnp.concatenate([q4[:, :group], zq[:, :group]], axis=-1),
                             jnp.concatenate([zq[:, group:], q4[:, group:]], axis=-1)], axis=1).reshape(D_MODEL, -1)
    qi4 = qi.reshape(D_MODEL, IDX_HEADS, IDX_DIM)
    qi_pad = jnp.concatenate([qi4, jnp.zeros_like(qi4)], axis=-1).reshape(D_MODEL, -1)
    kiw = jnp.concatenate([ki, wi, jnp.zeros((D_MODEL, LANES - IDX_DIM - IDX_HEADS), w.dtype)], axis=1)
    return jnp.concatenate([bg, cg, hv, q_pad, k, v, qi_pad, kiw], axis=1).astype(BF16)


def _pack_l0_w_out(w):
    group = ATT_HEADS // KV_HEADS
    wa, wb = w[:CONV_CH], w[CONV_CH:].reshape(ATT_HEADS, HEAD_DIM, D_MODEL)
    zb = jnp.zeros_like(wb)
    wb_pad = jnp.concatenate([jnp.concatenate([wb[:group], zb[:group]], axis=1),
                              jnp.concatenate([zb[group:], wb[group:]], axis=1)], axis=0).reshape(-1, D_MODEL)
    return jnp.concatenate([wa, wb_pad], axis=0).astype(BF16)


def _rope_tables(pos, reps):
    inv = ROPE_THETA ** (-jnp.arange(0, HEAD_DIM, 2, dtype=F32) / HEAD_DIM)
    ang = pos.astype(F32)[:, None] * inv[None, :]
    cos, sin = jnp.cos(ang), jnp.sin(ang)
    cos64 = jnp.concatenate([cos, cos], axis=1)
    sin64 = jnp.concatenate([-sin, sin], axis=1)
    one, zero = jnp.ones_like(cos64), jnp.zeros_like(cos64)
    tabs = (jnp.concatenate([cos64, cos64], 1), jnp.concatenate([sin64, sin64], 1),
            jnp.concatenate([cos64, one], 1), jnp.concatenate([sin64, zero], 1))
    return tuple(jnp.tile(tb, (reps, 1)) for tb in tabs)


def _ret_tables(pos):
    inv = ROPE_THETA ** (-jnp.linspace(0.0, 1.0, RET_DK // 2, dtype=F32))
    ang = pos.astype(F32)[:, None] * inv[None, :]
    return jnp.cos(ang), jnp.sin(ang)


def _pack_params(prm):
    r = prm['moe_router'][0]
    r_pad = jnp.concatenate([r, jnp.zeros((D_MODEL, LANES - N_EXPERTS), F32)], axis=1)
    r_hi = r_pad.astype(BF16)
    bf = lambda a: a.astype(BF16)
    return dict(
        l0_w_in=_pack_l0_w_in(prm['ab_w_in'][0]), l0_w_out=_pack_l0_w_out(prm['ab_w_out'][0]),
        conv_w=prm['ab_conv_w'][0],
        ffn=(bf(prm['ffn_w_gate'][0]), bf(prm['ffn_w_up'][0]), bf(prm['ffn_w_down'][0])),
        ret_w_in=bf(prm['ret_w_in'][0]), ret_w_out=bf(prm['ret_w_out'][0]),
        r_hi=r_hi, r_lo=(r_pad - r_hi.astype(F32)).astype(BF16),
        moe=(bf(prm['moe_w_gate'][0]), bf(prm['moe_w_up'][0]), bf(prm['moe_w_down'][0])),
        ple_w=bf(prm['ple_w']), ple_gate_w=bf(prm['ple_gate_w']),
        norm_mix=prm['norm_mix'][:, None, :], norm_ffn=prm['norm_ffn'][:, None, :],
        norm_ple=prm['norm_ple'][:, None, :], norm_final=prm['norm_final'][None, :],
    )


def _trunk(x, p, pos, conv_buf, ret_state, pk, paged):
    b, t, _ = x.shape
    m = b * t
    tm = min(m, TOKEN_TILE)
    x2 = x.reshape(m, D_MODEL)

    reps = max(1, tm // t)
    tabs = _rope_tables(pos, reps)
    bg, u, q, k, v, qi, kiw, kbf, vbf, kiwbf = _l0_in_proj(x2, pk['norm_mix'][0], pk['l0_w_in'], tabs, tm,
                                                          tabs[0].shape[0] // tm)
    seq = lambda a: a.reshape(b, t, a.shape[-1])
    if paged is None:
        yb = _dsa_prompt(seq(q), seq(qi), seq(kiw), seq(kbf), seq(vbf), seq(kiwbf), qb=min(t, 128))
    else:
        cache_k, cache_v, cache_ki, page_table = paged
        pad = lambda a: jnp.pad(seq(a), ((0, 0), (0, KEY_CHUNK - t), (0, 0)))
        n_pool = cache_k.shape[0]
        yb = _dsa_sample(seq(q).astype(F32), seq(qi).astype(F32), seq(kiw), pad(kbf), pad(vbf), pad(kiwbf),
                         cache_k.reshape(n_pool, PAGE_SIZE, LANES), cache_v.reshape(n_pool, PAGE_SIZE, LANES),
                         cache_ki, page_table).astype(BF16)
    u3 = seq(u)
    ya = _conv(u3, seq(bg), conv_buf, pk['conv_w'], tt=min(t, TOKEN_TILE))
    x2 = _mm_res([ya.reshape(m, CONV_CH), yb.reshape(m, ATT_HEADS * LANES)], pk['l0_w_out'], x2, tm)
    x2 = _ffn(x2, pk['norm_ffn'][0], *pk['ffn'], tm=tm, tf=D_FF // 2)
    x2 = _ple(x2, pk['norm_ple'][0], p[0].reshape(m, PLE_DIM), pk['ple_w'][0], pk['ple_gate_w'][0],
              pk['norm_final'], False, tm)
    new_k = k.reshape(1, b, t, KV_HEADS, HEAD_DIM)
    new_v = v.reshape(1, b, t, KV_HEADS, HEAD_DIM)
    new_ki = seq(kiw)[None, :, :, :IDX_DIM]
    new_conv = jnp.concatenate([conv_buf, u3], axis=1)[None, :, -(CONV_WIDTH - 1):]

    z = _norm_mm(x2, pk['norm_mix'][1], pk['ret_w_in'], tm, 1024).reshape(b, t, -1)
    cos_r, sin_r = _ret_tables(pos)
    if t % RET_CHUNK == 0:
        og, s_new = _retention(z, ret_state, cos_r, sin_r, tt=2 * RET_CHUNK, chunk=RET_CHUNK, chunk_rows=RET_CHUNK)
    else:
        rows = 16
        padt = lambda a: jnp.pad(a, ((0, 0),) * (a.ndim - 2) + ((0, rows - t), (0, 0)))
        og, s_new = _retention(padt(z), ret_state, padt(cos_r), padt(sin_r), tt=rows, chunk=t, chunk_rows=rows)
        og = og[:, :t]
    x2 = _mm_res([og.reshape(m, RET_HEADS * RET_DV)], pk['ret_w_out'], x2, tm)
    x2 = _moe(x2, pk['norm_ffn'][1], pk['r_hi'], pk['r_lo'], *pk['moe'], tm=min(m, MOE_TOKEN_TILE), tf=EXPERT_FF // 4)
    x2 = _ple(x2, pk['norm_ple'][1], p[1].reshape(m, PLE_DIM), pk['ple_w'][1], pk['ple_gate_w'][1],
              pk['norm_final'], True, tm)
    return x2.reshape(b, t, D_MODEL), new_k, new_v, new_ki, new_conv, s_new[None]


def kernel(x_prompt, x_sample, cache_k, cache_v, cache_kidx, state_conv, state_ret, page_table, p_prompt, p_sample,
           norm_mix, norm_ffn, norm_ple, norm_final, ab_w_in, ab_conv_w, ab_w_out, ffn_w_gate, ffn_w_up, ffn_w_down,
           ret_w_in, ret_w_out, moe_router, moe_w_gate, moe_w_up, moe_w_down, ple_w, ple_gate_w):
    prm = dict(norm_mix=norm_mix, norm_ffn=norm_ffn, norm_ple=norm_ple, norm_final=norm_final, ab_w_in=ab_w_in,
               ab_conv_w=ab_conv_w, ab_w_out=ab_w_out, ffn_w_gate=ffn_w_gate, ffn_w_up=ffn_w_up, ffn_w_down=ffn_w_down,
               ret_w_in=ret_w_in, ret_w_out=ret_w_out, moe_router=moe_router, moe_w_gate=moe_w_gate,
               moe_w_up=moe_w_up, moe_w_down=moe_w_down, ple_w=ple_w, ple_gate_w=ple_gate_w)
    pk = _pack_params(prm)
    b, t = x_prompt.shape[0], x_prompt.shape[1]
    db, ts = x_sample.shape[0], x_sample.shape[1]
    past_len = page_table.shape[1] * PAGE_SIZE
    dt = x_prompt.dtype

    conv0 = jnp.zeros((b, CONV_WIDTH - 1, CONV_CH), dt)
    ret0 = jnp.zeros((b, RET_HEADS, RET_DK, RET_DV), dt)
    y_p, k_p, v_p, ki_p, cb_p, rs_p = _trunk(x_prompt, p_prompt, jnp.arange(t, dtype=I32), conv0, ret0, pk, None)

    pos_s = past_len + jnp.arange(ts, dtype=I32)
    paged = (cache_k[0], cache_v[0], cache_kidx[0], page_table)
    y_s, k_s, v_s, ki_s, cb_s, rs_s = _trunk(x_sample, p_sample, pos_s, state_conv[0], state_ret[0], pk, paged)
    return (y_p, y_s, k_p, v_p, ki_p, cb_p, rs_p, k_s, v_s, ki_s, cb_s, rs_s)
```

```python
import functools
import math

import jax
import jax.numpy as jnp
import numpy as np
from jax import lax
from jax.experimental import pallas as pl
from jax.experimental.pallas import tpu as pltpu

F32 = jnp.float32
BF16 = jnp.bfloat16
I32 = jnp.int32

D_MODEL = 1024
PAGE_SIZE = 128
CONV_CH = D_MODEL // 2
CONV_WIDTH = 3
ATT_HEADS = 8
KV_HEADS = 2
HEAD_DIM = 64
IDX_HEADS = 4
IDX_DIM = 64
TOPK_MAX = 256
ROPE_THETA = 10000.0
RET_HEADS = 4
RET_DK = D_MODEL // RET_HEADS
RET_DV = 2 * RET_DK
RET_CHUNK = 128
D_FF = 2816
N_EXPERTS = 8
EXPERT_FF = 3584
PLE_DIM = 256
EPS = 1e-6
AB_SPLITS = (CONV_CH, CONV_CH, CONV_CH, ATT_HEADS * HEAD_DIM, KV_HEADS * HEAD_DIM, KV_HEADS * HEAD_DIM,
             IDX_HEADS * IDX_DIM, IDX_DIM, IDX_HEADS)

LANES = 128
VMEM_LIMIT = 48 * 1024 * 1024
INT_MIN = -2 ** 31
NEG_BIG = -1e30

C_BG, C_CG, C_HV, C_Q, C_K, C_V, C_QI, C_KIW, C_END = 0, 512, 1024, 1536, 2560, 2688, 2816, 3328, 3456
KEY_CHUNK = 512
TOKEN_TILE = 512


def _cparams(sem):
    return pltpu.CompilerParams(dimension_semantics=sem, vmem_limit_bytes=VMEM_LIMIT)


def _rms(x, g):
    return x * lax.rsqrt(jnp.mean(x * x, axis=-1, keepdims=True) + EPS) * g


def _dot(a, b):
    return jnp.dot(a, b, preferred_element_type=F32)


def _dot_nt(a, b):
    return lax.dot_general(a, b, (((1,), (1,)), ((), ())), preferred_element_type=F32)


def _dot_tn(a, b):
    return lax.dot_general(a, b, (((0,), (0,)), ((), ())), preferred_element_type=F32)


def _swap_halves64(x):
    lane = lax.broadcasted_iota(I32, x.shape, 1)
    from_above = pltpu.roll(x, LANES - 32, 1)
    from_below = pltpu.roll(x, 32, 1)
    return jnp.where((lane & 63) < 32, from_above, from_below)


def _l0_in_kernel(x_ref, g_ref, w_ref, wvt_ref, cos_ref, sin_ref, cosb_ref, sinb_ref,
                  bg_ref, u_ref, q_ref, k_ref, v_ref, qi_ref, kiw_ref, kbf_ref, vbf_ref, kiwbf_ref, vt_ref):
    h = _rms(x_ref[...], g_ref[...]).astype(BF16)
    cos, sin = cos_ref[...], sin_ref[...]

    def rot(z, c, s):
        return z * c + _swap_halves64(z) * s

    bg_ref[...] = _dot(h, w_ref[:, C_BG:C_CG])
    u_ref[...] = _dot(h, w_ref[:, C_CG:C_HV]) * _dot(h, w_ref[:, C_HV:C_Q])
    zq = _dot(h, w_ref[:, C_Q:C_K])
    for hd in range(ATT_HEADS):
        sl = slice(hd * LANES, (hd + 1) * LANES)
        q_ref[:, sl] = (rot(zq[:, sl], cos, sin) * (HEAD_DIM ** -0.5)).astype(BF16)
    k = rot(_dot(h, w_ref[:, C_K:C_V]), cos, sin)
    k_ref[...] = k
    kbf_ref[...] = k.astype(BF16)
    v = _dot(h, w_ref[:, C_V:C_QI])
    v_ref[...] = v
    vbf_ref[...] = v.astype(BF16)
    vt_ref[...] = _dot_nt(wvt_ref[...], h).astype(BF16)
    zqi = _dot(h, w_ref[:, C_QI:C_KIW])
    for hd in range(IDX_HEADS):
        sl = slice(hd * LANES, (hd + 1) * LANES)
        qi_ref[:, sl] = rot(zqi[:, sl], cos, sin).astype(BF16)
    kiw = rot(_dot(h, w_ref[:, C_KIW:C_END]), cosb_ref[...], sinb_ref[...])
    kiw_ref[...] = kiw
    kiwbf_ref[...] = kiw.astype(BF16)


def _l0_in_proj(x2, g, w, wvt, tabs, tm, n_tab_blocks):
    m = x2.shape[0]
    row = lambda i: (i, 0)
    const = lambda i: (0, 0)
    tab = lambda i: (i % n_tab_blocks, 0)
    widths = (512, 512, 1024, 128, 128, 512, 128, 128, 128, 128)
    dtypes = (F32, F32, BF16, F32, F32, BF16, F32, BF16, BF16, BF16)
    return pl.pallas_call(
        _l0_in_kernel,
        grid=(m // tm,),
        in_specs=[pl.BlockSpec((tm, D_MODEL), row), pl.BlockSpec((1, D_MODEL), const),
                  pl.BlockSpec((D_MODEL, C_END), const), pl.BlockSpec((LANES, D_MODEL), const)]
        + [pl.BlockSpec((tm, LANES), tab)] * 4,
        out_specs=[pl.BlockSpec((tm, wd), row) for wd in widths] + [pl.BlockSpec((LANES, tm), lambda i: (0, i))],
        out_shape=[jax.ShapeDtypeStruct((m, wd), dt) for wd, dt in zip(widths, dtypes)]
        + [jax.ShapeDtypeStruct((LANES, m), BF16)],
        compiler_params=_cparams(("parallel",)),
        name="l0_in_proj",
    )(x2, g, w, wvt, *tabs)


def _score_keys(score):
    bits = pltpu.bitcast(score, I32)
    return bits ^ ((bits >> 31) & jnp.int32(0x7FFFFFFF))


def _key_scores(key):
    return pltpu.bitcast(key ^ ((key >> 31) & jnp.int32(0x7FFFFFFF)), F32)


SEARCH_INTERP_STEPS = 14
SEARCH_MAX_STEPS = SEARCH_INTERP_STEPS + 34


def _topk_threshold(count_ge, count_tie_below, vmin, vmax, n_real, topk, n_index_bits):
    k = float(topk)
    lo0, hi0 = _score_keys(vmin), _score_keys(vmax) + 1
    take_all = n_real <= k
    zero, one = jnp.zeros_like(vmin), jnp.ones_like(vmin)
    active0 = jnp.where(jnp.logical_not(take_all) & (lo0 + 1 < hi0), 1.0, 0.0)

    def cond(st):
        return (st[0] < SEARCH_MAX_STEPS) & (jnp.max(st[-1]) > 0.0)

    def body(st):
        it, lo, hi, c_lo, c_hi, w_lo, w_hi, last, active = st
        act = active > 0.0
        f_lo = (c_lo - (k - 0.5)) * w_lo
        f_hi = ((k - 0.5) - c_hi) * w_hi
        v_lo, v_hi = _key_scores(lo), _key_scores(hi)
        guess = _score_keys(v_lo + (v_hi - v_lo) * (f_lo / (f_lo + f_hi)))
        mid = (lo >> 1) + (hi >> 1) + (lo & hi & 1)
        g = jnp.where(it < SEARCH_INTERP_STEPS, guess, mid)
        g = jnp.minimum(jnp.maximum(g, lo + 1), hi - 1)
        c = count_ge(g)
        hit = act & (c == k)
        up = act & (c > k)
        dn = act & (c < k)
        move_lo = hit | up
        lo = jnp.where(move_lo, g, lo)
        c_lo = jnp.where(move_lo, c, c_lo)
        hi = jnp.where(dn, g, hi)
        c_hi = jnp.where(dn, c, c_hi)
        w_hi = jnp.where(up, jnp.where(last > 0.0, w_hi * 0.5, one), jnp.where(dn, one, w_hi))
        w_lo = jnp.where(dn, jnp.where(last < 0.0, w_lo * 0.5, one), jnp.where(up, one, w_lo))
        last = jnp.where(up, one, jnp.where(dn, -one, last))
        active = jnp.where(act & jnp.logical_not(hit) & (lo + 1 < hi), 1.0, 0.0)
        return it + 1, lo, hi, c_lo, c_hi, w_lo, w_hi, last, active

    st = lax.while_loop(cond, body, (jnp.int32(0), lo0, hi0, n_real, zero, one, one, zero, active0))
    _, lo, _, c_lo, c_hi, _, _, _, _ = st
    thr = jnp.where(take_all, INT_MIN, lo)
    excess = jnp.logical_not(take_all) & (c_lo > k)
    need = k - c_hi

    def tie_search(_):
        def step(i, x):
            cand = x | lax.shift_left(jnp.int32(1), n_index_bits - 1 - i)
            return jnp.where(count_tie_below(thr, cand) < need, cand, x)
        return lax.fori_loop(0, n_index_bits, step, jnp.zeros_like(thr))

    any_excess = jnp.max(jnp.where(excess, 1.0, 0.0)) > 0.0
    cut = lax.cond(any_excess, tie_search, lambda _: jnp.zeros_like(thr), 0)
    cut = jnp.where(excess, cut, jnp.where(take_all, jnp.int32(-1), jnp.int32(2 ** 31 - 1)))
    return thr, cut


def _selection_bias(key, idx, thr, cut):
    sel = (key > thr) | ((key == thr) & (idx <= cut))
    return pltpu.bitcast(jnp.where(sel, 0.0, NEG_BIG).astype(F32), I32)


def _lane_fold(m):
    acc = m[:, 0:LANES]
    for c in range(1, m.shape[1] // LANES):
        acc = acc + m[:, c * LANES:(c + 1) * LANES]
    return acc


def _row_count(s_ref, n_chunks, pred):
    rows = s_ref.shape[0]

    def body(c, acc):
        k0 = pl.multiple_of(c * KEY_CHUNK, KEY_CHUNK)
        blk = s_ref[:, pl.ds(k0, KEY_CHUNK)]
        idx = k0 + lax.broadcasted_iota(I32, blk.shape, 1)
        return acc + _lane_fold(jnp.where(pred(blk, idx), 1.0, 0.0))

    acc = lax.fori_loop(0, n_chunks, body, jnp.zeros((rows, LANES), F32))
    return jnp.sum(acc, axis=1, keepdims=True)


def _row_select_to_bias(s_ref, n_chunks, n_real, topk):
    rows = s_ref.shape[0]
    inf = jnp.float32(jnp.inf)

    def min_max(c, carry):
        mn, mx = carry
        k0 = pl.multiple_of(c * KEY_CHUNK, KEY_CHUNK)
        blk = s_ref[:, pl.ds(k0, KEY_CHUNK)]
        v = _key_scores(blk)
        real = blk != INT_MIN
        for cc in range(KEY_CHUNK // LANES):
            sl = slice(cc * LANES, (cc + 1) * LANES)
            mn = jnp.minimum(mn, jnp.where(real[:, sl], v[:, sl], inf))
            mx = jnp.maximum(mx, jnp.where(real[:, sl], v[:, sl], -inf))
        return mn, mx

    mn, mx = lax.fori_loop(0, n_chunks, min_max, (jnp.full((rows, LANES), inf, F32), jnp.full((rows, LANES), -inf, F32)))
    vmin, vmax = jnp.min(mn, axis=1, keepdims=True), jnp.max(mx, axis=1, keepdims=True)
    thr, cut = _topk_threshold(
        lambda g: _row_count(s_ref, n_chunks, lambda blk, idx: blk >= g),
        lambda t, x: _row_count(s_ref, n_chunks, lambda blk, idx: (blk == t) & (idx < x)),
        vmin, vmax, n_real, topk, (s_ref.shape[1] - 1).bit_length())

    def to_bias(c, _):
        k0 = pl.multiple_of(c * KEY_CHUNK, KEY_CHUNK)
        blk = s_ref[:, pl.ds(k0, KEY_CHUNK)]
        idx = k0 + lax.broadcasted_iota(I32, blk.shape, 1)
        s_ref[:, pl.ds(k0, KEY_CHUNK)] = _selection_bias(blk, idx, thr, cut)
        return 0

    lax.fori_loop(0, n_chunks, to_bias, 0)


def _attend_group(qg, segments, bias_ref, rep):
    n = qg.shape[0]

    def step(load_kv, c, carry):
        m, l, acc = carry
        k0 = pl.multiple_of(c * KEY_CHUNK, KEY_CHUNK)
        kc, vc = load_kv(k0)
        bias = pltpu.bitcast(bias_ref[:, pl.ds(k0, KEY_CHUNK)], F32)
        s = _dot_nt(qg, kc) + jnp.concatenate([bias] * rep, axis=0)
        m_new = jnp.maximum(m, jnp.max(s, axis=1, keepdims=True))
        alpha = jnp.exp(m - m_new)
        p = jnp.exp(s - m_new)
        l = alpha * l + jnp.sum(p, axis=1, keepdims=True)
        acc = alpha * acc + _dot(p.astype(BF16), vc)
        return m_new, l, acc

    carry = (jnp.full((n, 1), NEG_BIG, F32), jnp.zeros((n, 1), F32), jnp.zeros((n, LANES), F32))
    for first, end, load_kv in segments:
        carry = lax.fori_loop(first, end, functools.partial(step, load_kv), carry)
    _, l, acc = carry
    return acc / l


def _stack_heads(x, first, count, width=LANES):
    x = x.astype(F32)
    return jnp.concatenate([x[:, (first + hd) * LANES:(first + hd) * LANES + width] for hd in range(count)],
                           axis=0).astype(BF16)


def _attend_all_heads(q, segments, bias_ref, o_ref):
    rows = q.shape[0]
    group = ATT_HEADS // KV_HEADS
    out = _attend_group(_stack_heads(q, 0, ATT_HEADS), segments, bias_ref, ATT_HEADS)
    for hd in range(ATT_HEADS):
        piece = out[hd * rows:(hd + 1) * rows]
        piece = jnp.where(_group_lane_mask(piece.shape, hd // group), piece, 0.0)
        o_ref[0, :, hd * LANES:(hd + 1) * LANES] = piece.astype(o_ref.dtype)


def _indexer_weights(kiw_q):
    return kiw_q[:, IDX_DIM:IDX_DIM + IDX_HEADS] * (IDX_HEADS ** -0.5 * IDX_DIM ** -0.5)


def _group_lane_mask(shape, g):
    lane = lax.broadcasted_iota(I32, shape, 1)
    return (lane >= g * HEAD_DIM) & (lane < (g + 1) * HEAD_DIM)


ATT_CHUNK = 256


def _sublane_fold(m, op=jnp.add):
    parts = [m[r * 8:(r + 1) * 8] for r in range(m.shape[0] // 8)]
    while len(parts) > 1:
        parts = [op(parts[n], parts[n + 1]) for n in range(0, len(parts) - 1, 2)] + (parts[-1:] if len(parts) % 2 else [])
    return parts[0]


def _col_reduce(m, op, reduce_fn):
    return reduce_fn(_sublane_fold(m, op), axis=0, keepdims=True)


def _col_count(s_ref, n_chunks, pred):
    cols = s_ref.shape[1]

    def body(c, acc):
        k0 = pl.multiple_of(c * KEY_CHUNK, KEY_CHUNK)
        blk = s_ref[pl.ds(k0, KEY_CHUNK), :]
        idx = k0 + lax.broadcasted_iota(I32, blk.shape, 0)
        return acc + _sublane_fold(jnp.where(pred(blk, idx), 1.0, 0.0))

    acc = lax.fori_loop(0, n_chunks, body, jnp.zeros((8, cols), F32))
    return jnp.sum(acc, axis=0, keepdims=True)


def _col_select_to_bias(s_ref, n_chunks, n_real, topk):
    cols = s_ref.shape[1]
    inf = jnp.float32(jnp.inf)

    def min_max(c, carry):
        mn, mx = carry
        k0 = pl.multiple_of(c * KEY_CHUNK, KEY_CHUNK)
        blk = s_ref[pl.ds(k0, KEY_CHUNK), :]
        v = _key_scores(blk)
        real = blk != INT_MIN
        mn = jnp.minimum(mn, _sublane_fold(jnp.where(real, v, inf), jnp.minimum))
        mx = jnp.maximum(mx, _sublane_fold(jnp.where(real, v, -inf), jnp.maximum))
        return mn, mx

    mn, mx = lax.fori_loop(0, n_chunks, min_max, (jnp.full((8, cols), inf, F32), jnp.full((8, cols), -inf, F32)))
    vmin, vmax = jnp.min(mn, axis=0, keepdims=True), jnp.max(mx, axis=0, keepdims=True)
    thr, cut = _topk_threshold(
        lambda g: _col_count(s_ref, n_chunks, lambda blk, idx: blk >= g),
        lambda t, x: _col_count(s_ref, n_chunks, lambda blk, idx: (blk == t) & (idx < x)),
        vmin, vmax, n_real, topk, (s_ref.shape[0] - 1).bit_length())

    def to_bias(c, _):
        k0 = pl.multiple_of(c * KEY_CHUNK, KEY_CHUNK)
        blk = s_ref[pl.ds(k0, KEY_CHUNK), :]
        idx = k0 + lax.broadcasted_iota(I32, blk.shape, 0)
        s_ref[pl.ds(k0, KEY_CHUNK), :] = _selection_bias(blk, idx, thr, cut)
        return 0

    lax.fori_loop(0, n_chunks, to_bias, 0)


def _dsa_prompt_kernel(q_ref, qi_ref, kiwq_ref, k_ref, vt_ref, kiw_ref, o_ref, s_ref, m_scr, l_scr, acc_scr,
                       sa_scr, sb_scr, *, qb, topk):
    j = pl.program_id(1)
    n_keys = j * qb + qb
    n_chunks = (n_keys + KEY_CHUNK - 1) // KEY_CHUNK
    qpos = j * qb + lax.broadcasted_iota(I32, (1, qb), 1)
    wt = kiwq_ref[0].T[IDX_DIM:IDX_DIM + IDX_HEADS] * (IDX_HEADS ** -0.5 * IDX_DIM ** -0.5)

    def scores(c, _):
        k0 = pl.multiple_of(c * KEY_CHUNK, KEY_CHUNK)
        kc = kiw_ref[0, pl.ds(k0, KEY_CHUNK), :]
        acc = jnp.zeros((KEY_CHUNK, qb), F32)
        for hd in range(IDX_HEADS):
            s = _dot_nt(kc, qi_ref[0, :, hd * LANES:(hd + 1) * LANES])
            acc = acc + jnp.maximum(s, 0.0) * wt[hd:hd + 1]
        kpos = k0 + lax.broadcasted_iota(I32, (KEY_CHUNK, qb), 0)
        s_ref[pl.ds(k0, KEY_CHUNK), :] = jnp.where(kpos <= qpos, _score_keys(acc), INT_MIN)
        return 0

    lax.fori_loop(0, n_chunks, scores, 0)
    _col_select_to_bias(s_ref, n_chunks, (qpos + 1).astype(F32), topk)

    m_scr[...] = jnp.full(m_scr.shape, NEG_BIG, F32)
    l_scr[...] = jnp.zeros(l_scr.shape, F32)
    acc_scr[...] = jnp.zeros(acc_scr.shape, F32)

    n_att = n_chunks * (KEY_CHUNK // ATT_CHUNK)

    def logits(step, buf):
        k0 = pl.multiple_of(jnp.minimum(step, n_att - 1) * ATT_CHUNK, ATT_CHUNK)
        kc = k_ref[0, pl.ds(k0, ATT_CHUNK), :]
        bias = pltpu.bitcast(s_ref[pl.ds(k0, ATT_CHUNK), :], F32)
        for hd in range(ATT_HEADS):
            buf[hd] = _dot_nt(kc, q_ref[0, :, hd * LANES:(hd + 1) * LANES]) + bias

    def accumulate(step, buf):
        k0 = pl.multiple_of(step * ATT_CHUNK, ATT_CHUNK)
        vtc = vt_ref[:, pl.ds(k0, ATT_CHUNK)]
        for hd in range(ATT_HEADS):
            s = buf[hd]
            m_prev = m_scr[hd:hd + 1, :]
            m_new = jnp.maximum(m_prev, _col_reduce(s, jnp.maximum, jnp.max))
            alpha = jnp.exp(m_prev - m_new)
            p = jnp.exp(s - m_new)
            l_scr[hd:hd + 1, :] = alpha * l_scr[hd:hd + 1, :] + _col_reduce(p, jnp.add, jnp.sum)
            acc_scr[hd] = alpha * acc_scr[hd] + _dot(vtc, p.astype(BF16))
            m_scr[hd:hd + 1, :] = m_new

    logits(0, sa_scr)

    def attend(c, _):
        logits(2 * c + 1, sb_scr)
        accumulate(2 * c, sa_scr)
        logits(2 * c + 2, sa_scr)
        accumulate(2 * c + 1, sb_scr)
        return 0

    lax.fori_loop(0, n_att // 2, attend, 0)

    group = ATT_HEADS // KV_HEADS
    for hd in range(ATT_HEADS):
        out_t = acc_scr[hd] / l_scr[hd:hd + 1, :]
        row = lax.broadcasted_iota(I32, out_t.shape, 0)
        g = hd // group
        out_t = jnp.where((row >= g * HEAD_DIM) & (row < (g + 1) * HEAD_DIM), out_t, 0.0)
        o_ref[0, :, hd * LANES:(hd + 1) * LANES] = out_t.T.astype(BF16)


def _dsa_prompt(q, qi, kiw, kbf, vt, kiwbf, qb):
    b, t, _ = q.shape
    topk = min(TOPK_MAX, t // 4)
    qblk = lambda width: pl.BlockSpec((1, qb, width), lambda bi, j: (bi, j, 0))
    full = pl.BlockSpec((1, t, LANES), lambda bi, j: (bi, 0, 0))
    return pl.pallas_call(
        functools.partial(_dsa_prompt_kernel, qb=qb, topk=topk),
        grid=(b, t // qb),
        in_specs=[qblk(ATT_HEADS * LANES), qblk(IDX_HEADS * LANES), qblk(LANES), full,
                  pl.BlockSpec((LANES, t), lambda bi, j: (0, bi)), full],
        out_specs=qblk(ATT_HEADS * LANES),
        out_shape=jax.ShapeDtypeStruct((b, t, ATT_HEADS * LANES), BF16),
        scratch_shapes=[pltpu.VMEM((pl.cdiv(t, KEY_CHUNK) * KEY_CHUNK, qb), I32), pltpu.VMEM((ATT_HEADS, qb), F32),
                        pltpu.VMEM((ATT_HEADS, qb), F32), pltpu.VMEM((ATT_HEADS, LANES, qb), F32),
                        pltpu.VMEM((ATT_HEADS, ATT_CHUNK, qb), F32), pltpu.VMEM((ATT_HEADS, ATT_CHUNK, qb), F32)],
        compiler_params=_cparams(("parallel", "arbitrary")),
        name="dsa_prompt",
    )(q, qi, kiw, kbf, vt, kiwbf)


def _dsa_sample_kernel(pt_ref, q_ref, qi_ref, kiwq_ref, kn_ref, vn_ref, kiwn_ref, ck_hbm, cv_hbm, cki_hbm,
                       o_ref, kbuf, vbuf, kibuf, s_ref, sems, *, ts, n_pages, topk):
    b = pl.program_id(0)
    past = n_pages * PAGE_SIZE

    streams = ((cki_hbm, kibuf), (ck_hbm, kbuf), (cv_hbm, vbuf))

    def page_copy(p, which):
        src, dst = streams[which]
        rows = pl.ds(pl.multiple_of(p * PAGE_SIZE, PAGE_SIZE), PAGE_SIZE)
        return pltpu.make_async_copy(src.at[pt_ref[b, p]], dst.at[rows], sems.at[which])

    def start_page(p, _):
        for which in range(len(streams)):
            page_copy(p, which).start()
        return 0

    lax.fori_loop(0, n_pages, start_page, 0)

    def wait_pages(which):
        def body(p, _):
            page_copy(p, which).wait()
            return 0
        lax.fori_loop(0, n_pages, body, 0)

    n_past_chunks = past // KEY_CHUNK
    n_chunks = n_past_chunks + 1
    qpos = lax.broadcasted_iota(I32, (ts, 1), 0)
    wq = _indexer_weights(kiwq_ref[0])
    qi_stack = _stack_heads(qi_ref[0], 0, IDX_HEADS, IDX_DIM)
    w_stack = jnp.concatenate([wq[:, hd:hd + 1] for hd in range(IDX_HEADS)], axis=0)

    def head_sum(x):
        acc = x[0:ts]
        for hd in range(1, IDX_HEADS):
            acc = acc + x[hd * ts:(hd + 1) * ts]
        return acc

    wait_pages(0)

    def past_scores(c, _):
        k0 = pl.multiple_of(c * KEY_CHUNK, KEY_CHUNK)
        kc = kibuf[pl.ds(k0, KEY_CHUNK), :].astype(BF16)
        acc = head_sum(jnp.maximum(_dot_nt(qi_stack, kc), 0.0) * w_stack)
        s_ref[:, pl.ds(k0, KEY_CHUNK)] = _score_keys(acc + 0.0)
        return 0

    lax.fori_loop(0, n_past_chunks, past_scores, 0)
    kin = kiwn_ref[0][:, 0:IDX_DIM]
    acc = head_sum(jnp.maximum(_dot_nt(qi_stack, kin), 0.0) * w_stack)
    kpos = lax.broadcasted_iota(I32, (ts, KEY_CHUNK), 1)
    s_ref[:, pl.ds(past, KEY_CHUNK)] = jnp.where(kpos <= qpos, _score_keys(acc + 0.0), INT_MIN)
    _row_select_to_bias(s_ref, n_chunks, (past + 1 + qpos).astype(F32), topk)

    wait_pages(1)
    wait_pages(2)

    def load_past(k0):
        return kbuf[pl.ds(k0, KEY_CHUNK), :].astype(BF16), vbuf[pl.ds(k0, KEY_CHUNK), :].astype(BF16)

    def load_new(k0):
        return kn_ref[0], vn_ref[0]

    segments = [(0, n_past_chunks, load_past), (n_past_chunks, n_chunks, load_new)]
    _attend_all_heads(q_ref[0], segments, s_ref, o_ref)


def _dsa_sample(q, qi, kiw, kn, vn, kiwn, cache_k, cache_v, cache_ki, page_table):
    b, ts, _ = q.shape
    n_pages = page_table.shape[1]
    past = n_pages * PAGE_SIZE
    topk = min(TOPK_MAX, (past + ts) // 4)
    blk = lambda rows, width: pl.BlockSpec((1, rows, width), lambda bi, pt: (bi, 0, 0))
    hbm = pl.BlockSpec(memory_space=pl.ANY)
    grid_spec = pltpu.PrefetchScalarGridSpec(
        num_scalar_prefetch=1,
        grid=(b,),
        in_specs=[blk(ts, ATT_HEADS * LANES), blk(ts, IDX_HEADS * LANES), blk(ts, LANES),
                  blk(KEY_CHUNK, LANES), blk(KEY_CHUNK, LANES), blk(KEY_CHUNK, LANES), hbm, hbm, hbm],
        out_specs=blk(ts, ATT_HEADS * LANES),
        scratch_shapes=[pltpu.VMEM((past, LANES), F32), pltpu.VMEM((past, LANES), F32),
                        pltpu.VMEM((past, IDX_DIM), F32), pltpu.VMEM((ts, past + KEY_CHUNK), I32),
                        pltpu.SemaphoreType.DMA((3,))],
    )
    return pl.pallas_call(
        functools.partial(_dsa_sample_kernel, ts=ts, n_pages=n_pages, topk=topk),
        grid_spec=grid_spec,
        out_shape=jax.ShapeDtypeStruct((b, ts, ATT_HEADS * LANES), F32),
        compiler_params=_cparams(("arbitrary",)),
        name="dsa_sample",
    )(page_table, q, qi, kiw, kn, vn, kiwn, cache_k, cache_v, cache_ki)


def _conv_kernel(u_ref, halo_ref, buf_ref, bg_ref, w_ref, ya_ref):
    i = pl.program_id(1)
    u = u_ref[0]
    halo, buf = halo_ref[0], buf_ref[0]
    first = i == 0
    prev1 = jnp.where(first, buf[1:2], halo[7:8])
    prev2 = jnp.where(first, buf[0:1], halo[6:7])
    row = lax.broadcasted_iota(I32, u.shape, 0)
    um1 = jnp.where(row == 0, prev1, pltpu.roll(u, 1, 0))
    um2 = jnp.where(row == 0, prev2, jnp.where(row == 1, prev1, pltpu.roll(u, 2, 0)))
    w = w_ref[...]
    conv = w[0:1] * um2 + w[1:2] * um1 + w[2:3] * u
    ya_ref[0] = (bg_ref[0] * conv).astype(BF16)


def _conv(u, bg, buf, w, tt):
    b, t, c = u.shape
    halo_rows = 8
    tile = pl.BlockSpec((1, tt, c), lambda bi, i: (bi, i, 0))
    halo = pl.BlockSpec((1, halo_rows, c), lambda bi, i: (bi, jnp.maximum(i * (tt // halo_rows) - 1, 0), 0))
    return pl.pallas_call(
        _conv_kernel,
        grid=(b, t // tt),
        in_specs=[tile, halo, pl.BlockSpec((1, CONV_WIDTH - 1, c), lambda bi, i: (bi, 0, 0)), tile,
                  pl.BlockSpec((CONV_WIDTH, c), lambda bi, i: (0, 0))],
        out_specs=tile,
        out_shape=jax.ShapeDtypeStruct((b, t, c), BF16),
        compiler_params=_cparams(("parallel", "parallel")),
        name="short_conv",
    )(u, u, buf, bg, w)


def _mm_res_kernel(*refs, n_in):
    a_refs, w_ref, x_ref, o_ref = refs[:n_in], refs[n_in], refs[n_in + 1], refs[n_in + 2]
    a = jnp.concatenate([r[...] for r in a_refs], axis=1) if n_in > 1 else a_refs[0][...]
    o_ref[...] = x_ref[...] + _dot(a, w_ref[...])


def _mm_res(a_list, w, x, tm):
    m = x.shape[0]
    row = lambda i: (i, 0)
    return pl.pallas_call(
        functools.partial(_mm_res_kernel, n_in=len(a_list)),
        grid=(m // tm,),
        in_specs=[pl.BlockSpec((tm, a.shape[1]), row) for a in a_list]
        + [pl.BlockSpec(w.shape, lambda i: (0, 0)), pl.BlockSpec((tm, D_MODEL), row)],
        out_specs=pl.BlockSpec((tm, D_MODEL), row),
        out_shape=jax.ShapeDtypeStruct((m, D_MODEL), F32),
        compiler_params=_cparams(("parallel",)),
        name="matmul_residual",
    )(*a_list, w, x)


def _norm_mm_kernel(x_ref, g_ref, w_ref, o_ref, h_scr):
    @pl.when(pl.program_id(1) == 0)
    def _():
        h_scr[...] = _rms(x_ref[...], g_ref[...]).astype(BF16)

    o_ref[...] = _dot(h_scr[...], w_ref[...])


def _norm_mm(x, g, w, tm, tn):
    m, n = x.shape[0], w.shape[1]
    return pl.pallas_call(
        _norm_mm_kernel,
        grid=(m // tm, n // tn),
        in_specs=[pl.BlockSpec((tm, D_MODEL), lambda i, j: (i, 0)), pl.BlockSpec((1, D_MODEL), lambda i, j: (0, 0)),
                  pl.BlockSpec((D_MODEL, tn), lambda i, j: (0, j))],
        out_specs=pl.BlockSpec((tm, tn), lambda i, j: (i, j)),
        out_shape=jax.ShapeDtypeStruct((m, n), F32),
        scratch_shapes=[pltpu.VMEM((tm, D_MODEL), BF16)],
        compiler_params=_cparams(("parallel", "arbitrary")),
        name="norm_matmul",
    )(x, g, w)


def _ffn_kernel(x_ref, g_ref, wg_ref, wu_ref, wd_ref, o_ref, h_scr, acc):
    f = pl.program_id(1)

    @pl.when(f == 0)
    def _():
        x = x_ref[...]
        h_scr[...] = _rms(x, g_ref[...]).astype(BF16)
        acc[...] = x

    h = h_scr[...]
    a = jax.nn.silu(_dot(h, wg_ref[...])) * _dot(h, wu_ref[...])
    acc[...] += _dot(a.astype(BF16), wd_ref[...])

    @pl.when(f == pl.num_programs(1) - 1)
    def _():
        o_ref[...] = acc[...]


def _ffn(x, g, wg, wu, wd, tm, tf):
    m, ff = x.shape[0], wg.shape[1]
    row = lambda i, f: (i, 0)
    return pl.pallas_call(
        _ffn_kernel,
        grid=(m // tm, ff // tf),
        in_specs=[pl.BlockSpec((tm, D_MODEL), row), pl.BlockSpec((1, D_MODEL), lambda i, f: (0, 0)),
                  pl.BlockSpec((D_MODEL, tf), lambda i, f: (0, f)), pl.BlockSpec((D_MODEL, tf), lambda i, f: (0, f)),
                  pl.BlockSpec((tf, D_MODEL), lambda i, f: (f, 0))],
        out_specs=pl.BlockSpec((tm, D_MODEL), row),
        out_shape=jax.ShapeDtypeStruct((m, D_MODEL), F32),
        scratch_shapes=[pltpu.VMEM((tm, D_MODEL), BF16), pltpu.VMEM((tm, D_MODEL), F32)],
        compiler_params=_cparams(("parallel", "arbitrary")),
        name="dense_swiglu",
    )(x, g, wg, wu, wd)


def _top2_gates(logits):
    lane = lax.broadcasted_iota(I32, logits.shape, 1).astype(F32)
    neg = jnp.float32(-jnp.inf)
    l1 = jnp.where(lane < N_EXPERTS, logits, neg)
    m1 = jnp.max(l1, axis=1, keepdims=True)
    i1 = jnp.min(jnp.where(l1 == m1, lane, float(LANES)), axis=1, keepdims=True)
    l2 = jnp.where(lane == i1, neg, l1)
    m2 = jnp.max(l2, axis=1, keepdims=True)
    i2 = jnp.min(jnp.where(l2 == m2, lane, float(LANES)), axis=1, keepdims=True)
    e = jnp.exp(m2 - m1)
    w1 = 1.0 / (1.0 + e)
    w2 = e / (1.0 + e)
    first, second = lane == i1, lane == i2
    return jnp.where(first, w1, jnp.where(second, w2, 0.0)), jnp.where(first | second, 1.0, 0.0)


MOE_SUB = 128
MOE_TOKEN_TILE = 1024


def _moe_route_kernel(x_ref, g_ref, rhi_ref, rlo_ref, h_ref, gate_ref, posc_ref, posr_ref, cnt_ref):
    tm = x_ref.shape[0]
    hn = _rms(x_ref[...], g_ref[...])
    h_hi = hn.astype(BF16)
    h_lo = (hn - h_hi.astype(F32)).astype(BF16)
    logits = _dot(h_hi, rhi_ref[...]) + (_dot(h_lo, rhi_ref[...]) + _dot(h_hi, rlo_ref[...]))
    gate, routed = _top2_gates(logits)
    h_ref[...] = h_hi
    gate_ref[...] = gate
    earlier = (lax.broadcasted_iota(I32, (tm, tm), 0) > lax.broadcasted_iota(I32, (tm, tm), 1))
    slot = _dot(jnp.where(earlier, 1.0, 0.0).astype(BF16), routed.astype(BF16))
    posc = jnp.where(routed > 0.0, slot, -1.0)
    posc_ref[...] = posc
    posr_ref[0] = posc.T[0:N_EXPERTS]
    cnt_ref[0] = jnp.broadcast_to(jnp.sum(routed, axis=0, keepdims=True), (8, LANES))


def _moe_expert_kernel(cnt_ref, x_ref, h_ref, gate_ref, posc_ref, posr_ref, wg_ref, wu_ref, wd_ref, o_ref, xg, yacc):
    i, e, f = pl.program_id(0), pl.program_id(1), pl.program_id(2)
    tm = x_ref.shape[0]
    n_sub = (cnt_ref[i * N_EXPERTS + e] + MOE_SUB - 1) // MOE_SUB

    @pl.when((e == 0) & (f == 0))
    def _():
        o_ref[...] = x_ref[...]

    @pl.when(f == 0)
    def _():
        posr = posr_ref[0]
        h = h_ref[...]

        def gather(s, _):
            base = pl.multiple_of(s * MOE_SUB, MOE_SUB)
            slot = (base + lax.broadcasted_iota(I32, (MOE_SUB, tm), 0)).astype(F32)
            onehot = jnp.where(posr == slot, 1.0, 0.0).astype(BF16)
            xg[pl.ds(base, MOE_SUB), :] = _dot(onehot, h).astype(BF16)
            yacc[pl.ds(base, MOE_SUB), :] = jnp.zeros((MOE_SUB, D_MODEL), F32)
            return 0

        lax.fori_loop(0, n_sub, gather, 0)

    def expert(s, _):
        rows = pl.ds(pl.multiple_of(s * MOE_SUB, MOE_SUB), MOE_SUB)
        xs = xg[rows, :]
        a = jax.nn.silu(_dot(xs, wg_ref[0])) * _dot(xs, wu_ref[0])
        yacc[rows, :] += _dot(a.astype(BF16), wd_ref[0])
        return 0

    lax.fori_loop(0, n_sub, expert, 0)

    @pl.when(f == pl.num_programs(2) - 1)
    def _():
        lane = lax.broadcasted_iota(I32, (tm, LANES), 1)
        mine = lane == e
        posc = jnp.sum(jnp.where(mine, posc_ref[...], 0.0), axis=1, keepdims=True)
        gate = jnp.sum(jnp.where(mine, gate_ref[...], 0.0), axis=1, keepdims=True)

        def scatter(s, _):
            base = pl.multiple_of(s * MOE_SUB, MOE_SUB)
            slot = (base + lax.broadcasted_iota(I32, (tm, MOE_SUB), 1)).astype(F32)
            onehot = jnp.where(posc == slot, 1.0, 0.0).astype(BF16)
            o_ref[...] += gate * _dot(onehot, yacc[pl.ds(base, MOE_SUB), :].astype(BF16))
            return 0

        lax.fori_loop(0, n_sub, scatter, 0)


def _moe(x, g, r_hi, r_lo, wg, wu, wd, tm, tf):
    m = x.shape[0]
    nt = m // tm
    row = lambda i: (i, 0)
    const = lambda i: (0, 0)
    h, gate, posc, posr, cnt = pl.pallas_call(
        _moe_route_kernel,
        grid=(nt,),
        in_specs=[pl.BlockSpec((tm, D_MODEL), row), pl.BlockSpec((1, D_MODEL), const),
                  pl.BlockSpec((D_MODEL, LANES), const), pl.BlockSpec((D_MODEL, LANES), const)],
        out_specs=[pl.BlockSpec((tm, D_MODEL), row), pl.BlockSpec((tm, LANES), row), pl.BlockSpec((tm, LANES), row),
                   pl.BlockSpec((1, N_EXPERTS, tm), lambda i: (i, 0, 0)), pl.BlockSpec((1, 8, LANES), lambda i: (i, 0, 0))],
        out_shape=[jax.ShapeDtypeStruct((m, D_MODEL), BF16), jax.ShapeDtypeStruct((m, LANES), F32),
                   jax.ShapeDtypeStruct((m, LANES), F32), jax.ShapeDtypeStruct((nt, N_EXPERTS, tm), F32),
                   jax.ShapeDtypeStruct((nt, 8, LANES), F32)],
        compiler_params=_cparams(("parallel",)),
        name="moe_route",
    )(x, g, r_hi, r_lo)
    counts = cnt[:, 0, :N_EXPERTS].astype(I32).reshape(nt * N_EXPERTS)
    posr = posr.reshape(nt * N_EXPERTS, 1, tm)
    row3 = lambda i, e, f, c: (i, 0)
    slot_rows = pl.cdiv(tm, MOE_SUB) * MOE_SUB
    grid_spec = pltpu.PrefetchScalarGridSpec(
        num_scalar_prefetch=1,
        grid=(nt, N_EXPERTS, EXPERT_FF // tf),
        in_specs=[pl.BlockSpec((tm, D_MODEL), row3), pl.BlockSpec((tm, D_MODEL), row3),
                  pl.BlockSpec((tm, LANES), row3), pl.BlockSpec((tm, LANES), row3),
                  pl.BlockSpec((1, 1, tm), lambda i, e, f, c: (i * N_EXPERTS + e, 0, 0)),
                  pl.BlockSpec((1, D_MODEL, tf), lambda i, e, f, c: (e, 0, f)),
                  pl.BlockSpec((1, D_MODEL, tf), lambda i, e, f, c: (e, 0, f)),
                  pl.BlockSpec((1, tf, D_MODEL), lambda i, e, f, c: (e, f, 0))],
        out_specs=pl.BlockSpec((tm, D_MODEL), row3),
        scratch_shapes=[pltpu.VMEM((slot_rows, D_MODEL), BF16), pltpu.VMEM((slot_rows, D_MODEL), F32)],
    )
    return pl.pallas_call(
        _moe_expert_kernel,
        grid_spec=grid_spec,
        out_shape=jax.ShapeDtypeStruct((m, D_MODEL), F32),
        compiler_params=_cparams(("parallel", "arbitrary", "arbitrary")),
        name="moe_experts",
    )(counts, x, h, gate, posc, posr, wg, wu, wd)


def _ple_kernel(x_ref, g_ref, p_ref, wp_ref, wgate_ref, gf_ref, o_ref, *, final_norm):
    x = x_ref[...]
    hp = _rms(x, g_ref[...]).astype(BF16)
    gate = jax.nn.sigmoid(_dot(hp, wgate_ref[...]))
    y = x + _dot(p_ref[...].astype(BF16), wp_ref[...]) * gate
    if final_norm:
        y = _rms(y, gf_ref[...])
    o_ref[...] = y


def _ple(x, g, p, wp, wgate, g_final, final_norm, tm):
    m = x.shape[0]
    row = lambda i: (i, 0)
    const = lambda i: (0, 0)
    return pl.pallas_call(
        functools.partial(_ple_kernel, final_norm=final_norm),
        grid=(m // tm,),
        in_specs=[pl.BlockSpec((tm, D_MODEL), row), pl.BlockSpec((1, D_MODEL), const), pl.BlockSpec((tm, PLE_DIM), row),
                  pl.BlockSpec((PLE_DIM, D_MODEL), const), pl.BlockSpec((D_MODEL, D_MODEL), const),
                  pl.BlockSpec((1, D_MODEL), const)],
        out_specs=pl.BlockSpec((tm, D_MODEL), row),
        out_shape=jax.ShapeDtypeStruct((m, D_MODEL), F32),
        compiler_params=_cparams(("parallel",)),
        name="per_layer_embedding",
    )(x, g, p, wp, wgate, g_final)


def _ret_kernel(q_ref, k_ref, v_ref, gate_ref, cos_ref, sin_ref, s0_ref, o_ref, sout_ref, state, *, chunk, chunk_rows):
    i = pl.program_id(1)

    @pl.when(i == 0)
    def _():
        state[...] = s0_ref[0]

    cos, sin = cos_ref[...], sin_ref[...]
    tt = q_ref.shape[1]
    r = chunk_rows
    ii = lax.broadcasted_iota(I32, (r, r), 0).astype(F32)
    jj = lax.broadcasted_iota(I32, (r, r), 1).astype(F32)
    rel = ii - jj
    icol = lax.broadcasted_iota(I32, (r, 1), 0).astype(F32)
    half = RET_DK // 2

    def rot(ref, hd):
        x1 = ref[0, :, hd * RET_DK:hd * RET_DK + half]
        x2 = ref[0, :, hd * RET_DK + half:(hd + 1) * RET_DK]
        return jnp.concatenate([x1 * cos - x2 * sin, x2 * cos + x1 * sin], axis=1)

    for hd in range(RET_HEADS):
        lg = math.log(1.0 - 2.0 ** (-5.0 - hd))
        d_in = jnp.where(rel >= 0, jnp.exp(lg * jnp.maximum(rel, 0.0)), 0.0)
        d_q = jnp.exp(lg * (icol + 1.0))
        d_k = jnp.exp(lg * (chunk - 1.0 - icol)) * (RET_DK ** -0.5)
        d_c = math.exp(lg * chunk)
        qr = rot(q_ref, hd)
        kr = rot(k_ref, hd)
        vsl = slice(hd * RET_DV, (hd + 1) * RET_DV)
        for c in range(tt // r):
            rows = slice(c * r, (c + 1) * r)
            qc = qr[rows].astype(BF16)
            kc = kr[rows]
            vc = v_ref[0, rows, vsl].astype(BF16)
            s_prev = state[hd]
            att = _dot_nt(qc, (kc * (RET_DK ** -0.5)).astype(BF16)) * d_in
            o = _dot(att.astype(BF16), vc) + _dot(qc, s_prev.astype(BF16)) * d_q
            state[hd] = s_prev * d_c + _dot_tn((kc * d_k).astype(BF16), vc)
            mu = jnp.mean(o, axis=-1, keepdims=True)
            var = jnp.mean(jnp.square(o - mu), axis=-1, keepdims=True)
            on = (o - mu) * lax.rsqrt(var + EPS)
            o_ref[0, rows, vsl] = (jax.nn.silu(gate_ref[0, rows, vsl]) * on).astype(BF16)

    @pl.when(i == pl.num_programs(1) - 1)
    def _():
        sout_ref[0] = state[...]


def _retention(z, s0, cos, sin, tt, chunk, chunk_rows):
    b, t, _ = z.shape
    hk, hv = RET_HEADS * RET_DK, RET_HEADS * RET_DV
    half = RET_DK // 2
    state_spec = pl.BlockSpec((1, RET_HEADS, RET_DK, RET_DV), lambda bi, i: (bi, 0, 0, 0))
    tab = pl.BlockSpec((tt, half), lambda bi, i: (i, 0))
    return pl.pallas_call(
        functools.partial(_ret_kernel, chunk=chunk, chunk_rows=chunk_rows),
        grid=(b, t // tt),
        in_specs=[pl.BlockSpec((1, tt, hk), lambda bi, i: (bi, i, 0)), pl.BlockSpec((1, tt, hk), lambda bi, i: (bi, i, 1)),
                  pl.BlockSpec((1, tt, hv), lambda bi, i: (bi, i, 1)), pl.BlockSpec((1, tt, hv), lambda bi, i: (bi, i, 2)),
                  tab, tab, state_spec],
        out_specs=[pl.BlockSpec((1, tt, hv), lambda bi, i: (bi, i, 0)), state_spec],
        out_shape=[jax.ShapeDtypeStruct((b, t, hv), BF16), jax.ShapeDtypeStruct(s0.shape, F32)],
        scratch_shapes=[pltpu.VMEM((RET_HEADS, RET_DK, RET_DV), F32)],
        compiler_params=_cparams(("parallel", "arbitrary")),
        name="retention",
    )(z, z, z, z, cos, sin, s0)


def _pack_l0_w_in(w):
    offs = np.cumsum((0,) + AB_SPLITS)
    bg, cg, hv, q, k, v, qi, ki, wi = [w[:, offs[n]:offs[n + 1]] for n in range(len(AB_SPLITS))]
    group = ATT_HEADS // KV_HEADS
    q4 = q.reshape(D_MODEL, ATT_HEADS, HEAD_DIM)
    zq = jnp.zeros_like(q4)
    q_pad = jnp.concatenate([jnp.concatenate([q4[:, :group], zq[:, :group]], axis=-1),
                             jnp.concatenate([zq[:, group:], q4[:, group:]], axis=-1)], axis=1).reshape(D_MODEL, -1)
    qi4 = qi.reshape(D_MODEL, IDX_HEADS, IDX_DIM)
    qi_pad = jnp.concatenate([qi4, jnp.zeros_like(qi4)], axis=-1).reshape(D_MODEL, -1)
    kiw = jnp.concatenate([ki, wi, jnp.zeros((D_MODEL, LANES - IDX_DIM - IDX_HEADS), w.dtype)], axis=1)
    return jnp.concatenate([bg, cg, hv, q_pad, k, v, qi_pad, kiw], axis=1).astype(BF16)


def _pack_l0_w_out(w):
    group = ATT_HEADS // KV_HEADS
    wa, wb = w[:CONV_CH], w[CONV_CH:].reshape(ATT_HEADS, HEAD_DIM, D_MODEL)
    zb = jnp.zeros_like(wb)
    wb_pad = jnp.concatenate([jnp.concatenate([wb[:group], zb[:group]], axis=1),
                              jnp.concatenate([zb[group:], wb[group:]], axis=1)], axis=0).reshape(-1, D_MODEL)
    return jnp.concatenate([wa, wb_pad], axis=0).astype(BF16)


def _rope_tables(pos, reps):
    inv = ROPE_THETA ** (-jnp.arange(0, HEAD_DIM, 2, dtype=F32) / HEAD_DIM)
    ang = pos.astype(F32)[:, None] * inv[None, :]
    cos, sin = jnp.cos(ang), jnp.sin(ang)
    cos64 = jnp.concatenate([cos, cos], axis=1)
    sin64 = jnp.concatenate([-sin, sin], axis=1)
    one, zero = jnp.ones_like(cos64), jnp.zeros_like(cos64)
    tabs = (jnp.concatenate([cos64, cos64], 1), jnp.concatenate([sin64, sin64], 1),
            jnp.concatenate([cos64, one], 1), jnp.concatenate([sin64, zero], 1))
    return tuple(jnp.tile(tb, (reps, 1)) for tb in tabs)


def _ret_tables(pos):
    inv = ROPE_THETA ** (-jnp.linspace(0.0, 1.0, RET_DK // 2, dtype=F32))
    ang = pos.astype(F32)[:, None] * inv[None, :]
    return jnp.cos(ang), jnp.sin(ang)


def _pack_params(prm):
    r = prm['moe_router'][0]
    r_pad = jnp.concatenate([r, jnp.zeros((D_MODEL, LANES - N_EXPERTS), F32)], axis=1)
    r_hi = r_pad.astype(BF16)
    bf = lambda a: a.astype(BF16)
    return dict(
        l0_w_in=_pack_l0_w_in(prm['ab_w_in'][0]), l0_w_out=_pack_l0_w_out(prm['ab_w_out'][0]),
        l0_w_vt=bf(prm['ab_w_in'][0][:, sum(AB_SPLITS[:5]):sum(AB_SPLITS[:6])].T),
        conv_w=prm['ab_conv_w'][0],
        ffn=(bf(prm['ffn_w_gate'][0]), bf(prm['ffn_w_up'][0]), bf(prm['ffn_w_down'][0])),
        ret_w_in=bf(prm['ret_w_in'][0]), ret_w_out=bf(prm['ret_w_out'][0]),
        r_hi=r_hi, r_lo=(r_pad - r_hi.astype(F32)).astype(BF16),
        moe=(bf(prm['moe_w_gate'][0]), bf(prm['moe_w_up'][0]), bf(prm['moe_w_down'][0])),
        ple_w=bf(prm['ple_w']), ple_gate_w=bf(prm['ple_gate_w']),
        norm_mix=prm['norm_mix'][:, None, :], norm_ffn=prm['norm_ffn'][:, None, :],
        norm_ple=prm['norm_ple'][:, None, :], norm_final=prm['norm_final'][None, :],
    )


def _trunk(x, p, pos, conv_buf, ret_state, pk, paged):
    b, t, _ = x.shape
    m = b * t
    tm = min(m, TOKEN_TILE)
    x2 = x.reshape(m, D_MODEL)

    reps = max(1, tm // t)
    tabs = _rope_tables(pos, reps)
    bg, u, q, k, v, qi, kiw, kbf, vbf, kiwbf, vt = _l0_in_proj(x2, pk['norm_mix'][0], pk['l0_w_in'], pk['l0_w_vt'], tabs,
                                                              tm, tabs[0].shape[0] // tm)
    seq = lambda a: a.reshape(b, t, a.shape[-1])
    if paged is None:
        yb = _dsa_prompt(seq(q), seq(qi), seq(kiw), seq(kbf), vt, seq(kiwbf), qb=min(t, 128))
    else:
        cache_k, cache_v, cache_ki, page_table = paged
        pad = lambda a: jnp.pad(seq(a), ((0, 0), (0, KEY_CHUNK - t), (0, 0)))
        n_pool = cache_k.shape[0]
        yb = _dsa_sample(seq(q).astype(F32), seq(qi).astype(F32), seq(kiw), pad(kbf), pad(vbf), pad(kiwbf),
                         cache_k.reshape(n_pool, PAGE_SIZE, LANES), cache_v.reshape(n_pool, PAGE_SIZE, LANES),
                         cache_ki, page_table).astype(BF16)
    u3 = seq(u)
    ya = _conv(u3, seq(bg), conv_buf, pk['conv_w'], tt=min(t, TOKEN_TILE))
    x2 = _mm_res([ya.reshape(m, CONV_CH), yb.reshape(m, ATT_HEADS * LANES)], pk['l0_w_out'], x2, tm)
    x2 = _ffn(x2, pk['norm_ffn'][0], *pk['ffn'], tm=tm, tf=D_FF // 2)
    x2 = _ple(x2, pk['norm_ple'][0], p[0].reshape(m, PLE_DIM), pk['ple_w'][0], pk['ple_gate_w'][0],
              pk['norm_final'], False, tm)
    new_k = k.reshape(1, b, t, KV_HEADS, HEAD_DIM)
    new_v = v.reshape(1, b, t, KV_HEADS, HEAD_DIM)
    new_ki = seq(kiw)[None, :, :, :IDX_DIM]
    new_conv = jnp.concatenate([conv_buf, u3], axis=1)[None, :, -(CONV_WIDTH - 1):]

    z = _norm_mm(x2, pk['norm_mix'][1], pk['ret_w_in'], tm, 1024).reshape(b, t, -1)
    cos_r, sin_r = _ret_tables(pos)
    if t % RET_CHUNK == 0:
        og, s_new = _retention(z, ret_state, cos_r, sin_r, tt=2 * RET_CHUNK, chunk=RET_CHUNK, chunk_rows=RET_CHUNK)
    else:
        rows = 16
        padt = lambda a: jnp.pad(a, ((0, 0),) * (a.ndim - 2) + ((0, rows - t), (0, 0)))
        og, s_new = _retention(padt(z), ret_state, padt(cos_r), padt(sin_r), tt=rows, chunk=t, chunk_rows=rows)
        og = og[:, :t]
    x2 = _mm_res([og.reshape(m, RET_HEADS * RET_DV)], pk['ret_w_out'], x2, tm)
    x2 = _moe(x2, pk['norm_ffn'][1], pk['r_hi'], pk['r_lo'], *pk['moe'], tm=min(m, MOE_TOKEN_TILE), tf=EXPERT_FF // 4)
    x2 = _ple(x2, pk['norm_ple'][1], p[1].reshape(m, PLE_DIM), pk['ple_w'][1], pk['ple_gate_w'][1],
              pk['norm_final'], True, tm)
    return x2.reshape(b, t, D_MODEL), new_k, new_v, new_ki, new_conv, s_new[None]


def kernel(x_prompt, x_sample, cache_k, cache_v, cache_kidx, state_conv, state_ret, page_table, p_prompt, p_sample,
           norm_mix, norm_ffn, norm_ple, norm_final, ab_w_in, ab_conv_w, ab_w_out, ffn_w_gate, ffn_w_up, ffn_w_down,
           ret_w_in, ret_w_out, moe_router, moe_w_gate, moe_w_up, moe_w_down, ple_w, ple_gate_w):
    prm = dict(norm_mix=norm_mix, norm_ffn=norm_ffn, norm_ple=norm_ple, norm_final=norm_final, ab_w_in=ab_w_in,
               ab_conv_w=ab_conv_w, ab_w_out=ab_w_out, ffn_w_gate=ffn_w_gate, ffn_w_up=ffn_w_up, ffn_w_down=ffn_w_down,
               ret_w_in=ret_w_in, ret_w_out=ret_w_out, moe_router=moe_router, moe_w_gate=moe_w_gate,
               moe_w_up=moe_w_up, moe_w_down=moe_w_down, ple_w=ple_w, ple_gate_w=ple_gate_w)
    pk = _pack_params(prm)
    b, t = x_prompt.shape[0], x_prompt.shape[1]
    db, ts = x_sample.shape[0], x_sample.shape[1]
    past_len = page_table.shape[1] * PAGE_SIZE
    dt = x_prompt.dtype

    conv0 = jnp.zeros((b, CONV_WIDTH - 1, CONV_CH), dt)
    ret0 = jnp.zeros((b, RET_HEADS, RET_DK, RET_DV), dt)
    y_p, k_p, v_p, ki_p, cb_p, rs_p = _trunk(x_prompt, p_prompt, jnp.arange(t, dtype=I32), conv0, ret0, pk, None)

    pos_s = past_len + jnp.arange(ts, dtype=I32)
    paged = (cache_k[0], cache_v[0], cache_kidx[0], page_table)
    y_s, k_s, v_s, ki_s, cb_s, rs_s = _trunk(x_sample, p_sample, pos_s, state_conv[0], state_ret[0], pk, paged)
    return (y_p, y_s, k_p, v_p, ki_p, cb_p, rs_p, k_s, v_s, ki_s, cb_s, rs_s)
```

```python
import functools
import math

import jax
import jax.numpy as jnp
import numpy as np
from jax import lax
from jax.experimental import pallas as pl
from jax.experimental.pallas import tpu as pltpu

F32 = jnp.float32
BF16 = jnp.bfloat16
I32 = jnp.int32

D_MODEL = 1024
PAGE_SIZE = 128
CONV_CH = D_MODEL // 2
CONV_WIDTH = 3
ATT_HEADS = 8
KV_HEADS = 2
HEAD_DIM = 64
IDX_HEADS = 4
IDX_DIM = 64
TOPK_MAX = 256
ROPE_THETA = 10000.0
RET_HEADS = 4
RET_DK = D_MODEL // RET_HEADS
RET_DV = 2 * RET_DK
RET_CHUNK = 128
D_FF = 2816
N_EXPERTS = 8
EXPERT_FF = 3584
PLE_DIM = 256
EPS = 1e-6
AB_SPLITS = (CONV_CH, CONV_CH, CONV_CH, ATT_HEADS * HEAD_DIM, KV_HEADS * HEAD_DIM, KV_HEADS * HEAD_DIM,
             IDX_HEADS * IDX_DIM, IDX_DIM, IDX_HEADS)

LANES = 128
VMEM_LIMIT = 48 * 1024 * 1024
INT_MIN = -2 ** 31
NEG_BIG = -1e30
LOG2_E = math.log2(math.e)

C_BG, C_CG, C_HV, C_Q, C_K, C_V, C_QI, C_KIW, C_END = 0, 512, 1024, 1536, 2560, 2688, 2816, 3328, 3456
KEY_CHUNK = 512
TOKEN_TILE = 512


def _cparams(sem):
    return pltpu.CompilerParams(dimension_semantics=sem, vmem_limit_bytes=VMEM_LIMIT)


def _rms(x, g):
    return x * lax.rsqrt(jnp.mean(x * x, axis=-1, keepdims=True) + EPS) * g


def _dot(a, b):
    return jnp.dot(a, b, preferred_element_type=F32)


def _dot_nt(a, b):
    return lax.dot_general(a, b, (((1,), (1,)), ((), ())), preferred_element_type=F32)


def _dot_tn(a, b):
    return lax.dot_general(a, b, (((0,), (0,)), ((), ())), preferred_element_type=F32)


def _swap_halves64(x):
    lane = lax.broadcasted_iota(I32, x.shape, 1)
    from_above = pltpu.roll(x, LANES - 32, 1)
    from_below = pltpu.roll(x, 32, 1)
    return jnp.where((lane & 63) < 32, from_above, from_below)


def _l0_in_kernel(x_ref, g_ref, w_ref, wvt_ref, cos_ref, sin_ref, cosb_ref, sinb_ref,
                  bg_ref, u_ref, q_ref, k_ref, v_ref, qi_ref, kiw_ref, kbf_ref, vbf_ref, kiwbf_ref, vt_ref):
    h = _rms(x_ref[...], g_ref[...]).astype(BF16)
    cos, sin = cos_ref[...], sin_ref[...]

    def rot(z, c, s):
        return z * c + _swap_halves64(z) * s

    bg_ref[...] = _dot(h, w_ref[:, C_BG:C_CG])
    u_ref[...] = _dot(h, w_ref[:, C_CG:C_HV]) * _dot(h, w_ref[:, C_HV:C_Q])
    zq = _dot(h, w_ref[:, C_Q:C_K])
    for hd in range(ATT_HEADS):
        sl = slice(hd * LANES, (hd + 1) * LANES)
        q_ref[:, sl] = (rot(zq[:, sl], cos, sin) * (HEAD_DIM ** -0.5 * LOG2_E)).astype(BF16)
    k = rot(_dot(h, w_ref[:, C_K:C_V]), cos, sin)
    k_ref[...] = k
    kbf_ref[...] = k.astype(BF16)
    v = _dot(h, w_ref[:, C_V:C_QI])
    v_ref[...] = v
    vbf_ref[...] = v.astype(BF16)
    vt_ref[...] = _dot_nt(wvt_ref[...], h).astype(BF16)
    zqi = _dot(h, w_ref[:, C_QI:C_KIW])
    for hd in range(IDX_HEADS):
        sl = slice(hd * LANES, (hd + 1) * LANES)
        qi_ref[:, sl] = rot(zqi[:, sl], cos, sin).astype(BF16)
    kiw = rot(_dot(h, w_ref[:, C_KIW:C_END]), cosb_ref[...], sinb_ref[...])
    kiw_ref[...] = kiw
    kiwbf_ref[...] = kiw.astype(BF16)


def _l0_in_proj(x2, g, w, wvt, tabs, tm, n_tab_blocks):
    m = x2.shape[0]
    row = lambda i: (i, 0)
    const = lambda i: (0, 0)
    tab = lambda i: (i % n_tab_blocks, 0)
    widths = (512, 512, 1024, 128, 128, 512, 128, 128, 128, 128)
    dtypes = (F32, F32, BF16, F32, F32, BF16, F32, BF16, BF16, BF16)
    return pl.pallas_call(
        _l0_in_kernel,
        grid=(m // tm,),
        in_specs=[pl.BlockSpec((tm, D_MODEL), row), pl.BlockSpec((1, D_MODEL), const),
                  pl.BlockSpec((D_MODEL, C_END), const), pl.BlockSpec((LANES, D_MODEL), const)]
        + [pl.BlockSpec((tm, LANES), tab)] * 4,
        out_specs=[pl.BlockSpec((tm, wd), row) for wd in widths] + [pl.BlockSpec((LANES, tm), lambda i: (0, i))],
        out_shape=[jax.ShapeDtypeStruct((m, wd), dt) for wd, dt in zip(widths, dtypes)]
        + [jax.ShapeDtypeStruct((LANES, m), BF16)],
        compiler_params=_cparams(("parallel",)),
        name="l0_in_proj",
    )(x2, g, w, wvt, *tabs)


def _score_keys(score):
    bits = pltpu.bitcast(score, I32)
    return bits ^ ((bits >> 31) & jnp.int32(0x7FFFFFFF))


def _key_scores(key):
    return pltpu.bitcast(key ^ ((key >> 31) & jnp.int32(0x7FFFFFFF)), F32)


SEARCH_PLAIN_STEPS = 8
SEARCH_SNAP_INTERP_STEPS = 8
SEARCH_MAX_STEPS = SEARCH_PLAIN_STEPS + SEARCH_SNAP_INTERP_STEPS + 34


def _topk_threshold(probe, count_tie_below, vmin, vmax, n_real, topk, n_index_bits):
    k = float(topk)
    lo0, hi0 = _score_keys(vmin), _score_keys(vmax) + 1
    take_all = n_real <= k
    zero, one = jnp.zeros_like(vmin), jnp.ones_like(vmin)
    active0 = jnp.where(jnp.logical_not(take_all) & (lo0 + 1 < hi0), 1.0, 0.0)

    def step(snap, st):
        it, lo, hi, c_lo, c_hi, w_lo, w_hi, last, active = st
        act = active > 0.0
        f_lo = (c_lo - (k - 0.5)) * w_lo
        f_hi = ((k - 0.5) - c_hi) * w_hi
        v_lo, v_hi = _key_scores(lo), _key_scores(hi)
        g = _score_keys(v_lo + (v_hi - v_lo) * (f_lo / (f_lo + f_hi)))
        if snap:
            mid = (lo >> 1) + (hi >> 1) + (lo & hi & 1)
            g = jnp.where(it < SEARCH_PLAIN_STEPS + SEARCH_SNAP_INTERP_STEPS, g, mid)
        g = jnp.minimum(jnp.maximum(g, lo + 1), hi - 1)
        c, key_up, key_dn = probe(g, snap)
        hit = act & (c == k)
        up = act & (c > k)
        dn = act & (c < k)
        lo = jnp.where(hit, g, jnp.where(up, key_up if snap else g, lo))
        c_lo = jnp.where(hit | up, c, c_lo)
        hi = jnp.where(dn, key_dn + 1 if snap else g, hi)
        c_hi = jnp.where(dn, c, c_hi)
        w_hi = jnp.where(up, jnp.where(last > 0.0, w_hi * 0.5, one), jnp.where(dn, one, w_hi))
        w_lo = jnp.where(dn, jnp.where(last < 0.0, w_lo * 0.5, one), jnp.where(up, one, w_lo))
        last = jnp.where(up, one, jnp.where(dn, -one, last))
        active = jnp.where(act & jnp.logical_not(hit) & (lo + 1 < hi), 1.0, 0.0)
        return it + 1, lo, hi, c_lo, c_hi, w_lo, w_hi, last, active

    def cond(limit, st):
        return (st[0] < limit) & (jnp.max(st[-1]) > 0.0)

    st = (jnp.int32(0), lo0, hi0, n_real, zero, one, one, zero, active0)
    st = lax.fori_loop(0, SEARCH_PLAIN_STEPS, lambda _, s: step(False, s), st)
    st = lax.while_loop(functools.partial(cond, SEARCH_MAX_STEPS), functools.partial(step, True), st)
    _, lo, _, c_lo, c_hi, _, _, _, _ = st
    thr = jnp.where(take_all, INT_MIN, lo)
    excess = jnp.logical_not(take_all) & (c_lo > k)
    need = k - c_hi

    def tie_search(_):
        def step(i, x):
            cand = x | lax.shift_left(jnp.int32(1), n_index_bits - 1 - i)
            return jnp.where(count_tie_below(thr, cand) < need, cand, x)
        return lax.fori_loop(0, n_index_bits, step, jnp.zeros_like(thr))

    any_excess = jnp.max(jnp.where(excess, 1.0, 0.0)) > 0.0
    cut = lax.cond(any_excess, tie_search, lambda _: jnp.zeros_like(thr), 0)
    cut = jnp.where(excess, cut, jnp.where(take_all, jnp.int32(-1), jnp.int32(2 ** 31 - 1)))
    return thr, cut


def _selection_bias(key, idx, thr, cut):
    sel = (key > thr) | ((key == thr) & (idx <= cut))
    return pltpu.bitcast(jnp.where(sel, 0.0, NEG_BIG).astype(F32), I32)


def _lane_fold(m):
    acc = m[:, 0:LANES]
    for c in range(1, m.shape[1] // LANES):
        acc = acc + m[:, c * LANES:(c + 1) * LANES]
    return acc


def _row_count(s_ref, n_chunks, pred):
    rows = s_ref.shape[0]

    def body(c, acc):
        k0 = pl.multiple_of(c * KEY_CHUNK, KEY_CHUNK)
        blk = s_ref[:, pl.ds(k0, KEY_CHUNK)]
        idx = k0 + lax.broadcasted_iota(I32, blk.shape, 1)
        return acc + _lane_fold(jnp.where(pred(blk, idx), 1.0, 0.0))

    acc = lax.fori_loop(0, n_chunks, body, jnp.zeros((rows, LANES), F32))
    return jnp.sum(acc, axis=1, keepdims=True)


def _int_reduce(x, take_min, axis):
    red = jnp.min if take_min else jnp.max
    hi = (x >> 16).astype(F32)
    lo = (x & 0xFFFF).astype(F32)
    m_hi = red(hi, axis=axis, keepdims=True)
    m_lo = red(jnp.where(hi == m_hi, lo, 65536.0 if take_min else -1.0), axis=axis, keepdims=True)
    return (m_hi.astype(I32) << 16) | m_lo.astype(I32)


def _row_probe(s_ref, n_chunks, g, snap):
    rows = s_ref.shape[0]
    n_groups = KEY_CHUNK // LANES

    def body(c, carry):
        k0 = pl.multiple_of(c * KEY_CHUNK, KEY_CHUNK)
        blk = s_ref[:, pl.ds(k0, KEY_CHUNK)]
        ge = blk >= g
        out = [carry[0] + _lane_fold(jnp.where(ge, 1.0, 0.0))]
        if snap:
            above, below = jnp.where(ge, blk, 2 ** 31 - 1), jnp.where(ge, INT_MIN, blk)
            up, dn = carry[1], carry[2]
            for cc in range(n_groups):
                up = jnp.minimum(up, above[:, cc * LANES:(cc + 1) * LANES])
                dn = jnp.maximum(dn, below[:, cc * LANES:(cc + 1) * LANES])
            out += [up, dn]
        return tuple(out)

    init = [jnp.zeros((rows, LANES), F32)]
    if snap:
        init += [jnp.full((rows, LANES), 2 ** 31 - 1, I32), jnp.full((rows, LANES), INT_MIN, I32)]
    res = lax.fori_loop(0, n_chunks, body, tuple(init))
    c = jnp.sum(res[0], axis=1, keepdims=True)
    if not snap:
        return c, None, None
    return c, _int_reduce(res[1], True, 1), _int_reduce(res[2], False, 1)


def _row_select_to_bias(s_ref, n_chunks, n_real, topk):
    rows = s_ref.shape[0]
    inf = jnp.float32(jnp.inf)

    def min_max(c, carry):
        mn, mx = carry
        k0 = pl.multiple_of(c * KEY_CHUNK, KEY_CHUNK)
        blk = s_ref[:, pl.ds(k0, KEY_CHUNK)]
        v = _key_scores(blk)
        real = blk != INT_MIN
        for cc in range(KEY_CHUNK // LANES):
            sl = slice(cc * LANES, (cc + 1) * LANES)
            mn = jnp.minimum(mn, jnp.where(real[:, sl], v[:, sl], inf))
            mx = jnp.maximum(mx, jnp.where(real[:, sl], v[:, sl], -inf))
        return mn, mx

    mn, mx = lax.fori_loop(0, n_chunks, min_max, (jnp.full((rows, LANES), inf, F32), jnp.full((rows, LANES), -inf, F32)))
    vmin, vmax = jnp.min(mn, axis=1, keepdims=True), jnp.max(mx, axis=1, keepdims=True)
    thr, cut = _topk_threshold(
        functools.partial(_row_probe, s_ref, n_chunks),
        lambda t, x: _row_count(s_ref, n_chunks, lambda blk, idx: (blk == t) & (idx < x)),
        vmin, vmax, n_real, topk, (s_ref.shape[1] - 1).bit_length())

    def to_bias(c, _):
        k0 = pl.multiple_of(c * KEY_CHUNK, KEY_CHUNK)
        blk = s_ref[:, pl.ds(k0, KEY_CHUNK)]
        idx = k0 + lax.broadcasted_iota(I32, blk.shape, 1)
        s_ref[:, pl.ds(k0, KEY_CHUNK)] = _selection_bias(blk, idx, thr, cut)
        return 0

    lax.fori_loop(0, n_chunks, to_bias, 0)


def _attend_group(qg, segments, bias_ref, rep):
    n = qg.shape[0]

    def step(load_kv, c, carry):
        m, l, acc = carry
        k0 = pl.multiple_of(c * KEY_CHUNK, KEY_CHUNK)
        ktc, vtc = load_kv(k0)
        bias = pltpu.bitcast(bias_ref[:, pl.ds(k0, KEY_CHUNK)], F32)
        s = _dot(qg, ktc) + jnp.concatenate([bias] * rep, axis=0)
        m_new = jnp.maximum(m, jnp.max(s, axis=1, keepdims=True))
        alpha = jnp.exp2(m - m_new)
        p = jnp.exp2(s - m_new)
        l = alpha * l + jnp.sum(p, axis=1, keepdims=True)
        acc = alpha * acc + _dot_nt(p.astype(BF16), vtc)
        return m_new, l, acc

    carry = (jnp.full((n, 1), NEG_BIG, F32), jnp.zeros((n, 1), F32), jnp.zeros((n, LANES), F32))
    for first, end, load_kv in segments:
        carry = lax.fori_loop(first, end, functools.partial(step, load_kv), carry)
    _, l, acc = carry
    return acc / l


def _stack_heads(x, first, count, width=LANES):
    x = x.astype(F32)
    return jnp.concatenate([x[:, (first + hd) * LANES:(first + hd) * LANES + width] for hd in range(count)],
                           axis=0).astype(BF16)


def _attend_all_heads(q, segments, bias_ref, o_ref):
    rows = q.shape[0]
    group = ATT_HEADS // KV_HEADS
    out = _attend_group(_stack_heads(q, 0, ATT_HEADS), segments, bias_ref, ATT_HEADS)
    for hd in range(ATT_HEADS):
        piece = out[hd * rows:(hd + 1) * rows]
        piece = jnp.where(_group_lane_mask(piece.shape, hd // group), piece, 0.0)
        o_ref[0, :, hd * LANES:(hd + 1) * LANES] = piece.astype(o_ref.dtype)


def _indexer_weights(kiw_q):
    return kiw_q[:, IDX_DIM:IDX_DIM + IDX_HEADS] * (IDX_HEADS ** -0.5 * IDX_DIM ** -0.5)


def _group_lane_mask(shape, g):
    lane = lax.broadcasted_iota(I32, shape, 1)
    return (lane >= g * HEAD_DIM) & (lane < (g + 1) * HEAD_DIM)


ATT_CHUNK = 256


def _sublane_fold(m, op=jnp.add):
    parts = [m[r * 8:(r + 1) * 8] for r in range(m.shape[0] // 8)]
    while len(parts) > 1:
        parts = [op(parts[n], parts[n + 1]) for n in range(0, len(parts) - 1, 2)] + (parts[-1:] if len(parts) % 2 else [])
    return parts[0]


def _col_reduce(m, op, reduce_fn):
    return reduce_fn(_sublane_fold(m, op), axis=0, keepdims=True)


def _col_count(s_ref, n_chunks, pred):
    cols = s_ref.shape[1]

    def body(c, acc):
        k0 = pl.multiple_of(c * KEY_CHUNK, KEY_CHUNK)
        blk = s_ref[pl.ds(k0, KEY_CHUNK), :]
        idx = k0 + lax.broadcasted_iota(I32, blk.shape, 0)
        return acc + _sublane_fold(jnp.where(pred(blk, idx), 1.0, 0.0))

    acc = lax.fori_loop(0, n_chunks, body, jnp.zeros((8, cols), F32))
    return jnp.sum(acc, axis=0, keepdims=True)


def _col_probe(s_ref, n_chunks, g, snap):
    cols = s_ref.shape[1]

    def body(c, carry):
        k0 = pl.multiple_of(c * KEY_CHUNK, KEY_CHUNK)
        blk = s_ref[pl.ds(k0, KEY_CHUNK), :]
        ge = blk >= g
        out = [carry[0] + _sublane_fold(jnp.where(ge, 1.0, 0.0))]
        if snap:
            out.append(jnp.minimum(carry[1], _sublane_fold(jnp.where(ge, blk, 2 ** 31 - 1), jnp.minimum)))
            out.append(jnp.maximum(carry[2], _sublane_fold(jnp.where(ge, INT_MIN, blk), jnp.maximum)))
        return tuple(out)

    init = [jnp.zeros((8, cols), F32)]
    if snap:
        init += [jnp.full((8, cols), 2 ** 31 - 1, I32), jnp.full((8, cols), INT_MIN, I32)]
    res = lax.fori_loop(0, n_chunks, body, tuple(init))
    c = jnp.sum(res[0], axis=0, keepdims=True)
    if not snap:
        return c, None, None
    return c, _int_reduce(res[1], True, 0), _int_reduce(res[2], False, 0)


def _col_select_to_bias(s_ref, n_chunks, n_real, topk):
    cols = s_ref.shape[1]
    inf = jnp.float32(jnp.inf)

    def min_max(c, carry):
        mn, mx = carry
        k0 = pl.multiple_of(c * KEY_CHUNK, KEY_CHUNK)
        blk = s_ref[pl.ds(k0, KEY_CHUNK), :]
        v = _key_scores(blk)
        real = blk != INT_MIN
        mn = jnp.minimum(mn, _sublane_fold(jnp.where(real, v, inf), jnp.minimum))
        mx = jnp.maximum(mx, _sublane_fold(jnp.where(real, v, -inf), jnp.maximum))
        return mn, mx

    mn, mx = lax.fori_loop(0, n_chunks, min_max, (jnp.full((8, cols), inf, F32), jnp.full((8, cols), -inf, F32)))
    vmin, vmax = jnp.min(mn, axis=0, keepdims=True), jnp.max(mx, axis=0, keepdims=True)
    thr, cut = _topk_threshold(
        functools.partial(_col_probe, s_ref, n_chunks),
        lambda t, x: _col_count(s_ref, n_chunks, lambda blk, idx: (blk == t) & (idx < x)),
        vmin, vmax, n_real, topk, (s_ref.shape[0] - 1).bit_length())

    def to_bias(c, _):
        k0 = pl.multiple_of(c * KEY_CHUNK, KEY_CHUNK)
        blk = s_ref[pl.ds(k0, KEY_CHUNK), :]
        idx = k0 + lax.broadcasted_iota(I32, blk.shape, 0)
        s_ref[pl.ds(k0, KEY_CHUNK), :] = _selection_bias(blk, idx, thr, cut)
        return 0

    lax.fori_loop(0, n_chunks, to_bias, 0)


def _dsa_prompt_kernel(q_ref, qi_ref, kiwq_ref, k_ref, vt_ref, kiw_ref, o_ref, s_ref, m_scr, l_scr, acc_scr,
                       sa_scr, sb_scr, *, qb, topk):
    j = pl.program_id(1)
    n_keys = j * qb + qb
    n_chunks = (n_keys + KEY_CHUNK - 1) // KEY_CHUNK
    qpos = j * qb + lax.broadcasted_iota(I32, (1, qb), 1)
    wt = kiwq_ref[0].T[IDX_DIM:IDX_DIM + IDX_HEADS] * (IDX_HEADS ** -0.5 * IDX_DIM ** -0.5)

    def scores(c, _):
        k0 = pl.multiple_of(c * KEY_CHUNK, KEY_CHUNK)
        kc = kiw_ref[0, pl.ds(k0, KEY_CHUNK), :]
        acc = jnp.zeros((KEY_CHUNK, qb), F32)
        for hd in range(IDX_HEADS):
            s = _dot_nt(kc, qi_ref[0, :, hd * LANES:(hd + 1) * LANES])
            acc = acc + jnp.maximum(s, 0.0) * wt[hd:hd + 1]
        kpos = k0 + lax.broadcasted_iota(I32, (KEY_CHUNK, qb), 0)
        s_ref[pl.ds(k0, KEY_CHUNK), :] = jnp.where(kpos <= qpos, _score_keys(acc), INT_MIN)
        return 0

    lax.fori_loop(0, n_chunks, scores, 0)
    _col_select_to_bias(s_ref, n_chunks, (qpos + 1).astype(F32), topk)

    m_scr[...] = jnp.full(m_scr.shape, NEG_BIG, F32)
    l_scr[...] = jnp.zeros(l_scr.shape, F32)
    acc_scr[...] = jnp.zeros(acc_scr.shape, F32)

    n_att = n_chunks * (KEY_CHUNK // ATT_CHUNK)

    def logits(step, buf):
        k0 = pl.multiple_of(jnp.minimum(step, n_att - 1) * ATT_CHUNK, ATT_CHUNK)
        kc = k_ref[0, pl.ds(k0, ATT_CHUNK), :]
        bias = pltpu.bitcast(s_ref[pl.ds(k0, ATT_CHUNK), :], F32)
        for hd in range(ATT_HEADS):
            buf[hd] = _dot_nt(kc, q_ref[0, :, hd * LANES:(hd + 1) * LANES]) + bias

    def accumulate(step, buf):
        k0 = pl.multiple_of(step * ATT_CHUNK, ATT_CHUNK)
        vtc = vt_ref[:, pl.ds(k0, ATT_CHUNK)]
        for hd in range(ATT_HEADS):
            s = buf[hd]
            m_prev = m_scr[hd:hd + 1, :]
            m_new = jnp.maximum(m_prev, _col_reduce(s, jnp.maximum, jnp.max))
            alpha = jnp.exp2(m_prev - m_new)
            p = jnp.exp2(s - m_new)
            l_scr[hd:hd + 1, :] = alpha * l_scr[hd:hd + 1, :] + _col_reduce(p, jnp.add, jnp.sum)
            acc_scr[hd] = alpha * acc_scr[hd] + _dot(vtc, p.astype(BF16))
            m_scr[hd:hd + 1, :] = m_new

    logits(0, sa_scr)

    def attend(c, _):
        logits(2 * c + 1, sb_scr)
        accumulate(2 * c, sa_scr)
        logits(2 * c + 2, sa_scr)
        accumulate(2 * c + 1, sb_scr)
        return 0

    lax.fori_loop(0, n_att // 2, attend, 0)

    group = ATT_HEADS // KV_HEADS
    for hd in range(ATT_HEADS):
        out_t = acc_scr[hd] / l_scr[hd:hd + 1, :]
        row = lax.broadcasted_iota(I32, out_t.shape, 0)
        g = hd // group
        out_t = jnp.where((row >= g * HEAD_DIM) & (row < (g + 1) * HEAD_DIM), out_t, 0.0)
        o_ref[0, :, hd * LANES:(hd + 1) * LANES] = out_t.T.astype(BF16)


def _dsa_prompt(q, qi, kiw, kbf, vt, kiwbf, qb):
    b, t, _ = q.shape
    topk = min(TOPK_MAX, t // 4)
    qblk = lambda width: pl.BlockSpec((1, qb, width), lambda bi, j: (bi, j, 0))
    full = pl.BlockSpec((1, t, LANES), lambda bi, j: (bi, 0, 0))
    return pl.pallas_call(
        functools.partial(_dsa_prompt_kernel, qb=qb, topk=topk),
        grid=(b, t // qb),
        in_specs=[qblk(ATT_HEADS * LANES), qblk(IDX_HEADS * LANES), qblk(LANES), full,
                  pl.BlockSpec((LANES, t), lambda bi, j: (0, bi)), full],
        out_specs=qblk(ATT_HEADS * LANES),
        out_shape=jax.ShapeDtypeStruct((b, t, ATT_HEADS * LANES), BF16),
        scratch_shapes=[pltpu.VMEM((pl.cdiv(t, KEY_CHUNK) * KEY_CHUNK, qb), I32), pltpu.VMEM((ATT_HEADS, qb), F32),
                        pltpu.VMEM((ATT_HEADS, qb), F32), pltpu.VMEM((ATT_HEADS, LANES, qb), F32),
                        pltpu.VMEM((ATT_HEADS, ATT_CHUNK, qb), F32), pltpu.VMEM((ATT_HEADS, ATT_CHUNK, qb), F32)],
        compiler_params=_cparams(("parallel", "arbitrary")),
        name="dsa_prompt",
    )(q, qi, kiw, kbf, vt, kiwbf)


def _dsa_sample_kernel(pt_ref, q_ref, qi_ref, kiwq_ref, kn_ref, vn_ref, kiwn_ref, ck_hbm, cv_hbm, cki_hbm,
                       o_ref, kbuf, vbuf, kibuf, s_ref, sems, *, ts, n_pages, topk):
    b = pl.program_id(0)
    past = n_pages * PAGE_SIZE

    streams = ((cki_hbm, kibuf), (ck_hbm, kbuf), (cv_hbm, vbuf))

    def page_copy(p, which):
        src, dst = streams[which]
        cols = pl.ds(pl.multiple_of(p * PAGE_SIZE, PAGE_SIZE), PAGE_SIZE)
        return pltpu.make_async_copy(src.at[pt_ref[b, p]], dst.at[:, cols], sems.at[which])

    def start_page(p, _):
        for which in range(len(streams)):
            page_copy(p, which).start()
        return 0

    lax.fori_loop(0, n_pages, start_page, 0)

    def wait_pages(which):
        def body(p, _):
            page_copy(p, which).wait()
            return 0
        lax.fori_loop(0, n_pages, body, 0)

    n_past_chunks = past // KEY_CHUNK
    n_chunks = n_past_chunks + 1
    qpos = lax.broadcasted_iota(I32, (ts, 1), 0)
    wq = _indexer_weights(kiwq_ref[0])
    qi_stack = _stack_heads(qi_ref[0], 0, IDX_HEADS, IDX_DIM)
    w_stack = jnp.concatenate([wq[:, hd:hd + 1] for hd in range(IDX_HEADS)], axis=0)

    def head_sum(x):
        acc = x[0:ts]
        for hd in range(1, IDX_HEADS):
            acc = acc + x[hd * ts:(hd + 1) * ts]
        return acc

    wait_pages(0)

    def past_scores(c, _):
        k0 = pl.multiple_of(c * KEY_CHUNK, KEY_CHUNK)
        ktc = kibuf[:, pl.ds(k0, KEY_CHUNK)].astype(BF16)
        acc = head_sum(jnp.maximum(_dot(qi_stack, ktc), 0.0) * w_stack)
        s_ref[:, pl.ds(k0, KEY_CHUNK)] = _score_keys(acc + 0.0)
        return 0

    lax.fori_loop(0, n_past_chunks, past_scores, 0)
    acc = head_sum(jnp.maximum(_dot(qi_stack, kiwn_ref[0, 0:IDX_DIM, :]), 0.0) * w_stack)
    kpos = lax.broadcasted_iota(I32, (ts, KEY_CHUNK), 1)
    s_ref[:, pl.ds(past, KEY_CHUNK)] = jnp.where(kpos <= qpos, _score_keys(acc + 0.0), INT_MIN)
    _row_select_to_bias(s_ref, n_chunks, (past + 1 + qpos).astype(F32), topk)

    wait_pages(1)
    wait_pages(2)

    def load_past(k0):
        return kbuf[:, pl.ds(k0, KEY_CHUNK)].astype(BF16), vbuf[:, pl.ds(k0, KEY_CHUNK)].astype(BF16)

    def load_new(k0):
        return kn_ref[0], vn_ref[0]

    segments = [(0, n_past_chunks, load_past), (n_past_chunks, n_chunks, load_new)]
    _attend_all_heads(q_ref[0], segments, s_ref, o_ref)


def _dsa_sample(q, qi, kiw, knt, vnt, kiwnt, cache_kt, cache_vt, cache_kit, page_table):
    b, ts, _ = q.shape
    n_pages = page_table.shape[1]
    past = n_pages * PAGE_SIZE
    topk = min(TOPK_MAX, (past + ts) // 4)
    blk = lambda rows, width: pl.BlockSpec((1, rows, width), lambda bi, pt: (bi, 0, 0))
    hbm = pl.BlockSpec(memory_space=pl.ANY)
    grid_spec = pltpu.PrefetchScalarGridSpec(
        num_scalar_prefetch=1,
        grid=(b,),
        in_specs=[blk(ts, ATT_HEADS * LANES), blk(ts, IDX_HEADS * LANES), blk(ts, LANES),
                  blk(LANES, KEY_CHUNK), blk(LANES, KEY_CHUNK), blk(LANES, KEY_CHUNK), hbm, hbm, hbm],
        out_specs=blk(ts, ATT_HEADS * LANES),
        scratch_shapes=[pltpu.VMEM((LANES, past), F32), pltpu.VMEM((LANES, past), F32),
                        pltpu.VMEM((IDX_DIM, past), F32), pltpu.VMEM((ts, past + KEY_CHUNK), I32),
                        pltpu.SemaphoreType.DMA((3,))],
    )
    return pl.pallas_call(
        functools.partial(_dsa_sample_kernel, ts=ts, n_pages=n_pages, topk=topk),
        grid_spec=grid_spec,
        out_shape=jax.ShapeDtypeStruct((b, ts, ATT_HEADS * LANES), F32),
        compiler_params=_cparams(("arbitrary",)),
        name="dsa_sample",
    )(page_table, q, qi, kiw, knt, vnt, kiwnt, cache_kt, cache_vt, cache_kit)


def _conv_kernel(u_ref, halo_ref, buf_ref, bg_ref, w_ref, ya_ref):
    i = pl.program_id(1)
    u = u_ref[0]
    halo, buf = halo_ref[0], buf_ref[0]
    first = i == 0
    prev1 = jnp.where(first, buf[1:2], halo[7:8])
    prev2 = jnp.where(first, buf[0:1], halo[6:7])
    row = lax.broadcasted_iota(I32, u.shape, 0)
    um1 = jnp.where(row == 0, prev1, pltpu.roll(u, 1, 0))
    um2 = jnp.where(row == 0, prev2, jnp.where(row == 1, prev1, pltpu.roll(u, 2, 0)))
    w = w_ref[...]
    conv = w[0:1] * um2 + w[1:2] * um1 + w[2:3] * u
    ya_ref[0] = (bg_ref[0] * conv).astype(BF16)


def _conv(u, bg, buf, w, tt):
    b, t, c = u.shape
    halo_rows = 8
    tile = pl.BlockSpec((1, tt, c), lambda bi, i: (bi, i, 0))
    halo = pl.BlockSpec((1, halo_rows, c), lambda bi, i: (bi, jnp.maximum(i * (tt // halo_rows) - 1, 0), 0))
    return pl.pallas_call(
        _conv_kernel,
        grid=(b, t // tt),
        in_specs=[tile, halo, pl.BlockSpec((1, CONV_WIDTH - 1, c), lambda bi, i: (bi, 0, 0)), tile,
                  pl.BlockSpec((CONV_WIDTH, c), lambda bi, i: (0, 0))],
        out_specs=tile,
        out_shape=jax.ShapeDtypeStruct((b, t, c), BF16),
        compiler_params=_cparams(("parallel", "parallel")),
        name="short_conv",
    )(u, u, buf, bg, w)


def _mm_res_kernel(*refs, n_in):
    a_refs, w_ref, x_ref, o_ref = refs[:n_in], refs[n_in], refs[n_in + 1], refs[n_in + 2]
    a = jnp.concatenate([r[...] for r in a_refs], axis=1) if n_in > 1 else a_refs[0][...]
    o_ref[...] = x_ref[...] + _dot(a, w_ref[...])


def _mm_res(a_list, w, x, tm):
    m = x.shape[0]
    row = lambda i: (i, 0)
    return pl.pallas_call(
        functools.partial(_mm_res_kernel, n_in=len(a_list)),
        grid=(m // tm,),
        in_specs=[pl.BlockSpec((tm, a.shape[1]), row) for a in a_list]
        + [pl.BlockSpec(w.shape, lambda i: (0, 0)), pl.BlockSpec((tm, D_MODEL), row)],
        out_specs=pl.BlockSpec((tm, D_MODEL), row),
        out_shape=jax.ShapeDtypeStruct((m, D_MODEL), F32),
        compiler_params=_cparams(("parallel",)),
        name="matmul_residual",
    )(*a_list, w, x)


def _norm_mm_kernel(x_ref, g_ref, w_ref, o_ref, h_scr):
    @pl.when(pl.program_id(1) == 0)
    def _():
        h_scr[...] = _rms(x_ref[...], g_ref[...]).astype(BF16)

    o_ref[...] = _dot(h_scr[...], w_ref[...])


def _norm_mm(x, g, w, tm, tn):
    m, n = x.shape[0], w.shape[1]
    return pl.pallas_call(
        _norm_mm_kernel,
        grid=(m // tm, n // tn),
        in_specs=[pl.BlockSpec((tm, D_MODEL), lambda i, j: (i, 0)), pl.BlockSpec((1, D_MODEL), lambda i, j: (0, 0)),
                  pl.BlockSpec((D_MODEL, tn), lambda i, j: (0, j))],
        out_specs=pl.BlockSpec((tm, tn), lambda i, j: (i, j)),
        out_shape=jax.ShapeDtypeStruct((m, n), F32),
        scratch_shapes=[pltpu.VMEM((tm, D_MODEL), BF16)],
        compiler_params=_cparams(("parallel", "arbitrary")),
        name="norm_matmul",
    )(x, g, w)


def _ffn_kernel(x_ref, g_ref, wg_ref, wu_ref, wd_ref, o_ref, h_scr, acc):
    f = pl.program_id(1)

    @pl.when(f == 0)
    def _():
        x = x_ref[...]
        h_scr[...] = _rms(x, g_ref[...]).astype(BF16)
        acc[...] = x

    h = h_scr[...]
    a = jax.nn.silu(_dot(h, wg_ref[...])) * _dot(h, wu_ref[...])
    acc[...] += _dot(a.astype(BF16), wd_ref[...])

    @pl.when(f == pl.num_programs(1) - 1)
    def _():
        o_ref[...] = acc[...]


def _ffn(x, g, wg, wu, wd, tm, tf):
    m, ff = x.shape[0], wg.shape[1]
    row = lambda i, f: (i, 0)
    return pl.pallas_call(
        _ffn_kernel,
        grid=(m // tm, ff // tf),
        in_specs=[pl.BlockSpec((tm, D_MODEL), row), pl.BlockSpec((1, D_MODEL), lambda i, f: (0, 0)),
                  pl.BlockSpec((D_MODEL, tf), lambda i, f: (0, f)), pl.BlockSpec((D_MODEL, tf), lambda i, f: (0, f)),
                  pl.BlockSpec((tf, D_MODEL), lambda i, f: (f, 0))],
        out_specs=pl.BlockSpec((tm, D_MODEL), row),
        out_shape=jax.ShapeDtypeStruct((m, D_MODEL), F32),
        scratch_shapes=[pltpu.VMEM((tm, D_MODEL), BF16), pltpu.VMEM((tm, D_MODEL), F32)],
        compiler_params=_cparams(("parallel", "arbitrary")),
        name="dense_swiglu",
    )(x, g, wg, wu, wd)


def _top2_gates(logits):
    lane = lax.broadcasted_iota(I32, logits.shape, 1).astype(F32)
    neg = jnp.float32(-jnp.inf)
    l1 = jnp.where(lane < N_EXPERTS, logits, neg)
    m1 = jnp.max(l1, axis=1, keepdims=True)
    i1 = jnp.min(jnp.where(l1 == m1, lane, float(LANES)), axis=1, keepdims=True)
    l2 = jnp.where(lane == i1, neg, l1)
    m2 = jnp.max(l2, axis=1, keepdims=True)
    i2 = jnp.min(jnp.where(l2 == m2, lane, float(LANES)), axis=1, keepdims=True)
    e = jnp.exp(m2 - m1)
    w1 = 1.0 / (1.0 + e)
    w2 = e / (1.0 + e)
    first, second = lane == i1, lane == i2
    return jnp.where(first, w1, jnp.where(second, w2, 0.0)), jnp.where(first | second, 1.0, 0.0)


MOE_SUB = 128
MOE_TOKEN_TILE = 1024


def _moe_route_kernel(x_ref, g_ref, rhi_ref, rlo_ref, h_ref, gate_ref, posc_ref, posr_ref, cnt_ref):
    tm = x_ref.shape[0]
    hn = _rms(x_ref[...], g_ref[...])
    h_hi = hn.astype(BF16)
    h_lo = (hn - h_hi.astype(F32)).astype(BF16)
    logits = _dot(h_hi, rhi_ref[...]) + (_dot(h_lo, rhi_ref[...]) + _dot(h_hi, rlo_ref[...]))
    gate, routed = _top2_gates(logits)
    h_ref[...] = h_hi
    gate_ref[...] = gate
    earlier = (lax.broadcasted_iota(I32, (tm, tm), 0) > lax.broadcasted_iota(I32, (tm, tm), 1))
    slot = _dot(jnp.where(earlier, 1.0, 0.0).astype(BF16), routed.astype(BF16))
    posc = jnp.where(routed > 0.0, slot, -1.0)
    posc_ref[...] = posc
    posr_ref[0] = posc.T[0:N_EXPERTS]
    cnt_ref[0] = jnp.broadcast_to(jnp.sum(routed, axis=0, keepdims=True), (8, LANES))


def _moe_expert_kernel(cnt_ref, x_ref, h_ref, gate_ref, posc_ref, posr_ref, wg_ref, wu_ref, wd_ref, o_ref, xg, yacc):
    i, e, f = pl.program_id(0), pl.program_id(1), pl.program_id(2)
    tm = x_ref.shape[0]
    n_sub = (cnt_ref[i * N_EXPERTS + e] + MOE_SUB - 1) // MOE_SUB

    @pl.when((e == 0) & (f == 0))
    def _():
        o_ref[...] = x_ref[...]

    @pl.when(f == 0)
    def _():
        posr = posr_ref[0]
        h = h_ref[...]

        def gather(s, _):
            base = pl.multiple_of(s * MOE_SUB, MOE_SUB)
            slot = (base + lax.broadcasted_iota(I32, (MOE_SUB, tm), 0)).astype(F32)
            onehot = jnp.where(posr == slot, 1.0, 0.0).astype(BF16)
            xg[pl.ds(base, MOE_SUB), :] = _dot(onehot, h).astype(BF16)
            yacc[pl.ds(base, MOE_SUB), :] = jnp.zeros((MOE_SUB, D_MODEL), F32)
            return 0

        lax.fori_loop(0, n_sub, gather, 0)

    def expert(s, _):
        rows = pl.ds(pl.multiple_of(s * MOE_SUB, MOE_SUB), MOE_SUB)
        xs = xg[rows, :]
        a = jax.nn.silu(_dot(xs, wg_ref[0])) * _dot(xs, wu_ref[0])
        yacc[rows, :] += _dot(a.astype(BF16), wd_ref[0])
        return 0

    lax.fori_loop(0, n_sub, expert, 0)

    @pl.when(f == pl.num_programs(2) - 1)
    def _():
        lane = lax.broadcasted_iota(I32, (tm, LANES), 1)
        mine = lane == e
        posc = jnp.sum(jnp.where(mine, posc_ref[...], 0.0), axis=1, keepdims=True)
        gate = jnp.sum(jnp.where(mine, gate_ref[...], 0.0), axis=1, keepdims=True)

        def scatter(s, _):
            base = pl.multiple_of(s * MOE_SUB, MOE_SUB)
            slot = (base + lax.broadcasted_iota(I32, (tm, MOE_SUB), 1)).astype(F32)
            onehot = jnp.where(posc == slot, 1.0, 0.0).astype(BF16)
            o_ref[...] += gate * _dot(onehot, yacc[pl.ds(base, MOE_SUB), :].astype(BF16))
            return 0

        lax.fori_loop(0, n_sub, scatter, 0)


def _moe(x, g, r_hi, r_lo, wg, wu, wd, tm, tf):
    m = x.shape[0]
    nt = m // tm
    row = lambda i: (i, 0)
    const = lambda i: (0, 0)
    h, gate, posc, posr, cnt = pl.pallas_call(
        _moe_route_kernel,
        grid=(nt,),
        in_specs=[pl.BlockSpec((tm, D_MODEL), row), pl.BlockSpec((1, D_MODEL), const),
                  pl.BlockSpec((D_MODEL, LANES), const), pl.BlockSpec((D_MODEL, LANES), const)],
        out_specs=[pl.BlockSpec((tm, D_MODEL), row), pl.BlockSpec((tm, LANES), row), pl.BlockSpec((tm, LANES), row),
                   pl.BlockSpec((1, N_EXPERTS, tm), lambda i: (i, 0, 0)), pl.BlockSpec((1, 8, LANES), lambda i: (i, 0, 0))],
        out_shape=[jax.ShapeDtypeStruct((m, D_MODEL), BF16), jax.ShapeDtypeStruct((m, LANES), F32),
                   jax.ShapeDtypeStruct((m, LANES), F32), jax.ShapeDtypeStruct((nt, N_EXPERTS, tm), F32),
                   jax.ShapeDtypeStruct((nt, 8, LANES), F32)],
        compiler_params=_cparams(("parallel",)),
        name="moe_route",
    )(x, g, r_hi, r_lo)
    counts = cnt[:, 0, :N_EXPERTS].astype(I32).reshape(nt * N_EXPERTS)
    posr = posr.reshape(nt * N_EXPERTS, 1, tm)
    row3 = lambda i, e, f, c: (i, 0)
    slot_rows = pl.cdiv(tm, MOE_SUB) * MOE_SUB
    grid_spec = pltpu.PrefetchScalarGridSpec(
        num_scalar_prefetch=1,
        grid=(nt, N_EXPERTS, EXPERT_FF // tf),
        in_specs=[pl.BlockSpec((tm, D_MODEL), row3), pl.BlockSpec((tm, D_MODEL), row3),
                  pl.BlockSpec((tm, LANES), row3), pl.BlockSpec((tm, LANES), row3),
                  pl.BlockSpec((1, 1, tm), lambda i, e, f, c: (i * N_EXPERTS + e, 0, 0)),
                  pl.BlockSpec((1, D_MODEL, tf), lambda i, e, f, c: (e, 0, f)),
                  pl.BlockSpec((1, D_MODEL, tf), lambda i, e, f, c: (e, 0, f)),
                  pl.BlockSpec((1, tf, D_MODEL), lambda i, e, f, c: (e, f, 0))],
        out_specs=pl.BlockSpec((tm, D_MODEL), row3),
        scratch_shapes=[pltpu.VMEM((slot_rows, D_MODEL), BF16), pltpu.VMEM((slot_rows, D_MODEL), F32)],
    )
    return pl.pallas_call(
        _moe_expert_kernel,
        grid_spec=grid_spec,
        out_shape=jax.ShapeDtypeStruct((m, D_MODEL), F32),
        compiler_params=_cparams(("parallel", "arbitrary", "arbitrary")),
        name="moe_experts",
    )(counts, x, h, gate, posc, posr, wg, wu, wd)


def _ple_kernel(x_ref, g_ref, p_ref, wp_ref, wgate_ref, gf_ref, o_ref, *, final_norm):
    x = x_ref[...]
    hp = _rms(x, g_ref[...]).astype(BF16)
    gate = jax.nn.sigmoid(_dot(hp, wgate_ref[...]))
    y = x + _dot(p_ref[...].astype(BF16), wp_ref[...]) * gate
    if final_norm:
        y = _rms(y, gf_ref[...])
    o_ref[...] = y


def _ple(x, g, p, wp, wgate, g_final, final_norm, tm):
    m = x.shape[0]
    row = lambda i: (i, 0)
    const = lambda i: (0, 0)
    return pl.pallas_call(
        functools.partial(_ple_kernel, final_norm=final_norm),
        grid=(m // tm,),
        in_specs=[pl.BlockSpec((tm, D_MODEL), row), pl.BlockSpec((1, D_MODEL), const), pl.BlockSpec((tm, PLE_DIM), row),
                  pl.BlockSpec((PLE_DIM, D_MODEL), const), pl.BlockSpec((D_MODEL, D_MODEL), const),
                  pl.BlockSpec((1, D_MODEL), const)],
        out_specs=pl.BlockSpec((tm, D_MODEL), row),
        out_shape=jax.ShapeDtypeStruct((m, D_MODEL), F32),
        compiler_params=_cparams(("parallel",)),
        name="per_layer_embedding",
    )(x, g, p, wp, wgate, g_final)


def _ret_kernel(q_ref, k_ref, v_ref, gate_ref, cos_ref, sin_ref, s0_ref, o_ref, sout_ref, state, *, chunk, chunk_rows):
    i = pl.program_id(1)

    @pl.when(i == 0)
    def _():
        state[...] = s0_ref[0]

    cos, sin = cos_ref[...], sin_ref[...]
    tt = q_ref.shape[1]
    r = chunk_rows
    ii = lax.broadcasted_iota(I32, (r, r), 0).astype(F32)
    jj = lax.broadcasted_iota(I32, (r, r), 1).astype(F32)
    rel = ii - jj
    icol = lax.broadcasted_iota(I32, (r, 1), 0).astype(F32)
    half = RET_DK // 2

    def rot(ref, hd):
        x1 = ref[0, :, hd * RET_DK:hd * RET_DK + half]
        x2 = ref[0, :, hd * RET_DK + half:(hd + 1) * RET_DK]
        return jnp.concatenate([x1 * cos - x2 * sin, x2 * cos + x1 * sin], axis=1)

    for hd in range(RET_HEADS):
        lg = math.log(1.0 - 2.0 ** (-5.0 - hd))
        d_in = jnp.where(rel >= 0, jnp.exp(lg * jnp.maximum(rel, 0.0)), 0.0)
        d_q = jnp.exp(lg * (icol + 1.0))
        d_k = jnp.exp(lg * (chunk - 1.0 - icol)) * (RET_DK ** -0.5)
        d_c = math.exp(lg * chunk)
        qr = rot(q_ref, hd)
        kr = rot(k_ref, hd)
        vsl = slice(hd * RET_DV, (hd + 1) * RET_DV)
        for c in range(tt // r):
            rows = slice(c * r, (c + 1) * r)
            qc = qr[rows].astype(BF16)
            kc = kr[rows]
            vc = v_ref[0, rows, vsl].astype(BF16)
            s_prev = state[hd]
            att = _dot_nt(qc, (kc * (RET_DK ** -0.5)).astype(BF16)) * d_in
            o = _dot(att.astype(BF16), vc) + _dot(qc, s_prev.astype(BF16)) * d_q
            state[hd] = s_prev * d_c + _dot_tn((kc * d_k).astype(BF16), vc)
            mu = jnp.mean(o, axis=-1, keepdims=True)
            var = jnp.mean(jnp.square(o - mu), axis=-1, keepdims=True)
            on = (o - mu) * lax.rsqrt(var + EPS)
            o_ref[0, rows, vsl] = (jax.nn.silu(gate_ref[0, rows, vsl]) * on).astype(BF16)

    @pl.when(i == pl.num_programs(1) - 1)
    def _():
        sout_ref[0] = state[...]


def _retention(z, s0, cos, sin, tt, chunk, chunk_rows):
    b, t, _ = z.shape
    hk, hv = RET_HEADS * RET_DK, RET_HEADS * RET_DV
    half = RET_DK // 2
    state_spec = pl.BlockSpec((1, RET_HEADS, RET_DK, RET_DV), lambda bi, i: (bi, 0, 0, 0))
    tab = pl.BlockSpec((tt, half), lambda bi, i: (i, 0))
    return pl.pallas_call(
        functools.partial(_ret_kernel, chunk=chunk, chunk_rows=chunk_rows),
        grid=(b, t // tt),
        in_specs=[pl.BlockSpec((1, tt, hk), lambda bi, i: (bi, i, 0)), pl.BlockSpec((1, tt, hk), lambda bi, i: (bi, i, 1)),
                  pl.BlockSpec((1, tt, hv), lambda bi, i: (bi, i, 1)), pl.BlockSpec((1, tt, hv), lambda bi, i: (bi, i, 2)),
                  tab, tab, state_spec],
        out_specs=[pl.BlockSpec((1, tt, hv), lambda bi, i: (bi, i, 0)), state_spec],
        out_shape=[jax.ShapeDtypeStruct((b, t, hv), BF16), jax.ShapeDtypeStruct(s0.shape, F32)],
        scratch_shapes=[pltpu.VMEM((RET_HEADS, RET_DK, RET_DV), F32)],
        compiler_params=_cparams(("parallel", "arbitrary")),
        name="retention",
    )(z, z, z, z, cos, sin, s0)


def _pack_l0_w_in(w):
    offs = np.cumsum((0,) + AB_SPLITS)
    bg, cg, hv, q, k, v, qi, ki, wi = [w[:, offs[n]:offs[n + 1]] for n in range(len(AB_SPLITS))]
    group = ATT_HEADS // KV_HEADS
    q4 = q.reshape(D_MODEL, ATT_HEADS, HEAD_DIM)
    zq = jnp.zeros_like(q4)
    q_pad = jnp.concatenate([jnp.concatenate([q4[:, :group], zq[:, :group]], axis=-1),
                             jnp.concatenate([zq[:, group:], q4[:, group:]], axis=-1)], axis=1).reshape(D_MODEL, -1)
    qi4 = qi.reshape(D_MODEL, IDX_HEADS, IDX_DIM)
    qi_pad = jnp.concatenate([qi4, jnp.zeros_like(qi4)], axis=-1).reshape(D_MODEL, -1)
    kiw = jnp.concatenate([ki, wi, jnp.zeros((D_MODEL, LANES - IDX_DIM - IDX_HEADS), w.dtype)], axis=1)
    return jnp.concatenate([bg, cg, hv, q_pad, k, v, qi_pad, kiw], axis=1).astype(BF16)


def _pack_l0_w_out(w):
    group = ATT_HEADS // KV_HEADS
    wa, wb = w[:CONV_CH], w[CONV_CH:].reshape(ATT_HEADS, HEAD_DIM, D_MODEL)
    zb = jnp.zeros_like(wb)
    wb_pad = jnp.concatenate([jnp.concatenate([wb[:group], zb[:group]], axis=1),
                              jnp.concatenate([zb[group:], wb[group:]], axis=1)], axis=0).reshape(-1, D_MODEL)
    return jnp.concatenate([wa, wb_pad], axis=0).astype(BF16)


def _rope_tables(pos, reps):
    inv = ROPE_THETA ** (-jnp.arange(0, HEAD_DIM, 2, dtype=F32) / HEAD_DIM)
    ang = pos.astype(F32)[:, None] * inv[None, :]
    cos, sin = jnp.cos(ang), jnp.sin(ang)
    cos64 = jnp.concatenate([cos, cos], axis=1)
    sin64 = jnp.concatenate([-sin, sin], axis=1)
    one, zero = jnp.ones_like(cos64), jnp.zeros_like(cos64)
    tabs = (jnp.concatenate([cos64, cos64], 1), jnp.concatenate([sin64, sin64], 1),
            jnp.concatenate([cos64, one], 1), jnp.concatenate([sin64, zero], 1))
    return tuple(jnp.tile(tb, (reps, 1)) for tb in tabs)


def _ret_tables(pos):
    inv = ROPE_THETA ** (-jnp.linspace(0.0, 1.0, RET_DK // 2, dtype=F32))
    ang = pos.astype(F32)[:, None] * inv[None, :]
    return jnp.cos(ang), jnp.sin(ang)


def _pack_params(prm):
    r = prm['moe_router'][0]
    r_pad = jnp.concatenate([r, jnp.zeros((D_MODEL, LANES - N_EXPERTS), F32)], axis=1)
    r_hi = r_pad.astype(BF16)
    bf = lambda a: a.astype(BF16)
    return dict(
        l0_w_in=_pack_l0_w_in(prm['ab_w_in'][0]), l0_w_out=_pack_l0_w_out(prm['ab_w_out'][0]),
        l0_w_vt=bf(prm['ab_w_in'][0][:, sum(AB_SPLITS[:5]):sum(AB_SPLITS[:6])].T),
        conv_w=prm['ab_conv_w'][0],
        ffn=(bf(prm['ffn_w_gate'][0]), bf(prm['ffn_w_up'][0]), bf(prm['ffn_w_down'][0])),
        ret_w_in=bf(prm['ret_w_in'][0]), ret_w_out=bf(prm['ret_w_out'][0]),
        r_hi=r_hi, r_lo=(r_pad - r_hi.astype(F32)).astype(BF16),
        moe=(bf(prm['moe_w_gate'][0]), bf(prm['moe_w_up'][0]), bf(prm['moe_w_down'][0])),
        ple_w=bf(prm['ple_w']), ple_gate_w=bf(prm['ple_gate_w']),
        norm_mix=prm['norm_mix'][:, None, :], norm_ffn=prm['norm_ffn'][:, None, :],
        norm_ple=prm['norm_ple'][:, None, :], norm_final=prm['norm_final'][None, :],
    )


def _trunk(x, p, pos, conv_buf, ret_state, pk, paged):
    b, t, _ = x.shape
    m = b * t
    tm = min(m, TOKEN_TILE)
    x2 = x.reshape(m, D_MODEL)

    reps = max(1, tm // t)
    tabs = _rope_tables(pos, reps)
    bg, u, q, k, v, qi, kiw, kbf, vbf, kiwbf, vt = _l0_in_proj(x2, pk['norm_mix'][0], pk['l0_w_in'], pk['l0_w_vt'], tabs,
                                                              tm, tabs[0].shape[0] // tm)
    seq = lambda a: a.reshape(b, t, a.shape[-1])
    if paged is None:
        yb = _dsa_prompt(seq(q), seq(qi), seq(kiw), seq(kbf), vt, seq(kiwbf), qb=min(t, 128))
    else:
        cache_k, cache_v, cache_ki, page_table = paged
        padt = lambda a: jnp.swapaxes(jnp.pad(seq(a), ((0, 0), (0, KEY_CHUNK - t), (0, 0))), 1, 2)
        n_pool = cache_k.shape[0]
        pages_t = lambda c: jnp.transpose(c, (0, 2, 3, 1)).reshape(n_pool, LANES, PAGE_SIZE)
        yb = _dsa_sample(seq(q).astype(F32), seq(qi).astype(F32), seq(kiw), padt(kbf), padt(vbf), padt(kiwbf),
                         pages_t(cache_k), pages_t(cache_v), jnp.swapaxes(cache_ki, 1, 2), page_table).astype(BF16)
    u3 = seq(u)
    ya = _conv(u3, seq(bg), conv_buf, pk['conv_w'], tt=min(t, TOKEN_TILE))
    x2 = _mm_res([ya.reshape(m, CONV_CH), yb.reshape(m, ATT_HEADS * LANES)], pk['l0_w_out'], x2, tm)
    x2 = _ffn(x2, pk['norm_ffn'][0], *pk['ffn'], tm=tm, tf=D_FF // 2)
    x2 = _ple(x2, pk['norm_ple'][0], p[0].reshape(m, PLE_DIM), pk['ple_w'][0], pk['ple_gate_w'][0],
              pk['norm_final'], False, tm)
    new_k = k.reshape(1, b, t, KV_HEADS, HEAD_DIM)
    new_v = v.reshape(1, b, t, KV_HEADS, HEAD_DIM)
    new_ki = seq(kiw)[None, :, :, :IDX_DIM]
    new_conv = jnp.concatenate([conv_buf, u3], axis=1)[None, :, -(CONV_WIDTH - 1):]

    z = _norm_mm(x2, pk['norm_mix'][1], pk['ret_w_in'], tm, 1024).reshape(b, t, -1)
    cos_r, sin_r = _ret_tables(pos)
    if t % RET_CHUNK == 0:
        og, s_new = _retention(z, ret_state, cos_r, sin_r, tt=2 * RET_CHUNK, chunk=RET_CHUNK, chunk_rows=RET_CHUNK)
    else:
        rows = 16
        padt = lambda a: jnp.pad(a, ((0, 0),) * (a.ndim - 2) + ((0, rows - t), (0, 0)))
        og, s_new = _retention(padt(z), ret_state, padt(cos_r), padt(sin_r), tt=rows, chunk=t, chunk_rows=rows)
        og = og[:, :t]
    x2 = _mm_res([og.reshape(m, RET_HEADS * RET_DV)], pk['ret_w_out'], x2, tm)
    x2 = _moe(x2, pk['norm_ffn'][1], pk['r_hi'], pk['r_lo'], *pk['moe'], tm=min(m, MOE_TOKEN_TILE), tf=EXPERT_FF // 4)
    x2 = _ple(x2, pk['norm_ple'][1], p[1].reshape(m, PLE_DIM), pk['ple_w'][1], pk['ple_gate_w'][1],
              pk['norm_final'], True, tm)
    return x2.reshape(b, t, D_MODEL), new_k, new_v, new_ki, new_conv, s_new[None]


def kernel(x_prompt, x_sample, cache_k, cache_v, cache_kidx, state_conv, state_ret, page_table, p_prompt, p_sample,
           norm_mix, norm_ffn, norm_ple, norm_final, ab_w_in, ab_conv_w, ab_w_out, ffn_w_gate, ffn_w_up, ffn_w_down,
           ret_w_in, ret_w_out, moe_router, moe_w_gate, moe_w_up, moe_w_down, ple_w, ple_gate_w):
    prm = dict(norm_mix=norm_mix, norm_ffn=norm_ffn, norm_ple=norm_ple, norm_final=norm_final, ab_w_in=ab_w_in,
               ab_conv_w=ab_conv_w, ab_w_out=ab_w_out, ffn_w_gate=ffn_w_gate, ffn_w_up=ffn_w_up, ffn_w_down=ffn_w_down,
               ret_w_in=ret_w_in, ret_w_out=ret_w_out, moe_router=moe_router, moe_w_gate=moe_w_gate,
               moe_w_up=moe_w_up, moe_w_down=moe_w_down, ple_w=ple_w, ple_gate_w=ple_gate_w)
    pk = _pack_params(prm)
    b, t = x_prompt.shape[0], x_prompt.shape[1]
    db, ts = x_sample.shape[0], x_sample.shape[1]
    past_len = page_table.shape[1] * PAGE_SIZE
    dt = x_prompt.dtype

    conv0 = jnp.zeros((b, CONV_WIDTH - 1, CONV_CH), dt)
    ret0 = jnp.zeros((b, RET_HEADS, RET_DK, RET_DV), dt)
    y_p, k_p, v_p, ki_p, cb_p, rs_p = _trunk(x_prompt, p_prompt, jnp.arange(t, dtype=I32), conv0, ret0, pk, None)

    pos_s = past_len + jnp.arange(ts, dtype=I32)
    paged = (cache_k[0], cache_v[0], cache_kidx[0], page_table)
    y_s, k_s, v_s, ki_s, cb_s, rs_s = _trunk(x_sample, p_sample, pos_s, state_conv[0], state_ret[0], pk, paged)
    return (y_p, y_s, k_p, v_p, ki_p, cb_p, rs_p, k_s, v_s, ki_s, cb_s, rs_s)
```

```python
import functools
import math

import jax
import jax.numpy as jnp
import numpy as np
from jax import lax
from jax.experimental import pallas as pl
from jax.experimental.pallas import tpu as pltpu

F32 = jnp.float32
BF16 = jnp.bfloat16
I32 = jnp.int32

D_MODEL = 1024
PAGE_SIZE = 128
CONV_CH = D_MODEL // 2
CONV_WIDTH = 3
ATT_HEADS = 8
KV_HEADS = 2
HEAD_DIM = 64
IDX_HEADS = 4
IDX_DIM = 64
TOPK_MAX = 256
ROPE_THETA = 10000.0
RET_HEADS = 4
RET_DK = D_MODEL // RET_HEADS
RET_DV = 2 * RET_DK
RET_CHUNK = 128
D_FF = 2816
N_EXPERTS = 8
EXPERT_FF = 3584
PLE_DIM = 256
EPS = 1e-6
AB_SPLITS = (CONV_CH, CONV_CH, CONV_CH, ATT_HEADS * HEAD_DIM, KV_HEADS * HEAD_DIM, KV_HEADS * HEAD_DIM,
             IDX_HEADS * IDX_DIM, IDX_DIM, IDX_HEADS)

LANES = 128
VMEM_LIMIT = 48 * 1024 * 1024
INT_MIN = -2 ** 31
NEG_BIG = -1e30
LOG2_E = math.log2(math.e)

C_BG, C_CG, C_HV, C_Q, C_K, C_V, C_QI, C_KIW, C_END = 0, 512, 1024, 1536, 2560, 2688, 2816, 3328, 3456
KEY_CHUNK = 512
VT_ROWS = LANES + 16
TOKEN_TILE = 512
QUERY_BLOCK = 128


def _cparams(sem):
    return pltpu.CompilerParams(dimension_semantics=sem, vmem_limit_bytes=VMEM_LIMIT)


def _rms(x, g):
    return x * lax.rsqrt(jnp.mean(x * x, axis=-1, keepdims=True) + EPS) * g


def _dot(a, b):
    return jnp.dot(a, b, preferred_element_type=F32)


def _dot_nt(a, b):
    return lax.dot_general(a, b, (((1,), (1,)), ((), ())), preferred_element_type=F32)


def _dot_tn(a, b):
    return lax.dot_general(a, b, (((0,), (0,)), ((), ())), preferred_element_type=F32)


def _swap_halves64(x):
    lane = lax.broadcasted_iota(I32, x.shape, 1)
    from_above = pltpu.roll(x, LANES - 32, 1)
    from_below = pltpu.roll(x, 32, 1)
    return jnp.where((lane & 63) < 32, from_above, from_below)


def _l0_in_kernel(x_ref, g_ref, w_ref, wvt_ref, cos_ref, sin_ref, cosb_ref, sinb_ref,
                  bg_ref, u_ref, q_ref, k_ref, v_ref, qi_ref, kiw_ref, kbf_ref, vbf_ref, kiwbf_ref, vt_ref, *, head_major):
    h = _rms(x_ref[...], g_ref[...]).astype(BF16)
    cos, sin = cos_ref[...], sin_ref[...]

    def store_head(ref, hd, val):
        if head_major:
            for r in range(ref.shape[0]):
                ref[r, hd] = val[r * QUERY_BLOCK:(r + 1) * QUERY_BLOCK]
        else:
            ref[:, hd * LANES:(hd + 1) * LANES] = val

    def rot(z, c, s):
        return z * c + _swap_halves64(z) * s

    bg_ref[...] = _dot(h, w_ref[:, C_BG:C_CG])
    u_ref[...] = _dot(h, w_ref[:, C_CG:C_HV]) * _dot(h, w_ref[:, C_HV:C_Q])
    zq = _dot(h, w_ref[:, C_Q:C_K])
    for hd in range(ATT_HEADS):
        sl = slice(hd * LANES, (hd + 1) * LANES)
        store_head(q_ref, hd, (rot(zq[:, sl], cos, sin) * (HEAD_DIM ** -0.5 * LOG2_E)).astype(BF16))
    k = rot(_dot(h, w_ref[:, C_K:C_V]), cos, sin)
    k_ref[...] = k
    kbf_ref[...] = k.astype(BF16)
    v = _dot(h, w_ref[:, C_V:C_QI])
    v_ref[...] = v
    vbf_ref[...] = v.astype(BF16)
    vt_ref[0:LANES, :] = _dot_nt(wvt_ref[...], h).astype(BF16)
    vt_ref[LANES:VT_ROWS, :] = jnp.ones((VT_ROWS - LANES, vt_ref.shape[1]), BF16)
    zqi = _dot(h, w_ref[:, C_QI:C_KIW])
    for hd in range(IDX_HEADS):
        sl = slice(hd * LANES, (hd + 1) * LANES)
        store_head(qi_ref, hd, rot(zqi[:, sl], cos, sin).astype(BF16))
    kiw = rot(_dot(h, w_ref[:, C_KIW:C_END]), cosb_ref[...], sinb_ref[...])
    kiw_ref[...] = kiw
    kiwbf_ref[...] = kiw.astype(BF16)


def _l0_in_proj(x2, g, w, wvt, tabs, tm, n_tab_blocks, head_major):
    m = x2.shape[0]
    row = lambda i: (i, 0)
    const = lambda i: (0, 0)
    tab = lambda i: (i % n_tab_blocks, 0)
    widths = (512, 512, 1024, 128, 128, 512, 128, 128, 128, 128)
    dtypes = (F32, F32, BF16, F32, F32, BF16, F32, BF16, BF16, BF16)
    out_specs = [pl.BlockSpec((tm, wd), row) for wd in widths]
    out_shape = [jax.ShapeDtypeStruct((m, wd), dt) for wd, dt in zip(widths, dtypes)]
    if head_major:
        for n in (2, 5):
            heads = widths[n] // LANES
            out_specs[n] = pl.BlockSpec((tm // QUERY_BLOCK, heads, QUERY_BLOCK, LANES), lambda i: (i, 0, 0, 0))
            out_shape[n] = jax.ShapeDtypeStruct((m // QUERY_BLOCK, heads, QUERY_BLOCK, LANES), BF16)
    return pl.pallas_call(
        functools.partial(_l0_in_kernel, head_major=head_major),
        grid=(m // tm,),
        in_specs=[pl.BlockSpec((tm, D_MODEL), row), pl.BlockSpec((1, D_MODEL), const),
                  pl.BlockSpec((D_MODEL, C_END), const), pl.BlockSpec((LANES, D_MODEL), const)]
        + [pl.BlockSpec((tm, LANES), tab)] * 4,
        out_specs=out_specs + [pl.BlockSpec((VT_ROWS, tm), lambda i: (0, i))],
        out_shape=out_shape + [jax.ShapeDtypeStruct((VT_ROWS, m), BF16)],
        compiler_params=_cparams(("parallel",)),
        name="l0_in_proj",
    )(x2, g, w, wvt, *tabs)


def _score_keys(score):
    bits = pltpu.bitcast(score, I32)
    return bits ^ ((bits >> 31) & jnp.int32(0x7FFFFFFF))


def _key_scores(key):
    return pltpu.bitcast(key ^ ((key >> 31) & jnp.int32(0x7FFFFFFF)), F32)


SEARCH_PLAIN_STEPS = 8
SEARCH_SNAP_INTERP_STEPS = 8
SEARCH_MAX_STEPS = SEARCH_PLAIN_STEPS + SEARCH_SNAP_INTERP_STEPS + 34


def _topk_threshold(probe, count_tie_below, vmin, vmax, n_real, topk, n_index_bits):
    k = float(topk)
    lo0, hi0 = _score_keys(vmin), _score_keys(vmax) + 1
    take_all = n_real <= k
    zero, one = jnp.zeros_like(vmin), jnp.ones_like(vmin)
    active0 = jnp.where(jnp.logical_not(take_all) & (lo0 + 1 < hi0), 1.0, 0.0)

    def step(snap, st):
        it, lo, hi, c_lo, c_hi, w_lo, w_hi, last, active = st
        act = active > 0.0
        f_lo = (c_lo - (k - 0.5)) * w_lo
        f_hi = ((k - 0.5) - c_hi) * w_hi
        v_lo, v_hi = _key_scores(lo), _key_scores(hi)
        g = _score_keys(v_lo + (v_hi - v_lo) * (f_lo / (f_lo + f_hi)))
        if snap:
            mid = (lo >> 1) + (hi >> 1) + (lo & hi & 1)
            g = jnp.where(it < SEARCH_PLAIN_STEPS + SEARCH_SNAP_INTERP_STEPS, g, mid)
        g = jnp.minimum(jnp.maximum(g, lo + 1), hi - 1)
        c, key_up, key_dn = probe(g, snap)
        hit = act & (c == k)
        up = act & (c > k)
        dn = act & (c < k)
        lo = jnp.where(hit, g, jnp.where(up, key_up if snap else g, lo))
        c_lo = jnp.where(hit | up, c, c_lo)
        hi = jnp.where(dn, key_dn + 1 if snap else g, hi)
        c_hi = jnp.where(dn, c, c_hi)
        w_hi = jnp.where(up, jnp.where(last > 0.0, w_hi * 0.5, one), jnp.where(dn, one, w_hi))
        w_lo = jnp.where(dn, jnp.where(last < 0.0, w_lo * 0.5, one), jnp.where(up, one, w_lo))
        last = jnp.where(up, one, jnp.where(dn, -one, last))
        active = jnp.where(act & jnp.logical_not(hit) & (lo + 1 < hi), 1.0, 0.0)
        return it + 1, lo, hi, c_lo, c_hi, w_lo, w_hi, last, active

    def cond(limit, st):
        return (st[0] < limit) & (jnp.max(st[-1]) > 0.0)

    st = (jnp.int32(0), lo0, hi0, n_real, zero, one, one, zero, active0)
    st = lax.fori_loop(0, SEARCH_PLAIN_STEPS, lambda _, s: step(False, s), st)
    st = lax.while_loop(functools.partial(cond, SEARCH_MAX_STEPS), functools.partial(step, True), st)
    _, lo, _, c_lo, c_hi, _, _, _, _ = st
    thr = jnp.where(take_all, INT_MIN, lo)
    excess = jnp.logical_not(take_all) & (c_lo > k)
    need = k - c_hi

    def tie_search(_):
        def step(i, x):
            cand = x | lax.shift_left(jnp.int32(1), n_index_bits - 1 - i)
            return jnp.where(count_tie_below(thr, cand) < need, cand, x)
        return lax.fori_loop(0, n_index_bits, step, jnp.zeros_like(thr))

    any_excess = jnp.max(jnp.where(excess, 1.0, 0.0)) > 0.0
    cut = lax.cond(any_excess, tie_search, lambda _: jnp.zeros_like(thr), 0)
    cut = jnp.where(excess, cut, jnp.where(take_all, jnp.int32(-1), jnp.int32(2 ** 31 - 1)))
    return thr, cut


def _selection_bias(key, idx, thr, cut):
    sel = (key > thr) | ((key == thr) & (idx <= cut))
    return pltpu.bitcast(jnp.where(sel, 0.0, NEG_BIG).astype(F32), I32)


def _lane_fold(m):
    acc = m[:, 0:LANES]
    for c in range(1, m.shape[1] // LANES):
        acc = acc + m[:, c * LANES:(c + 1) * LANES]
    return acc


def _row_count(s_ref, n_chunks, pred):
    rows = s_ref.shape[0]

    def body(c, acc):
        k0 = pl.multiple_of(c * KEY_CHUNK, KEY_CHUNK)
        blk = s_ref[:, pl.ds(k0, KEY_CHUNK)]
        idx = k0 + lax.broadcasted_iota(I32, blk.shape, 1)
        return acc + _lane_fold(jnp.where(pred(blk, idx), 1.0, 0.0))

    acc = lax.fori_loop(0, n_chunks, body, jnp.zeros((rows, LANES), F32))
    return jnp.sum(acc, axis=1, keepdims=True)


def _int_reduce(x, take_min, axis):
    red = jnp.min if take_min else jnp.max
    hi = (x >> 16).astype(F32)
    lo = (x & 0xFFFF).astype(F32)
    m_hi = red(hi, axis=axis, keepdims=True)
    m_lo = red(jnp.where(hi == m_hi, lo, 65536.0 if take_min else -1.0), axis=axis, keepdims=True)
    return (m_hi.astype(I32) << 16) | m_lo.astype(I32)


def _row_probe(s_ref, n_chunks, g, snap):
    rows = s_ref.shape[0]
    n_groups = KEY_CHUNK // LANES

    def body(c, carry):
        k0 = pl.multiple_of(c * KEY_CHUNK, KEY_CHUNK)
        blk = s_ref[:, pl.ds(k0, KEY_CHUNK)]
        ge = blk >= g
        out = [carry[0] + _lane_fold(jnp.where(ge, 1.0, 0.0))]
        if snap:
            above, below = jnp.where(ge, blk, 2 ** 31 - 1), jnp.where(ge, INT_MIN, blk)
            up, dn = carry[1], carry[2]
            for cc in range(n_groups):
                up = jnp.minimum(up, above[:, cc * LANES:(cc + 1) * LANES])
                dn = jnp.maximum(dn, below[:, cc * LANES:(cc + 1) * LANES])
            out += [up, dn]
        return tuple(out)

    init = [jnp.zeros((rows, LANES), F32)]
    if snap:
        init += [jnp.full((rows, LANES), 2 ** 31 - 1, I32), jnp.full((rows, LANES), INT_MIN, I32)]
    res = lax.fori_loop(0, n_chunks, body, tuple(init))
    c = jnp.sum(res[0], axis=1, keepdims=True)
    if not snap:
        return c, None, None
    return c, _int_reduce(res[1], True, 1), _int_reduce(res[2], False, 1)


def _row_select_to_bias(s_ref, n_chunks, n_real, topk):
    rows = s_ref.shape[0]
    inf = jnp.float32(jnp.inf)

    def min_max(c, carry):
        mn, mx = carry
        k0 = pl.multiple_of(c * KEY_CHUNK, KEY_CHUNK)
        blk = s_ref[:, pl.ds(k0, KEY_CHUNK)]
        v = _key_scores(blk)
        real = blk != INT_MIN
        for cc in range(KEY_CHUNK // LANES):
            sl = slice(cc * LANES, (cc + 1) * LANES)
            mn = jnp.minimum(mn, jnp.where(real[:, sl], v[:, sl], inf))
            mx = jnp.maximum(mx, jnp.where(real[:, sl], v[:, sl], -inf))
        return mn, mx

    mn, mx = lax.fori_loop(0, n_chunks, min_max, (jnp.full((rows, LANES), inf, F32), jnp.full((rows, LANES), -inf, F32)))
    vmin, vmax = jnp.min(mn, axis=1, keepdims=True), jnp.max(mx, axis=1, keepdims=True)
    thr, cut = _topk_threshold(
        functools.partial(_row_probe, s_ref, n_chunks),
        lambda t, x: _row_count(s_ref, n_chunks, lambda blk, idx: (blk == t) & (idx < x)),
        vmin, vmax, n_real, topk, (s_ref.shape[1] - 1).bit_length())

    def to_bias(c, _):
        k0 = pl.multiple_of(c * KEY_CHUNK, KEY_CHUNK)
        blk = s_ref[:, pl.ds(k0, KEY_CHUNK)]
        idx = k0 + lax.broadcasted_iota(I32, blk.shape, 1)
        s_ref[:, pl.ds(k0, KEY_CHUNK)] = _selection_bias(blk, idx, thr, cut)
        return 0

    lax.fori_loop(0, n_chunks, to_bias, 0)


def _attend_group(qg, segments, bias_ref, rep):
    n = qg.shape[0]

    def step(load_kv, c, carry):
        m, l, acc = carry
        k0 = pl.multiple_of(c * KEY_CHUNK, KEY_CHUNK)
        ktc, vtc = load_kv(k0)
        bias = pltpu.bitcast(bias_ref[:, pl.ds(k0, KEY_CHUNK)], F32)
        s = _dot(qg, ktc) + jnp.concatenate([bias] * rep, axis=0)
        m_new = jnp.maximum(m, jnp.max(s, axis=1, keepdims=True))
        alpha = jnp.exp2(m - m_new)
        p = jnp.exp2(s - m_new)
        l = alpha * l + jnp.sum(p, axis=1, keepdims=True)
        acc = alpha * acc + _dot_nt(p.astype(BF16), vtc)
        return m_new, l, acc

    carry = (jnp.full((n, 1), NEG_BIG, F32), jnp.zeros((n, 1), F32), jnp.zeros((n, LANES), F32))
    for first, end, load_kv in segments:
        carry = lax.fori_loop(first, end, functools.partial(step, load_kv), carry)
    _, l, acc = carry
    return acc / l


def _stack_heads(x, first, count, width=LANES):
    x = x.astype(F32)
    return jnp.concatenate([x[:, (first + hd) * LANES:(first + hd) * LANES + width] for hd in range(count)],
                           axis=0).astype(BF16)


def _attend_all_heads(q, segments, bias_ref, o_ref):
    rows = q.shape[0]
    group = ATT_HEADS // KV_HEADS
    out = _attend_group(_stack_heads(q, 0, ATT_HEADS), segments, bias_ref, ATT_HEADS)
    for hd in range(ATT_HEADS):
        piece = out[hd * rows:(hd + 1) * rows]
        piece = jnp.where(_group_lane_mask(piece.shape, hd // group), piece, 0.0)
        o_ref[0, :, hd * LANES:(hd + 1) * LANES] = piece.astype(o_ref.dtype)


def _indexer_weights(kiw_q):
    return kiw_q[:, IDX_DIM:IDX_DIM + IDX_HEADS] * (IDX_HEADS ** -0.5 * IDX_DIM ** -0.5)


def _group_lane_mask(shape, g):
    lane = lax.broadcasted_iota(I32, shape, 1)
    return (lane >= g * HEAD_DIM) & (lane < (g + 1) * HEAD_DIM)


ATT_CHUNK = 256


def _sublane_fold(m, op=jnp.add):
    parts = [m[r * 8:(r + 1) * 8] for r in range(m.shape[0] // 8)]
    while len(parts) > 1:
        parts = [op(parts[n], parts[n + 1]) for n in range(0, len(parts) - 1, 2)] + (parts[-1:] if len(parts) % 2 else [])
    return parts[0]


def _col_reduce(m, op, reduce_fn):
    return reduce_fn(_sublane_fold(m, op), axis=0, keepdims=True)


def _col_count(s_ref, n_chunks, pred):
    cols = s_ref.shape[1]

    def body(c, acc):
        k0 = pl.multiple_of(c * KEY_CHUNK, KEY_CHUNK)
        blk = s_ref[pl.ds(k0, KEY_CHUNK), :]
        idx = k0 + lax.broadcasted_iota(I32, blk.shape, 0)
        return acc + _sublane_fold(jnp.where(pred(blk, idx), 1.0, 0.0))

    acc = lax.fori_loop(0, n_chunks, body, jnp.zeros((8, cols), F32))
    return jnp.sum(acc, axis=0, keepdims=True)


def _col_probe(s_ref, n_chunks, g, snap):
    cols = s_ref.shape[1]

    def body(c, carry):
        k0 = pl.multiple_of(c * KEY_CHUNK, KEY_CHUNK)
        blk = s_ref[pl.ds(k0, KEY_CHUNK), :]
        ge = blk >= g
        out = [carry[0] + _sublane_fold(jnp.where(ge, 1.0, 0.0))]
        if snap:
            out.append(jnp.minimum(carry[1], _sublane_fold(jnp.where(ge, blk, 2 ** 31 - 1), jnp.minimum)))
            out.append(jnp.maximum(carry[2], _sublane_fold(jnp.where(ge, INT_MIN, blk), jnp.maximum)))
        return tuple(out)

    init = [jnp.zeros((8, cols), F32)]
    if snap:
        init += [jnp.full((8, cols), 2 ** 31 - 1, I32), jnp.full((8, cols), INT_MIN, I32)]
    res = lax.fori_loop(0, n_chunks, body, tuple(init))
    c = jnp.sum(res[0], axis=0, keepdims=True)
    if not snap:
        return c, None, None
    return c, _int_reduce(res[1], True, 0), _int_reduce(res[2], False, 0)


def _col_select_to_bias(s_ref, n_chunks, n_real, topk, vmin, vmax):
    thr, cut = _topk_threshold(
        functools.partial(_col_probe, s_ref, n_chunks),
        lambda t, x: _col_count(s_ref, n_chunks, lambda blk, idx: (blk == t) & (idx < x)),
        vmin, vmax, n_real, topk, (s_ref.shape[0] - 1).bit_length())

    def to_bias(c, _):
        k0 = pl.multiple_of(c * KEY_CHUNK, KEY_CHUNK)
        blk = s_ref[pl.ds(k0, KEY_CHUNK), :]
        idx = k0 + lax.broadcasted_iota(I32, blk.shape, 0)
        s_ref[pl.ds(k0, KEY_CHUNK), :] = _selection_bias(blk, idx, thr, cut)
        return 0

    lax.fori_loop(0, n_chunks, to_bias, 0)


def _dsa_prompt_kernel(q_ref, qi_ref, kiwq_ref, k_ref, vt_ref, kiw_ref, o_ref, s_ref, m_scr, acc_scr,
                       sa_scr, sb_scr, *, qb, topk):
    j = pl.program_id(1)
    n_keys = j * qb + qb
    n_chunks = (n_keys + KEY_CHUNK - 1) // KEY_CHUNK
    qpos = j * qb + lax.broadcasted_iota(I32, (1, qb), 1)
    wt = kiwq_ref[0].T[IDX_DIM:IDX_DIM + IDX_HEADS] * (IDX_HEADS ** -0.5 * IDX_DIM ** -0.5)

    inf = jnp.float32(jnp.inf)

    def scores(c, carry):
        mn, mx = carry
        k0 = pl.multiple_of(c * KEY_CHUNK, KEY_CHUNK)
        kc = kiw_ref[0, pl.ds(k0, KEY_CHUNK), :]
        acc = jnp.zeros((KEY_CHUNK, qb), F32)
        for pr in range(IDX_HEADS // 2):
            s2 = _dot_nt(kc, qi_ref[0, 2 * pr:2 * pr + 2].reshape(2 * qb, LANES))
            acc = acc + jnp.maximum(s2[:, :qb], 0.0) * wt[2 * pr:2 * pr + 1]
            acc = acc + jnp.maximum(s2[:, qb:], 0.0) * wt[2 * pr + 1:2 * pr + 2]
        causal = k0 + lax.broadcasted_iota(I32, (KEY_CHUNK, qb), 0) <= qpos
        s_ref[pl.ds(k0, KEY_CHUNK), :] = jnp.where(causal, _score_keys(acc), INT_MIN)
        mn = jnp.minimum(mn, _sublane_fold(jnp.where(causal, acc, inf), jnp.minimum))
        mx = jnp.maximum(mx, _sublane_fold(jnp.where(causal, acc, -inf), jnp.maximum))
        return mn, mx

    mn, mx = lax.fori_loop(0, n_chunks, scores, (jnp.full((8, qb), inf, F32), jnp.full((8, qb), -inf, F32)))
    _col_select_to_bias(s_ref, n_chunks, (qpos + 1).astype(F32), topk,
                        jnp.min(mn, axis=0, keepdims=True), jnp.max(mx, axis=0, keepdims=True))

    m_scr[...] = jnp.full(m_scr.shape, NEG_BIG, F32)
    acc_scr[...] = jnp.zeros(acc_scr.shape, F32)

    n_att = n_chunks * (KEY_CHUNK // ATT_CHUNK)

    def logits(step, buf):
        k0 = pl.multiple_of(jnp.minimum(step, n_att - 1) * ATT_CHUNK, ATT_CHUNK)
        kc = k_ref[0, pl.ds(k0, ATT_CHUNK), :]
        bias = pltpu.bitcast(s_ref[pl.ds(k0, ATT_CHUNK), :], F32)
        for pr in range(ATT_HEADS // 2):
            s2 = _dot_nt(kc, q_ref[0, 2 * pr:2 * pr + 2].reshape(2 * qb, LANES))
            buf[2 * pr] = s2[:, :qb] + bias
            buf[2 * pr + 1] = s2[:, qb:] + bias

    def accumulate(step, buf):
        k0 = pl.multiple_of(step * ATT_CHUNK, ATT_CHUNK)
        vtc = vt_ref[:, pl.ds(k0, ATT_CHUNK)]
        for pr in range(ATT_HEADS // 2):
            ps, alphas = [], []
            for hd in (2 * pr, 2 * pr + 1):
                s = buf[hd]
                m_prev = m_scr[hd:hd + 1, :]
                m_new = jnp.maximum(m_prev, _col_reduce(s, jnp.maximum, jnp.max))
                alphas.append(jnp.exp2(m_prev - m_new))
                ps.append(jnp.exp2(s - m_new).astype(BF16))
                m_scr[hd:hd + 1, :] = m_new
            acc_scr[pr] = jnp.concatenate(alphas, axis=1) * acc_scr[pr] + _dot(vtc, jnp.concatenate(ps, axis=1))

    logits(0, sa_scr)

    def attend(c, _):
        logits(2 * c + 1, sb_scr)
        accumulate(2 * c, sa_scr)
        logits(2 * c + 2, sa_scr)
        accumulate(2 * c + 1, sb_scr)
        return 0

    lax.fori_loop(0, n_att // 2, attend, 0)

    group = ATT_HEADS // KV_HEADS
    for hd in range(ATT_HEADS):
        cols = slice((hd % 2) * qb, (hd % 2 + 1) * qb)
        out_t = acc_scr[hd // 2, 0:LANES, cols] / acc_scr[hd // 2, LANES:LANES + 1, cols]
        row = lax.broadcasted_iota(I32, out_t.shape, 0)
        g = hd // group
        out_t = jnp.where((row >= g * HEAD_DIM) & (row < (g + 1) * HEAD_DIM), out_t, 0.0)
        o_ref[0, :, hd * LANES:(hd + 1) * LANES] = out_t.T.astype(BF16)


def _dsa_prompt(q, qi, kiw, kbf, vt, kiwbf):
    b, t, _ = kiw.shape
    qb = QUERY_BLOCK
    nb = t // qb
    topk = min(TOPK_MAX, t // 4)
    qblk = lambda width: pl.BlockSpec((1, qb, width), lambda bi, j: (bi, j, 0))
    hblk = lambda heads: pl.BlockSpec((1, heads, qb, LANES), lambda bi, j: (bi * nb + j, 0, 0, 0))
    full = pl.BlockSpec((1, t, LANES), lambda bi, j: (bi, 0, 0))
    return pl.pallas_call(
        functools.partial(_dsa_prompt_kernel, qb=qb, topk=topk),
        grid=(b, nb),
        in_specs=[hblk(ATT_HEADS), hblk(IDX_HEADS), qblk(LANES), full,
                  pl.BlockSpec((VT_ROWS, t), lambda bi, j: (0, bi)), full],
        out_specs=qblk(ATT_HEADS * LANES),
        out_shape=jax.ShapeDtypeStruct((b, t, ATT_HEADS * LANES), BF16),
        scratch_shapes=[pltpu.VMEM((pl.cdiv(t, KEY_CHUNK) * KEY_CHUNK, qb), I32), pltpu.VMEM((ATT_HEADS, qb), F32),
                        pltpu.VMEM((ATT_HEADS // 2, VT_ROWS, 2 * qb), F32),
                        pltpu.VMEM((ATT_HEADS, ATT_CHUNK, qb), F32), pltpu.VMEM((ATT_HEADS, ATT_CHUNK, qb), F32)],
        compiler_params=_cparams(("parallel", "arbitrary")),
        name="dsa_prompt",
    )(q, qi, kiw, kbf, vt, kiwbf)


def _dsa_sample_kernel(pt_ref, q_ref, qi_ref, kiwq_ref, kn_ref, vn_ref, kiwn_ref, ck_hbm, cv_hbm, cki_hbm,
                       o_ref, kbuf, vbuf, kibuf, s_ref, sems, *, ts, n_pages, topk):
    b = pl.program_id(0)
    past = n_pages * PAGE_SIZE

    streams = ((cki_hbm, kibuf), (ck_hbm, kbuf), (cv_hbm, vbuf))

    def page_copy(p, which):
        src, dst = streams[which]
        cols = pl.ds(pl.multiple_of(p * PAGE_SIZE, PAGE_SIZE), PAGE_SIZE)
        return pltpu.make_async_copy(src.at[pt_ref[b, p]], dst.at[:, cols], sems.at[which])

    def start_page(p, _):
        for which in range(len(streams)):
            page_copy(p, which).start()
        return 0

    lax.fori_loop(0, n_pages, start_page, 0)

    def wait_pages(which):
        def body(p, _):
            page_copy(p, which).wait()
            return 0
        lax.fori_loop(0, n_pages, body, 0)

    n_past_chunks = past // KEY_CHUNK
    n_chunks = n_past_chunks + 1
    qpos = lax.broadcasted_iota(I32, (ts, 1), 0)
    wq = _indexer_weights(kiwq_ref[0])
    qi_stack = _stack_heads(qi_ref[0], 0, IDX_HEADS, IDX_DIM)
    w_stack = jnp.concatenate([wq[:, hd:hd + 1] for hd in range(IDX_HEADS)], axis=0)

    def head_sum(x):
        acc = x[0:ts]
        for hd in range(1, IDX_HEADS):
            acc = acc + x[hd * ts:(hd + 1) * ts]
        return acc

    wait_pages(0)

    def past_scores(c, _):
        k0 = pl.multiple_of(c * KEY_CHUNK, KEY_CHUNK)
        ktc = kibuf[:, pl.ds(k0, KEY_CHUNK)].astype(BF16)
        acc = head_sum(jnp.maximum(_dot(qi_stack, ktc), 0.0) * w_stack)
        s_ref[:, pl.ds(k0, KEY_CHUNK)] = _score_keys(acc + 0.0)
        return 0

    lax.fori_loop(0, n_past_chunks, past_scores, 0)
    acc = head_sum(jnp.maximum(_dot(qi_stack, kiwn_ref[0, 0:IDX_DIM, :]), 0.0) * w_stack)
    kpos = lax.broadcasted_iota(I32, (ts, KEY_CHUNK), 1)
    s_ref[:, pl.ds(past, KEY_CHUNK)] = jnp.where(kpos <= qpos, _score_keys(acc + 0.0), INT_MIN)
    _row_select_to_bias(s_ref, n_chunks, (past + 1 + qpos).astype(F32), topk)

    wait_pages(1)
    wait_pages(2)

    def load_past(k0):
        return kbuf[:, pl.ds(k0, KEY_CHUNK)].astype(BF16), vbuf[:, pl.ds(k0, KEY_CHUNK)].astype(BF16)

    def load_new(k0):
        return kn_ref[0], vn_ref[0]

    segments = [(0, n_past_chunks, load_past), (n_past_chunks, n_chunks, load_new)]
    _attend_all_heads(q_ref[0], segments, s_ref, o_ref)


def _dsa_sample(q, qi, kiw, knt, vnt, kiwnt, cache_kt, cache_vt, cache_kit, page_table):
    b, ts, _ = q.shape
    n_pages = page_table.shape[1]
    past = n_pages * PAGE_SIZE
    topk = min(TOPK_MAX, (past + ts) // 4)
    blk = lambda rows, width: pl.BlockSpec((1, rows, width), lambda bi, pt: (bi, 0, 0))
    hbm = pl.BlockSpec(memory_space=pl.ANY)
    grid_spec = pltpu.PrefetchScalarGridSpec(
        num_scalar_prefetch=1,
        grid=(b,),
        in_specs=[blk(ts, ATT_HEADS * LANES), blk(ts, IDX_HEADS * LANES), blk(ts, LANES),
                  blk(LANES, KEY_CHUNK), blk(LANES, KEY_CHUNK), blk(LANES, KEY_CHUNK), hbm, hbm, hbm],
        out_specs=blk(ts, ATT_HEADS * LANES),
        scratch_shapes=[pltpu.VMEM((LANES, past), F32), pltpu.VMEM((LANES, past), F32),
                        pltpu.VMEM((IDX_DIM, past), F32), pltpu.VMEM((ts, past + KEY_CHUNK), I32),
                        pltpu.SemaphoreType.DMA((3,))],
    )
    return pl.pallas_call(
        functools.partial(_dsa_sample_kernel, ts=ts, n_pages=n_pages, topk=topk),
        grid_spec=grid_spec,
        out_shape=jax.ShapeDtypeStruct((b, ts, ATT_HEADS * LANES), F32),
        compiler_params=_cparams(("arbitrary",)),
        name="dsa_sample",
    )(page_table, q, qi, kiw, knt, vnt, kiwnt, cache_kt, cache_vt, cache_kit)


def _conv_kernel(u_ref, halo_ref, buf_ref, bg_ref, w_ref, ya_ref):
    i = pl.program_id(1)
    u = u_ref[0]
    halo, buf = halo_ref[0], buf_ref[0]
    first = i == 0
    prev1 = jnp.where(first, buf[1:2], halo[7:8])
    prev2 = jnp.where(first, buf[0:1], halo[6:7])
    row = lax.broadcasted_iota(I32, u.shape, 0)
    um1 = jnp.where(row == 0, prev1, pltpu.roll(u, 1, 0))
    um2 = jnp.where(row == 0, prev2, jnp.where(row == 1, prev1, pltpu.roll(u, 2, 0)))
    w = w_ref[...]
    conv = w[0:1] * um2 + w[1:2] * um1 + w[2:3] * u
    ya_ref[0] = (bg_ref[0] * conv).astype(BF16)


def _conv(u, bg, buf, w, tt):
    b, t, c = u.shape
    halo_rows = 8
    tile = pl.BlockSpec((1, tt, c), lambda bi, i: (bi, i, 0))
    halo = pl.BlockSpec((1, halo_rows, c), lambda bi, i: (bi, jnp.maximum(i * (tt // halo_rows) - 1, 0), 0))
    return pl.pallas_call(
        _conv_kernel,
        grid=(b, t // tt),
        in_specs=[tile, halo, pl.BlockSpec((1, CONV_WIDTH - 1, c), lambda bi, i: (bi, 0, 0)), tile,
                  pl.BlockSpec((CONV_WIDTH, c), lambda bi, i: (0, 0))],
        out_specs=tile,
        out_shape=jax.ShapeDtypeStruct((b, t, c), BF16),
        compiler_params=_cparams(("parallel", "parallel")),
        name="short_conv",
    )(u, u, buf, bg, w)


def _mm_res_kernel(*refs, n_in):
    a_refs, w_ref, x_ref, o_ref = refs[:n_in], refs[n_in], refs[n_in + 1], refs[n_in + 2]
    a = jnp.concatenate([r[...] for r in a_refs], axis=1) if n_in > 1 else a_refs[0][...]
    o_ref[...] = x_ref[...] + _dot(a, w_ref[...])


def _mm_res(a_list, w, x, tm):
    m = x.shape[0]
    row = lambda i: (i, 0)
    return pl.pallas_call(
        functools.partial(_mm_res_kernel, n_in=len(a_list)),
        grid=(m // tm,),
        in_specs=[pl.BlockSpec((tm, a.shape[1]), row) for a in a_list]
        + [pl.BlockSpec(w.shape, lambda i: (0, 0)), pl.BlockSpec((tm, D_MODEL), row)],
        out_specs=pl.BlockSpec((tm, D_MODEL), row),
        out_shape=jax.ShapeDtypeStruct((m, D_MODEL), F32),
        compiler_params=_cparams(("parallel",)),
        name="matmul_residual",
    )(*a_list, w, x)


def _norm_mm_kernel(x_ref, g_ref, w_ref, o_ref, h_scr):
    @pl.when(pl.program_id(1) == 0)
    def _():
        h_scr[...] = _rms(x_ref[...], g_ref[...]).astype(BF16)

    o_ref[...] = _dot(h_scr[...], w_ref[...])


def _norm_mm(x, g, w, tm, tn):
    m, n = x.shape[0], w.shape[1]
    return pl.pallas_call(
        _norm_mm_kernel,
        grid=(m // tm, n // tn),
        in_specs=[pl.BlockSpec((tm, D_MODEL), lambda i, j: (i, 0)), pl.BlockSpec((1, D_MODEL), lambda i, j: (0, 0)),
                  pl.BlockSpec((D_MODEL, tn), lambda i, j: (0, j))],
        out_specs=pl.BlockSpec((tm, tn), lambda i, j: (i, j)),
        out_shape=jax.ShapeDtypeStruct((m, n), F32),
        scratch_shapes=[pltpu.VMEM((tm, D_MODEL), BF16)],
        compiler_params=_cparams(("parallel", "arbitrary")),
        name="norm_matmul",
    )(x, g, w)


def _ffn_kernel(x_ref, g_ref, wg_ref, wu_ref, wd_ref, o_ref, h_scr, acc):
    f = pl.program_id(1)

    @pl.when(f == 0)
    def _():
        x = x_ref[...]
        h_scr[...] = _rms(x, g_ref[...]).astype(BF16)
        acc[...] = x

    h = h_scr[...]
    a = jax.nn.silu(_dot(h, wg_ref[...])) * _dot(h, wu_ref[...])
    acc[...] += _dot(a.astype(BF16), wd_ref[...])

    @pl.when(f == pl.num_programs(1) - 1)
    def _():
        o_ref[...] = acc[...]


def _ffn(x, g, wg, wu, wd, tm, tf):
    m, ff = x.shape[0], wg.shape[1]
    row = lambda i, f: (i, 0)
    return pl.pallas_call(
        _ffn_kernel,
        grid=(m // tm, ff // tf),
        in_specs=[pl.BlockSpec((tm, D_MODEL), row), pl.BlockSpec((1, D_MODEL), lambda i, f: (0, 0)),
                  pl.BlockSpec((D_MODEL, tf), lambda i, f: (0, f)), pl.BlockSpec((D_MODEL, tf), lambda i, f: (0, f)),
                  pl.BlockSpec((tf, D_MODEL), lambda i, f: (f, 0))],
        out_specs=pl.BlockSpec((tm, D_MODEL), row),
        out_shape=jax.ShapeDtypeStruct((m, D_MODEL), F32),
        scratch_shapes=[pltpu.VMEM((tm, D_MODEL), BF16), pltpu.VMEM((tm, D_MODEL), F32)],
        compiler_params=_cparams(("parallel", "arbitrary")),
        name="dense_swiglu",
    )(x, g, wg, wu, wd)


def _top2_gates(logits):
    lane = lax.broadcasted_iota(I32, logits.shape, 1).astype(F32)
    neg = jnp.float32(-jnp.inf)
    l1 = jnp.where(lane < N_EXPERTS, logits, neg)
    m1 = jnp.max(l1, axis=1, keepdims=True)
    i1 = jnp.min(jnp.where(l1 == m1, lane, float(LANES)), axis=1, keepdims=True)
    l2 = jnp.where(lane == i1, neg, l1)
    m2 = jnp.max(l2, axis=1, keepdims=True)
    i2 = jnp.min(jnp.where(l2 == m2, lane, float(LANES)), axis=1, keepdims=True)
    e = jnp.exp(m2 - m1)
    w1 = 1.0 / (1.0 + e)
    w2 = e / (1.0 + e)
    first, second = lane == i1, lane == i2
    return jnp.where(first, w1, jnp.where(second, w2, 0.0)), jnp.where(first | second, 1.0, 0.0)


MOE_SUB = 128
MOE_TOKEN_TILE = 1024


def _moe_route_kernel(x_ref, g_ref, rhi_ref, rlo_ref, h_ref, gate_ref, posc_ref, posr_ref, cnt_ref):
    tm = x_ref.shape[0]
    hn = _rms(x_ref[...], g_ref[...])
    h_hi = hn.astype(BF16)
    h_lo = (hn - h_hi.astype(F32)).astype(BF16)
    logits = _dot(h_hi, rhi_ref[...]) + (_dot(h_lo, rhi_ref[...]) + _dot(h_hi, rlo_ref[...]))
    gate, routed = _top2_gates(logits)
    h_ref[...] = h_hi
    gate_ref[...] = gate
    earlier = (lax.broadcasted_iota(I32, (tm, tm), 0) > lax.broadcasted_iota(I32, (tm, tm), 1))
    slot = _dot(jnp.where(earlier, 1.0, 0.0).astype(BF16), routed.astype(BF16))
    posc = jnp.where(routed > 0.0, slot, -1.0)
    posc_ref[...] = posc
    posr_ref[0] = posc.T[0:N_EXPERTS]
    cnt_ref[0] = jnp.broadcast_to(jnp.sum(routed, axis=0, keepdims=True), (8, LANES))


def _moe_expert_kernel(cnt_ref, h_ref, gate_ref, posc_ref, posr_ref, wg_ref, wu_ref, wd_ref, o_ref, xg, yacc):
    i, e, f = pl.program_id(0), pl.program_id(1), pl.program_id(2)
    tm = h_ref.shape[0]
    n_sub = (cnt_ref[i * N_EXPERTS + e] + MOE_SUB - 1) // MOE_SUB

    @pl.when((e == 0) & (f == 0))
    def _():
        o_ref[...] = jnp.zeros(o_ref.shape, F32)

    @pl.when(f == 0)
    def _():
        posr = posr_ref[0]
        h = h_ref[...]

        def gather(s, _):
            base = pl.multiple_of(s * MOE_SUB, MOE_SUB)
            slot = (base + lax.broadcasted_iota(I32, (MOE_SUB, tm), 0)).astype(F32)
            onehot = jnp.where(posr == slot, 1.0, 0.0).astype(BF16)
            xg[pl.ds(base, MOE_SUB), :] = _dot(onehot, h).astype(BF16)
            yacc[pl.ds(base, MOE_SUB), :] = jnp.zeros((MOE_SUB, D_MODEL), F32)
            return 0

        lax.fori_loop(0, n_sub, gather, 0)

    def expert(s, _):
        rows = pl.ds(pl.multiple_of(s * MOE_SUB, MOE_SUB), MOE_SUB)
        xs = xg[rows, :]
        a = jax.nn.silu(_dot(xs, wg_ref[0])) * _dot(xs, wu_ref[0])
        yacc[rows, :] += _dot(a.astype(BF16), wd_ref[0])
        return 0

    lax.fori_loop(0, n_sub, expert, 0)

    @pl.when(f == pl.num_programs(2) - 1)
    def _():
        lane = lax.broadcasted_iota(I32, (tm, LANES), 1)
        mine = lane == e
        posc = jnp.sum(jnp.where(mine, posc_ref[...], 0.0), axis=1, keepdims=True)
        gate = jnp.sum(jnp.where(mine, gate_ref[...], 0.0), axis=1, keepdims=True)

        def scatter(s, _):
            base = pl.multiple_of(s * MOE_SUB, MOE_SUB)
            slot = (base + lax.broadcasted_iota(I32, (tm, MOE_SUB), 1)).astype(F32)
            onehot = jnp.where(posc == slot, 1.0, 0.0).astype(BF16)
            o_ref[...] += gate * _dot(onehot, yacc[pl.ds(base, MOE_SUB), :].astype(BF16))
            return 0

        lax.fori_loop(0, n_sub, scatter, 0)


def _moe(x, g, r_hi, r_lo, wg, wu, wd, tm, tf):
    m = x.shape[0]
    nt = m // tm
    row = lambda i: (i, 0)
    const = lambda i: (0, 0)
    h, gate, posc, posr, cnt = pl.pallas_call(
        _moe_route_kernel,
        grid=(nt,),
        in_specs=[pl.BlockSpec((tm, D_MODEL), row), pl.BlockSpec((1, D_MODEL), const),
                  pl.BlockSpec((D_MODEL, LANES), const), pl.BlockSpec((D_MODEL, LANES), const)],
        out_specs=[pl.BlockSpec((tm, D_MODEL), row), pl.BlockSpec((tm, LANES), row), pl.BlockSpec((tm, LANES), row),
                   pl.BlockSpec((1, N_EXPERTS, tm), lambda i: (i, 0, 0)), pl.BlockSpec((1, 8, LANES), lambda i: (i, 0, 0))],
        out_shape=[jax.ShapeDtypeStruct((m, D_MODEL), BF16), jax.ShapeDtypeStruct((m, LANES), F32),
                   jax.ShapeDtypeStruct((m, LANES), F32), jax.ShapeDtypeStruct((nt, N_EXPERTS, tm), F32),
                   jax.ShapeDtypeStruct((nt, 8, LANES), F32)],
        compiler_params=_cparams(("parallel",)),
        name="moe_route",
    )(x, g, r_hi, r_lo)
    counts = cnt[:, 0, :N_EXPERTS].astype(I32).reshape(nt * N_EXPERTS)
    posr = posr.reshape(nt * N_EXPERTS, 1, tm)
    row3 = lambda i, e, f, c: (i, 0)
    slot_rows = pl.cdiv(tm, MOE_SUB) * MOE_SUB
    grid_spec = pltpu.PrefetchScalarGridSpec(
        num_scalar_prefetch=1,
        grid=(nt, N_EXPERTS, EXPERT_FF // tf),
        in_specs=[pl.BlockSpec((tm, D_MODEL), row3),
                  pl.BlockSpec((tm, LANES), row3), pl.BlockSpec((tm, LANES), row3),
                  pl.BlockSpec((1, 1, tm), lambda i, e, f, c: (i * N_EXPERTS + e, 0, 0)),
                  pl.BlockSpec((1, D_MODEL, tf), lambda i, e, f, c: (e, 0, f)),
                  pl.BlockSpec((1, D_MODEL, tf), lambda i, e, f, c: (e, 0, f)),
                  pl.BlockSpec((1, tf, D_MODEL), lambda i, e, f, c: (e, f, 0))],
        out_specs=pl.BlockSpec((tm, D_MODEL), row3),
        scratch_shapes=[pltpu.VMEM((slot_rows, D_MODEL), BF16), pltpu.VMEM((slot_rows, D_MODEL), F32)],
    )
    return pl.pallas_call(
        _moe_expert_kernel,
        grid_spec=grid_spec,
        out_shape=jax.ShapeDtypeStruct((m, D_MODEL), F32),
        compiler_params=_cparams(("parallel", "arbitrary", "arbitrary")),
        name="moe_experts",
    )(counts, h, gate, posc, posr, wg, wu, wd)


def _ple_kernel(*refs, final_norm, n_addends):
    x_refs, (g_ref, p_ref, wp_ref, wgate_ref, gf_ref, o_ref) = refs[:n_addends], refs[n_addends:]
    x = x_refs[0][...]
    for r in x_refs[1:]:
        x = x + r[...]
    hp = _rms(x, g_ref[...]).astype(BF16)
    gate = jax.nn.sigmoid(_dot(hp, wgate_ref[...]))
    y = x + _dot(p_ref[...].astype(BF16), wp_ref[...]) * gate
    if final_norm:
        y = _rms(y, gf_ref[...])
    o_ref[...] = y


def _ple(xs, g, p, wp, wgate, g_final, final_norm, tm):
    m = xs[0].shape[0]
    row = lambda i: (i, 0)
    const = lambda i: (0, 0)
    return pl.pallas_call(
        functools.partial(_ple_kernel, final_norm=final_norm, n_addends=len(xs)),
        grid=(m // tm,),
        in_specs=[pl.BlockSpec((tm, D_MODEL), row)] * len(xs)
        + [pl.BlockSpec((1, D_MODEL), const), pl.BlockSpec((tm, PLE_DIM), row),
           pl.BlockSpec((PLE_DIM, D_MODEL), const), pl.BlockSpec((D_MODEL, D_MODEL), const),
           pl.BlockSpec((1, D_MODEL), const)],
        out_specs=pl.BlockSpec((tm, D_MODEL), row),
        out_shape=jax.ShapeDtypeStruct((m, D_MODEL), F32),
        compiler_params=_cparams(("parallel",)),
        name="per_layer_embedding",
    )(*xs, g, p, wp, wgate, g_final)


def _ret_kernel(q_ref, k_ref, v_ref, gate_ref, cos_ref, sin_ref, s0_ref, o_ref, sout_ref, state, *, chunk, chunk_rows):
    i = pl.program_id(1)

    @pl.when(i == 0)
    def _():
        state[...] = s0_ref[0]

    cos, sin = cos_ref[...], sin_ref[...]
    tt = q_ref.shape[1]
    r = chunk_rows
    ii = lax.broadcasted_iota(I32, (r, r), 0).astype(F32)
    jj = lax.broadcasted_iota(I32, (r, r), 1).astype(F32)
    rel = ii - jj
    icol = lax.broadcasted_iota(I32, (r, 1), 0).astype(F32)
    half = RET_DK // 2

    def rot(ref, hd):
        x1 = ref[0, :, hd * RET_DK:hd * RET_DK + half]
        x2 = ref[0, :, hd * RET_DK + half:(hd + 1) * RET_DK]
        return jnp.concatenate([x1 * cos - x2 * sin, x2 * cos + x1 * sin], axis=1)

    for hd in range(RET_HEADS):
        lg = math.log(1.0 - 2.0 ** (-5.0 - hd))
        d_in = jnp.where(rel >= 0, jnp.exp(lg * jnp.maximum(rel, 0.0)), 0.0)
        d_q = jnp.exp(lg * (icol + 1.0))
        d_k = jnp.exp(lg * (chunk - 1.0 - icol)) * (RET_DK ** -0.5)
        d_c = math.exp(lg * chunk)
        qr = rot(q_ref, hd)
        kr = rot(k_ref, hd)
        vsl = slice(hd * RET_DV, (hd + 1) * RET_DV)
        for c in range(tt // r):
            rows = slice(c * r, (c + 1) * r)
            qc = qr[rows].astype(BF16)
            kc = kr[rows]
            vc = v_ref[0, rows, vsl].astype(BF16)
            s_prev = state[hd]
            att = _dot_nt(qc, (kc * (RET_DK ** -0.5)).astype(BF16)) * d_in
            o = _dot(att.astype(BF16), vc) + _dot(qc, s_prev.astype(BF16)) * d_q
            state[hd] = s_prev * d_c + _dot_tn((kc * d_k).astype(BF16), vc)
            mu = jnp.mean(o, axis=-1, keepdims=True)
            var = jnp.mean(jnp.square(o - mu), axis=-1, keepdims=True)
            on = (o - mu) * lax.rsqrt(var + EPS)
            o_ref[0, rows, vsl] = (jax.nn.silu(gate_ref[0, rows, vsl]) * on).astype(BF16)

    @pl.when(i == pl.num_programs(1) - 1)
    def _():
        sout_ref[0] = state[...]


def _retention(z, s0, cos, sin, tt, chunk, chunk_rows):
    b, t, _ = z.shape
    hk, hv = RET_HEADS * RET_DK, RET_HEADS * RET_DV
    half = RET_DK // 2
    state_spec = pl.BlockSpec((1, RET_HEADS, RET_DK, RET_DV), lambda bi, i: (bi, 0, 0, 0))
    tab = pl.BlockSpec((tt, half), lambda bi, i: (i, 0))
    return pl.pallas_call(
        functools.partial(_ret_kernel, chunk=chunk, chunk_rows=chunk_rows),
        grid=(b, t // tt),
        in_specs=[pl.BlockSpec((1, tt, hk), lambda bi, i: (bi, i, 0)), pl.BlockSpec((1, tt, hk), lambda bi, i: (bi, i, 1)),
                  pl.BlockSpec((1, tt, hv), lambda bi, i: (bi, i, 1)), pl.BlockSpec((1, tt, hv), lambda bi, i: (bi, i, 2)),
                  tab, tab, state_spec],
        out_specs=[pl.BlockSpec((1, tt, hv), lambda bi, i: (bi, i, 0)), state_spec],
        out_shape=[jax.ShapeDtypeStruct((b, t, hv), BF16), jax.ShapeDtypeStruct(s0.shape, F32)],
        scratch_shapes=[pltpu.VMEM((RET_HEADS, RET_DK, RET_DV), F32)],
        compiler_params=_cparams(("parallel", "arbitrary")),
        name="retention",
    )(z, z, z, z, cos, sin, s0)


def _pack_l0_w_in(w):
    offs = np.cumsum((0,) + AB_SPLITS)
    bg, cg, hv, q, k, v, qi, ki, wi = [w[:, offs[n]:offs[n + 1]] for n in range(len(AB_SPLITS))]
    group = ATT_HEADS // KV_HEADS
    q4 = q.reshape(D_MODEL, ATT_HEADS, HEAD_DIM)
    zq = jnp.zeros_like(q4)
    q_pad = jnp.concatenate([jnp.concatenate([q4[:, :group], zq[:, :group]], axis=-1),
                             jnp.concatenate([zq[:, group:], q4[:, group:]], axis=-1)], axis=1).reshape(D_MODEL, -1)
    qi4 = qi.reshape(D_MODEL, IDX_HEADS, IDX_DIM)
    qi_pad = jnp.concatenate([qi4, jnp.zeros_like(qi4)], axis=-1).reshape(D_MODEL, -1)
    kiw = jnp.concatenate([ki, wi, jnp.zeros((D_MODEL, LANES - IDX_DIM - IDX_HEADS), w.dtype)], axis=1)
    return jnp.concatenate([bg, cg, hv, q_pad, k, v, qi_pad, kiw], axis=1).astype(BF16)


def _pack_l0_w_out(w):
    group = ATT_HEADS // KV_HEADS
    wa, wb = w[:CONV_CH], w[CONV_CH:].reshape(ATT_HEADS, HEAD_DIM, D_MODEL)
    zb = jnp.zeros_like(wb)
    wb_pad = jnp.concatenate([jnp.concatenate([wb[:group], zb[:group]], axis=1),
                              jnp.concatenate([zb[group:], wb[group:]], axis=1)], axis=0).reshape(-1, D_MODEL)
    return jnp.concatenate([wa, wb_pad], axis=0).astype(BF16)


def _rope_tables(pos, reps):
    inv = ROPE_THETA ** (-jnp.arange(0, HEAD_DIM, 2, dtype=F32) / HEAD_DIM)
    ang = pos.astype(F32)[:, None] * inv[None, :]
    cos, sin = jnp.cos(ang), jnp.sin(ang)
    cos64 = jnp.concatenate([cos, cos], axis=1)
    sin64 = jnp.concatenate([-sin, sin], axis=1)
    one, zero = jnp.ones_like(cos64), jnp.zeros_like(cos64)
    tabs = (jnp.concatenate([cos64, cos64], 1), jnp.concatenate([sin64, sin64], 1),
            jnp.concatenate([cos64, one], 1), jnp.concatenate([sin64, zero], 1))
    return tuple(jnp.tile(tb, (reps, 1)) for tb in tabs)


def _ret_tables(pos):
    inv = ROPE_THETA ** (-jnp.linspace(0.0, 1.0, RET_DK // 2, dtype=F32))
    ang = pos.astype(F32)[:, None] * inv[None, :]
    return jnp.cos(ang), jnp.sin(ang)


def _pack_params(prm):
    r = prm['moe_router'][0]
    r_pad = jnp.concatenate([r, jnp.zeros((D_MODEL, LANES - N_EXPERTS), F32)], axis=1)
    r_hi = r_pad.astype(BF16)
    bf = lambda a: a.astype(BF16)
    return dict(
        l0_w_in=_pack_l0_w_in(prm['ab_w_in'][0]), l0_w_out=_pack_l0_w_out(prm['ab_w_out'][0]),
        l0_w_vt=bf(prm['ab_w_in'][0][:, sum(AB_SPLITS[:5]):sum(AB_SPLITS[:6])].T),
        conv_w=prm['ab_conv_w'][0],
        ffn=(bf(prm['ffn_w_gate'][0]), bf(prm['ffn_w_up'][0]), bf(prm['ffn_w_down'][0])),
        ret_w_in=bf(prm['ret_w_in'][0]), ret_w_out=bf(prm['ret_w_out'][0]),
        r_hi=r_hi, r_lo=(r_pad - r_hi.astype(F32)).astype(BF16),
        moe=(bf(prm['moe_w_gate'][0]), bf(prm['moe_w_up'][0]), bf(prm['moe_w_down'][0])),
        ple_w=bf(prm['ple_w']), ple_gate_w=bf(prm['ple_gate_w']),
        norm_mix=prm['norm_mix'][:, None, :], norm_ffn=prm['norm_ffn'][:, None, :],
        norm_ple=prm['norm_ple'][:, None, :], norm_final=prm['norm_final'][None, :],
    )


def _trunk(x, p, pos, conv_buf, ret_state, pk, paged):
    b, t, _ = x.shape
    m = b * t
    tm = min(m, TOKEN_TILE)
    x2 = x.reshape(m, D_MODEL)

    reps = max(1, tm // t)
    tabs = _rope_tables(pos, reps)
    bg, u, q, k, v, qi, kiw, kbf, vbf, kiwbf, vt = _l0_in_proj(x2, pk['norm_mix'][0], pk['l0_w_in'], pk['l0_w_vt'], tabs,
                                                              tm, tabs[0].shape[0] // tm, head_major=paged is None)
    seq = lambda a: a.reshape(b, t, a.shape[-1])
    if paged is None:
        yb = _dsa_prompt(q, qi, seq(kiw), seq(kbf), vt, seq(kiwbf))
    else:
        cache_k, cache_v, cache_ki, page_table = paged
        padt = lambda a: jnp.swapaxes(jnp.pad(seq(a), ((0, 0), (0, KEY_CHUNK - t), (0, 0))), 1, 2)
        n_pool = cache_k.shape[0]
        pages_t = lambda c: jnp.transpose(c, (0, 2, 3, 1)).reshape(n_pool, LANES, PAGE_SIZE)
        yb = _dsa_sample(seq(q).astype(F32), seq(qi).astype(F32), seq(kiw), padt(kbf), padt(vbf), padt(kiwbf),
                         pages_t(cache_k), pages_t(cache_v), jnp.swapaxes(cache_ki, 1, 2), page_table).astype(BF16)
    u3 = seq(u)
    ya = _conv(u3, seq(bg), conv_buf, pk['conv_w'], tt=min(t, TOKEN_TILE))
    x2 = _mm_res([ya.reshape(m, CONV_CH), yb.reshape(m, ATT_HEADS * LANES)], pk['l0_w_out'], x2, tm)
    x2 = _ffn(x2, pk['norm_ffn'][0], *pk['ffn'], tm=tm, tf=D_FF // 2)
    x2 = _ple([x2], pk['norm_ple'][0], p[0].reshape(m, PLE_DIM), pk['ple_w'][0], pk['ple_gate_w'][0],
              pk['norm_final'], False, tm)
    new_k = k.reshape(1, b, t, KV_HEADS, HEAD_DIM)
    new_v = v.reshape(1, b, t, KV_HEADS, HEAD_DIM)
    new_ki = seq(kiw)[None, :, :, :IDX_DIM]
    new_conv = jnp.concatenate([conv_buf, u3], axis=1)[None, :, -(CONV_WIDTH - 1):]

    z = _norm_mm(x2, pk['norm_mix'][1], pk['ret_w_in'], tm, 1024).reshape(b, t, -1)
    cos_r, sin_r = _ret_tables(pos)
    if t % RET_CHUNK == 0:
        og, s_new = _retention(z, ret_state, cos_r, sin_r, tt=2 * RET_CHUNK, chunk=RET_CHUNK, chunk_rows=RET_CHUNK)
    else:
        rows = 16
        padt = lambda a: jnp.pad(a, ((0, 0),) * (a.ndim - 2) + ((0, rows - t), (0, 0)))
        og, s_new = _retention(padt(z), ret_state, padt(cos_r), padt(sin_r), tt=rows, chunk=t, chunk_rows=rows)
        og = og[:, :t]
    x2 = _mm_res([og.reshape(m, RET_HEADS * RET_DV)], pk['ret_w_out'], x2, tm)
    moe = _moe(x2, pk['norm_ffn'][1], pk['r_hi'], pk['r_lo'], *pk['moe'], tm=min(m, MOE_TOKEN_TILE), tf=EXPERT_FF // 2)
    x2 = _ple([x2, moe], pk['norm_ple'][1], p[1].reshape(m, PLE_DIM), pk['ple_w'][1], pk['ple_gate_w'][1],
              pk['norm_final'], True, tm)
    return x2.reshape(b, t, D_MODEL), new_k, new_v, new_ki, new_conv, s_new[None]


def kernel(x_prompt, x_sample, cache_k, cache_v, cache_kidx, state_conv, state_ret, page_table, p_prompt, p_sample,
           norm_mix, norm_ffn, norm_ple, norm_final, ab_w_in, ab_conv_w, ab_w_out, ffn_w_gate, ffn_w_up, ffn_w_down,
           ret_w_in, ret_w_out, moe_router, moe_w_gate, moe_w_up, moe_w_down, ple_w, ple_gate_w):
    prm = dict(norm_mix=norm_mix, norm_ffn=norm_ffn, norm_ple=norm_ple, norm_final=norm_final, ab_w_in=ab_w_in,
               ab_conv_w=ab_conv_w, ab_w_out=ab_w_out, ffn_w_gate=ffn_w_gate, ffn_w_up=ffn_w_up, ffn_w_down=ffn_w_down,
               ret_w_in=ret_w_in, ret_w_out=ret_w_out, moe_router=moe_router, moe_w_gate=moe_w_gate,
               moe_w_up=moe_w_up, moe_w_down=moe_w_down, ple_w=ple_w, ple_gate_w=ple_gate_w)
    pk = _pack_params(prm)
    b, t = x_prompt.shape[0], x_prompt.shape[1]
    db, ts = x_sample.shape[0], x_sample.shape[1]
    past_len = page_table.shape[1] * PAGE_SIZE
    dt = x_prompt.dtype

    conv0 = jnp.zeros((b, CONV_WIDTH - 1, CONV_CH), dt)
    ret0 = jnp.zeros((b, RET_HEADS, RET_DK, RET_DV), dt)
    y_p, k_p, v_p, ki_p, cb_p, rs_p = _trunk(x_prompt, p_prompt, jnp.arange(t, dtype=I32), conv0, ret0, pk, None)

    pos_s = past_len + jnp.arange(ts, dtype=I32)
    paged = (cache_k[0], cache_v[0], cache_kidx[0], page_table)
    y_s, k_s, v_s, ki_s, cb_s, rs_s = _trunk(x_sample, p_sample, pos_s, state_conv[0], state_ret[0], pk, paged)
    return (y_p, y_s, k_p, v_p, ki_p, cb_p, rs_p, k_s, v_s, ki_s, cb_s, rs_s)
```

```python
import functools
import math

import jax
import jax.numpy as jnp
import numpy as np
from jax import lax
from jax.experimental import pallas as pl
from jax.experimental.pallas import tpu as pltpu

F32 = jnp.float32
BF16 = jnp.bfloat16
I32 = jnp.int32

D_MODEL = 1024
PAGE_SIZE = 128
CONV_CH = D_MODEL // 2
CONV_WIDTH = 3
ATT_HEADS = 8
KV_HEADS = 2
HEAD_DIM = 64
IDX_HEADS = 4
IDX_DIM = 64
TOPK_MAX = 256
ROPE_THETA = 10000.0
RET_HEADS = 4
RET_DK = D_MODEL // RET_HEADS
RET_DV = 2 * RET_DK
RET_CHUNK = 128
D_FF = 2816
N_EXPERTS = 8
EXPERT_FF = 3584
PLE_DIM = 256
EPS = 1e-6
AB_SPLITS = (CONV_CH, CONV_CH, CONV_CH, ATT_HEADS * HEAD_DIM, KV_HEADS * HEAD_DIM, KV_HEADS * HEAD_DIM,
             IDX_HEADS * IDX_DIM, IDX_DIM, IDX_HEADS)

LANES = 128
VMEM_LIMIT = 48 * 1024 * 1024
INT_MIN = -2 ** 31
NEG_BIG = -1e30
LOG2_E = math.log2(math.e)

C_BG, C_CG, C_HV, C_Q, C_K, C_V, C_QI, C_KIW, C_END = 0, 512, 1024, 1536, 2560, 2688, 2816, 3328, 3456
KEY_CHUNK = 512
VT_ROWS = LANES + 16
TOKEN_TILE = 512
QUERY_BLOCK = 128


def _cparams(sem):
    return pltpu.CompilerParams(dimension_semantics=sem, vmem_limit_bytes=VMEM_LIMIT)


def _rms(x, g):
    return x * lax.rsqrt(jnp.mean(x * x, axis=-1, keepdims=True) + EPS) * g


def _dot(a, b):
    return jnp.dot(a, b, preferred_element_type=F32)


def _dot_nt(a, b):
    return lax.dot_general(a, b, (((1,), (1,)), ((), ())), preferred_element_type=F32)


def _dot_tn(a, b):
    return lax.dot_general(a, b, (((0,), (0,)), ((), ())), preferred_element_type=F32)


def _swap_halves64(x):
    lane = lax.broadcasted_iota(I32, x.shape, 1)
    from_above = pltpu.roll(x, LANES - 32, 1)
    from_below = pltpu.roll(x, 32, 1)
    return jnp.where((lane & 63) < 32, from_above, from_below)


def _l0_in_kernel(x_ref, g_ref, w_ref, wvt_ref, cos_ref, sin_ref, cosb_ref, sinb_ref,
                  bg_ref, u_ref, q_ref, k_ref, v_ref, qi_ref, kiw_ref, kbf_ref, vbf_ref, kiwbf_ref, vt_ref, *, head_major):
    h = _rms(x_ref[...], g_ref[...]).astype(BF16)
    cos, sin = cos_ref[...], sin_ref[...]

    def store_head(ref, hd, val):
        if head_major:
            for r in range(ref.shape[0]):
                ref[r, hd] = val[r * QUERY_BLOCK:(r + 1) * QUERY_BLOCK]
        else:
            ref[:, hd * LANES:(hd + 1) * LANES] = val

    def rot(z, c, s):
        return z * c + _swap_halves64(z) * s

    bg_ref[...] = _dot(h, w_ref[:, C_BG:C_CG])
    u_ref[...] = _dot(h, w_ref[:, C_CG:C_HV]) * _dot(h, w_ref[:, C_HV:C_Q])
    zq = _dot(h, w_ref[:, C_Q:C_K])
    for hd in range(ATT_HEADS):
        sl = slice(hd * LANES, (hd + 1) * LANES)
        store_head(q_ref, hd, (rot(zq[:, sl], cos, sin) * (HEAD_DIM ** -0.5 * LOG2_E)).astype(BF16))
    k = rot(_dot(h, w_ref[:, C_K:C_V]), cos, sin)
    k_ref[...] = k
    kbf_ref[...] = k.astype(BF16)
    v = _dot(h, w_ref[:, C_V:C_QI])
    v_ref[...] = v
    vbf_ref[...] = v.astype(BF16)
    vt_ref[0:LANES, :] = _dot_nt(wvt_ref[...], h).astype(BF16)
    vt_ref[LANES:VT_ROWS, :] = jnp.ones((VT_ROWS - LANES, vt_ref.shape[1]), BF16)
    zqi = _dot(h, w_ref[:, C_QI:C_KIW])
    for hd in range(IDX_HEADS):
        sl = slice(hd * LANES, (hd + 1) * LANES)
        store_head(qi_ref, hd, rot(zqi[:, sl], cos, sin).astype(BF16))
    kiw = rot(_dot(h, w_ref[:, C_KIW:C_END]), cosb_ref[...], sinb_ref[...])
    kiw_ref[...] = kiw
    kiwbf_ref[...] = kiw.astype(BF16)


def _l0_in_proj(x2, g, w, wvt, tabs, tm, n_tab_blocks, head_major):
    m = x2.shape[0]
    row = lambda i: (i, 0)
    const = lambda i: (0, 0)
    tab = lambda i: (i % n_tab_blocks, 0)
    widths = (512, 512, 1024, 128, 128, 512, 128, 128, 128, 128)
    dtypes = (F32, F32, BF16, F32, F32, BF16, F32, BF16, BF16, BF16)
    out_specs = [pl.BlockSpec((tm, wd), row) for wd in widths]
    out_shape = [jax.ShapeDtypeStruct((m, wd), dt) for wd, dt in zip(widths, dtypes)]
    if head_major:
        for n in (2, 5):
            heads = widths[n] // LANES
            out_specs[n] = pl.BlockSpec((tm // QUERY_BLOCK, heads, QUERY_BLOCK, LANES), lambda i: (i, 0, 0, 0))
            out_shape[n] = jax.ShapeDtypeStruct((m // QUERY_BLOCK, heads, QUERY_BLOCK, LANES), BF16)
    return pl.pallas_call(
        functools.partial(_l0_in_kernel, head_major=head_major),
        grid=(m // tm,),
        in_specs=[pl.BlockSpec((tm, D_MODEL), row), pl.BlockSpec((1, D_MODEL), const),
                  pl.BlockSpec((D_MODEL, C_END), const), pl.BlockSpec((LANES, D_MODEL), const)]
        + [pl.BlockSpec((tm, LANES), tab)] * 4,
        out_specs=out_specs + [pl.BlockSpec((VT_ROWS, tm), lambda i: (0, i))],
        out_shape=out_shape + [jax.ShapeDtypeStruct((VT_ROWS, m), BF16)],
        compiler_params=_cparams(("parallel",)),
        name="l0_in_proj",
    )(x2, g, w, wvt, *tabs)


def _score_keys(score):
    bits = pltpu.bitcast(score, I32)
    return bits ^ ((bits >> 31) & jnp.int32(0x7FFFFFFF))


def _key_scores(key):
    return pltpu.bitcast(key ^ ((key >> 31) & jnp.int32(0x7FFFFFFF)), F32)


SEARCH_PLAIN_STEPS = 8
SEARCH_SNAP_INTERP_STEPS = 8
SEARCH_MAX_STEPS = SEARCH_PLAIN_STEPS + SEARCH_SNAP_INTERP_STEPS + 34


def _topk_threshold(probe, count_tie_below, vmin, vmax, n_real, topk, n_index_bits):
    k = float(topk)
    lo0, hi0 = _score_keys(vmin), _score_keys(vmax) + 1
    take_all = n_real <= k
    zero, one = jnp.zeros_like(vmin), jnp.ones_like(vmin)
    active0 = jnp.where(jnp.logical_not(take_all) & (lo0 + 1 < hi0), 1.0, 0.0)

    def step(snap, st):
        it, lo, hi, c_lo, c_hi, w_lo, w_hi, last, active = st
        act = active > 0.0
        f_lo = (c_lo - (k - 0.5)) * w_lo
        f_hi = ((k - 0.5) - c_hi) * w_hi
        v_lo, v_hi = _key_scores(lo), _key_scores(hi)
        g = _score_keys(v_lo + (v_hi - v_lo) * (f_lo / (f_lo + f_hi)))
        if snap:
            mid = (lo >> 1) + (hi >> 1) + (lo & hi & 1)
            g = jnp.where(it < SEARCH_PLAIN_STEPS + SEARCH_SNAP_INTERP_STEPS, g, mid)
        g = jnp.minimum(jnp.maximum(g, lo + 1), hi - 1)
        c, key_up, key_dn = probe(g, snap)
        hit = act & (c == k)
        up = act & (c > k)
        dn = act & (c < k)
        lo = jnp.where(hit, g, jnp.where(up, key_up if snap else g, lo))
        c_lo = jnp.where(hit | up, c, c_lo)
        hi = jnp.where(dn, key_dn + 1 if snap else g, hi)
        c_hi = jnp.where(dn, c, c_hi)
        w_hi = jnp.where(up, jnp.where(last > 0.0, w_hi * 0.5, one), jnp.where(dn, one, w_hi))
        w_lo = jnp.where(dn, jnp.where(last < 0.0, w_lo * 0.5, one), jnp.where(up, one, w_lo))
        last = jnp.where(up, one, jnp.where(dn, -one, last))
        active = jnp.where(act & jnp.logical_not(hit) & (lo + 1 < hi), 1.0, 0.0)
        return it + 1, lo, hi, c_lo, c_hi, w_lo, w_hi, last, active

    def cond(limit, st):
        return (st[0] < limit) & (jnp.max(st[-1]) > 0.0)

    st = (jnp.int32(0), lo0, hi0, n_real, zero, one, one, zero, active0)
    st = lax.fori_loop(0, SEARCH_PLAIN_STEPS, lambda _, s: step(False, s), st)
    st = lax.while_loop(functools.partial(cond, SEARCH_MAX_STEPS), functools.partial(step, True), st)
    _, lo, _, c_lo, c_hi, _, _, _, _ = st
    thr = jnp.where(take_all, INT_MIN, lo)
    excess = jnp.logical_not(take_all) & (c_lo > k)
    need = k - c_hi

    def tie_search(_):
        def step(i, x):
            cand = x | lax.shift_left(jnp.int32(1), n_index_bits - 1 - i)
            return jnp.where(count_tie_below(thr, cand) < need, cand, x)
        return lax.fori_loop(0, n_index_bits, step, jnp.zeros_like(thr))

    any_excess = jnp.max(jnp.where(excess, 1.0, 0.0)) > 0.0
    cut = lax.cond(any_excess, tie_search, lambda _: jnp.zeros_like(thr), 0)
    cut = jnp.where(excess, cut, jnp.where(take_all, jnp.int32(-1), jnp.int32(2 ** 31 - 1)))
    return thr, cut


def _selection_bias(key, idx, thr, cut):
    sel = (key > thr) | ((key == thr) & (idx <= cut))
    return pltpu.bitcast(jnp.where(sel, 0.0, NEG_BIG).astype(F32), I32)


ROW_SLAB = 2048


def _lane_fold(m, op=jnp.add):
    parts = [m[:, c * LANES:(c + 1) * LANES] for c in range(m.shape[1] // LANES)]
    while len(parts) > 1:
        parts = [op(parts[n], parts[n + 1]) for n in range(0, len(parts) - 1, 2)] + (parts[-1:] if len(parts) % 2 else [])
    return parts[0]


def _row_slabs(n_chunks):
    width = n_chunks * KEY_CHUNK
    return [(k0, min(ROW_SLAB, width - k0)) for k0 in range(0, width, ROW_SLAB)]


def _row_count(s_ref, n_chunks, pred):
    acc = None
    for k0, size in _row_slabs(n_chunks):
        blk = s_ref[:, k0:k0 + size]
        idx = k0 + lax.broadcasted_iota(I32, blk.shape, 1)
        part = _lane_fold(jnp.where(pred(blk, idx), 1.0, 0.0))
        acc = part if acc is None else acc + part
    return jnp.sum(acc, axis=1, keepdims=True)


def _int_reduce(x, take_min, axis):
    red = jnp.min if take_min else jnp.max
    hi = (x >> 16).astype(F32)
    lo = (x & 0xFFFF).astype(F32)
    m_hi = red(hi, axis=axis, keepdims=True)
    m_lo = red(jnp.where(hi == m_hi, lo, 65536.0 if take_min else -1.0), axis=axis, keepdims=True)
    return (m_hi.astype(I32) << 16) | m_lo.astype(I32)


def _row_probe(s_ref, n_chunks, g, snap):
    cnt = up = dn = None
    for k0, size in _row_slabs(n_chunks):
        blk = s_ref[:, k0:k0 + size]
        ge = blk >= g
        part = _lane_fold(jnp.where(ge, 1.0, 0.0))
        cnt = part if cnt is None else cnt + part
        if snap:
            above = _lane_fold(jnp.where(ge, blk, 2 ** 31 - 1), jnp.minimum)
            below = _lane_fold(jnp.where(ge, INT_MIN, blk), jnp.maximum)
            up = above if up is None else jnp.minimum(up, above)
            dn = below if dn is None else jnp.maximum(dn, below)
    c = jnp.sum(cnt, axis=1, keepdims=True)
    if not snap:
        return c, None, None
    return c, _int_reduce(up, True, 1), _int_reduce(dn, False, 1)


def _row_select_to_bias(s_ref, n_chunks, n_real, topk):
    inf = jnp.float32(jnp.inf)
    mn = mx = None
    for k0, size in _row_slabs(n_chunks):
        blk = s_ref[:, k0:k0 + size]
        v = _key_scores(blk)
        real = blk != INT_MIN
        lo_part = _lane_fold(jnp.where(real, v, inf), jnp.minimum)
        hi_part = _lane_fold(jnp.where(real, v, -inf), jnp.maximum)
        mn = lo_part if mn is None else jnp.minimum(mn, lo_part)
        mx = hi_part if mx is None else jnp.maximum(mx, hi_part)
    vmin, vmax = jnp.min(mn, axis=1, keepdims=True), jnp.max(mx, axis=1, keepdims=True)
    thr, cut = _topk_threshold(
        functools.partial(_row_probe, s_ref, n_chunks),
        lambda t, x: _row_count(s_ref, n_chunks, lambda blk, idx: (blk == t) & (idx < x)),
        vmin, vmax, n_real, topk, (s_ref.shape[1] - 1).bit_length())
    for k0, size in _row_slabs(n_chunks):
        blk = s_ref[:, k0:k0 + size]
        idx = k0 + lax.broadcasted_iota(I32, blk.shape, 1)
        s_ref[:, k0:k0 + size] = _selection_bias(blk, idx, thr, cut)


def _attend_group(qg, segments, bias_ref, rep):
    n = qg.shape[0]

    def step(chunk, base, load_kv, c, carry):
        m, l, acc = carry
        k0 = pl.multiple_of(base + c * chunk, KEY_CHUNK)
        ktc, vtc = load_kv(k0)
        bias = pltpu.bitcast(bias_ref[:, pl.ds(k0, chunk)], F32)
        s = _dot(qg, ktc) + jnp.concatenate([bias] * rep, axis=0)
        m_new = jnp.maximum(m, jnp.max(s, axis=1, keepdims=True))
        alpha = jnp.exp2(m - m_new)
        p = jnp.exp2(s - m_new)
        l = alpha * l + jnp.sum(p, axis=1, keepdims=True)
        acc = alpha * acc + _dot_nt(p.astype(BF16), vtc)
        return m_new, l, acc

    carry = (jnp.full((n, 1), NEG_BIG, F32), jnp.zeros((n, 1), F32), jnp.zeros((n, LANES), F32))
    for n_steps, chunk, base, load_kv in segments:
        carry = lax.fori_loop(0, n_steps, functools.partial(step, chunk, base, load_kv), carry)
    _, l, acc = carry
    return acc / l


def _stack_heads(x, first, count, width=LANES):
    x = x.astype(F32)
    return jnp.concatenate([x[:, (first + hd) * LANES:(first + hd) * LANES + width] for hd in range(count)],
                           axis=0).astype(BF16)


def _attend_all_heads(q, segments, bias_ref, o_ref):
    rows = q.shape[0]
    group = ATT_HEADS // KV_HEADS
    out = _attend_group(_stack_heads(q, 0, ATT_HEADS), segments, bias_ref, ATT_HEADS)
    for hd in range(ATT_HEADS):
        piece = out[hd * rows:(hd + 1) * rows]
        piece = jnp.where(_group_lane_mask(piece.shape, hd // group), piece, 0.0)
        o_ref[0, :, hd * LANES:(hd + 1) * LANES] = piece.astype(o_ref.dtype)


def _indexer_weights(kiw_q):
    return kiw_q[:, IDX_DIM:IDX_DIM + IDX_HEADS] * (IDX_HEADS ** -0.5 * IDX_DIM ** -0.5)


def _group_lane_mask(shape, g):
    lane = lax.broadcasted_iota(I32, shape, 1)
    return (lane >= g * HEAD_DIM) & (lane < (g + 1) * HEAD_DIM)


ATT_CHUNK = 256


def _sublane_fold(m, op=jnp.add):
    parts = [m[r * 8:(r + 1) * 8] for r in range(m.shape[0] // 8)]
    while len(parts) > 1:
        parts = [op(parts[n], parts[n + 1]) for n in range(0, len(parts) - 1, 2)] + (parts[-1:] if len(parts) % 2 else [])
    return parts[0]


def _col_reduce(m, op, reduce_fn):
    return reduce_fn(_sublane_fold(m, op), axis=0, keepdims=True)


def _col_count(s_ref, n_chunks, pred):
    cols = s_ref.shape[1]

    def body(c, acc):
        k0 = pl.multiple_of(c * KEY_CHUNK, KEY_CHUNK)
        blk = s_ref[pl.ds(k0, KEY_CHUNK), :]
        idx = k0 + lax.broadcasted_iota(I32, blk.shape, 0)
        return acc + _sublane_fold(jnp.where(pred(blk, idx), 1.0, 0.0))

    acc = lax.fori_loop(0, n_chunks, body, jnp.zeros((8, cols), F32))
    return jnp.sum(acc, axis=0, keepdims=True)


def _col_probe(s_ref, n_chunks, g, snap):
    cols = s_ref.shape[1]

    def body(c, carry):
        k0 = pl.multiple_of(c * KEY_CHUNK, KEY_CHUNK)
        blk = s_ref[pl.ds(k0, KEY_CHUNK), :]
        ge = blk >= g
        out = [carry[0] + _sublane_fold(jnp.where(ge, 1.0, 0.0))]
        if snap:
            out.append(jnp.minimum(carry[1], _sublane_fold(jnp.where(ge, blk, 2 ** 31 - 1), jnp.minimum)))
            out.append(jnp.maximum(carry[2], _sublane_fold(jnp.where(ge, INT_MIN, blk), jnp.maximum)))
        return tuple(out)

    init = [jnp.zeros((8, cols), F32)]
    if snap:
        init += [jnp.full((8, cols), 2 ** 31 - 1, I32), jnp.full((8, cols), INT_MIN, I32)]
    res = lax.fori_loop(0, n_chunks, body, tuple(init))
    c = jnp.sum(res[0], axis=0, keepdims=True)
    if not snap:
        return c, None, None
    return c, _int_reduce(res[1], True, 0), _int_reduce(res[2], False, 0)


def _col_select_to_bias(s_ref, n_chunks, n_real, topk, vmin, vmax):
    thr, cut = _topk_threshold(
        functools.partial(_col_probe, s_ref, n_chunks),
        lambda t, x: _col_count(s_ref, n_chunks, lambda blk, idx: (blk == t) & (idx < x)),
        vmin, vmax, n_real, topk, (s_ref.shape[0] - 1).bit_length())

    def to_bias(c, _):
        k0 = pl.multiple_of(c * KEY_CHUNK, KEY_CHUNK)
        blk = s_ref[pl.ds(k0, KEY_CHUNK), :]
        idx = k0 + lax.broadcasted_iota(I32, blk.shape, 0)
        s_ref[pl.ds(k0, KEY_CHUNK), :] = _selection_bias(blk, idx, thr, cut)
        return 0

    lax.fori_loop(0, n_chunks, to_bias, 0)


def _dsa_prompt_kernel(q_ref, qi_ref, kiwq_ref, k_ref, vt_ref, kiw_ref, o_ref, s_ref, m_scr, acc_scr,
                       sa_scr, sb_scr, *, qb, topk):
    j = pl.program_id(1)
    n_keys = j * qb + qb
    n_chunks = (n_keys + KEY_CHUNK - 1) // KEY_CHUNK
    qpos = j * qb + lax.broadcasted_iota(I32, (1, qb), 1)
    wt = kiwq_ref[0].T[IDX_DIM:IDX_DIM + IDX_HEADS] * (IDX_HEADS ** -0.5 * IDX_DIM ** -0.5)

    inf = jnp.float32(jnp.inf)

    def scores(c, carry):
        mn, mx = carry
        k0 = pl.multiple_of(c * KEY_CHUNK, KEY_CHUNK)
        kc = kiw_ref[0, pl.ds(k0, KEY_CHUNK), :]
        acc = jnp.zeros((KEY_CHUNK, qb), F32)
        for pr in range(IDX_HEADS // 2):
            s2 = _dot_nt(kc, qi_ref[0, 2 * pr:2 * pr + 2].reshape(2 * qb, LANES))
            acc = acc + jnp.maximum(s2[:, :qb], 0.0) * wt[2 * pr:2 * pr + 1]
            acc = acc + jnp.maximum(s2[:, qb:], 0.0) * wt[2 * pr + 1:2 * pr + 2]
        causal = k0 + lax.broadcasted_iota(I32, (KEY_CHUNK, qb), 0) <= qpos
        s_ref[pl.ds(k0, KEY_CHUNK), :] = jnp.where(causal, _score_keys(acc), INT_MIN)
        mn = jnp.minimum(mn, _sublane_fold(jnp.where(causal, acc, inf), jnp.minimum))
        mx = jnp.maximum(mx, _sublane_fold(jnp.where(causal, acc, -inf), jnp.maximum))
        return mn, mx

    mn, mx = lax.fori_loop(0, n_chunks, scores, (jnp.full((8, qb), inf, F32), jnp.full((8, qb), -inf, F32)))
    _col_select_to_bias(s_ref, n_chunks, (qpos + 1).astype(F32), topk,
                        jnp.min(mn, axis=0, keepdims=True), jnp.max(mx, axis=0, keepdims=True))

    m_scr[...] = jnp.full(m_scr.shape, NEG_BIG, F32)
    acc_scr[...] = jnp.zeros(acc_scr.shape, F32)

    n_att = n_chunks * (KEY_CHUNK // ATT_CHUNK)

    def logits(step, buf):
        k0 = pl.multiple_of(jnp.minimum(step, n_att - 1) * ATT_CHUNK, ATT_CHUNK)
        kc = k_ref[0, pl.ds(k0, ATT_CHUNK), :]
        bias = pltpu.bitcast(s_ref[pl.ds(k0, ATT_CHUNK), :], F32)
        for pr in range(ATT_HEADS // 2):
            s2 = _dot_nt(kc, q_ref[0, 2 * pr:2 * pr + 2].reshape(2 * qb, LANES))
            buf[2 * pr] = s2[:, :qb] + bias
            buf[2 * pr + 1] = s2[:, qb:] + bias

    def accumulate(step, buf):
        k0 = pl.multiple_of(step * ATT_CHUNK, ATT_CHUNK)
        vtc = vt_ref[:, pl.ds(k0, ATT_CHUNK)]
        for pr in range(ATT_HEADS // 2):
            ps, alphas = [], []
            for hd in (2 * pr, 2 * pr + 1):
                s = buf[hd]
                m_prev = m_scr[hd:hd + 1, :]
                m_new = jnp.maximum(m_prev, _col_reduce(s, jnp.maximum, jnp.max))
                alphas.append(jnp.exp2(m_prev - m_new))
                ps.append(jnp.exp2(s - m_new).astype(BF16))
                m_scr[hd:hd + 1, :] = m_new
            acc_scr[pr] = jnp.concatenate(alphas, axis=1) * acc_scr[pr] + _dot(vtc, jnp.concatenate(ps, axis=1))

    logits(0, sa_scr)

    def attend(c, _):
        logits(2 * c + 1, sb_scr)
        accumulate(2 * c, sa_scr)
        logits(2 * c + 2, sa_scr)
        accumulate(2 * c + 1, sb_scr)
        return 0

    lax.fori_loop(0, n_att // 2, attend, 0)

    group = ATT_HEADS // KV_HEADS
    for hd in range(ATT_HEADS):
        cols = slice((hd % 2) * qb, (hd % 2 + 1) * qb)
        out_t = acc_scr[hd // 2, 0:LANES, cols] / acc_scr[hd // 2, LANES:LANES + 1, cols]
        row = lax.broadcasted_iota(I32, out_t.shape, 0)
        g = hd // group
        out_t = jnp.where((row >= g * HEAD_DIM) & (row < (g + 1) * HEAD_DIM), out_t, 0.0)
        o_ref[0, :, hd * LANES:(hd + 1) * LANES] = out_t.T.astype(BF16)


def _dsa_prompt(q, qi, kiw, kbf, vt, kiwbf):
    b, t, _ = kiw.shape
    qb = QUERY_BLOCK
    nb = t // qb
    topk = min(TOPK_MAX, t // 4)
    qblk = lambda width: pl.BlockSpec((1, qb, width), lambda bi, j: (bi, j, 0))
    hblk = lambda heads: pl.BlockSpec((1, heads, qb, LANES), lambda bi, j: (bi * nb + j, 0, 0, 0))
    full = pl.BlockSpec((1, t, LANES), lambda bi, j: (bi, 0, 0))
    return pl.pallas_call(
        functools.partial(_dsa_prompt_kernel, qb=qb, topk=topk),
        grid=(b, nb),
        in_specs=[hblk(ATT_HEADS), hblk(IDX_HEADS), qblk(LANES), full,
                  pl.BlockSpec((VT_ROWS, t), lambda bi, j: (0, bi)), full],
        out_specs=qblk(ATT_HEADS * LANES),
        out_shape=jax.ShapeDtypeStruct((b, t, ATT_HEADS * LANES), BF16),
        scratch_shapes=[pltpu.VMEM((pl.cdiv(t, KEY_CHUNK) * KEY_CHUNK, qb), I32), pltpu.VMEM((ATT_HEADS, qb), F32),
                        pltpu.VMEM((ATT_HEADS // 2, VT_ROWS, 2 * qb), F32),
                        pltpu.VMEM((ATT_HEADS, ATT_CHUNK, qb), F32), pltpu.VMEM((ATT_HEADS, ATT_CHUNK, qb), F32)],
        compiler_params=_cparams(("parallel", "arbitrary")),
        name="dsa_prompt",
    )(q, qi, kiw, kbf, vt, kiwbf)


def _dsa_sample_kernel(pt_ref, q_ref, qi_ref, kiwq_ref, kn_ref, vn_ref, kiwn_ref, ck_hbm, cv_hbm, cki_hbm,
                       o_ref, kbuf, vbuf, kibuf, s_ref, sems, *, ts, n_pages, topk):
    b = pl.program_id(0)
    past = n_pages * PAGE_SIZE

    streams = ((cki_hbm, kibuf), (ck_hbm, kbuf), (cv_hbm, vbuf))

    def page_copy(p, which):
        src, dst = streams[which]
        cols = pl.ds(pl.multiple_of(p * PAGE_SIZE, PAGE_SIZE), PAGE_SIZE)
        return pltpu.make_async_copy(src.at[pt_ref[b, p]], dst.at[:, cols], sems.at[which])

    def start_page(p, _):
        for which in range(len(streams)):
            page_copy(p, which).start()
        return 0

    lax.fori_loop(0, n_pages, start_page, 0)

    def wait_pages(which):
        def body(p, _):
            page_copy(p, which).wait()
            return 0
        lax.fori_loop(0, n_pages, body, 0)

    n_past_chunks = past // KEY_CHUNK
    n_chunks = n_past_chunks + 1
    qpos = lax.broadcasted_iota(I32, (ts, 1), 0)
    wq = _indexer_weights(kiwq_ref[0])
    qi_stack = _stack_heads(qi_ref[0], 0, IDX_HEADS, IDX_DIM)
    w_stack = jnp.concatenate([wq[:, hd:hd + 1] for hd in range(IDX_HEADS)], axis=0)

    def head_sum(x):
        acc = x[0:ts]
        for hd in range(1, IDX_HEADS):
            acc = acc + x[hd * ts:(hd + 1) * ts]
        return acc

    wait_pages(0)

    step_keys = math.gcd(past, ROW_SLAB)

    def past_scores(c, _):
        k0 = pl.multiple_of(c * step_keys, KEY_CHUNK)
        ktc = kibuf[:, pl.ds(k0, step_keys)].astype(BF16)
        acc = head_sum(jnp.maximum(_dot(qi_stack, ktc), 0.0) * w_stack)
        s_ref[:, pl.ds(k0, step_keys)] = _score_keys(acc + 0.0)
        return 0

    lax.fori_loop(0, past // step_keys, past_scores, 0)
    acc = head_sum(jnp.maximum(_dot(qi_stack, kiwn_ref[0, 0:IDX_DIM, :]), 0.0) * w_stack)
    kpos = lax.broadcasted_iota(I32, (ts, KEY_CHUNK), 1)
    s_ref[:, pl.ds(past, KEY_CHUNK)] = jnp.where(kpos <= qpos, _score_keys(acc + 0.0), INT_MIN)
    _row_select_to_bias(s_ref, n_chunks, (past + 1 + qpos).astype(F32), topk)

    wait_pages(1)
    wait_pages(2)

    def load_past(k0):
        return kbuf[:, pl.ds(k0, step_keys)].astype(BF16), vbuf[:, pl.ds(k0, step_keys)].astype(BF16)

    def load_new(k0):
        return kn_ref[0], vn_ref[0]

    segments = [(past // step_keys, step_keys, 0, load_past), (1, KEY_CHUNK, past, load_new)]
    _attend_all_heads(q_ref[0], segments, s_ref, o_ref)


def _dsa_sample(q, qi, kiw, knt, vnt, kiwnt, cache_kt, cache_vt, cache_kit, page_table):
    b, ts, _ = q.shape
    n_pages = page_table.shape[1]
    past = n_pages * PAGE_SIZE
    topk = min(TOPK_MAX, (past + ts) // 4)
    blk = lambda rows, width: pl.BlockSpec((1, rows, width), lambda bi, pt: (bi, 0, 0))
    hbm = pl.BlockSpec(memory_space=pl.ANY)
    grid_spec = pltpu.PrefetchScalarGridSpec(
        num_scalar_prefetch=1,
        grid=(b,),
        in_specs=[blk(ts, ATT_HEADS * LANES), blk(ts, IDX_HEADS * LANES), blk(ts, LANES),
                  blk(LANES, KEY_CHUNK), blk(LANES, KEY_CHUNK), blk(LANES, KEY_CHUNK), hbm, hbm, hbm],
        out_specs=blk(ts, ATT_HEADS * LANES),
        scratch_shapes=[pltpu.VMEM((LANES, past), F32), pltpu.VMEM((LANES, past), F32),
                        pltpu.VMEM((IDX_DIM, past), F32), pltpu.VMEM((ts, past + KEY_CHUNK), I32),
                        pltpu.SemaphoreType.DMA((3,))],
    )
    return pl.pallas_call(
        functools.partial(_dsa_sample_kernel, ts=ts, n_pages=n_pages, topk=topk),
        grid_spec=grid_spec,
        out_shape=jax.ShapeDtypeStruct((b, ts, ATT_HEADS * LANES), F32),
        compiler_params=_cparams(("arbitrary",)),
        name="dsa_sample",
    )(page_table, q, qi, kiw, knt, vnt, kiwnt, cache_kt, cache_vt, cache_kit)


def _conv_kernel(u_ref, halo_ref, buf_ref, bg_ref, w_ref, ya_ref):
    i = pl.program_id(1)
    u = u_ref[0]
    halo, buf = halo_ref[0], buf_ref[0]
    first = i == 0
    prev1 = jnp.where(first, buf[1:2], halo[7:8])
    prev2 = jnp.where(first, buf[0:1], halo[6:7])
    row = lax.broadcasted_iota(I32, u.shape, 0)
    um1 = jnp.where(row == 0, prev1, pltpu.roll(u, 1, 0))
    um2 = jnp.where(row == 0, prev2, jnp.where(row == 1, prev1, pltpu.roll(u, 2, 0)))
    w = w_ref[...]
    conv = w[0:1] * um2 + w[1:2] * um1 + w[2:3] * u
    ya_ref[0] = (bg_ref[0] * conv).astype(BF16)


def _conv(u, bg, buf, w, tt):
    b, t, c = u.shape
    halo_rows = 8
    tile = pl.BlockSpec((1, tt, c), lambda bi, i: (bi, i, 0))
    halo = pl.BlockSpec((1, halo_rows, c), lambda bi, i: (bi, jnp.maximum(i * (tt // halo_rows) - 1, 0), 0))
    return pl.pallas_call(
        _conv_kernel,
        grid=(b, t // tt),
        in_specs=[tile, halo, pl.BlockSpec((1, CONV_WIDTH - 1, c), lambda bi, i: (bi, 0, 0)), tile,
                  pl.BlockSpec((CONV_WIDTH, c), lambda bi, i: (0, 0))],
        out_specs=tile,
        out_shape=jax.ShapeDtypeStruct((b, t, c), BF16),
        compiler_params=_cparams(("parallel", "parallel")),
        name="short_conv",
    )(u, u, buf, bg, w)


def _mm_res_kernel(*refs, n_in):
    a_refs, w_ref, x_ref, o_ref = refs[:n_in], refs[n_in], refs[n_in + 1], refs[n_in + 2]
    a = jnp.concatenate([r[...] for r in a_refs], axis=1) if n_in > 1 else a_refs[0][...]
    o_ref[...] = x_ref[...] + _dot(a, w_ref[...])


def _mm_res(a_list, w, x, tm):
    m = x.shape[0]
    row = lambda i: (i, 0)
    return pl.pallas_call(
        functools.partial(_mm_res_kernel, n_in=len(a_list)),
        grid=(m // tm,),
        in_specs=[pl.BlockSpec((tm, a.shape[1]), row) for a in a_list]
        + [pl.BlockSpec(w.shape, lambda i: (0, 0)), pl.BlockSpec((tm, D_MODEL), row)],
        out_specs=pl.BlockSpec((tm, D_MODEL), row),
        out_shape=jax.ShapeDtypeStruct((m, D_MODEL), F32),
        compiler_params=_cparams(("parallel",)),
        name="matmul_residual",
    )(*a_list, w, x)


def _ret_in_kernel(x_ref, g_ref, w_ref, qk_ref, vg_ref):
    h = _rms(x_ref[...], g_ref[...]).astype(BF16)
    n_qk = qk_ref.shape[1]
    qk_ref[...] = _dot(h, w_ref[:, :n_qk])
    vg_ref[...] = _dot(h, w_ref[:, n_qk:]).astype(BF16)


def _ret_in_proj(x, g, w, tm):
    m, n = x.shape[0], w.shape[1]
    n_qk = 2 * RET_HEADS * RET_DK
    row = lambda i: (i, 0)
    const = lambda i: (0, 0)
    return pl.pallas_call(
        _ret_in_kernel,
        grid=(m // tm,),
        in_specs=[pl.BlockSpec((tm, D_MODEL), row), pl.BlockSpec((1, D_MODEL), const), pl.BlockSpec((D_MODEL, n), const)],
        out_specs=[pl.BlockSpec((tm, n_qk), row), pl.BlockSpec((tm, n - n_qk), row)],
        out_shape=[jax.ShapeDtypeStruct((m, n_qk), F32), jax.ShapeDtypeStruct((m, n - n_qk), BF16)],
        compiler_params=_cparams(("parallel",)),
        name="retention_in_proj",
    )(x, g, w)


def _ffn_kernel(x_ref, g_ref, wg_ref, wu_ref, wd_ref, o_ref, h_scr, acc):
    f = pl.program_id(1)

    @pl.when(f == 0)
    def _():
        x = x_ref[...]
        h_scr[...] = _rms(x, g_ref[...]).astype(BF16)
        acc[...] = x

    h = h_scr[...]
    a = jax.nn.silu(_dot(h, wg_ref[...])) * _dot(h, wu_ref[...])
    acc[...] += _dot(a.astype(BF16), wd_ref[...])

    @pl.when(f == pl.num_programs(1) - 1)
    def _():
        o_ref[...] = acc[...]


def _ffn(x, g, wg, wu, wd, tm, tf):
    m, ff = x.shape[0], wg.shape[1]
    row = lambda i, f: (i, 0)
    return pl.pallas_call(
        _ffn_kernel,
        grid=(m // tm, ff // tf),
        in_specs=[pl.BlockSpec((tm, D_MODEL), row), pl.BlockSpec((1, D_MODEL), lambda i, f: (0, 0)),
                  pl.BlockSpec((D_MODEL, tf), lambda i, f: (0, f)), pl.BlockSpec((D_MODEL, tf), lambda i, f: (0, f)),
                  pl.BlockSpec((tf, D_MODEL), lambda i, f: (f, 0))],
        out_specs=pl.BlockSpec((tm, D_MODEL), row),
        out_shape=jax.ShapeDtypeStruct((m, D_MODEL), F32),
        scratch_shapes=[pltpu.VMEM((tm, D_MODEL), BF16), pltpu.VMEM((tm, D_MODEL), F32)],
        compiler_params=_cparams(("parallel", "arbitrary")),
        name="dense_swiglu",
    )(x, g, wg, wu, wd)


def _top2_gates(logits):
    lane = lax.broadcasted_iota(I32, logits.shape, 1).astype(F32)
    neg = jnp.float32(-jnp.inf)
    l1 = jnp.where(lane < N_EXPERTS, logits, neg)
    m1 = jnp.max(l1, axis=1, keepdims=True)
    i1 = jnp.min(jnp.where(l1 == m1, lane, float(LANES)), axis=1, keepdims=True)
    l2 = jnp.where(lane == i1, neg, l1)
    m2 = jnp.max(l2, axis=1, keepdims=True)
    i2 = jnp.min(jnp.where(l2 == m2, lane, float(LANES)), axis=1, keepdims=True)
    e = jnp.exp(m2 - m1)
    w1 = 1.0 / (1.0 + e)
    w2 = e / (1.0 + e)
    first, second = lane == i1, lane == i2
    return jnp.where(first, w1, jnp.where(second, w2, 0.0)), jnp.where(first | second, 1.0, 0.0)


MOE_SUB = 128
MOE_TOKEN_TILE = 1024


def _moe_route_kernel(x_ref, g_ref, rhi_ref, rlo_ref, h_ref, gate_ref, posc_ref, posr_ref, cnt_ref):
    tm = x_ref.shape[0]
    hn = _rms(x_ref[...], g_ref[...])
    h_hi = hn.astype(BF16)
    h_lo = (hn - h_hi.astype(F32)).astype(BF16)
    logits = _dot(h_hi, rhi_ref[...]) + (_dot(h_lo, rhi_ref[...]) + _dot(h_hi, rlo_ref[...]))
    gate, routed = _top2_gates(logits)
    h_ref[...] = h_hi
    gate_ref[...] = gate
    earlier = (lax.broadcasted_iota(I32, (tm, tm), 0) > lax.broadcasted_iota(I32, (tm, tm), 1))
    slot = _dot(jnp.where(earlier, 1.0, 0.0).astype(BF16), routed.astype(BF16))
    posc = jnp.where(routed > 0.0, slot, -1.0)
    posc_ref[...] = posc
    posr_ref[0] = posc.T[0:N_EXPERTS]
    cnt_ref[0] = jnp.broadcast_to(jnp.sum(routed, axis=0, keepdims=True), (8, LANES))


def _moe_expert_kernel(cnt_ref, h_ref, gate_ref, posc_ref, posr_ref, wg_ref, wu_ref, wd_ref, o_ref, xg, yacc):
    i, e, f = pl.program_id(0), pl.program_id(1), pl.program_id(2)
    tm = h_ref.shape[0]
    n_sub = (cnt_ref[i * N_EXPERTS + e] + MOE_SUB - 1) // MOE_SUB

    @pl.when((e == 0) & (f == 0))
    def _():
        o_ref[...] = jnp.zeros(o_ref.shape, F32)

    @pl.when(f == 0)
    def _():
        posr = posr_ref[0]
        h = h_ref[...]

        def gather(s, _):
            base = pl.multiple_of(s * MOE_SUB, MOE_SUB)
            slot = (base + lax.broadcasted_iota(I32, (MOE_SUB, tm), 0)).astype(F32)
            onehot = jnp.where(posr == slot, 1.0, 0.0).astype(BF16)
            xg[pl.ds(base, MOE_SUB), :] = _dot(onehot, h).astype(BF16)
            yacc[pl.ds(base, MOE_SUB), :] = jnp.zeros((MOE_SUB, D_MODEL), F32)
            return 0

        lax.fori_loop(0, n_sub, gather, 0)

    def expert(s, _):
        rows = pl.ds(pl.multiple_of(s * MOE_SUB, MOE_SUB), MOE_SUB)
        xs = xg[rows, :]
        a = jax.nn.silu(_dot(xs, wg_ref[0])) * _dot(xs, wu_ref[0])
        yacc[rows, :] += _dot(a.astype(BF16), wd_ref[0])
        return 0

    lax.fori_loop(0, n_sub, expert, 0)

    @pl.when(f == pl.num_programs(2) - 1)
    def _():
        lane = lax.broadcasted_iota(I32, (tm, LANES), 1)
        mine = lane == e
        posc = jnp.sum(jnp.where(mine, posc_ref[...], 0.0), axis=1, keepdims=True)
        gate = jnp.sum(jnp.where(mine, gate_ref[...], 0.0), axis=1, keepdims=True)

        def scatter(s, _):
            base = pl.multiple_of(s * MOE_SUB, MOE_SUB)
            slot = (base + lax.broadcasted_iota(I32, (tm, MOE_SUB), 1)).astype(F32)
            onehot = jnp.where(posc == slot, 1.0, 0.0).astype(BF16)
            o_ref[...] += gate * _dot(onehot, yacc[pl.ds(base, MOE_SUB), :].astype(BF16))
            return 0

        lax.fori_loop(0, n_sub, scatter, 0)


def _moe(x, g, r_hi, r_lo, wg, wu, wd, tm, tf):
    m = x.shape[0]
    nt = m // tm
    row = lambda i: (i, 0)
    const = lambda i: (0, 0)
    h, gate, posc, posr, cnt = pl.pallas_call(
        _moe_route_kernel,
        grid=(nt,),
        in_specs=[pl.BlockSpec((tm, D_MODEL), row), pl.BlockSpec((1, D_MODEL), const),
                  pl.BlockSpec((D_MODEL, LANES), const), pl.BlockSpec((D_MODEL, LANES), const)],
        out_specs=[pl.BlockSpec((tm, D_MODEL), row), pl.BlockSpec((tm, LANES), row), pl.BlockSpec((tm, LANES), row),
                   pl.BlockSpec((1, N_EXPERTS, tm), lambda i: (i, 0, 0)), pl.BlockSpec((1, 8, LANES), lambda i: (i, 0, 0))],
        out_shape=[jax.ShapeDtypeStruct((m, D_MODEL), BF16), jax.ShapeDtypeStruct((m, LANES), F32),
                   jax.ShapeDtypeStruct((m, LANES), F32), jax.ShapeDtypeStruct((nt, N_EXPERTS, tm), F32),
                   jax.ShapeDtypeStruct((nt, 8, LANES), F32)],
        compiler_params=_cparams(("parallel",)),
        name="moe_route",
    )(x, g, r_hi, r_lo)
    counts = cnt[:, 0, :N_EXPERTS].astype(I32).reshape(nt * N_EXPERTS)
    posr = posr.reshape(nt * N_EXPERTS, 1, tm)
    row3 = lambda i, e, f, c: (i, 0)
    slot_rows = pl.cdiv(tm, MOE_SUB) * MOE_SUB
    grid_spec = pltpu.PrefetchScalarGridSpec(
        num_scalar_prefetch=1,
        grid=(nt, N_EXPERTS, EXPERT_FF // tf),
        in_specs=[pl.BlockSpec((tm, D_MODEL), row3),
                  pl.BlockSpec((tm, LANES), row3), pl.BlockSpec((tm, LANES), row3),
                  pl.BlockSpec((1, 1, tm), lambda i, e, f, c: (i * N_EXPERTS + e, 0, 0)),
                  pl.BlockSpec((1, D_MODEL, tf), lambda i, e, f, c: (e, 0, f)),
                  pl.BlockSpec((1, D_MODEL, tf), lambda i, e, f, c: (e, 0, f)),
                  pl.BlockSpec((1, tf, D_MODEL), lambda i, e, f, c: (e, f, 0))],
        out_specs=pl.BlockSpec((tm, D_MODEL), row3),
        scratch_shapes=[pltpu.VMEM((slot_rows, D_MODEL), BF16), pltpu.VMEM((slot_rows, D_MODEL), F32)],
    )
    return pl.pallas_call(
        _moe_expert_kernel,
        grid_spec=grid_spec,
        out_shape=jax.ShapeDtypeStruct((m, D_MODEL), F32),
        compiler_params=_cparams(("parallel", "arbitrary", "arbitrary")),
        name="moe_experts",
    )(counts, h, gate, posc, posr, wg, wu, wd)


def _ple_kernel(*refs, final_norm, n_addends):
    x_refs, (g_ref, p_ref, wp_ref, wgate_ref, gf_ref, o_ref) = refs[:n_addends], refs[n_addends:]
    x = x_refs[0][...]
    for r in x_refs[1:]:
        x = x + r[...]
    hp = _rms(x, g_ref[...]).astype(BF16)
    gate = jax.nn.sigmoid(_dot(hp, wgate_ref[...]))
    y = x + _dot(p_ref[...].astype(BF16), wp_ref[...]) * gate
    if final_norm:
        y = _rms(y, gf_ref[...])
    o_ref[...] = y


def _ple(xs, g, p, wp, wgate, g_final, final_norm, tm):
    m = xs[0].shape[0]
    row = lambda i: (i, 0)
    const = lambda i: (0, 0)
    return pl.pallas_call(
        functools.partial(_ple_kernel, final_norm=final_norm, n_addends=len(xs)),
        grid=(m // tm,),
        in_specs=[pl.BlockSpec((tm, D_MODEL), row)] * len(xs)
        + [pl.BlockSpec((1, D_MODEL), const), pl.BlockSpec((tm, PLE_DIM), row),
           pl.BlockSpec((PLE_DIM, D_MODEL), const), pl.BlockSpec((D_MODEL, D_MODEL), const),
           pl.BlockSpec((1, D_MODEL), const)],
        out_specs=pl.BlockSpec((tm, D_MODEL), row),
        out_shape=jax.ShapeDtypeStruct((m, D_MODEL), F32),
        compiler_params=_cparams(("parallel",)),
        name="per_layer_embedding",
    )(*xs, g, p, wp, wgate, g_final)


def _ret_kernel(q_ref, k_ref, v_ref, gate_ref, cos_ref, sin_ref, s0_ref, o_ref, sout_ref, state, *, chunk, chunk_rows):
    i = pl.program_id(1)

    @pl.when(i == 0)
    def _():
        state[...] = s0_ref[0]

    cos, sin = cos_ref[...], sin_ref[...]
    tt = q_ref.shape[1]
    r = chunk_rows
    ii = lax.broadcasted_iota(I32, (r, r), 0).astype(F32)
    jj = lax.broadcasted_iota(I32, (r, r), 1).astype(F32)
    rel = ii - jj
    icol = lax.broadcasted_iota(I32, (r, 1), 0).astype(F32)
    half = RET_DK // 2

    def rot(ref, hd):
        x1 = ref[0, :, hd * RET_DK:hd * RET_DK + half]
        x2 = ref[0, :, hd * RET_DK + half:(hd + 1) * RET_DK]
        return jnp.concatenate([x1 * cos - x2 * sin, x2 * cos + x1 * sin], axis=1)

    for hd in range(RET_HEADS):
        lg = math.log(1.0 - 2.0 ** (-5.0 - hd))
        d_in = jnp.where(rel >= 0, jnp.exp(lg * jnp.maximum(rel, 0.0)), 0.0)
        d_q = jnp.exp(lg * (icol + 1.0))
        d_k = jnp.exp(lg * (chunk - 1.0 - icol)) * (RET_DK ** -0.5)
        d_c = math.exp(lg * chunk)
        qr = rot(q_ref, hd)
        kr = rot(k_ref, hd)
        vsl = slice(hd * RET_DV, (hd + 1) * RET_DV)
        for c in range(tt // r):
            rows = slice(c * r, (c + 1) * r)
            qc = qr[rows].astype(BF16)
            kc = kr[rows]
            vc = v_ref[0, rows, vsl]
            s_prev = state[hd]
            att = _dot_nt(qc, (kc * (RET_DK ** -0.5)).astype(BF16)) * d_in
            o = _dot(att.astype(BF16), vc) + _dot(qc, s_prev.astype(BF16)) * d_q
            state[hd] = s_prev * d_c + _dot_tn((kc * d_k).astype(BF16), vc)
            mu = jnp.mean(o, axis=-1, keepdims=True)
            var = jnp.mean(jnp.square(o - mu), axis=-1, keepdims=True)
            on = (o - mu) * lax.rsqrt(var + EPS)
            o_ref[0, rows, vsl] = (jax.nn.silu(gate_ref[0, rows, vsl].astype(F32)) * on).astype(BF16)

    @pl.when(i == pl.num_programs(1) - 1)
    def _():
        sout_ref[0] = state[...]


def _retention(zqk, zvg, s0, cos, sin, tt, chunk, chunk_rows):
    b, t, _ = zqk.shape
    hk, hv = RET_HEADS * RET_DK, RET_HEADS * RET_DV
    half = RET_DK // 2
    state_spec = pl.BlockSpec((1, RET_HEADS, RET_DK, RET_DV), lambda bi, i: (bi, 0, 0, 0))
    tab = pl.BlockSpec((tt, half), lambda bi, i: (i, 0))
    return pl.pallas_call(
        functools.partial(_ret_kernel, chunk=chunk, chunk_rows=chunk_rows),
        grid=(b, t // tt),
        in_specs=[pl.BlockSpec((1, tt, hk), lambda bi, i: (bi, i, 0)), pl.BlockSpec((1, tt, hk), lambda bi, i: (bi, i, 1)),
                  pl.BlockSpec((1, tt, hv), lambda bi, i: (bi, i, 0)), pl.BlockSpec((1, tt, hv), lambda bi, i: (bi, i, 1)),
                  tab, tab, state_spec],
        out_specs=[pl.BlockSpec((1, tt, hv), lambda bi, i: (bi, i, 0)), state_spec],
        out_shape=[jax.ShapeDtypeStruct((b, t, hv), BF16), jax.ShapeDtypeStruct(s0.shape, F32)],
        scratch_shapes=[pltpu.VMEM((RET_HEADS, RET_DK, RET_DV), F32)],
        compiler_params=_cparams(("parallel", "arbitrary")),
        name="retention",
    )(zqk, zqk, zvg, zvg, cos, sin, s0)


def _pack_l0_w_in(w):
    offs = np.cumsum((0,) + AB_SPLITS)
    bg, cg, hv, q, k, v, qi, ki, wi = [w[:, offs[n]:offs[n + 1]] for n in range(len(AB_SPLITS))]
    group = ATT_HEADS // KV_HEADS
    q4 = q.reshape(D_MODEL, ATT_HEADS, HEAD_DIM)
    zq = jnp.zeros_like(q4)
    q_pad = jnp.concatenate([jnp.concatenate([q4[:, :group], zq[:, :group]], axis=-1),
                             jnp.concatenate([zq[:, group:], q4[:, group:]], axis=-1)], axis=1).reshape(D_MODEL, -1)
    qi4 = qi.reshape(D_MODEL, IDX_HEADS, IDX_DIM)
    qi_pad = jnp.concatenate([qi4, jnp.zeros_like(qi4)], axis=-1).reshape(D_MODEL, -1)
    kiw = jnp.concatenate([ki, wi, jnp.zeros((D_MODEL, LANES - IDX_DIM - IDX_HEADS), w.dtype)], axis=1)
    return jnp.concatenate([bg, cg, hv, q_pad, k, v, qi_pad, kiw], axis=1).astype(BF16)


def _pack_l0_w_out(w):
    group = ATT_HEADS // KV_HEADS
    wa, wb = w[:CONV_CH], w[CONV_CH:].reshape(ATT_HEADS, HEAD_DIM, D_MODEL)
    zb = jnp.zeros_like(wb)
    wb_pad = jnp.concatenate([jnp.concatenate([wb[:group], zb[:group]], axis=1),
                              jnp.concatenate([zb[group:], wb[group:]], axis=1)], axis=0).reshape(-1, D_MODEL)
    return jnp.concatenate([wa, wb_pad], axis=0).astype(BF16)


def _rope_tables(pos, reps):
    inv = ROPE_THETA ** (-jnp.arange(0, HEAD_DIM, 2, dtype=F32) / HEAD_DIM)
    ang = pos.astype(F32)[:, None] * inv[None, :]
    cos, sin = jnp.cos(ang), jnp.sin(ang)
    cos64 = jnp.concatenate([cos, cos], axis=1)
    sin64 = jnp.concatenate([-sin, sin], axis=1)
    one, zero = jnp.ones_like(cos64), jnp.zeros_like(cos64)
    tabs = (jnp.concatenate([cos64, cos64], 1), jnp.concatenate([sin64, sin64], 1),
            jnp.concatenate([cos64, one], 1), jnp.concatenate([sin64, zero], 1))
    return tuple(jnp.tile(tb, (reps, 1)) for tb in tabs)


def _ret_tables(pos):
    inv = ROPE_THETA ** (-jnp.linspace(0.0, 1.0, RET_DK // 2, dtype=F32))
    ang = pos.astype(F32)[:, None] * inv[None, :]
    return jnp.cos(ang), jnp.sin(ang)


def _pack_params(prm):
    r = prm['moe_router'][0]
    r_pad = jnp.concatenate([r, jnp.zeros((D_MODEL, LANES - N_EXPERTS), F32)], axis=1)
    r_hi = r_pad.astype(BF16)
    bf = lambda a: a.astype(BF16)
    return dict(
        l0_w_in=_pack_l0_w_in(prm['ab_w_in'][0]), l0_w_out=_pack_l0_w_out(prm['ab_w_out'][0]),
        l0_w_vt=bf(prm['ab_w_in'][0][:, sum(AB_SPLITS[:5]):sum(AB_SPLITS[:6])].T),
        conv_w=prm['ab_conv_w'][0],
        ffn=(bf(prm['ffn_w_gate'][0]), bf(prm['ffn_w_up'][0]), bf(prm['ffn_w_down'][0])),
        ret_w_in=bf(prm['ret_w_in'][0]), ret_w_out=bf(prm['ret_w_out'][0]),
        r_hi=r_hi, r_lo=(r_pad - r_hi.astype(F32)).astype(BF16),
        moe=(bf(prm['moe_w_gate'][0]), bf(prm['moe_w_up'][0]), bf(prm['moe_w_down'][0])),
        ple_w=bf(prm['ple_w']), ple_gate_w=bf(prm['ple_gate_w']),
        norm_mix=prm['norm_mix'][:, None, :], norm_ffn=prm['norm_ffn'][:, None, :],
        norm_ple=prm['norm_ple'][:, None, :], norm_final=prm['norm_final'][None, :],
    )


def _trunk(x, p, pos, conv_buf, ret_state, pk, paged):
    b, t, _ = x.shape
    m = b * t
    tm = min(m, TOKEN_TILE)
    x2 = x.reshape(m, D_MODEL)

    reps = max(1, tm // t)
    tabs = _rope_tables(pos, reps)
    bg, u, q, k, v, qi, kiw, kbf, vbf, kiwbf, vt = _l0_in_proj(x2, pk['norm_mix'][0], pk['l0_w_in'], pk['l0_w_vt'], tabs,
                                                              tm, tabs[0].shape[0] // tm, head_major=paged is None)
    seq = lambda a: a.reshape(b, t, a.shape[-1])
    if paged is None:
        yb = _dsa_prompt(q, qi, seq(kiw), seq(kbf), vt, seq(kiwbf))
    else:
        cache_k, cache_v, cache_ki, page_table = paged
        padt = lambda a: jnp.swapaxes(jnp.pad(seq(a), ((0, 0), (0, KEY_CHUNK - t), (0, 0))), 1, 2)
        n_pool = cache_k.shape[0]
        pages_t = lambda c: jnp.transpose(c, (0, 2, 3, 1)).reshape(n_pool, LANES, PAGE_SIZE)
        yb = _dsa_sample(seq(q).astype(F32), seq(qi).astype(F32), seq(kiw), padt(kbf), padt(vbf), padt(kiwbf),
                         pages_t(cache_k), pages_t(cache_v), jnp.swapaxes(cache_ki, 1, 2), page_table).astype(BF16)
    u3 = seq(u)
    ya = _conv(u3, seq(bg), conv_buf, pk['conv_w'], tt=min(t, TOKEN_TILE))
    x2 = _mm_res([ya.reshape(m, CONV_CH), yb.reshape(m, ATT_HEADS * LANES)], pk['l0_w_out'], x2, tm)
    x2 = _ffn(x2, pk['norm_ffn'][0], *pk['ffn'], tm=tm, tf=D_FF // 2)
    x2 = _ple([x2], pk['norm_ple'][0], p[0].reshape(m, PLE_DIM), pk['ple_w'][0], pk['ple_gate_w'][0],
              pk['norm_final'], False, tm)
    new_k = k.reshape(1, b, t, KV_HEADS, HEAD_DIM)
    new_v = v.reshape(1, b, t, KV_HEADS, HEAD_DIM)
    new_ki = seq(kiw)[None, :, :, :IDX_DIM]
    new_conv = jnp.concatenate([conv_buf, u3], axis=1)[None, :, -(CONV_WIDTH - 1):]

    zqk, zvg = _ret_in_proj(x2, pk['norm_mix'][1], pk['ret_w_in'], min(m, 256))
    zqk, zvg = zqk.reshape(b, t, -1), zvg.reshape(b, t, -1)
    cos_r, sin_r = _ret_tables(pos)
    if t % RET_CHUNK == 0:
        og, s_new = _retention(zqk, zvg, ret_state, cos_r, sin_r, tt=2 * RET_CHUNK, chunk=RET_CHUNK,
                               chunk_rows=RET_CHUNK)
    else:
        rows = 16
        padt = lambda a: jnp.pad(a, ((0, 0),) * (a.ndim - 2) + ((0, rows - t), (0, 0)))
        og, s_new = _retention(padt(zqk), padt(zvg), ret_state, padt(cos_r), padt(sin_r), tt=rows, chunk=t,
                               chunk_rows=rows)
        og = og[:, :t]
    x2 = _mm_res([og.reshape(m, RET_HEADS * RET_DV)], pk['ret_w_out'], x2, tm)
    moe = _moe(x2, pk['norm_ffn'][1], pk['r_hi'], pk['r_lo'], *pk['moe'], tm=min(m, MOE_TOKEN_TILE), tf=EXPERT_FF // 2)
    x2 = _ple([x2, moe], pk['norm_ple'][1], p[1].reshape(m, PLE_DIM), pk['ple_w'][1], pk['ple_gate_w'][1],
              pk['norm_final'], True, tm)
    return x2.reshape(b, t, D_MODEL), new_k, new_v, new_ki, new_conv, s_new[None]


def kernel(x_prompt, x_sample, cache_k, cache_v, cache_kidx, state_conv, state_ret, page_table, p_prompt, p_sample,
           norm_mix, norm_ffn, norm_ple, norm_final, ab_w_in, ab_conv_w, ab_w_out, ffn_w_gate, ffn_w_up, ffn_w_down,
           ret_w_in, ret_w_out, moe_router, moe_w_gate, moe_w_up, moe_w_down, ple_w, ple_gate_w):
    prm = dict(norm_mix=norm_mix, norm_ffn=norm_ffn, norm_ple=norm_ple, norm_final=norm_final, ab_w_in=ab_w_in,
               ab_conv_w=ab_conv_w, ab_w_out=ab_w_out, ffn_w_gate=ffn_w_gate, ffn_w_up=ffn_w_up, ffn_w_down=ffn_w_down,
               ret_w_in=ret_w_in, ret_w_out=ret_w_out, moe_router=moe_router, moe_w_gate=moe_w_gate,
               moe_w_up=moe_w_up, moe_w_down=moe_w_down, ple_w=ple_w, ple_gate_w=ple_gate_w)
    pk = _pack_params(prm)
    b, t = x_prompt.shape[0], x_prompt.shape[1]
    db, ts = x_sample.shape[0], x_sample.shape[1]
    past_len = page_table.shape[1] * PAGE_SIZE
    dt = x_prompt.dtype

    conv0 = jnp.zeros((b, CONV_WIDTH - 1, CONV_CH), dt)
    ret0 = jnp.zeros((b, RET_HEADS, RET_DK, RET_DV), dt)
    y_p, k_p, v_p, ki_p, cb_p, rs_p = _trunk(x_prompt, p_prompt, jnp.arange(t, dtype=I32), conv0, ret0, pk, None)

    pos_s = past_len + jnp.arange(ts, dtype=I32)
    paged = (cache_k[0], cache_v[0], cache_kidx[0], page_table)
    y_s, k_s, v_s, ki_s, cb_s, rs_s = _trunk(x_sample, p_sample, pos_s, state_conv[0], state_ret[0], pk, paged)
    return (y_p, y_s, k_p, v_p, ki_p, cb_p, rs_p, k_s, v_s, ki_s, cb_s, rs_s)
```

```python
import functools
import math

import jax
import jax.numpy as jnp
import numpy as np
from jax import lax
from jax.experimental import pallas as pl
from jax.experimental.pallas import tpu as pltpu

F32 = jnp.float32
BF16 = jnp.bfloat16
I32 = jnp.int32

D_MODEL = 1024
PAGE_SIZE = 128
CONV_CH = D_MODEL // 2
CONV_WIDTH = 3
ATT_HEADS = 8
KV_HEADS = 2
HEAD_DIM = 64
IDX_HEADS = 4
IDX_DIM = 64
TOPK_MAX = 256
ROPE_THETA = 10000.0
RET_HEADS = 4
RET_DK = D_MODEL // RET_HEADS
RET_DV = 2 * RET_DK
RET_CHUNK = 128
D_FF = 2816
N_EXPERTS = 8
EXPERT_FF = 3584
PLE_DIM = 256
EPS = 1e-6
AB_SPLITS = (CONV_CH, CONV_CH, CONV_CH, ATT_HEADS * HEAD_DIM, KV_HEADS * HEAD_DIM, KV_HEADS * HEAD_DIM,
             IDX_HEADS * IDX_DIM, IDX_DIM, IDX_HEADS)

LANES = 128
VMEM_LIMIT = 48 * 1024 * 1024
INT_MIN = -2 ** 31
NEG_BIG = -1e30
LOG2_E = math.log2(math.e)

C_BG, C_CG, C_HV, C_Q, C_K, C_V, C_QI, C_KIW, C_END = 0, 512, 1024, 1536, 2560, 2688, 2816, 3328, 3456
KEY_CHUNK = 512
VT_ROWS = LANES + 16
TOKEN_TILE = 512
QUERY_BLOCK = 128


def _cparams(sem):
    return pltpu.CompilerParams(dimension_semantics=sem, vmem_limit_bytes=VMEM_LIMIT)


def _rms(x, g):
    return x * lax.rsqrt(jnp.mean(x * x, axis=-1, keepdims=True) + EPS) * g


def _dot(a, b):
    return jnp.dot(a, b, preferred_element_type=F32)


def _dot_nt(a, b):
    return lax.dot_general(a, b, (((1,), (1,)), ((), ())), preferred_element_type=F32)


def _dot_tn(a, b):
    return lax.dot_general(a, b, (((0,), (0,)), ((), ())), preferred_element_type=F32)


def _swap_halves64(x):
    lane = lax.broadcasted_iota(I32, x.shape, 1)
    from_above = pltpu.roll(x, LANES - 32, 1)
    from_below = pltpu.roll(x, 32, 1)
    return jnp.where((lane & 63) < 32, from_above, from_below)


def _l0_in_kernel(x_ref, g_ref, w_ref, wvt_ref, cos_ref, sin_ref, cosb_ref, sinb_ref,
                  bg_ref, u_ref, q_ref, k_ref, v_ref, qi_ref, kiw_ref, kbf_ref, vbf_ref, kiwbf_ref, vt_ref, *, head_major):
    h = _rms(x_ref[...], g_ref[...]).astype(BF16)
    cos, sin = cos_ref[...], sin_ref[...]

    def store_head(ref, hd, val):
        if head_major:
            for r in range(ref.shape[0]):
                ref[r, hd] = val[r * QUERY_BLOCK:(r + 1) * QUERY_BLOCK]
        else:
            ref[:, hd * LANES:(hd + 1) * LANES] = val

    def rot(z, c, s):
        return z * c + _swap_halves64(z) * s

    bg_ref[...] = _dot(h, w_ref[:, C_BG:C_CG])
    u_ref[...] = _dot(h, w_ref[:, C_CG:C_HV]) * _dot(h, w_ref[:, C_HV:C_Q])
    zq = _dot(h, w_ref[:, C_Q:C_K])
    for hd in range(ATT_HEADS):
        sl = slice(hd * LANES, (hd + 1) * LANES)
        store_head(q_ref, hd, (rot(zq[:, sl], cos, sin) * (HEAD_DIM ** -0.5 * LOG2_E)).astype(BF16))
    k = rot(_dot(h, w_ref[:, C_K:C_V]), cos, sin)
    k_ref[...] = k
    kbf_ref[...] = k.astype(BF16)
    v = _dot(h, w_ref[:, C_V:C_QI])
    v_ref[...] = v
    vbf_ref[...] = v.astype(BF16)
    vt_ref[0:LANES, :] = _dot_nt(wvt_ref[...], h).astype(BF16)
    vt_ref[LANES:VT_ROWS, :] = jnp.ones((VT_ROWS - LANES, vt_ref.shape[1]), BF16)
    zqi = _dot(h, w_ref[:, C_QI:C_KIW])
    for hd in range(IDX_HEADS):
        sl = slice(hd * LANES, (hd + 1) * LANES)
        store_head(qi_ref, hd, rot(zqi[:, sl], cos, sin).astype(BF16))
    kiw = rot(_dot(h, w_ref[:, C_KIW:C_END]), cosb_ref[...], sinb_ref[...])
    kiw_ref[...] = kiw
    kiwbf_ref[...] = kiw.astype(BF16)


def _l0_in_proj(x2, g, w, wvt, tabs, tm, n_tab_blocks, head_major):
    m = x2.shape[0]
    row = lambda i: (i, 0)
    const = lambda i: (0, 0)
    tab = lambda i: (i % n_tab_blocks, 0)
    widths = (512, 512, 1024, 128, 128, 512, 128, 128, 128, 128)
    dtypes = (F32, F32, BF16, F32, F32, BF16, F32, BF16, BF16, BF16)
    out_specs = [pl.BlockSpec((tm, wd), row) for wd in widths]
    out_shape = [jax.ShapeDtypeStruct((m, wd), dt) for wd, dt in zip(widths, dtypes)]
    if head_major:
        for n in (2, 5):
            heads = widths[n] // LANES
            out_specs[n] = pl.BlockSpec((tm // QUERY_BLOCK, heads, QUERY_BLOCK, LANES), lambda i: (i, 0, 0, 0))
            out_shape[n] = jax.ShapeDtypeStruct((m // QUERY_BLOCK, heads, QUERY_BLOCK, LANES), BF16)
    return pl.pallas_call(
        functools.partial(_l0_in_kernel, head_major=head_major),
        grid=(m // tm,),
        in_specs=[pl.BlockSpec((tm, D_MODEL), row), pl.BlockSpec((1, D_MODEL), const),
                  pl.BlockSpec((D_MODEL, C_END), const), pl.BlockSpec((LANES, D_MODEL), const)]
        + [pl.BlockSpec((tm, LANES), tab)] * 4,
        out_specs=out_specs + [pl.BlockSpec((VT_ROWS, tm), lambda i: (0, i))],
        out_shape=out_shape + [jax.ShapeDtypeStruct((VT_ROWS, m), BF16)],
        compiler_params=_cparams(("parallel",)),
        name="l0_in_proj",
    )(x2, g, w, wvt, *tabs)


def _score_keys(score):
    bits = pltpu.bitcast(score, I32)
    return bits ^ ((bits >> 31) & jnp.int32(0x7FFFFFFF))


def _key_scores(key):
    return pltpu.bitcast(key ^ ((key >> 31) & jnp.int32(0x7FFFFFFF)), F32)


SEARCH_PLAIN_STEPS = 8
SEARCH_SNAP_INTERP_STEPS = 8
SEARCH_MAX_STEPS = SEARCH_PLAIN_STEPS + SEARCH_SNAP_INTERP_STEPS + 34


def _topk_threshold(probe, count_tie_below, vmin, vmax, n_real, topk, n_index_bits):
    k = float(topk)
    lo0, hi0 = _score_keys(vmin), _score_keys(vmax) + 1
    take_all = n_real <= k
    zero, one = jnp.zeros_like(vmin), jnp.ones_like(vmin)
    active0 = jnp.where(jnp.logical_not(take_all) & (lo0 + 1 < hi0), 1.0, 0.0)

    def step(snap, st):
        it, lo, hi, c_lo, c_hi, w_lo, w_hi, last, active = st
        act = active > 0.0
        f_lo = (c_lo - (k - 0.5)) * w_lo
        f_hi = ((k - 0.5) - c_hi) * w_hi
        v_lo, v_hi = _key_scores(lo), _key_scores(hi)
        g = _score_keys(v_lo + (v_hi - v_lo) * (f_lo / (f_lo + f_hi)))
        if snap:
            mid = (lo >> 1) + (hi >> 1) + (lo & hi & 1)
            g = jnp.where(it < SEARCH_PLAIN_STEPS + SEARCH_SNAP_INTERP_STEPS, g, mid)
        g = jnp.minimum(jnp.maximum(g, lo + 1), hi - 1)
        c, key_up, key_dn = probe(g, snap)
        hit = act & (c == k)
        up = act & (c > k)
        dn = act & (c < k)
        lo = jnp.where(hit, g, jnp.where(up, key_up if snap else g, lo))
        c_lo = jnp.where(hit | up, c, c_lo)
        hi = jnp.where(dn, key_dn + 1 if snap else g, hi)
        c_hi = jnp.where(dn, c, c_hi)
        w_hi = jnp.where(up, jnp.where(last > 0.0, w_hi * 0.5, one), jnp.where(dn, one, w_hi))
        w_lo = jnp.where(dn, jnp.where(last < 0.0, w_lo * 0.5, one), jnp.where(up, one, w_lo))
        last = jnp.where(up, one, jnp.where(dn, -one, last))
        active = jnp.where(act & jnp.logical_not(hit) & (lo + 1 < hi), 1.0, 0.0)
        return it + 1, lo, hi, c_lo, c_hi, w_lo, w_hi, last, active

    def cond(limit, st):
        return (st[0] < limit) & (jnp.max(st[-1]) > 0.0)

    st = (jnp.int32(0), lo0, hi0, n_real, zero, one, one, zero, active0)
    st = lax.fori_loop(0, SEARCH_PLAIN_STEPS, lambda _, s: step(False, s), st)
    st = lax.while_loop(functools.partial(cond, SEARCH_MAX_STEPS), functools.partial(step, True), st)
    _, lo, _, c_lo, c_hi, _, _, _, _ = st
    thr = jnp.where(take_all, INT_MIN, lo)
    excess = jnp.logical_not(take_all) & (c_lo > k)
    need = k - c_hi

    def tie_search(_):
        def step(i, x):
            cand = x | lax.shift_left(jnp.int32(1), n_index_bits - 1 - i)
            return jnp.where(count_tie_below(thr, cand) < need, cand, x)
        return lax.fori_loop(0, n_index_bits, step, jnp.zeros_like(thr))

    any_excess = jnp.max(jnp.where(excess, 1.0, 0.0)) > 0.0
    cut = lax.cond(any_excess, tie_search, lambda _: jnp.zeros_like(thr), 0)
    cut = jnp.where(excess, cut, jnp.where(take_all, jnp.int32(-1), jnp.int32(2 ** 31 - 1)))
    return thr, cut


def _selection_bias(key, idx, thr, cut):
    sel = (key > thr) | ((key == thr) & (idx <= cut))
    return pltpu.bitcast(jnp.where(sel, 0.0, NEG_BIG).astype(F32), I32)


ROW_SLAB = 2048


def _lane_fold(m, op=jnp.add):
    parts = [m[:, c * LANES:(c + 1) * LANES] for c in range(m.shape[1] // LANES)]
    while len(parts) > 1:
        parts = [op(parts[n], parts[n + 1]) for n in range(0, len(parts) - 1, 2)] + (parts[-1:] if len(parts) % 2 else [])
    return parts[0]


def _row_slabs(n_chunks):
    width = n_chunks * KEY_CHUNK
    return [(k0, min(ROW_SLAB, width - k0)) for k0 in range(0, width, ROW_SLAB)]


def _row_count(s_ref, n_chunks, pred):
    acc = None
    for k0, size in _row_slabs(n_chunks):
        blk = s_ref[:, k0:k0 + size]
        idx = k0 + lax.broadcasted_iota(I32, blk.shape, 1)
        part = _lane_fold(jnp.where(pred(blk, idx), 1.0, 0.0))
        acc = part if acc is None else acc + part
    return jnp.sum(acc, axis=1, keepdims=True)


def _int_reduce(x, take_min, axis):
    red = jnp.min if take_min else jnp.max
    hi = (x >> 16).astype(F32)
    lo = (x & 0xFFFF).astype(F32)
    m_hi = red(hi, axis=axis, keepdims=True)
    m_lo = red(jnp.where(hi == m_hi, lo, 65536.0 if take_min else -1.0), axis=axis, keepdims=True)
    return (m_hi.astype(I32) << 16) | m_lo.astype(I32)


def _row_probe(s_ref, n_chunks, g, snap):
    cnt = up = dn = None
    for k0, size in _row_slabs(n_chunks):
        blk = s_ref[:, k0:k0 + size]
        ge = blk >= g
        part = _lane_fold(jnp.where(ge, 1.0, 0.0))
        cnt = part if cnt is None else cnt + part
        if snap:
            above = _lane_fold(jnp.where(ge, blk, 2 ** 31 - 1), jnp.minimum)
            below = _lane_fold(jnp.where(ge, INT_MIN, blk), jnp.maximum)
            up = above if up is None else jnp.minimum(up, above)
            dn = below if dn is None else jnp.maximum(dn, below)
    c = jnp.sum(cnt, axis=1, keepdims=True)
    if not snap:
        return c, None, None
    return c, _int_reduce(up, True, 1), _int_reduce(dn, False, 1)


def _row_select_to_bias(s_ref, n_chunks, n_real, topk):
    inf = jnp.float32(jnp.inf)
    mn = mx = None
    for k0, size in _row_slabs(n_chunks):
        blk = s_ref[:, k0:k0 + size]
        v = _key_scores(blk)
        real = blk != INT_MIN
        lo_part = _lane_fold(jnp.where(real, v, inf), jnp.minimum)
        hi_part = _lane_fold(jnp.where(real, v, -inf), jnp.maximum)
        mn = lo_part if mn is None else jnp.minimum(mn, lo_part)
        mx = hi_part if mx is None else jnp.maximum(mx, hi_part)
    vmin, vmax = jnp.min(mn, axis=1, keepdims=True), jnp.max(mx, axis=1, keepdims=True)
    thr, cut = _topk_threshold(
        functools.partial(_row_probe, s_ref, n_chunks),
        lambda t, x: _row_count(s_ref, n_chunks, lambda blk, idx: (blk == t) & (idx < x)),
        vmin, vmax, n_real, topk, (s_ref.shape[1] - 1).bit_length())
    for k0, size in _row_slabs(n_chunks):
        blk = s_ref[:, k0:k0 + size]
        idx = k0 + lax.broadcasted_iota(I32, blk.shape, 1)
        s_ref[:, k0:k0 + size] = _selection_bias(blk, idx, thr, cut)


def _attend_group(qg, segments, bias_ref, rep):
    n = qg.shape[0]

    def step(chunk, base, load_kv, c, carry):
        m, l, acc = carry
        k0 = pl.multiple_of(base + c * chunk, KEY_CHUNK)
        ktc, vtc = load_kv(k0)
        bias = pltpu.bitcast(bias_ref[:, pl.ds(k0, chunk)], F32)
        s = _dot(qg, ktc) + jnp.concatenate([bias] * rep, axis=0)
        m_new = jnp.maximum(m, jnp.max(s, axis=1, keepdims=True))
        alpha = jnp.exp2(m - m_new)
        p = jnp.exp2(s - m_new)
        l = alpha * l + jnp.sum(p, axis=1, keepdims=True)
        acc = alpha * acc + _dot_nt(p.astype(BF16), vtc)
        return m_new, l, acc

    carry = (jnp.full((n, 1), NEG_BIG, F32), jnp.zeros((n, 1), F32), jnp.zeros((n, LANES), F32))
    for n_steps, chunk, base, load_kv in segments:
        carry = lax.fori_loop(0, n_steps, functools.partial(step, chunk, base, load_kv), carry)
    _, l, acc = carry
    return acc / l


def _stack_heads(x, first, count, width=LANES):
    x = x.astype(F32)
    return jnp.concatenate([x[:, (first + hd) * LANES:(first + hd) * LANES + width] for hd in range(count)],
                           axis=0).astype(BF16)


def _attend_all_heads(q, segments, bias_ref, o_ref):
    rows = q.shape[0]
    group = ATT_HEADS // KV_HEADS
    out = _attend_group(_stack_heads(q, 0, ATT_HEADS), segments, bias_ref, ATT_HEADS)
    for hd in range(ATT_HEADS):
        piece = out[hd * rows:(hd + 1) * rows]
        piece = jnp.where(_group_lane_mask(piece.shape, hd // group), piece, 0.0)
        o_ref[0, :, hd * LANES:(hd + 1) * LANES] = piece.astype(o_ref.dtype)


def _indexer_weights(kiw_q):
    return kiw_q[:, IDX_DIM:IDX_DIM + IDX_HEADS] * (IDX_HEADS ** -0.5 * IDX_DIM ** -0.5)


def _group_lane_mask(shape, g):
    lane = lax.broadcasted_iota(I32, shape, 1)
    return (lane >= g * HEAD_DIM) & (lane < (g + 1) * HEAD_DIM)


ATT_CHUNK = 256


def _sublane_fold(m, op=jnp.add):
    parts = [m[r * 8:(r + 1) * 8] for r in range(m.shape[0] // 8)]
    while len(parts) > 1:
        parts = [op(parts[n], parts[n + 1]) for n in range(0, len(parts) - 1, 2)] + (parts[-1:] if len(parts) % 2 else [])
    return parts[0]


def _col_reduce(m, op, reduce_fn):
    return reduce_fn(_sublane_fold(m, op), axis=0, keepdims=True)


def _col_count(s_ref, n_chunks, pred):
    cols = s_ref.shape[1]

    def body(c, acc):
        k0 = pl.multiple_of(c * KEY_CHUNK, KEY_CHUNK)
        blk = s_ref[pl.ds(k0, KEY_CHUNK), :]
        idx = k0 + lax.broadcasted_iota(I32, blk.shape, 0)
        return acc + _sublane_fold(jnp.where(pred(blk, idx), 1.0, 0.0))

    acc = lax.fori_loop(0, n_chunks, body, jnp.zeros((8, cols), F32))
    return jnp.sum(acc, axis=0, keepdims=True)


def _col_probe(s_ref, n_chunks, g, snap):
    cols = s_ref.shape[1]

    def body(c, carry):
        k0 = pl.multiple_of(c * KEY_CHUNK, KEY_CHUNK)
        blk = s_ref[pl.ds(k0, KEY_CHUNK), :]
        ge = blk >= g
        out = [carry[0] + _sublane_fold(jnp.where(ge, 1.0, 0.0))]
        if snap:
            out.append(jnp.minimum(carry[1], _sublane_fold(jnp.where(ge, blk, 2 ** 31 - 1), jnp.minimum)))
            out.append(jnp.maximum(carry[2], _sublane_fold(jnp.where(ge, INT_MIN, blk), jnp.maximum)))
        return tuple(out)

    init = [jnp.zeros((8, cols), F32)]
    if snap:
        init += [jnp.full((8, cols), 2 ** 31 - 1, I32), jnp.full((8, cols), INT_MIN, I32)]
    res = lax.fori_loop(0, n_chunks, body, tuple(init))
    c = jnp.sum(res[0], axis=0, keepdims=True)
    if not snap:
        return c, None, None
    return c, _int_reduce(res[1], True, 0), _int_reduce(res[2], False, 0)


def _col_select_to_bias(s_ref, n_chunks, n_real, topk, vmin, vmax):
    thr, cut = _topk_threshold(
        functools.partial(_col_probe, s_ref, n_chunks),
        lambda t, x: _col_count(s_ref, n_chunks, lambda blk, idx: (blk == t) & (idx < x)),
        vmin, vmax, n_real, topk, (s_ref.shape[0] - 1).bit_length())

    def to_bias(c, _):
        k0 = pl.multiple_of(c * KEY_CHUNK, KEY_CHUNK)
        blk = s_ref[pl.ds(k0, KEY_CHUNK), :]
        idx = k0 + lax.broadcasted_iota(I32, blk.shape, 0)
        s_ref[pl.ds(k0, KEY_CHUNK), :] = _selection_bias(blk, idx, thr, cut)
        return 0

    lax.fori_loop(0, n_chunks, to_bias, 0)


def _dsa_prompt_kernel(q_ref, qi_ref, kiwq_ref, k_ref, vt_ref, kiw_ref, o_ref, s_ref, m_scr, acc_scr,
                       sa_scr, sb_scr, ia_scr, ib_scr, *, qb, topk):
    j = pl.program_id(1)
    n_keys = j * qb + qb
    n_chunks = (n_keys + KEY_CHUNK - 1) // KEY_CHUNK
    qpos = j * qb + lax.broadcasted_iota(I32, (1, qb), 1)
    wt = kiwq_ref[0].T[IDX_DIM:IDX_DIM + IDX_HEADS] * (IDX_HEADS ** -0.5 * IDX_DIM ** -0.5)

    inf = jnp.float32(jnp.inf)

    last_chunk = s_ref.shape[0] // KEY_CHUNK - 1

    def index_dots(c, buf):
        k0 = pl.multiple_of(jnp.minimum(c, last_chunk) * KEY_CHUNK, KEY_CHUNK)
        kc = kiw_ref[0, pl.ds(k0, KEY_CHUNK), :]
        for pr in range(IDX_HEADS // 2):
            buf[pr] = _dot_nt(kc, qi_ref[0, 2 * pr:2 * pr + 2].reshape(2 * qb, LANES))

    def score_chunk(c, buf, carry):
        mn, mx = carry
        k0 = pl.multiple_of(jnp.minimum(c, last_chunk) * KEY_CHUNK, KEY_CHUNK)
        acc = jnp.zeros((KEY_CHUNK, qb), F32)
        for pr in range(IDX_HEADS // 2):
            acc = acc + jnp.maximum(buf[pr, :, 0:qb], 0.0) * wt[2 * pr:2 * pr + 1]
            acc = acc + jnp.maximum(buf[pr, :, qb:2 * qb], 0.0) * wt[2 * pr + 1:2 * pr + 2]
        causal = k0 + lax.broadcasted_iota(I32, (KEY_CHUNK, qb), 0) <= qpos
        s_ref[pl.ds(k0, KEY_CHUNK), :] = jnp.where(causal, _score_keys(acc), INT_MIN)
        mn = jnp.minimum(mn, _sublane_fold(jnp.where(causal, acc, inf), jnp.minimum))
        mx = jnp.maximum(mx, _sublane_fold(jnp.where(causal, acc, -inf), jnp.maximum))
        return mn, mx

    index_dots(0, ia_scr)

    def scores(t, carry):
        index_dots(2 * t + 1, ib_scr)
        carry = score_chunk(2 * t, ia_scr, carry)
        index_dots(2 * t + 2, ia_scr)
        return score_chunk(2 * t + 1, ib_scr, carry)

    mn, mx = lax.fori_loop(0, (n_chunks + 1) // 2, scores,
                           (jnp.full((8, qb), inf, F32), jnp.full((8, qb), -inf, F32)))
    _col_select_to_bias(s_ref, n_chunks, (qpos + 1).astype(F32), topk,
                        jnp.min(mn, axis=0, keepdims=True), jnp.max(mx, axis=0, keepdims=True))

    m_scr[...] = jnp.full(m_scr.shape, NEG_BIG, F32)
    acc_scr[...] = jnp.zeros(acc_scr.shape, F32)

    n_att = n_chunks * (KEY_CHUNK // ATT_CHUNK)

    def logits(step, buf):
        k0 = pl.multiple_of(jnp.minimum(step, n_att - 1) * ATT_CHUNK, ATT_CHUNK)
        kc = k_ref[0, pl.ds(k0, ATT_CHUNK), :]
        bias = pltpu.bitcast(s_ref[pl.ds(k0, ATT_CHUNK), :], F32)
        for pr in range(ATT_HEADS // 2):
            s2 = _dot_nt(kc, q_ref[0, 2 * pr:2 * pr + 2].reshape(2 * qb, LANES))
            buf[2 * pr] = s2[:, :qb] + bias
            buf[2 * pr + 1] = s2[:, qb:] + bias

    def accumulate(step, buf):
        k0 = pl.multiple_of(step * ATT_CHUNK, ATT_CHUNK)
        vtc = vt_ref[:, pl.ds(k0, ATT_CHUNK)]
        for pr in range(ATT_HEADS // 2):
            ps, alphas = [], []
            for hd in (2 * pr, 2 * pr + 1):
                s = buf[hd]
                m_prev = m_scr[hd:hd + 1, :]
                m_new = jnp.maximum(m_prev, _col_reduce(s, jnp.maximum, jnp.max))
                alphas.append(jnp.exp2(m_prev - m_new))
                ps.append(jnp.exp2(s - m_new).astype(BF16))
                m_scr[hd:hd + 1, :] = m_new
            acc_scr[pr] = jnp.concatenate(alphas, axis=1) * acc_scr[pr] + _dot(vtc, jnp.concatenate(ps, axis=1))

    logits(0, sa_scr)

    def attend(c, _):
        logits(2 * c + 1, sb_scr)
        accumulate(2 * c, sa_scr)
        logits(2 * c + 2, sa_scr)
        accumulate(2 * c + 1, sb_scr)
        return 0

    lax.fori_loop(0, n_att // 2, attend, 0)

    group = ATT_HEADS // KV_HEADS
    for hd in range(ATT_HEADS):
        cols = slice((hd % 2) * qb, (hd % 2 + 1) * qb)
        out_t = acc_scr[hd // 2, 0:LANES, cols] / acc_scr[hd // 2, LANES:LANES + 1, cols]
        row = lax.broadcasted_iota(I32, out_t.shape, 0)
        g = hd // group
        out_t = jnp.where((row >= g * HEAD_DIM) & (row < (g + 1) * HEAD_DIM), out_t, 0.0)
        o_ref[0, :, hd * LANES:(hd + 1) * LANES] = out_t.T.astype(BF16)


def _dsa_prompt(q, qi, kiw, kbf, vt, kiwbf):
    b, t, _ = kiw.shape
    qb = QUERY_BLOCK
    nb = t // qb
    topk = min(TOPK_MAX, t // 4)
    qblk = lambda width: pl.BlockSpec((1, qb, width), lambda bi, j: (bi, j, 0))
    hblk = lambda heads: pl.BlockSpec((1, heads, qb, LANES), lambda bi, j: (bi * nb + j, 0, 0, 0))
    full = pl.BlockSpec((1, t, LANES), lambda bi, j: (bi, 0, 0))
    return pl.pallas_call(
        functools.partial(_dsa_prompt_kernel, qb=qb, topk=topk),
        grid=(b, nb),
        in_specs=[hblk(ATT_HEADS), hblk(IDX_HEADS), qblk(LANES), full,
                  pl.BlockSpec((VT_ROWS, t), lambda bi, j: (0, bi)), full],
        out_specs=qblk(ATT_HEADS * LANES),
        out_shape=jax.ShapeDtypeStruct((b, t, ATT_HEADS * LANES), BF16),
        scratch_shapes=[pltpu.VMEM((pl.cdiv(t, KEY_CHUNK) * KEY_CHUNK, qb), I32), pltpu.VMEM((ATT_HEADS, qb), F32),
                        pltpu.VMEM((ATT_HEADS // 2, VT_ROWS, 2 * qb), F32),
                        pltpu.VMEM((ATT_HEADS, ATT_CHUNK, qb), F32), pltpu.VMEM((ATT_HEADS, ATT_CHUNK, qb), F32),
                        pltpu.VMEM((IDX_HEADS // 2, KEY_CHUNK, 2 * qb), F32),
                        pltpu.VMEM((IDX_HEADS // 2, KEY_CHUNK, 2 * qb), F32)],
        compiler_params=_cparams(("parallel", "arbitrary")),
        name="dsa_prompt",
    )(q, qi, kiw, kbf, vt, kiwbf)


def _dsa_sample_kernel(pt_ref, q_ref, qi_ref, kiwq_ref, kn_ref, vn_ref, kiwn_ref, ck_hbm, cv_hbm, cki_hbm,
                       o_ref, kbuf, vbuf, kibuf, s_ref, sems, *, ts, n_pages, topk):
    b = pl.program_id(0)
    past = n_pages * PAGE_SIZE

    streams = ((cki_hbm, kibuf), (ck_hbm, kbuf), (cv_hbm, vbuf))

    def page_copy(p, which):
        src, dst = streams[which]
        cols = pl.ds(pl.multiple_of(p * PAGE_SIZE, PAGE_SIZE), PAGE_SIZE)
        return pltpu.make_async_copy(src.at[pt_ref[b, p]], dst.at[:, cols], sems.at[which])

    def start_page(p, _):
        for which in range(len(streams)):
            page_copy(p, which).start()
        return 0

    lax.fori_loop(0, n_pages, start_page, 0)

    def wait_pages(which):
        def body(p, _):
            page_copy(p, which).wait()
            return 0
        lax.fori_loop(0, n_pages, body, 0)

    n_past_chunks = past // KEY_CHUNK
    n_chunks = n_past_chunks + 1
    qpos = lax.broadcasted_iota(I32, (ts, 1), 0)
    wq = _indexer_weights(kiwq_ref[0])
    qi_stack = _stack_heads(qi_ref[0], 0, IDX_HEADS, IDX_DIM)
    w_stack = jnp.concatenate([wq[:, hd:hd + 1] for hd in range(IDX_HEADS)], axis=0)

    def head_sum(x):
        acc = x[0:ts]
        for hd in range(1, IDX_HEADS):
            acc = acc + x[hd * ts:(hd + 1) * ts]
        return acc

    wait_pages(0)

    step_keys = math.gcd(past, ROW_SLAB)

    def past_scores(c, _):
        k0 = pl.multiple_of(c * step_keys, KEY_CHUNK)
        ktc = kibuf[:, pl.ds(k0, step_keys)].astype(BF16)
        acc = head_sum(jnp.maximum(_dot(qi_stack, ktc), 0.0) * w_stack)
        s_ref[:, pl.ds(k0, step_keys)] = _score_keys(acc + 0.0)
        return 0

    lax.fori_loop(0, past // step_keys, past_scores, 0)
    acc = head_sum(jnp.maximum(_dot(qi_stack, kiwn_ref[0, 0:IDX_DIM, :]), 0.0) * w_stack)
    kpos = lax.broadcasted_iota(I32, (ts, KEY_CHUNK), 1)
    s_ref[:, pl.ds(past, KEY_CHUNK)] = jnp.where(kpos <= qpos, _score_keys(acc + 0.0), INT_MIN)
    _row_select_to_bias(s_ref, n_chunks, (past + 1 + qpos).astype(F32), topk)

    wait_pages(1)
    wait_pages(2)

    def load_past(k0):
        return kbuf[:, pl.ds(k0, step_keys)].astype(BF16), vbuf[:, pl.ds(k0, step_keys)].astype(BF16)

    def load_new(k0):
        return kn_ref[0], vn_ref[0]

    segments = [(past // step_keys, step_keys, 0, load_past), (1, KEY_CHUNK, past, load_new)]
    _attend_all_heads(q_ref[0], segments, s_ref, o_ref)


def _dsa_sample(q, qi, kiw, knt, vnt, kiwnt, cache_kt, cache_vt, cache_kit, page_table):
    b, ts, _ = q.shape
    n_pages = page_table.shape[1]
    past = n_pages * PAGE_SIZE
    topk = min(TOPK_MAX, (past + ts) // 4)
    blk = lambda rows, width: pl.BlockSpec((1, rows, width), lambda bi, pt: (bi, 0, 0))
    hbm = pl.BlockSpec(memory_space=pl.ANY)
    grid_spec = pltpu.PrefetchScalarGridSpec(
        num_scalar_prefetch=1,
        grid=(b,),
        in_specs=[blk(ts, ATT_HEADS * LANES), blk(ts, IDX_HEADS * LANES), blk(ts, LANES),
                  blk(LANES, KEY_CHUNK), blk(LANES, KEY_CHUNK), blk(LANES, KEY_CHUNK), hbm, hbm, hbm],
        out_specs=blk(ts, ATT_HEADS * LANES),
        scratch_shapes=[pltpu.VMEM((LANES, past), F32), pltpu.VMEM((LANES, past), F32),
                        pltpu.VMEM((IDX_DIM, past), F32), pltpu.VMEM((ts, past + KEY_CHUNK), I32),
                        pltpu.SemaphoreType.DMA((3,))],
    )
    return pl.pallas_call(
        functools.partial(_dsa_sample_kernel, ts=ts, n_pages=n_pages, topk=topk),
        grid_spec=grid_spec,
        out_shape=jax.ShapeDtypeStruct((b, ts, ATT_HEADS * LANES), F32),
        compiler_params=_cparams(("arbitrary",)),
        name="dsa_sample",
    )(page_table, q, qi, kiw, knt, vnt, kiwnt, cache_kt, cache_vt, cache_kit)


def _conv_kernel(u_ref, halo_ref, buf_ref, bg_ref, w_ref, ya_ref):
    i = pl.program_id(1)
    u = u_ref[0]
    halo, buf = halo_ref[0], buf_ref[0]
    first = i == 0
    prev1 = jnp.where(first, buf[1:2], halo[7:8])
    prev2 = jnp.where(first, buf[0:1], halo[6:7])
    row = lax.broadcasted_iota(I32, u.shape, 0)
    um1 = jnp.where(row == 0, prev1, pltpu.roll(u, 1, 0))
    um2 = jnp.where(row == 0, prev2, jnp.where(row == 1, prev1, pltpu.roll(u, 2, 0)))
    w = w_ref[...]
    conv = w[0:1] * um2 + w[1:2] * um1 + w[2:3] * u
    ya_ref[0] = (bg_ref[0] * conv).astype(BF16)


def _conv(u, bg, buf, w, tt):
    b, t, c = u.shape
    halo_rows = 8
    tile = pl.BlockSpec((1, tt, c), lambda bi, i: (bi, i, 0))
    halo = pl.BlockSpec((1, halo_rows, c), lambda bi, i: (bi, jnp.maximum(i * (tt // halo_rows) - 1, 0), 0))
    return pl.pallas_call(
        _conv_kernel,
        grid=(b, t // tt),
        in_specs=[tile, halo, pl.BlockSpec((1, CONV_WIDTH - 1, c), lambda bi, i: (bi, 0, 0)), tile,
                  pl.BlockSpec((CONV_WIDTH, c), lambda bi, i: (0, 0))],
        out_specs=tile,
        out_shape=jax.ShapeDtypeStruct((b, t, c), BF16),
        compiler_params=_cparams(("parallel", "parallel")),
        name="short_conv",
    )(u, u, buf, bg, w)


def _mm_res_kernel(*refs, n_in):
    a_refs, w_ref, x_ref, o_ref = refs[:n_in], refs[n_in], refs[n_in + 1], refs[n_in + 2]
    a = jnp.concatenate([r[...] for r in a_refs], axis=1) if n_in > 1 else a_refs[0][...]
    o_ref[...] = x_ref[...] + _dot(a, w_ref[...])


def _mm_res(a_list, w, x, tm):
    m = x.shape[0]
    row = lambda i: (i, 0)
    return pl.pallas_call(
        functools.partial(_mm_res_kernel, n_in=len(a_list)),
        grid=(m // tm,),
        in_specs=[pl.BlockSpec((tm, a.shape[1]), row) for a in a_list]
        + [pl.BlockSpec(w.shape, lambda i: (0, 0)), pl.BlockSpec((tm, D_MODEL), row)],
        out_specs=pl.BlockSpec((tm, D_MODEL), row),
        out_shape=jax.ShapeDtypeStruct((m, D_MODEL), F32),
        compiler_params=_cparams(("parallel",)),
        name="matmul_residual",
    )(*a_list, w, x)


def _ret_in_kernel(x_ref, g_ref, w_ref, qk_ref, vg_ref):
    h = _rms(x_ref[...], g_ref[...]).astype(BF16)
    n_qk = qk_ref.shape[1]
    qk_ref[...] = _dot(h, w_ref[:, :n_qk])
    vg_ref[...] = _dot(h, w_ref[:, n_qk:]).astype(BF16)


def _ret_in_proj(x, g, w, tm):
    m, n = x.shape[0], w.shape[1]
    n_qk = 2 * RET_HEADS * RET_DK
    row = lambda i: (i, 0)
    const = lambda i: (0, 0)
    return pl.pallas_call(
        _ret_in_kernel,
        grid=(m // tm,),
        in_specs=[pl.BlockSpec((tm, D_MODEL), row), pl.BlockSpec((1, D_MODEL), const), pl.BlockSpec((D_MODEL, n), const)],
        out_specs=[pl.BlockSpec((tm, n_qk), row), pl.BlockSpec((tm, n - n_qk), row)],
        out_shape=[jax.ShapeDtypeStruct((m, n_qk), F32), jax.ShapeDtypeStruct((m, n - n_qk), BF16)],
        compiler_params=_cparams(("parallel",)),
        name="retention_in_proj",
    )(x, g, w)


def _ffn_kernel(x_ref, g_ref, wg_ref, wu_ref, wd_ref, o_ref, h_scr, acc):
    f = pl.program_id(1)

    @pl.when(f == 0)
    def _():
        x = x_ref[...]
        h_scr[...] = _rms(x, g_ref[...]).astype(BF16)
        acc[...] = x

    h = h_scr[...]
    a = jax.nn.silu(_dot(h, wg_ref[...])) * _dot(h, wu_ref[...])
    acc[...] += _dot(a.astype(BF16), wd_ref[...])

    @pl.when(f == pl.num_programs(1) - 1)
    def _():
        o_ref[...] = acc[...]


def _ffn(x, g, wg, wu, wd, tm, tf):
    m, ff = x.shape[0], wg.shape[1]
    row = lambda i, f: (i, 0)
    return pl.pallas_call(
        _ffn_kernel,
        grid=(m // tm, ff // tf),
        in_specs=[pl.BlockSpec((tm, D_MODEL), row), pl.BlockSpec((1, D_MODEL), lambda i, f: (0, 0)),
                  pl.BlockSpec((D_MODEL, tf), lambda i, f: (0, f)), pl.BlockSpec((D_MODEL, tf), lambda i, f: (0, f)),
                  pl.BlockSpec((tf, D_MODEL), lambda i, f: (f, 0))],
        out_specs=pl.BlockSpec((tm, D_MODEL), row),
        out_shape=jax.ShapeDtypeStruct((m, D_MODEL), F32),
        scratch_shapes=[pltpu.VMEM((tm, D_MODEL), BF16), pltpu.VMEM((tm, D_MODEL), F32)],
        compiler_params=_cparams(("parallel", "arbitrary")),
        name="dense_swiglu",
    )(x, g, wg, wu, wd)


def _top2_gates(logits):
    lane = lax.broadcasted_iota(I32, logits.shape, 1).astype(F32)
    neg = jnp.float32(-jnp.inf)
    l1 = jnp.where(lane < N_EXPERTS, logits, neg)
    m1 = jnp.max(l1, axis=1, keepdims=True)
    i1 = jnp.min(jnp.where(l1 == m1, lane, float(LANES)), axis=1, keepdims=True)
    l2 = jnp.where(lane == i1, neg, l1)
    m2 = jnp.max(l2, axis=1, keepdims=True)
    i2 = jnp.min(jnp.where(l2 == m2, lane, float(LANES)), axis=1, keepdims=True)
    e = jnp.exp(m2 - m1)
    w1 = 1.0 / (1.0 + e)
    w2 = e / (1.0 + e)
    first, second = lane == i1, lane == i2
    return jnp.where(first, w1, jnp.where(second, w2, 0.0)), jnp.where(first | second, 1.0, 0.0)


MOE_SUB = 144
MOE_TOKEN_TILE = 1024


def _moe_route_kernel(x_ref, g_ref, rhi_ref, rlo_ref, h_ref, gate_ref, posc_ref, posr_ref, cnt_ref):
    tm = x_ref.shape[0]
    hn = _rms(x_ref[...], g_ref[...])
    h_hi = hn.astype(BF16)
    h_lo = (hn - h_hi.astype(F32)).astype(BF16)
    logits = _dot(h_hi, rhi_ref[...]) + (_dot(h_lo, rhi_ref[...]) + _dot(h_hi, rlo_ref[...]))
    gate, routed = _top2_gates(logits)
    h_ref[...] = h_hi
    gate_ref[...] = gate
    earlier = (lax.broadcasted_iota(I32, (tm, tm), 0) > lax.broadcasted_iota(I32, (tm, tm), 1))
    slot = _dot(jnp.where(earlier, 1.0, 0.0).astype(BF16), routed.astype(BF16))
    posc = jnp.where(routed > 0.0, slot, -1.0)
    posc_ref[...] = posc
    posr_ref[0] = posc.T[0:N_EXPERTS]
    cnt_ref[0] = jnp.broadcast_to(jnp.sum(routed, axis=0, keepdims=True), (8, LANES))


def _moe_expert_kernel(cnt_ref, h_ref, gate_ref, posc_ref, posr_ref, wg_ref, wu_ref, wd_ref, o_ref, xg, yacc):
    i, e, f = pl.program_id(0), pl.program_id(1), pl.program_id(2)
    tm = h_ref.shape[0]
    n_sub = (cnt_ref[i * N_EXPERTS + e] + MOE_SUB - 1) // MOE_SUB

    @pl.when((e == 0) & (f == 0))
    def _():
        o_ref[...] = jnp.zeros(o_ref.shape, F32)

    @pl.when(f == 0)
    def _():
        posr = posr_ref[0]
        h = h_ref[...]

        def gather(s, _):
            base = pl.multiple_of(s * MOE_SUB, MOE_SUB)
            slot = (base + lax.broadcasted_iota(I32, (MOE_SUB, tm), 0)).astype(F32)
            onehot = jnp.where(posr == slot, 1.0, 0.0).astype(BF16)
            xg[pl.ds(base, MOE_SUB), :] = _dot(onehot, h).astype(BF16)
            yacc[pl.ds(base, MOE_SUB), :] = jnp.zeros((MOE_SUB, D_MODEL), F32)
            return 0

        lax.fori_loop(0, n_sub, gather, 0)

    def expert(s, _):
        rows = pl.ds(pl.multiple_of(s * MOE_SUB, MOE_SUB), MOE_SUB)
        xs = xg[rows, :]
        a = jax.nn.silu(_dot(xs, wg_ref[0])) * _dot(xs, wu_ref[0])
        yacc[rows, :] += _dot(a.astype(BF16), wd_ref[0])
        return 0

    lax.fori_loop(0, n_sub, expert, 0)

    @pl.when(f == pl.num_programs(2) - 1)
    def _():
        lane = lax.broadcasted_iota(I32, (tm, LANES), 1)
        mine = lane == e
        posc = jnp.sum(jnp.where(mine, posc_ref[...], 0.0), axis=1, keepdims=True)
        gate = jnp.sum(jnp.where(mine, gate_ref[...], 0.0), axis=1, keepdims=True)

        def scatter(s, _):
            base = pl.multiple_of(s * MOE_SUB, MOE_SUB)
            slot = (base + lax.broadcasted_iota(I32, (tm, MOE_SUB), 1)).astype(F32)
            onehot = jnp.where(posc == slot, 1.0, 0.0).astype(BF16)
            o_ref[...] += gate * _dot(onehot, yacc[pl.ds(base, MOE_SUB), :].astype(BF16))
            return 0

        lax.fori_loop(0, n_sub, scatter, 0)


def _moe(x, g, r_hi, r_lo, wg, wu, wd, tm, tf):
    m = x.shape[0]
    nt = m // tm
    row = lambda i: (i, 0)
    const = lambda i: (0, 0)
    h, gate, posc, posr, cnt = pl.pallas_call(
        _moe_route_kernel,
        grid=(nt,),
        in_specs=[pl.BlockSpec((tm, D_MODEL), row), pl.BlockSpec((1, D_MODEL), const),
                  pl.BlockSpec((D_MODEL, LANES), const), pl.BlockSpec((D_MODEL, LANES), const)],
        out_specs=[pl.BlockSpec((tm, D_MODEL), row), pl.BlockSpec((tm, LANES), row), pl.BlockSpec((tm, LANES), row),
                   pl.BlockSpec((1, N_EXPERTS, tm), lambda i: (i, 0, 0)), pl.BlockSpec((1, 8, LANES), lambda i: (i, 0, 0))],
        out_shape=[jax.ShapeDtypeStruct((m, D_MODEL), BF16), jax.ShapeDtypeStruct((m, LANES), F32),
                   jax.ShapeDtypeStruct((m, LANES), F32), jax.ShapeDtypeStruct((nt, N_EXPERTS, tm), F32),
                   jax.ShapeDtypeStruct((nt, 8, LANES), F32)],
        compiler_params=_cparams(("parallel",)),
        name="moe_route",
    )(x, g, r_hi, r_lo)
    counts = cnt[:, 0, :N_EXPERTS].astype(I32).reshape(nt * N_EXPERTS)
    posr = posr.reshape(nt * N_EXPERTS, 1, tm)
    row3 = lambda i, e, f, c: (i, 0)
    slot_rows = pl.cdiv(tm, MOE_SUB) * MOE_SUB
    grid_spec = pltpu.PrefetchScalarGridSpec(
        num_scalar_prefetch=1,
        grid=(nt, N_EXPERTS, EXPERT_FF // tf),
        in_specs=[pl.BlockSpec((tm, D_MODEL), row3),
                  pl.BlockSpec((tm, LANES), row3), pl.BlockSpec((tm, LANES), row3),
                  pl.BlockSpec((1, 1, tm), lambda i, e, f, c: (i * N_EXPERTS + e, 0, 0)),
                  pl.BlockSpec((1, D_MODEL, tf), lambda i, e, f, c: (e, 0, f)),
                  pl.BlockSpec((1, D_MODEL, tf), lambda i, e, f, c: (e, 0, f)),
                  pl.BlockSpec((1, tf, D_MODEL), lambda i, e, f, c: (e, f, 0))],
        out_specs=pl.BlockSpec((tm, D_MODEL), row3),
        scratch_shapes=[pltpu.VMEM((slot_rows, D_MODEL), BF16), pltpu.VMEM((slot_rows, D_MODEL), F32)],
    )
    return pl.pallas_call(
        _moe_expert_kernel,
        grid_spec=grid_spec,
        out_shape=jax.ShapeDtypeStruct((m, D_MODEL), F32),
        compiler_params=_cparams(("parallel", "arbitrary", "arbitrary")),
        name="moe_experts",
    )(counts, h, gate, posc, posr, wg, wu, wd)


def _ple_kernel(*refs, final_norm, n_addends):
    x_refs, (g_ref, p_ref, wp_ref, wgate_ref, gf_ref, o_ref) = refs[:n_addends], refs[n_addends:]
    x = x_refs[0][...]
    for r in x_refs[1:]:
        x = x + r[...]
    hp = _rms(x, g_ref[...]).astype(BF16)
    gate = jax.nn.sigmoid(_dot(hp, wgate_ref[...]))
    y = x + _dot(p_ref[...].astype(BF16), wp_ref[...]) * gate
    if final_norm:
        y = _rms(y, gf_ref[...])
    o_ref[...] = y


def _ple(xs, g, p, wp, wgate, g_final, final_norm, tm):
    m = xs[0].shape[0]
    row = lambda i: (i, 0)
    const = lambda i: (0, 0)
    return pl.pallas_call(
        functools.partial(_ple_kernel, final_norm=final_norm, n_addends=len(xs)),
        grid=(m // tm,),
        in_specs=[pl.BlockSpec((tm, D_MODEL), row)] * len(xs)
        + [pl.BlockSpec((1, D_MODEL), const), pl.BlockSpec((tm, PLE_DIM), row),
           pl.BlockSpec((PLE_DIM, D_MODEL), const), pl.BlockSpec((D_MODEL, D_MODEL), const),
           pl.BlockSpec((1, D_MODEL), const)],
        out_specs=pl.BlockSpec((tm, D_MODEL), row),
        out_shape=jax.ShapeDtypeStruct((m, D_MODEL), F32),
        compiler_params=_cparams(("parallel",)),
        name="per_layer_embedding",
    )(*xs, g, p, wp, wgate, g_final)


def _ret_kernel(q_ref, k_ref, v_ref, gate_ref, cos_ref, sin_ref, s0_ref, o_ref, sout_ref, state, *, chunk, chunk_rows):
    i = pl.program_id(1)

    @pl.when(i == 0)
    def _():
        state[...] = s0_ref[0]

    cos, sin = cos_ref[...], sin_ref[...]
    tt = q_ref.shape[1]
    r = chunk_rows
    ii = lax.broadcasted_iota(I32, (r, r), 0).astype(F32)
    jj = lax.broadcasted_iota(I32, (r, r), 1).astype(F32)
    rel = ii - jj
    icol = lax.broadcasted_iota(I32, (r, 1), 0).astype(F32)
    half = RET_DK // 2

    def rot(ref, hd):
        x1 = ref[0, :, hd * RET_DK:hd * RET_DK + half]
        x2 = ref[0, :, hd * RET_DK + half:(hd + 1) * RET_DK]
        return jnp.concatenate([x1 * cos - x2 * sin, x2 * cos + x1 * sin], axis=1)

    for hd in range(RET_HEADS):
        lg = math.log(1.0 - 2.0 ** (-5.0 - hd))
        d_in = jnp.where(rel >= 0, jnp.exp(lg * jnp.maximum(rel, 0.0)), 0.0)
        d_q = jnp.exp(lg * (icol + 1.0))
        d_k = jnp.exp(lg * (chunk - 1.0 - icol)) * (RET_DK ** -0.5)
        d_c = math.exp(lg * chunk)
        qr = rot(q_ref, hd)
        kr = rot(k_ref, hd)
        vsl = slice(hd * RET_DV, (hd + 1) * RET_DV)
        for c in range(tt // r):
            rows = slice(c * r, (c + 1) * r)
            qc = qr[rows].astype(BF16)
            kc = kr[rows]
            vc = v_ref[0, rows, vsl]
            s_prev = state[hd]
            att = _dot_nt(qc, (kc * (RET_DK ** -0.5)).astype(BF16)) * d_in
            o = _dot(att.astype(BF16), vc) + _dot(qc, s_prev.astype(BF16)) * d_q
            state[hd] = s_prev * d_c + _dot_tn((kc * d_k).astype(BF16), vc)
            mu = jnp.mean(o, axis=-1, keepdims=True)
            var = jnp.mean(jnp.square(o - mu), axis=-1, keepdims=True)
            on = (o - mu) * lax.rsqrt(var + EPS)
            o_ref[0, rows, vsl] = (jax.nn.silu(gate_ref[0, rows, vsl].astype(F32)) * on).astype(BF16)

    @pl.when(i == pl.num_programs(1) - 1)
    def _():
        sout_ref[0] = state[...]


def _retention(zqk, zvg, s0, cos, sin, tt, chunk, chunk_rows):
    b, t, _ = zqk.shape
    hk, hv = RET_HEADS * RET_DK, RET_HEADS * RET_DV
    half = RET_DK // 2
    state_spec = pl.BlockSpec((1, RET_HEADS, RET_DK, RET_DV), lambda bi, i: (bi, 0, 0, 0))
    tab = pl.BlockSpec((tt, half), lambda bi, i: (i, 0))
    return pl.pallas_call(
        functools.partial(_ret_kernel, chunk=chunk, chunk_rows=chunk_rows),
        grid=(b, t // tt),
        in_specs=[pl.BlockSpec((1, tt, hk), lambda bi, i: (bi, i, 0)), pl.BlockSpec((1, tt, hk), lambda bi, i: (bi, i, 1)),
                  pl.BlockSpec((1, tt, hv), lambda bi, i: (bi, i, 0)), pl.BlockSpec((1, tt, hv), lambda bi, i: (bi, i, 1)),
                  tab, tab, state_spec],
        out_specs=[pl.BlockSpec((1, tt, hv), lambda bi, i: (bi, i, 0)), state_spec],
        out_shape=[jax.ShapeDtypeStruct((b, t, hv), BF16), jax.ShapeDtypeStruct(s0.shape, F32)],
        scratch_shapes=[pltpu.VMEM((RET_HEADS, RET_DK, RET_DV), F32)],
        compiler_params=_cparams(("parallel", "arbitrary")),
        name="retention",
    )(zqk, zqk, zvg, zvg, cos, sin, s0)


def _pack_l0_w_in(w):
    offs = np.cumsum((0,) + AB_SPLITS)
    bg, cg, hv, q, k, v, qi, ki, wi = [w[:, offs[n]:offs[n + 1]] for n in range(len(AB_SPLITS))]
    group = ATT_HEADS // KV_HEADS
    q4 = q.reshape(D_MODEL, ATT_HEADS, HEAD_DIM)
    zq = jnp.zeros_like(q4)
    q_pad = jnp.concatenate([jnp.concatenate([q4[:, :group], zq[:, :group]], axis=-1),
                             jnp.concatenate([zq[:, group:], q4[:, group:]], axis=-1)], axis=1).reshape(D_MODEL, -1)
    qi4 = qi.reshape(D_MODEL, IDX_HEADS, IDX_DIM)
    qi_pad = jnp.concatenate([qi4, jnp.zeros_like(qi4)], axis=-1).reshape(D_MODEL, -1)
    kiw = jnp.concatenate([ki, wi, jnp.zeros((D_MODEL, LANES - IDX_DIM - IDX_HEADS), w.dtype)], axis=1)
    return jnp.concatenate([bg, cg, hv, q_pad, k, v, qi_pad, kiw], axis=1).astype(BF16)


def _pack_l0_w_out(w):
    group = ATT_HEADS // KV_HEADS
    wa, wb = w[:CONV_CH], w[CONV_CH:].reshape(ATT_HEADS, HEAD_DIM, D_MODEL)
    zb = jnp.zeros_like(wb)
    wb_pad = jnp.concatenate([jnp.concatenate([wb[:group], zb[:group]], axis=1),
                              jnp.concatenate([zb[group:], wb[group:]], axis=1)], axis=0).reshape(-1, D_MODEL)
    return jnp.concatenate([wa, wb_pad], axis=0).astype(BF16)


def _rope_tables(pos, reps):
    inv = ROPE_THETA ** (-jnp.arange(0, HEAD_DIM, 2, dtype=F32) / HEAD_DIM)
    ang = pos.astype(F32)[:, None] * inv[None, :]
    cos, sin = jnp.cos(ang), jnp.sin(ang)
    cos64 = jnp.concatenate([cos, cos], axis=1)
    sin64 = jnp.concatenate([-sin, sin], axis=1)
    one, zero = jnp.ones_like(cos64), jnp.zeros_like(cos64)
    tabs = (jnp.concatenate([cos64, cos64], 1), jnp.concatenate([sin64, sin64], 1),
            jnp.concatenate([cos64, one], 1), jnp.concatenate([sin64, zero], 1))
    return tuple(jnp.tile(tb, (reps, 1)) for tb in tabs)


def _ret_tables(pos):
    inv = ROPE_THETA ** (-jnp.linspace(0.0, 1.0, RET_DK // 2, dtype=F32))
    ang = pos.astype(F32)[:, None] * inv[None, :]
    return jnp.cos(ang), jnp.sin(ang)


def _pack_params(prm):
    r = prm['moe_router'][0]
    r_pad = jnp.concatenate([r, jnp.zeros((D_MODEL, LANES - N_EXPERTS), F32)], axis=1)
    r_hi = r_pad.astype(BF16)
    bf = lambda a: a.astype(BF16)
    return dict(
        l0_w_in=_pack_l0_w_in(prm['ab_w_in'][0]), l0_w_out=_pack_l0_w_out(prm['ab_w_out'][0]),
        l0_w_vt=bf(prm['ab_w_in'][0][:, sum(AB_SPLITS[:5]):sum(AB_SPLITS[:6])].T),
        conv_w=prm['ab_conv_w'][0],
        ffn=(bf(prm['ffn_w_gate'][0]), bf(prm['ffn_w_up'][0]), bf(prm['ffn_w_down'][0])),
        ret_w_in=bf(prm['ret_w_in'][0]), ret_w_out=bf(prm['ret_w_out'][0]),
        r_hi=r_hi, r_lo=(r_pad - r_hi.astype(F32)).astype(BF16),
        moe=(bf(prm['moe_w_gate'][0]), bf(prm['moe_w_up'][0]), bf(prm['moe_w_down'][0])),
        ple_w=bf(prm['ple_w']), ple_gate_w=bf(prm['ple_gate_w']),
        norm_mix=prm['norm_mix'][:, None, :], norm_ffn=prm['norm_ffn'][:, None, :],
        norm_ple=prm['norm_ple'][:, None, :], norm_final=prm['norm_final'][None, :],
    )


def _trunk(x, p, pos, conv_buf, ret_state, pk, paged):
    b, t, _ = x.shape
    m = b * t
    tm = min(m, TOKEN_TILE)
    moe_tm = min(m, MOE_TOKEN_TILE)
    assert m % tm == 0 and m % moe_tm == 0 and m % min(m, 256) == 0, "token count must divide into whole tiles"
    assert paged is not None or (t % QUERY_BLOCK == 0 and t % tm == 0), "prompt length must divide into whole tiles"
    assert paged is None or (tm % t == 0 and t <= 8), "sample sequences are one sublane tile long at most"
    x2 = x.reshape(m, D_MODEL)

    reps = max(1, tm // t)
    tabs = _rope_tables(pos, reps)
    bg, u, q, k, v, qi, kiw, kbf, vbf, kiwbf, vt = _l0_in_proj(x2, pk['norm_mix'][0], pk['l0_w_in'], pk['l0_w_vt'], tabs,
                                                              tm, tabs[0].shape[0] // tm, head_major=paged is None)
    seq = lambda a: a.reshape(b, t, a.shape[-1])
    if paged is None:
        yb = _dsa_prompt(q, qi, seq(kiw), seq(kbf), vt, seq(kiwbf))
    else:
        cache_k, cache_v, cache_ki, page_table = paged
        padt = lambda a: jnp.swapaxes(jnp.pad(seq(a), ((0, 0), (0, KEY_CHUNK - t), (0, 0))), 1, 2)
        n_pool = cache_k.shape[0]
        pages_t = lambda c: jnp.transpose(c, (0, 2, 3, 1)).reshape(n_pool, LANES, PAGE_SIZE)
        yb = _dsa_sample(seq(q).astype(F32), seq(qi).astype(F32), seq(kiw), padt(kbf), padt(vbf), padt(kiwbf),
                         pages_t(cache_k), pages_t(cache_v), jnp.swapaxes(cache_ki, 1, 2), page_table).astype(BF16)
    u3 = seq(u)
    ya = _conv(u3, seq(bg), conv_buf, pk['conv_w'], tt=min(t, TOKEN_TILE))
    x2 = _mm_res([ya.reshape(m, CONV_CH), yb.reshape(m, ATT_HEADS * LANES)], pk['l0_w_out'], x2, tm)
    x2 = _ffn(x2, pk['norm_ffn'][0], *pk['ffn'], tm=tm, tf=D_FF // 2)
    x2 = _ple([x2], pk['norm_ple'][0], p[0].reshape(m, PLE_DIM), pk['ple_w'][0], pk['ple_gate_w'][0],
              pk['norm_final'], False, tm)
    new_k = k.reshape(1, b, t, KV_HEADS, HEAD_DIM)
    new_v = v.reshape(1, b, t, KV_HEADS, HEAD_DIM)
    new_ki = seq(kiw)[None, :, :, :IDX_DIM]
    new_conv = jnp.concatenate([conv_buf, u3], axis=1)[None, :, -(CONV_WIDTH - 1):]

    zqk, zvg = _ret_in_proj(x2, pk['norm_mix'][1], pk['ret_w_in'], min(m, 256))
    zqk, zvg = zqk.reshape(b, t, -1), zvg.reshape(b, t, -1)
    cos_r, sin_r = _ret_tables(pos)
    if t % RET_CHUNK == 0:
        og, s_new = _retention(zqk, zvg, ret_state, cos_r, sin_r, tt=2 * RET_CHUNK, chunk=RET_CHUNK,
                               chunk_rows=RET_CHUNK)
    else:
        rows = 16
        padt = lambda a: jnp.pad(a, ((0, 0),) * (a.ndim - 2) + ((0, rows - t), (0, 0)))
        og, s_new = _retention(padt(zqk), padt(zvg), ret_state, padt(cos_r), padt(sin_r), tt=rows, chunk=t,
                               chunk_rows=rows)
        og = og[:, :t]
    x2 = _mm_res([og.reshape(m, RET_HEADS * RET_DV)], pk['ret_w_out'], x2, tm)
    moe = _moe(x2, pk['norm_ffn'][1], pk['r_hi'], pk['r_lo'], *pk['moe'], tm=moe_tm, tf=EXPERT_FF // 2)
    x2 = _ple([x2, moe], pk['norm_ple'][1], p[1].reshape(m, PLE_DIM), pk['ple_w'][1], pk['ple_gate_w'][1],
              pk['norm_final'], True, tm)
    return x2.reshape(b, t, D_MODEL), new_k, new_v, new_ki, new_conv, s_new[None]


def kernel(x_prompt, x_sample, cache_k, cache_v, cache_kidx, state_conv, state_ret, page_table, p_prompt, p_sample,
           norm_mix, norm_ffn, norm_ple, norm_final, ab_w_in, ab_conv_w, ab_w_out, ffn_w_gate, ffn_w_up, ffn_w_down,
           ret_w_in, ret_w_out, moe_router, moe_w_gate, moe_w_up, moe_w_down, ple_w, ple_gate_w):
    prm = dict(norm_mix=norm_mix, norm_ffn=norm_ffn, norm_ple=norm_ple, norm_final=norm_final, ab_w_in=ab_w_in,
               ab_conv_w=ab_conv_w, ab_w_out=ab_w_out, ffn_w_gate=ffn_w_gate, ffn_w_up=ffn_w_up, ffn_w_down=ffn_w_down,
               ret_w_in=ret_w_in, ret_w_out=ret_w_out, moe_router=moe_router, moe_w_gate=moe_w_gate,
               moe_w_up=moe_w_up, moe_w_down=moe_w_down, ple_w=ple_w, ple_gate_w=ple_gate_w)
    pk = _pack_params(prm)
    b, t = x_prompt.shape[0], x_prompt.shape[1]
    db, ts = x_sample.shape[0], x_sample.shape[1]
    past_len = page_table.shape[1] * PAGE_SIZE
    dt = x_prompt.dtype

    conv0 = jnp.zeros((b, CONV_WIDTH - 1, CONV_CH), dt)
    ret0 = jnp.zeros((b, RET_HEADS, RET_DK, RET_DV), dt)
    y_p, k_p, v_p, ki_p, cb_p, rs_p = _trunk(x_prompt, p_prompt, jnp.arange(t, dtype=I32), conv0, ret0, pk, None)

    pos_s = past_len + jnp.arange(ts, dtype=I32)
    paged = (cache_k[0], cache_v[0], cache_kidx[0], page_table)
    y_s, k_s, v_s, ki_s, cb_s, rs_s = _trunk(x_sample, p_sample, pos_s, state_conv[0], state_ret[0], pk, paged)
    return (y_p, y_s, k_p, v_p, ki_p, cb_p, rs_p, k_s, v_s, ki_s, cb_s, rs_s)
```

```python
import functools
import math

import jax
import jax.numpy as jnp
import numpy as np
from jax import lax
from jax.experimental import pallas as pl
from jax.experimental.pallas import tpu as pltpu

F32 = jnp.float32
BF16 = jnp.bfloat16
I32 = jnp.int32

D_MODEL = 1024
PAGE_SIZE = 128
CONV_CH = D_MODEL // 2
CONV_WIDTH = 3
ATT_HEADS = 8
KV_HEADS = 2
HEAD_DIM = 64
IDX_HEADS = 4
IDX_DIM = 64
TOPK_MAX = 256
ROPE_THETA = 10000.0
RET_HEADS = 4
RET_DK = D_MODEL // RET_HEADS
RET_DV = 2 * RET_DK
RET_CHUNK = 128
D_FF = 2816
N_EXPERTS = 8
EXPERT_FF = 3584
PLE_DIM = 256
EPS = 1e-6
AB_SPLITS = (CONV_CH, CONV_CH, CONV_CH, ATT_HEADS * HEAD_DIM, KV_HEADS * HEAD_DIM, KV_HEADS * HEAD_DIM,
             IDX_HEADS * IDX_DIM, IDX_DIM, IDX_HEADS)

LANES = 128
VMEM_LIMIT = 48 * 1024 * 1024
INT_MIN = -2 ** 31
NEG_BIG = -1e30
LOG2_E = math.log2(math.e)

C_BG, C_CG, C_HV, C_Q, C_K, C_V, C_QI, C_KIW, C_END = 0, 512, 1024, 1536, 2560, 2688, 2816, 3328, 3456
KEY_CHUNK = 512
VT_ROWS = LANES + 16
TOKEN_TILE = 512
QUERY_BLOCK = 128


def _cparams(sem):
    return pltpu.CompilerParams(dimension_semantics=sem, vmem_limit_bytes=VMEM_LIMIT)


def _rms(x, g):
    return x * lax.rsqrt(jnp.mean(x * x, axis=-1, keepdims=True) + EPS) * g


def _dot(a, b):
    return jnp.dot(a, b, preferred_element_type=F32)


def _dot_nt(a, b):
    return lax.dot_general(a, b, (((1,), (1,)), ((), ())), preferred_element_type=F32)


def _dot_tn(a, b):
    return lax.dot_general(a, b, (((0,), (0,)), ((), ())), preferred_element_type=F32)


def _swap_halves64(x):
    lane = lax.broadcasted_iota(I32, x.shape, 1)
    from_above = pltpu.roll(x, LANES - 32, 1)
    from_below = pltpu.roll(x, 32, 1)
    return jnp.where((lane & 63) < 32, from_above, from_below)


def _l0_in_kernel(x_ref, g_ref, w_ref, wvt_ref, cos_ref, sin_ref, cosb_ref, sinb_ref,
                  bg_ref, u_ref, q_ref, k_ref, v_ref, qi_ref, kiw_ref, kbf_ref, vbf_ref, kiwbf_ref, vt_ref, *, head_major):
    h = _rms(x_ref[...], g_ref[...]).astype(BF16)
    cos, sin = cos_ref[...], sin_ref[...]

    def store_head(ref, hd, val):
        if head_major:
            for r in range(ref.shape[0]):
                ref[r, hd] = val[r * QUERY_BLOCK:(r + 1) * QUERY_BLOCK]
        else:
            ref[:, hd * LANES:(hd + 1) * LANES] = val

    def rot(z, c, s):
        return z * c + _swap_halves64(z) * s

    bg_ref[...] = _dot(h, w_ref[:, C_BG:C_CG])
    u_ref[...] = _dot(h, w_ref[:, C_CG:C_HV]) * _dot(h, w_ref[:, C_HV:C_Q])
    zq = _dot(h, w_ref[:, C_Q:C_K])
    for hd in range(ATT_HEADS):
        sl = slice(hd * LANES, (hd + 1) * LANES)
        store_head(q_ref, hd, (rot(zq[:, sl], cos, sin) * (HEAD_DIM ** -0.5 * LOG2_E)).astype(BF16))
    k = rot(_dot(h, w_ref[:, C_K:C_V]), cos, sin)
    k_ref[...] = k
    kbf_ref[...] = k.astype(BF16)
    v = _dot(h, w_ref[:, C_V:C_QI])
    v_ref[...] = v
    vbf_ref[...] = v.astype(BF16)
    vt_ref[0:LANES, :] = _dot_nt(wvt_ref[...], h).astype(BF16)
    vt_ref[LANES:VT_ROWS, :] = jnp.ones((VT_ROWS - LANES, vt_ref.shape[1]), BF16)
    zqi = _dot(h, w_ref[:, C_QI:C_KIW])
    for hd in range(IDX_HEADS):
        sl = slice(hd * LANES, (hd + 1) * LANES)
        store_head(qi_ref, hd, rot(zqi[:, sl], cos, sin).astype(BF16))
    kiw = rot(_dot(h, w_ref[:, C_KIW:C_END]), cosb_ref[...], sinb_ref[...])
    kiw_ref[...] = kiw
    kiwbf_ref[...] = kiw.astype(BF16)


def _l0_in_proj(x2, g, w, wvt, tabs, tm, n_tab_blocks, head_major):
    m = x2.shape[0]
    row = lambda i: (i, 0)
    const = lambda i: (0, 0)
    tab = lambda i: (i % n_tab_blocks, 0)
    widths = (512, 512, 1024, 128, 128, 512, 128, 128, 128, 128)
    dtypes = (F32, F32, BF16, F32, F32, BF16, F32, BF16, BF16, BF16)
    out_specs = [pl.BlockSpec((tm, wd), row) for wd in widths]
    out_shape = [jax.ShapeDtypeStruct((m, wd), dt) for wd, dt in zip(widths, dtypes)]
    if head_major:
        for n in (2, 5):
            heads = widths[n] // LANES
            out_specs[n] = pl.BlockSpec((tm // QUERY_BLOCK, heads, QUERY_BLOCK, LANES), lambda i: (i, 0, 0, 0))
            out_shape[n] = jax.ShapeDtypeStruct((m // QUERY_BLOCK, heads, QUERY_BLOCK, LANES), BF16)
    return pl.pallas_call(
        functools.partial(_l0_in_kernel, head_major=head_major),
        grid=(m // tm,),
        in_specs=[pl.BlockSpec((tm, D_MODEL), row), pl.BlockSpec((1, D_MODEL), const),
                  pl.BlockSpec((D_MODEL, C_END), const), pl.BlockSpec((LANES, D_MODEL), const)]
        + [pl.BlockSpec((tm, LANES), tab)] * 4,
        out_specs=out_specs + [pl.BlockSpec((VT_ROWS, tm), lambda i: (0, i))],
        out_shape=out_shape + [jax.ShapeDtypeStruct((VT_ROWS, m), BF16)],
        compiler_params=_cparams(("parallel",)),
        name="l0_in_proj",
    )(x2, g, w, wvt, *tabs)


def _score_keys(score):
    bits = pltpu.bitcast(score, I32)
    return bits ^ ((bits >> 31) & jnp.int32(0x7FFFFFFF))


def _key_scores(key):
    return pltpu.bitcast(key ^ ((key >> 31) & jnp.int32(0x7FFFFFFF)), F32)


SEARCH_PLAIN_STEPS = 8
SEARCH_SNAP_INTERP_STEPS = 8
SEARCH_MAX_STEPS = SEARCH_PLAIN_STEPS + SEARCH_SNAP_INTERP_STEPS + 34


NO_TIE_LIMIT = 1e9


def _topk_threshold(probe, vmin, vmax, n_real, topk):
    k = float(topk)
    lo0, hi0 = _score_keys(vmin), _score_keys(vmax) + 1
    take_all = n_real <= k
    zero, one = jnp.zeros_like(vmin), jnp.ones_like(vmin)
    active0 = jnp.where(jnp.logical_not(take_all) & (lo0 + 1 < hi0), 1.0, 0.0)

    def step(snap, st):
        it, lo, hi, c_lo, c_hi, w_lo, w_hi, last, active = st
        act = active > 0.0
        f_lo = (c_lo - (k - 0.5)) * w_lo
        f_hi = ((k - 0.5) - c_hi) * w_hi
        v_lo, v_hi = _key_scores(lo), _key_scores(hi)
        g = _score_keys(v_lo + (v_hi - v_lo) * (f_lo / (f_lo + f_hi)))
        if snap:
            mid = (lo >> 1) + (hi >> 1) + (lo & hi & 1)
            g = jnp.where(it < SEARCH_PLAIN_STEPS + SEARCH_SNAP_INTERP_STEPS, g, mid)
        g = jnp.minimum(jnp.maximum(g, lo + 1), hi - 1)
        c, key_up, key_dn = probe(g, snap)
        hit = act & (c == k)
        up = act & (c > k)
        dn = act & (c < k)
        lo = jnp.where(hit, g, jnp.where(up, key_up if snap else g, lo))
        c_lo = jnp.where(hit | up, c, c_lo)
        hi = jnp.where(dn, key_dn + 1 if snap else g, hi)
        c_hi = jnp.where(dn, c, c_hi)
        w_hi = jnp.where(up, jnp.where(last > 0.0, w_hi * 0.5, one), jnp.where(dn, one, w_hi))
        w_lo = jnp.where(dn, jnp.where(last < 0.0, w_lo * 0.5, one), jnp.where(up, one, w_lo))
        last = jnp.where(up, one, jnp.where(dn, -one, last))
        active = jnp.where(act & jnp.logical_not(hit) & (lo + 1 < hi), 1.0, 0.0)
        return it + 1, lo, hi, c_lo, c_hi, w_lo, w_hi, last, active

    def cond(limit, st):
        return (st[0] < limit) & (jnp.max(st[-1]) > 0.0)

    st = (jnp.int32(0), lo0, hi0, n_real, zero, one, one, zero, active0)
    st = lax.fori_loop(0, SEARCH_PLAIN_STEPS, lambda _, s: step(False, s), st)
    st = lax.while_loop(functools.partial(cond, SEARCH_MAX_STEPS), functools.partial(step, True), st)
    _, lo, _, c_lo, c_hi, _, _, _, _ = st
    thr = jnp.where(take_all, INT_MIN, lo)
    excess = jnp.logical_not(take_all) & (c_lo > k)
    ties = jnp.where(excess, k - c_hi, jnp.where(take_all, 0.0, NO_TIE_LIMIT))
    return thr, ties, jnp.max(jnp.where(excess, 1.0, 0.0)) > 0.0


def _tie_prefix_matrix(lower):
    r = lax.broadcasted_iota(I32, (KEY_CHUNK, KEY_CHUNK), 0)
    c = lax.broadcasted_iota(I32, (KEY_CHUNK, KEY_CHUNK), 1)
    return jnp.where((r >= c) if lower else (r <= c), 1.0, 0.0).astype(BF16)


def _bias_bits(sel):
    return pltpu.bitcast(jnp.where(sel, 0.0, NEG_BIG).astype(F32), I32)


ROW_SLAB = 2048


def _lane_fold(m, op=jnp.add):
    parts = [m[:, c * LANES:(c + 1) * LANES] for c in range(m.shape[1] // LANES)]
    while len(parts) > 1:
        parts = [op(parts[n], parts[n + 1]) for n in range(0, len(parts) - 1, 2)] + (parts[-1:] if len(parts) % 2 else [])
    return parts[0]


def _row_slabs(n_chunks):
    width = n_chunks * KEY_CHUNK
    return [(k0, min(ROW_SLAB, width - k0)) for k0 in range(0, width, ROW_SLAB)]


def _int_reduce(x, take_min, axis):
    red = jnp.min if take_min else jnp.max
    hi = (x >> 16).astype(F32)
    lo = (x & 0xFFFF).astype(F32)
    m_hi = red(hi, axis=axis, keepdims=True)
    m_lo = red(jnp.where(hi == m_hi, lo, 65536.0 if take_min else -1.0), axis=axis, keepdims=True)
    return (m_hi.astype(I32) << 16) | m_lo.astype(I32)


def _row_probe(s_ref, n_chunks, g, snap):
    cnt = up = dn = None
    for k0, size in _row_slabs(n_chunks):
        blk = s_ref[:, k0:k0 + size]
        ge = blk >= g
        part = _lane_fold(jnp.where(ge, 1.0, 0.0))
        cnt = part if cnt is None else cnt + part
        if snap:
            above = _lane_fold(jnp.where(ge, blk, 2 ** 31 - 1), jnp.minimum)
            below = _lane_fold(jnp.where(ge, INT_MIN, blk), jnp.maximum)
            up = above if up is None else jnp.minimum(up, above)
            dn = below if dn is None else jnp.maximum(dn, below)
    c = jnp.sum(cnt, axis=1, keepdims=True)
    if not snap:
        return c, None, None
    return c, _int_reduce(up, True, 1), _int_reduce(dn, False, 1)


def _row_select_to_bias(s_ref, n_chunks, n_real, topk):
    inf = jnp.float32(jnp.inf)
    mn = mx = None
    for k0, size in _row_slabs(n_chunks):
        blk = s_ref[:, k0:k0 + size]
        v = _key_scores(blk)
        real = blk != INT_MIN
        lo_part = _lane_fold(jnp.where(real, v, inf), jnp.minimum)
        hi_part = _lane_fold(jnp.where(real, v, -inf), jnp.maximum)
        mn = lo_part if mn is None else jnp.minimum(mn, lo_part)
        mx = hi_part if mx is None else jnp.maximum(mx, hi_part)
    vmin, vmax = jnp.min(mn, axis=1, keepdims=True), jnp.max(mx, axis=1, keepdims=True)
    thr, ties, any_excess = _topk_threshold(functools.partial(_row_probe, s_ref, n_chunks), vmin, vmax, n_real, topk)

    def plain(_):
        thr_ge = jnp.maximum(thr, INT_MIN + 1)
        for k0, size in _row_slabs(n_chunks):
            s_ref[:, k0:k0 + size] = _bias_bits(s_ref[:, k0:k0 + size] >= thr_ge)
        return 0

    def with_ties(_):
        upper = _tie_prefix_matrix(lower=False)
        seen = jnp.zeros_like(ties)
        for c in range(n_chunks):
            blk = s_ref[:, c * KEY_CHUNK:(c + 1) * KEY_CHUNK]
            tie = jnp.where(blk == thr, 1.0, 0.0)
            rows = tie.shape[0]
            tie16 = jnp.concatenate([tie, tie], axis=0).astype(BF16) if rows % 16 else tie.astype(BF16)
            rank = _dot(tie16, upper)[0:rows] + seen
            s_ref[:, c * KEY_CHUNK:(c + 1) * KEY_CHUNK] = _bias_bits((blk > thr) | ((tie > 0.0) & (rank <= ties)))
            seen = seen + jnp.sum(tie, axis=1, keepdims=True)
        return 0

    lax.cond(any_excess, with_ties, plain, 0)


def _attend_group(qg, segments, bias_ref, rep):
    n = qg.shape[0]

    def step(chunk, base, load_kv, c, carry):
        m, l, acc = carry
        k0 = pl.multiple_of(base + c * chunk, KEY_CHUNK)
        ktc, vtc = load_kv(k0)
        bias = pltpu.bitcast(bias_ref[:, pl.ds(k0, chunk)], F32)
        s = _dot(qg, ktc) + jnp.concatenate([bias] * rep, axis=0)
        m_new = jnp.maximum(m, jnp.max(s, axis=1, keepdims=True))
        alpha = jnp.exp2(m - m_new)
        p = jnp.exp2(s - m_new)
        l = alpha * l + jnp.sum(p, axis=1, keepdims=True)
        acc = alpha * acc + _dot_nt(p.astype(BF16), vtc)
        return m_new, l, acc

    carry = (jnp.full((n, 1), NEG_BIG, F32), jnp.zeros((n, 1), F32), jnp.zeros((n, LANES), F32))
    for n_steps, chunk, base, load_kv in segments:
        carry = lax.fori_loop(0, n_steps, functools.partial(step, chunk, base, load_kv), carry)
    _, l, acc = carry
    return acc / l


def _stack_heads(x, first, count, width=LANES):
    x = x.astype(F32)
    return jnp.concatenate([x[:, (first + hd) * LANES:(first + hd) * LANES + width] for hd in range(count)],
                           axis=0).astype(BF16)


def _attend_all_heads(q, segments, bias_ref, o_ref):
    rows = q.shape[0]
    group = ATT_HEADS // KV_HEADS
    out = _attend_group(_stack_heads(q, 0, ATT_HEADS), segments, bias_ref, ATT_HEADS)
    for hd in range(ATT_HEADS):
        piece = out[hd * rows:(hd + 1) * rows]
        piece = jnp.where(_group_lane_mask(piece.shape, hd // group), piece, 0.0)
        o_ref[0, :, hd * LANES:(hd + 1) * LANES] = piece.astype(o_ref.dtype)


def _indexer_weights(kiw_q):
    return kiw_q[:, IDX_DIM:IDX_DIM + IDX_HEADS] * (IDX_HEADS ** -0.5 * IDX_DIM ** -0.5)


def _group_lane_mask(shape, g):
    lane = lax.broadcasted_iota(I32, shape, 1)
    return (lane >= g * HEAD_DIM) & (lane < (g + 1) * HEAD_DIM)


ATT_CHUNK = 256


def _sublane_fold(m, op=jnp.add):
    parts = [m[r * 8:(r + 1) * 8] for r in range(m.shape[0] // 8)]
    while len(parts) > 1:
        parts = [op(parts[n], parts[n + 1]) for n in range(0, len(parts) - 1, 2)] + (parts[-1:] if len(parts) % 2 else [])
    return parts[0]


def _col_reduce(m, op, reduce_fn):
    return reduce_fn(_sublane_fold(m, op), axis=0, keepdims=True)


def _col_probe(s_ref, n_chunks, g, snap):
    cols = s_ref.shape[1]

    def body(c, carry):
        k0 = pl.multiple_of(c * KEY_CHUNK, KEY_CHUNK)
        blk = s_ref[pl.ds(k0, KEY_CHUNK), :]
        ge = blk >= g
        out = [carry[0] + _sublane_fold(jnp.where(ge, 1.0, 0.0))]
        if snap:
            out.append(jnp.minimum(carry[1], _sublane_fold(jnp.where(ge, blk, 2 ** 31 - 1), jnp.minimum)))
            out.append(jnp.maximum(carry[2], _sublane_fold(jnp.where(ge, INT_MIN, blk), jnp.maximum)))
        return tuple(out)

    init = [jnp.zeros((8, cols), F32)]
    if snap:
        init += [jnp.full((8, cols), 2 ** 31 - 1, I32), jnp.full((8, cols), INT_MIN, I32)]
    res = lax.fori_loop(0, n_chunks, body, tuple(init))
    c = jnp.sum(res[0], axis=0, keepdims=True)
    if not snap:
        return c, None, None
    return c, _int_reduce(res[1], True, 0), _int_reduce(res[2], False, 0)


def _col_select_to_bias(s_ref, n_chunks, n_real, topk, vmin, vmax):
    thr, ties, any_excess = _topk_threshold(functools.partial(_col_probe, s_ref, n_chunks), vmin, vmax, n_real, topk)

    def plain(_):
        thr_ge = jnp.maximum(thr, INT_MIN + 1)

        def body(c, _):
            k0 = pl.multiple_of(c * KEY_CHUNK, KEY_CHUNK)
            s_ref[pl.ds(k0, KEY_CHUNK), :] = _bias_bits(s_ref[pl.ds(k0, KEY_CHUNK), :] >= thr_ge)
            return 0

        return lax.fori_loop(0, n_chunks, body, 0)

    def with_ties(_):
        lower = _tie_prefix_matrix(lower=True)

        def body(c, seen):
            k0 = pl.multiple_of(c * KEY_CHUNK, KEY_CHUNK)
            blk = s_ref[pl.ds(k0, KEY_CHUNK), :]
            tie = jnp.where(blk == thr, 1.0, 0.0)
            rank = _dot(lower, tie.astype(BF16)) + seen
            s_ref[pl.ds(k0, KEY_CHUNK), :] = _bias_bits((blk > thr) | ((tie > 0.0) & (rank <= ties)))
            return seen + jnp.sum(_sublane_fold(tie), axis=0, keepdims=True)

        lax.fori_loop(0, n_chunks, body, jnp.zeros_like(ties))
        return 0

    lax.cond(any_excess, with_ties, plain, 0)


def _dsa_prompt_kernel(q_ref, qi_ref, kiwq_ref, k_ref, vt_ref, kiw_ref, o_ref, s_ref, m_scr, acc_scr,
                       sa_scr, sb_scr, ia_scr, ib_scr, *, qb, topk):
    j = pl.program_id(1)
    n_keys = j * qb + qb
    n_chunks = (n_keys + KEY_CHUNK - 1) // KEY_CHUNK
    qpos = j * qb + lax.broadcasted_iota(I32, (1, qb), 1)
    wt = kiwq_ref[0].T[IDX_DIM:IDX_DIM + IDX_HEADS] * (IDX_HEADS ** -0.5 * IDX_DIM ** -0.5)

    inf = jnp.float32(jnp.inf)

    last_chunk = s_ref.shape[0] // KEY_CHUNK - 1

    def index_dots(c, buf):
        k0 = pl.multiple_of(jnp.minimum(c, last_chunk) * KEY_CHUNK, KEY_CHUNK)
        kc = kiw_ref[0, pl.ds(k0, KEY_CHUNK), :]
        for pr in range(IDX_HEADS // 2):
            buf[pr] = _dot_nt(kc, qi_ref[0, 2 * pr:2 * pr + 2].reshape(2 * qb, LANES))

    def score_chunk(c, buf, carry):
        mn, mx = carry
        k0 = pl.multiple_of(jnp.minimum(c, last_chunk) * KEY_CHUNK, KEY_CHUNK)
        acc = jnp.zeros((KEY_CHUNK, qb), F32)
        for pr in range(IDX_HEADS // 2):
            acc = acc + jnp.maximum(buf[pr, :, 0:qb], 0.0) * wt[2 * pr:2 * pr + 1]
            acc = acc + jnp.maximum(buf[pr, :, qb:2 * qb], 0.0) * wt[2 * pr + 1:2 * pr + 2]
        causal = k0 + lax.broadcasted_iota(I32, (KEY_CHUNK, qb), 0) <= qpos
        s_ref[pl.ds(k0, KEY_CHUNK), :] = jnp.where(causal, _score_keys(acc), INT_MIN)
        mn = jnp.minimum(mn, _sublane_fold(jnp.where(causal, acc, inf), jnp.minimum))
        mx = jnp.maximum(mx, _sublane_fold(jnp.where(causal, acc, -inf), jnp.maximum))
        return mn, mx

    index_dots(0, ia_scr)

    def scores(t, carry):
        index_dots(2 * t + 1, ib_scr)
        carry = score_chunk(2 * t, ia_scr, carry)
        index_dots(2 * t + 2, ia_scr)
        return score_chunk(2 * t + 1, ib_scr, carry)

    mn, mx = lax.fori_loop(0, (n_chunks + 1) // 2, scores,
                           (jnp.full((8, qb), inf, F32), jnp.full((8, qb), -inf, F32)))
    _col_select_to_bias(s_ref, n_chunks, (qpos + 1).astype(F32), topk,
                        jnp.min(mn, axis=0, keepdims=True), jnp.max(mx, axis=0, keepdims=True))

    m_scr[...] = jnp.full(m_scr.shape, NEG_BIG, F32)
    acc_scr[...] = jnp.zeros(acc_scr.shape, F32)

    n_att = n_chunks * (KEY_CHUNK // ATT_CHUNK)

    def logits(step, buf):
        k0 = pl.multiple_of(jnp.minimum(step, n_att - 1) * ATT_CHUNK, ATT_CHUNK)
        kc = k_ref[0, pl.ds(k0, ATT_CHUNK), :]
        bias = pltpu.bitcast(s_ref[pl.ds(k0, ATT_CHUNK), :], F32)
        for pr in range(ATT_HEADS // 2):
            s2 = _dot_nt(kc, q_ref[0, 2 * pr:2 * pr + 2].reshape(2 * qb, LANES))
            buf[2 * pr] = s2[:, :qb] + bias
            buf[2 * pr + 1] = s2[:, qb:] + bias

    def accumulate(step, buf):
        k0 = pl.multiple_of(step * ATT_CHUNK, ATT_CHUNK)
        vtc = vt_ref[:, pl.ds(k0, ATT_CHUNK)]
        for pr in range(ATT_HEADS // 2):
            ps, alphas = [], []
            for hd in (2 * pr, 2 * pr + 1):
                s = buf[hd]
                m_prev = m_scr[hd:hd + 1, :]
                m_new = jnp.maximum(m_prev, _col_reduce(s, jnp.maximum, jnp.max))
                alphas.append(jnp.exp2(m_prev - m_new))
                ps.append(jnp.exp2(s - m_new).astype(BF16))
                m_scr[hd:hd + 1, :] = m_new
            acc_scr[pr] = jnp.concatenate(alphas, axis=1) * acc_scr[pr] + _dot(vtc, jnp.concatenate(ps, axis=1))

    logits(0, sa_scr)

    def attend(c, _):
        logits(2 * c + 1, sb_scr)
        accumulate(2 * c, sa_scr)
        logits(2 * c + 2, sa_scr)
        accumulate(2 * c + 1, sb_scr)
        return 0

    lax.fori_loop(0, n_att // 2, attend, 0)

    group = ATT_HEADS // KV_HEADS
    for hd in range(ATT_HEADS):
        cols = slice((hd % 2) * qb, (hd % 2 + 1) * qb)
        out_t = acc_scr[hd // 2, 0:LANES, cols] / acc_scr[hd // 2, LANES:LANES + 1, cols]
        row = lax.broadcasted_iota(I32, out_t.shape, 0)
        g = hd // group
        out_t = jnp.where((row >= g * HEAD_DIM) & (row < (g + 1) * HEAD_DIM), out_t, 0.0)
        o_ref[0, :, hd * LANES:(hd + 1) * LANES] = out_t.T.astype(BF16)


def _dsa_prompt(q, qi, kiw, kbf, vt, kiwbf):
    b, t, _ = kiw.shape
    qb = QUERY_BLOCK
    nb = t // qb
    topk = min(TOPK_MAX, t // 4)
    qblk = lambda width: pl.BlockSpec((1, qb, width), lambda bi, j: (bi, j, 0))
    hblk = lambda heads: pl.BlockSpec((1, heads, qb, LANES), lambda bi, j: (bi * nb + j, 0, 0, 0))
    full = pl.BlockSpec((1, t, LANES), lambda bi, j: (bi, 0, 0))
    return pl.pallas_call(
        functools.partial(_dsa_prompt_kernel, qb=qb, topk=topk),
        grid=(b, nb),
        in_specs=[hblk(ATT_HEADS), hblk(IDX_HEADS), qblk(LANES), full,
                  pl.BlockSpec((VT_ROWS, t), lambda bi, j: (0, bi)), full],
        out_specs=qblk(ATT_HEADS * LANES),
        out_shape=jax.ShapeDtypeStruct((b, t, ATT_HEADS * LANES), BF16),
        scratch_shapes=[pltpu.VMEM((pl.cdiv(t, KEY_CHUNK) * KEY_CHUNK, qb), I32), pltpu.VMEM((ATT_HEADS, qb), F32),
                        pltpu.VMEM((ATT_HEADS // 2, VT_ROWS, 2 * qb), F32),
                        pltpu.VMEM((ATT_HEADS, ATT_CHUNK, qb), F32), pltpu.VMEM((ATT_HEADS, ATT_CHUNK, qb), F32),
                        pltpu.VMEM((IDX_HEADS // 2, KEY_CHUNK, 2 * qb), F32),
                        pltpu.VMEM((IDX_HEADS // 2, KEY_CHUNK, 2 * qb), F32)],
        compiler_params=_cparams(("parallel", "arbitrary")),
        name="dsa_prompt",
    )(q, qi, kiw, kbf, vt, kiwbf)


def _dsa_sample_kernel(pt_ref, q_ref, qi_ref, kiwq_ref, kn_ref, vn_ref, kiwn_ref, ck_hbm, cv_hbm, cki_hbm,
                       o_ref, kbuf, vbuf, kibuf, s_ref, sems, *, ts, n_pages, topk):
    b = pl.program_id(0)
    past = n_pages * PAGE_SIZE

    streams = ((cki_hbm, kibuf), (ck_hbm, kbuf), (cv_hbm, vbuf))

    def page_copy(p, which):
        src, dst = streams[which]
        cols = pl.ds(pl.multiple_of(p * PAGE_SIZE, PAGE_SIZE), PAGE_SIZE)
        return pltpu.make_async_copy(src.at[pt_ref[b, p]], dst.at[:, cols], sems.at[which])

    def start_page(p, _):
        for which in range(len(streams)):
            page_copy(p, which).start()
        return 0

    lax.fori_loop(0, n_pages, start_page, 0)

    def wait_pages(which):
        def body(p, _):
            page_copy(p, which).wait()
            return 0
        lax.fori_loop(0, n_pages, body, 0)

    n_past_chunks = past // KEY_CHUNK
    n_chunks = n_past_chunks + 1
    qpos = lax.broadcasted_iota(I32, (ts, 1), 0)
    wq = _indexer_weights(kiwq_ref[0])
    qi_stack = _stack_heads(qi_ref[0], 0, IDX_HEADS, IDX_DIM)
    w_stack = jnp.concatenate([wq[:, hd:hd + 1] for hd in range(IDX_HEADS)], axis=0)

    def head_sum(x):
        acc = x[0:ts]
        for hd in range(1, IDX_HEADS):
            acc = acc + x[hd * ts:(hd + 1) * ts]
        return acc

    wait_pages(0)

    step_keys = math.gcd(past, ROW_SLAB)

    def past_scores(c, _):
        k0 = pl.multiple_of(c * step_keys, KEY_CHUNK)
        ktc = kibuf[:, pl.ds(k0, step_keys)].astype(BF16)
        acc = head_sum(jnp.maximum(_dot(qi_stack, ktc), 0.0) * w_stack)
        s_ref[:, pl.ds(k0, step_keys)] = _score_keys(acc + 0.0)
        return 0

    lax.fori_loop(0, past // step_keys, past_scores, 0)
    acc = head_sum(jnp.maximum(_dot(qi_stack, kiwn_ref[0, 0:IDX_DIM, :]), 0.0) * w_stack)
    kpos = lax.broadcasted_iota(I32, (ts, KEY_CHUNK), 1)
    s_ref[:, pl.ds(past, KEY_CHUNK)] = jnp.where(kpos <= qpos, _score_keys(acc + 0.0), INT_MIN)
    _row_select_to_bias(s_ref, n_chunks, (past + 1 + qpos).astype(F32), topk)

    wait_pages(1)
    wait_pages(2)

    def load_past(k0):
        return kbuf[:, pl.ds(k0, step_keys)].astype(BF16), vbuf[:, pl.ds(k0, step_keys)].astype(BF16)

    def load_new(k0):
        return kn_ref[0], vn_ref[0]

    segments = [(past // step_keys, step_keys, 0, load_past), (1, KEY_CHUNK, past, load_new)]
    _attend_all_heads(q_ref[0], segments, s_ref, o_ref)


def _dsa_sample(q, qi, kiw, knt, vnt, kiwnt, cache_kt, cache_vt, cache_kit, page_table):
    b, ts, _ = q.shape
    n_pages = page_table.shape[1]
    past = n_pages * PAGE_SIZE
    topk = min(TOPK_MAX, (past + ts) // 4)
    blk = lambda rows, width: pl.BlockSpec((1, rows, width), lambda bi, pt: (bi, 0, 0))
    hbm = pl.BlockSpec(memory_space=pl.ANY)
    grid_spec = pltpu.PrefetchScalarGridSpec(
        num_scalar_prefetch=1,
        grid=(b,),
        in_specs=[blk(ts, ATT_HEADS * LANES), blk(ts, IDX_HEADS * LANES), blk(ts, LANES),
                  blk(LANES, KEY_CHUNK), blk(LANES, KEY_CHUNK), blk(LANES, KEY_CHUNK), hbm, hbm, hbm],
        out_specs=blk(ts, ATT_HEADS * LANES),
        scratch_shapes=[pltpu.VMEM((LANES, past), F32), pltpu.VMEM((LANES, past), F32),
                        pltpu.VMEM((IDX_DIM, past), F32), pltpu.VMEM((ts, past + KEY_CHUNK), I32),
                        pltpu.SemaphoreType.DMA((3,))],
    )
    return pl.pallas_call(
        functools.partial(_dsa_sample_kernel, ts=ts, n_pages=n_pages, topk=topk),
        grid_spec=grid_spec,
        out_shape=jax.ShapeDtypeStruct((b, ts, ATT_HEADS * LANES), F32),
        compiler_params=_cparams(("arbitrary",)),
        name="dsa_sample",
    )(page_table, q, qi, kiw, knt, vnt, kiwnt, cache_kt, cache_vt, cache_kit)


def _conv_kernel(u_ref, halo_ref, buf_ref, bg_ref, w_ref, ya_ref):
    i = pl.program_id(1)
    u = u_ref[0]
    halo, buf = halo_ref[0], buf_ref[0]
    first = i == 0
    prev1 = jnp.where(first, buf[1:2], halo[7:8])
    prev2 = jnp.where(first, buf[0:1], halo[6:7])
    row = lax.broadcasted_iota(I32, u.shape, 0)
    um1 = jnp.where(row == 0, prev1, pltpu.roll(u, 1, 0))
    um2 = jnp.where(row == 0, prev2, jnp.where(row == 1, prev1, pltpu.roll(u, 2, 0)))
    w = w_ref[...]
    conv = w[0:1] * um2 + w[1:2] * um1 + w[2:3] * u
    ya_ref[0] = (bg_ref[0] * conv).astype(BF16)


def _conv(u, bg, buf, w, tt):
    b, t, c = u.shape
    halo_rows = 8
    tile = pl.BlockSpec((1, tt, c), lambda bi, i: (bi, i, 0))
    halo = pl.BlockSpec((1, halo_rows, c), lambda bi, i: (bi, jnp.maximum(i * (tt // halo_rows) - 1, 0), 0))
    return pl.pallas_call(
        _conv_kernel,
        grid=(b, t // tt),
        in_specs=[tile, halo, pl.BlockSpec((1, CONV_WIDTH - 1, c), lambda bi, i: (bi, 0, 0)), tile,
                  pl.BlockSpec((CONV_WIDTH, c), lambda bi, i: (0, 0))],
        out_specs=tile,
        out_shape=jax.ShapeDtypeStruct((b, t, c), BF16),
        compiler_params=_cparams(("parallel", "parallel")),
        name="short_conv",
    )(u, u, buf, bg, w)


def _mm_res_kernel(*refs, n_in):
    a_refs, w_ref, x_ref, o_ref = refs[:n_in], refs[n_in], refs[n_in + 1], refs[n_in + 2]
    a = jnp.concatenate([r[...] for r in a_refs], axis=1) if n_in > 1 else a_refs[0][...]
    o_ref[...] = x_ref[...] + _dot(a, w_ref[...])


def _mm_res(a_list, w, x, tm):
    m = x.shape[0]
    row = lambda i: (i, 0)
    return pl.pallas_call(
        functools.partial(_mm_res_kernel, n_in=len(a_list)),
        grid=(m // tm,),
        in_specs=[pl.BlockSpec((tm, a.shape[1]), row) for a in a_list]
        + [pl.BlockSpec(w.shape, lambda i: (0, 0)), pl.BlockSpec((tm, D_MODEL), row)],
        out_specs=pl.BlockSpec((tm, D_MODEL), row),
        out_shape=jax.ShapeDtypeStruct((m, D_MODEL), F32),
        compiler_params=_cparams(("parallel",)),
        name="matmul_residual",
    )(*a_list, w, x)


def _ret_in_kernel(x_ref, g_ref, w_ref, qk_ref, vg_ref):
    h = _rms(x_ref[...], g_ref[...]).astype(BF16)
    n_qk = qk_ref.shape[1]
    qk_ref[...] = _dot(h, w_ref[:, :n_qk])
    vg_ref[...] = _dot(h, w_ref[:, n_qk:]).astype(BF16)


def _ret_in_proj(x, g, w, tm):
    m, n = x.shape[0], w.shape[1]
    n_qk = 2 * RET_HEADS * RET_DK
    row = lambda i: (i, 0)
    const = lambda i: (0, 0)
    return pl.pallas_call(
        _ret_in_kernel,
        grid=(m // tm,),
        in_specs=[pl.BlockSpec((tm, D_MODEL), row), pl.BlockSpec((1, D_MODEL), const), pl.BlockSpec((D_MODEL, n), const)],
        out_specs=[pl.BlockSpec((tm, n_qk), row), pl.BlockSpec((tm, n - n_qk), row)],
        out_shape=[jax.ShapeDtypeStruct((m, n_qk), F32), jax.ShapeDtypeStruct((m, n - n_qk), BF16)],
        compiler_params=_cparams(("parallel",)),
        name="retention_in_proj",
    )(x, g, w)


def _ffn_kernel(x_ref, g_ref, wg_ref, wu_ref, wd_ref, o_ref, h_scr, acc):
    f = pl.program_id(1)

    @pl.when(f == 0)
    def _():
        x = x_ref[...]
        h_scr[...] = _rms(x, g_ref[...]).astype(BF16)
        acc[...] = x

    h = h_scr[...]
    a = jax.nn.silu(_dot(h, wg_ref[...])) * _dot(h, wu_ref[...])
    acc[...] += _dot(a.astype(BF16), wd_ref[...])

    @pl.when(f == pl.num_programs(1) - 1)
    def _():
        o_ref[...] = acc[...]


def _ffn(x, g, wg, wu, wd, tm, tf):
    m, ff = x.shape[0], wg.shape[1]
    row = lambda i, f: (i, 0)
    return pl.pallas_call(
        _ffn_kernel,
        grid=(m // tm, ff // tf),
        in_specs=[pl.BlockSpec((tm, D_MODEL), row), pl.BlockSpec((1, D_MODEL), lambda i, f: (0, 0)),
                  pl.BlockSpec((D_MODEL, tf), lambda i, f: (0, f)), pl.BlockSpec((D_MODEL, tf), lambda i, f: (0, f)),
                  pl.BlockSpec((tf, D_MODEL), lambda i, f: (f, 0))],
        out_specs=pl.BlockSpec((tm, D_MODEL), row),
        out_shape=jax.ShapeDtypeStruct((m, D_MODEL), F32),
        scratch_shapes=[pltpu.VMEM((tm, D_MODEL), BF16), pltpu.VMEM((tm, D_MODEL), F32)],
        compiler_params=_cparams(("parallel", "arbitrary")),
        name="dense_swiglu",
    )(x, g, wg, wu, wd)


def _top2_gates(logits):
    lane = lax.broadcasted_iota(I32, logits.shape, 1).astype(F32)
    neg = jnp.float32(-jnp.inf)
    l1 = jnp.where(lane < N_EXPERTS, logits, neg)
    m1 = jnp.max(l1, axis=1, keepdims=True)
    i1 = jnp.min(jnp.where(l1 == m1, lane, float(LANES)), axis=1, keepdims=True)
    l2 = jnp.where(lane == i1, neg, l1)
    m2 = jnp.max(l2, axis=1, keepdims=True)
    i2 = jnp.min(jnp.where(l2 == m2, lane, float(LANES)), axis=1, keepdims=True)
    e = jnp.exp(m2 - m1)
    w1 = 1.0 / (1.0 + e)
    w2 = e / (1.0 + e)
    first, second = lane == i1, lane == i2
    return jnp.where(first, w1, jnp.where(second, w2, 0.0)), jnp.where(first | second, 1.0, 0.0)


MOE_SUB = 144
MOE_TOKEN_TILE = 1024


def _moe_route_kernel(x_ref, g_ref, rhi_ref, rlo_ref, h_ref, gate_ref, posc_ref, posr_ref, cnt_ref):
    tm = x_ref.shape[0]
    hn = _rms(x_ref[...], g_ref[...])
    h_hi = hn.astype(BF16)
    h_lo = (hn - h_hi.astype(F32)).astype(BF16)
    logits = _dot(h_hi, rhi_ref[...]) + (_dot(h_lo, rhi_ref[...]) + _dot(h_hi, rlo_ref[...]))
    gate, routed = _top2_gates(logits)
    h_ref[...] = h_hi
    gate_ref[...] = gate
    earlier = (lax.broadcasted_iota(I32, (tm, tm), 0) > lax.broadcasted_iota(I32, (tm, tm), 1))
    slot = _dot(jnp.where(earlier, 1.0, 0.0).astype(BF16), routed.astype(BF16))
    posc = jnp.where(routed > 0.0, slot, -1.0)
    posc_ref[...] = posc
    posr_ref[0] = posc.T[0:N_EXPERTS]
    cnt_ref[0] = jnp.broadcast_to(jnp.sum(routed, axis=0, keepdims=True), (8, LANES))


def _moe_expert_kernel(cnt_ref, h_ref, gate_ref, posc_ref, posr_ref, wg_ref, wu_ref, wd_ref, o_ref, xg, yacc):
    i, e, f = pl.program_id(0), pl.program_id(1), pl.program_id(2)
    tm = h_ref.shape[0]
    n_sub = (cnt_ref[i * N_EXPERTS + e] + MOE_SUB - 1) // MOE_SUB

    @pl.when((e == 0) & (f == 0))
    def _():
        o_ref[...] = jnp.zeros(o_ref.shape, F32)

    @pl.when(f == 0)
    def _():
        posr = posr_ref[0]
        h = h_ref[...]

        def gather(s, _):
            base = pl.multiple_of(s * MOE_SUB, MOE_SUB)
            slot = (base + lax.broadcasted_iota(I32, (MOE_SUB, tm), 0)).astype(F32)
            onehot = jnp.where(posr == slot, 1.0, 0.0).astype(BF16)
            xg[pl.ds(base, MOE_SUB), :] = _dot(onehot, h).astype(BF16)
            yacc[pl.ds(base, MOE_SUB), :] = jnp.zeros((MOE_SUB, D_MODEL), F32)
            return 0

        lax.fori_loop(0, n_sub, gather, 0)

    def expert(s, _):
        rows = pl.ds(pl.multiple_of(s * MOE_SUB, MOE_SUB), MOE_SUB)
        xs = xg[rows, :]
        a = jax.nn.silu(_dot(xs, wg_ref[0])) * _dot(xs, wu_ref[0])
        yacc[rows, :] += _dot(a.astype(BF16), wd_ref[0])
        return 0

    lax.fori_loop(0, n_sub, expert, 0)

    @pl.when(f == pl.num_programs(2) - 1)
    def _():
        lane = lax.broadcasted_iota(I32, (tm, LANES), 1)
        mine = lane == e
        posc = jnp.sum(jnp.where(mine, posc_ref[...], 0.0), axis=1, keepdims=True)
        gate = jnp.sum(jnp.where(mine, gate_ref[...], 0.0), axis=1, keepdims=True)

        def scatter(s, _):
            base = pl.multiple_of(s * MOE_SUB, MOE_SUB)
            slot = (base + lax.broadcasted_iota(I32, (tm, MOE_SUB), 1)).astype(F32)
            onehot = jnp.where(posc == slot, 1.0, 0.0).astype(BF16)
            o_ref[...] += gate * _dot(onehot, yacc[pl.ds(base, MOE_SUB), :].astype(BF16))
            return 0

        lax.fori_loop(0, n_sub, scatter, 0)


def _moe(x, g, r_hi, r_lo, wg, wu, wd, tm, tf):
    m = x.shape[0]
    nt = m // tm
    row = lambda i: (i, 0)
    const = lambda i: (0, 0)
    h, gate, posc, posr, cnt = pl.pallas_call(
        _moe_route_kernel,
        grid=(nt,),
        in_specs=[pl.BlockSpec((tm, D_MODEL), row), pl.BlockSpec((1, D_MODEL), const),
                  pl.BlockSpec((D_MODEL, LANES), const), pl.BlockSpec((D_MODEL, LANES), const)],
        out_specs=[pl.BlockSpec((tm, D_MODEL), row), pl.BlockSpec((tm, LANES), row), pl.BlockSpec((tm, LANES), row),
                   pl.BlockSpec((1, N_EXPERTS, tm), lambda i: (i, 0, 0)), pl.BlockSpec((1, 8, LANES), lambda i: (i, 0, 0))],
        out_shape=[jax.ShapeDtypeStruct((m, D_MODEL), BF16), jax.ShapeDtypeStruct((m, LANES), F32),
                   jax.ShapeDtypeStruct((m, LANES), F32), jax.ShapeDtypeStruct((nt, N_EXPERTS, tm), F32),
                   jax.ShapeDtypeStruct((nt, 8, LANES), F32)],
        compiler_params=_cparams(("parallel",)),
        name="moe_route",
    )(x, g, r_hi, r_lo)
    counts = cnt[:, 0, :N_EXPERTS].astype(I32).reshape(nt * N_EXPERTS)
    posr = posr.reshape(nt * N_EXPERTS, 1, tm)
    row3 = lambda i, e, f, c: (i, 0)
    slot_rows = pl.cdiv(tm, MOE_SUB) * MOE_SUB
    grid_spec = pltpu.PrefetchScalarGridSpec(
        num_scalar_prefetch=1,
        grid=(nt, N_EXPERTS, EXPERT_FF // tf),
        in_specs=[pl.BlockSpec((tm, D_MODEL), row3),
                  pl.BlockSpec((tm, LANES), row3), pl.BlockSpec((tm, LANES), row3),
                  pl.BlockSpec((1, 1, tm), lambda i, e, f, c: (i * N_EXPERTS + e, 0, 0)),
                  pl.BlockSpec((1, D_MODEL, tf), lambda i, e, f, c: (e, 0, f)),
                  pl.BlockSpec((1, D_MODEL, tf), lambda i, e, f, c: (e, 0, f)),
                  pl.BlockSpec((1, tf, D_MODEL), lambda i, e, f, c: (e, f, 0))],
        out_specs=pl.BlockSpec((tm, D_MODEL), row3),
        scratch_shapes=[pltpu.VMEM((slot_rows, D_MODEL), BF16), pltpu.VMEM((slot_rows, D_MODEL), F32)],
    )
    return pl.pallas_call(
        _moe_expert_kernel,
        grid_spec=grid_spec,
        out_shape=jax.ShapeDtypeStruct((m, D_MODEL), F32),
        compiler_params=_cparams(("parallel", "arbitrary", "arbitrary")),
        name="moe_experts",
    )(counts, h, gate, posc, posr, wg, wu, wd)


def _ple_kernel(*refs, final_norm, n_addends):
    x_refs, (g_ref, p_ref, wp_ref, wgate_ref, gf_ref, o_ref) = refs[:n_addends], refs[n_addends:]
    x = x_refs[0][...]
    for r in x_refs[1:]:
        x = x + r[...]
    hp = _rms(x, g_ref[...]).astype(BF16)
    gate = jax.nn.sigmoid(_dot(hp, wgate_ref[...]))
    y = x + _dot(p_ref[...].astype(BF16), wp_ref[...]) * gate
    if final_norm:
        y = _rms(y, gf_ref[...])
    o_ref[...] = y


def _ple(xs, g, p, wp, wgate, g_final, final_norm, tm):
    m = xs[0].shape[0]
    row = lambda i: (i, 0)
    const = lambda i: (0, 0)
    return pl.pallas_call(
        functools.partial(_ple_kernel, final_norm=final_norm, n_addends=len(xs)),
        grid=(m // tm,),
        in_specs=[pl.BlockSpec((tm, D_MODEL), row)] * len(xs)
        + [pl.BlockSpec((1, D_MODEL), const), pl.BlockSpec((tm, PLE_DIM), row),
           pl.BlockSpec((PLE_DIM, D_MODEL), const), pl.BlockSpec((D_MODEL, D_MODEL), const),
           pl.BlockSpec((1, D_MODEL), const)],
        out_specs=pl.BlockSpec((tm, D_MODEL), row),
        out_shape=jax.ShapeDtypeStruct((m, D_MODEL), F32),
        compiler_params=_cparams(("parallel",)),
        name="per_layer_embedding",
    )(*xs, g, p, wp, wgate, g_final)


def _ret_kernel(q_ref, k_ref, v_ref, gate_ref, cos_ref, sin_ref, s0_ref, o_ref, sout_ref, state, *, chunk, chunk_rows):
    i = pl.program_id(1)

    @pl.when(i == 0)
    def _():
        state[...] = s0_ref[0]

    cos, sin = cos_ref[...], sin_ref[...]
    tt = q_ref.shape[1]
    r = chunk_rows
    ii = lax.broadcasted_iota(I32, (r, r), 0).astype(F32)
    jj = lax.broadcasted_iota(I32, (r, r), 1).astype(F32)
    rel = ii - jj
    icol = lax.broadcasted_iota(I32, (r, 1), 0).astype(F32)
    half = RET_DK // 2

    def rot(ref, hd):
        x1 = ref[0, :, hd * RET_DK:hd * RET_DK + half]
        x2 = ref[0, :, hd * RET_DK + half:(hd + 1) * RET_DK]
        return jnp.concatenate([x1 * cos - x2 * sin, x2 * cos + x1 * sin], axis=1)

    for hd in range(RET_HEADS):
        lg = math.log(1.0 - 2.0 ** (-5.0 - hd))
        d_in = jnp.where(rel >= 0, jnp.exp(lg * jnp.maximum(rel, 0.0)), 0.0)
        d_q = jnp.exp(lg * (icol + 1.0))
        d_k = jnp.exp(lg * (chunk - 1.0 - icol)) * (RET_DK ** -0.5)
        d_c = math.exp(lg * chunk)
        qr = rot(q_ref, hd)
        kr = rot(k_ref, hd)
        vsl = slice(hd * RET_DV, (hd + 1) * RET_DV)
        for c in range(tt // r):
            rows = slice(c * r, (c + 1) * r)
            qc = qr[rows].astype(BF16)
            kc = kr[rows]
            vc = v_ref[0, rows, vsl]
            s_prev = state[hd]
            att = _dot_nt(qc, (kc * (RET_DK ** -0.5)).astype(BF16)) * d_in
            o = _dot(att.astype(BF16), vc) + _dot(qc, s_prev.astype(BF16)) * d_q
            state[hd] = s_prev * d_c + _dot_tn((kc * d_k).astype(BF16), vc)
            mu = jnp.mean(o, axis=-1, keepdims=True)
            var = jnp.mean(jnp.square(o - mu), axis=-1, keepdims=True)
            on = (o - mu) * lax.rsqrt(var + EPS)
            o_ref[0, rows, vsl] = (jax.nn.silu(gate_ref[0, rows, vsl].astype(F32)) * on).astype(BF16)

    @pl.when(i == pl.num_programs(1) - 1)
    def _():
        sout_ref[0] = state[...]


def _retention(zqk, zvg, s0, cos, sin, tt, chunk, chunk_rows):
    b, t, _ = zqk.shape
    hk, hv = RET_HEADS * RET_DK, RET_HEADS * RET_DV
    half = RET_DK // 2
    state_spec = pl.BlockSpec((1, RET_HEADS, RET_DK, RET_DV), lambda bi, i: (bi, 0, 0, 0))
    tab = pl.BlockSpec((tt, half), lambda bi, i: (i, 0))
    return pl.pallas_call(
        functools.partial(_ret_kernel, chunk=chunk, chunk_rows=chunk_rows),
        grid=(b, t // tt),
        in_specs=[pl.BlockSpec((1, tt, hk), lambda bi, i: (bi, i, 0)), pl.BlockSpec((1, tt, hk), lambda bi, i: (bi, i, 1)),
                  pl.BlockSpec((1, tt, hv), lambda bi, i: (bi, i, 0)), pl.BlockSpec((1, tt, hv), lambda bi, i: (bi, i, 1)),
                  tab, tab, state_spec],
        out_specs=[pl.BlockSpec((1, tt, hv), lambda bi, i: (bi, i, 0)), state_spec],
        out_shape=[jax.ShapeDtypeStruct((b, t, hv), BF16), jax.ShapeDtypeStruct(s0.shape, F32)],
        scratch_shapes=[pltpu.VMEM((RET_HEADS, RET_DK, RET_DV), F32)],
        compiler_params=_cparams(("parallel", "arbitrary")),
        name="retention",
    )(zqk, zqk, zvg, zvg, cos, sin, s0)


def _pack_l0_w_in(w):
    offs = np.cumsum((0,) + AB_SPLITS)
    bg, cg, hv, q, k, v, qi, ki, wi = [w[:, offs[n]:offs[n + 1]] for n in range(len(AB_SPLITS))]
    group = ATT_HEADS // KV_HEADS
    q4 = q.reshape(D_MODEL, ATT_HEADS, HEAD_DIM)
    zq = jnp.zeros_like(q4)
    q_pad = jnp.concatenate([jnp.concatenate([q4[:, :group], zq[:, :group]], axis=-1),
                             jnp.concatenate([zq[:, group:], q4[:, group:]], axis=-1)], axis=1).reshape(D_MODEL, -1)
    qi4 = qi.reshape(D_MODEL, IDX_HEADS, IDX_DIM)
    qi_pad = jnp.concatenate([qi4, jnp.zeros_like(qi4)], axis=-1).reshape(D_MODEL, -1)
    kiw = jnp.concatenate([ki, wi, jnp.zeros((D_MODEL, LANES - IDX_DIM - IDX_HEADS), w.dtype)], axis=1)
    return jnp.concatenate([bg, cg, hv, q_pad, k, v, qi_pad, kiw], axis=1).astype(BF16)


def _pack_l0_w_out(w):
    group = ATT_HEADS // KV_HEADS
    wa, wb = w[:CONV_CH], w[CONV_CH:].reshape(ATT_HEADS, HEAD_DIM, D_MODEL)
    zb = jnp.zeros_like(wb)
    wb_pad = jnp.concatenate([jnp.concatenate([wb[:group], zb[:group]], axis=1),
                              jnp.concatenate([zb[group:], wb[group:]], axis=1)], axis=0).reshape(-1, D_MODEL)
    return jnp.concatenate([wa, wb_pad], axis=0).astype(BF16)


def _rope_tables(pos, reps):
    inv = ROPE_THETA ** (-jnp.arange(0, HEAD_DIM, 2, dtype=F32) / HEAD_DIM)
    ang = pos.astype(F32)[:, None] * inv[None, :]
    cos, sin = jnp.cos(ang), jnp.sin(ang)
    cos64 = jnp.concatenate([cos, cos], axis=1)
    sin64 = jnp.concatenate([-sin, sin], axis=1)
    one, zero = jnp.ones_like(cos64), jnp.zeros_like(cos64)
    tabs = (jnp.concatenate([cos64, cos64], 1), jnp.concatenate([sin64, sin64], 1),
            jnp.concatenate([cos64, one], 1), jnp.concatenate([sin64, zero], 1))
    return tuple(jnp.tile(tb, (reps, 1)) for tb in tabs)


def _ret_tables(pos):
    inv = ROPE_THETA ** (-jnp.linspace(0.0, 1.0, RET_DK // 2, dtype=F32))
    ang = pos.astype(F32)[:, None] * inv[None, :]
    return jnp.cos(ang), jnp.sin(ang)


def _pack_params(prm):
    r = prm['moe_router'][0]
    r_pad = jnp.concatenate([r, jnp.zeros((D_MODEL, LANES - N_EXPERTS), F32)], axis=1)
    r_hi = r_pad.astype(BF16)
    bf = lambda a: a.astype(BF16)
    return dict(
        l0_w_in=_pack_l0_w_in(prm['ab_w_in'][0]), l0_w_out=_pack_l0_w_out(prm['ab_w_out'][0]),
        l0_w_vt=bf(prm['ab_w_in'][0][:, sum(AB_SPLITS[:5]):sum(AB_SPLITS[:6])].T),
        conv_w=prm['ab_conv_w'][0],
        ffn=(bf(prm['ffn_w_gate'][0]), bf(prm['ffn_w_up'][0]), bf(prm['ffn_w_down'][0])),
        ret_w_in=bf(prm['ret_w_in'][0]), ret_w_out=bf(prm['ret_w_out'][0]),
        r_hi=r_hi, r_lo=(r_pad - r_hi.astype(F32)).astype(BF16),
        moe=(bf(prm['moe_w_gate'][0]), bf(prm['moe_w_up'][0]), bf(prm['moe_w_down'][0])),
        ple_w=bf(prm['ple_w']), ple_gate_w=bf(prm['ple_gate_w']),
        norm_mix=prm['norm_mix'][:, None, :], norm_ffn=prm['norm_ffn'][:, None, :],
        norm_ple=prm['norm_ple'][:, None, :], norm_final=prm['norm_final'][None, :],
    )


def _trunk(x, p, pos, conv_buf, ret_state, pk, paged):
    b, t, _ = x.shape
    m = b * t
    tm = min(m, TOKEN_TILE)
    moe_tm = min(m, MOE_TOKEN_TILE)
    assert m % tm == 0 and m % moe_tm == 0 and m % min(m, 256) == 0, "token count must divide into whole tiles"
    assert paged is not None or (t % QUERY_BLOCK == 0 and t % tm == 0), "prompt length must divide into whole tiles"
    assert paged is None or (tm % t == 0 and t <= 8), "sample sequences are one sublane tile long at most"
    x2 = x.reshape(m, D_MODEL)

    reps = max(1, tm // t)
    tabs = _rope_tables(pos, reps)
    bg, u, q, k, v, qi, kiw, kbf, vbf, kiwbf, vt = _l0_in_proj(x2, pk['norm_mix'][0], pk['l0_w_in'], pk['l0_w_vt'], tabs,
                                                              tm, tabs[0].shape[0] // tm, head_major=paged is None)
    seq = lambda a: a.reshape(b, t, a.shape[-1])
    if paged is None:
        yb = _dsa_prompt(q, qi, seq(kiw), seq(kbf), vt, seq(kiwbf))
    else:
        cache_k, cache_v, cache_ki, page_table = paged
        padt = lambda a: jnp.swapaxes(jnp.pad(seq(a), ((0, 0), (0, KEY_CHUNK - t), (0, 0))), 1, 2)
        n_pool = cache_k.shape[0]
        pages_t = lambda c: jnp.transpose(c, (0, 2, 3, 1)).reshape(n_pool, LANES, PAGE_SIZE)
        yb = _dsa_sample(seq(q).astype(F32), seq(qi).astype(F32), seq(kiw), padt(kbf), padt(vbf), padt(kiwbf),
                         pages_t(cache_k), pages_t(cache_v), jnp.swapaxes(cache_ki, 1, 2), page_table).astype(BF16)
    u3 = seq(u)
    ya = _conv(u3, seq(bg), conv_buf, pk['conv_w'], tt=min(t, TOKEN_TILE))
    x2 = _mm_res([ya.reshape(m, CONV_CH), yb.reshape(m, ATT_HEADS * LANES)], pk['l0_w_out'], x2, tm)
    x2 = _ffn(x2, pk['norm_ffn'][0], *pk['ffn'], tm=tm, tf=D_FF // 2)
    x2 = _ple([x2], pk['norm_ple'][0], p[0].reshape(m, PLE_DIM), pk['ple_w'][0], pk['ple_gate_w'][0],
              pk['norm_final'], False, tm)
    new_k = k.reshape(1, b, t, KV_HEADS, HEAD_DIM)
    new_v = v.reshape(1, b, t, KV_HEADS, HEAD_DIM)
    new_ki = seq(kiw)[None, :, :, :IDX_DIM]
    new_conv = jnp.concatenate([conv_buf, u3], axis=1)[None, :, -(CONV_WIDTH - 1):]

    zqk, zvg = _ret_in_proj(x2, pk['norm_mix'][1], pk['ret_w_in'], min(m, 256))
    zqk, zvg = zqk.reshape(b, t, -1), zvg.reshape(b, t, -1)
    cos_r, sin_r = _ret_tables(pos)
    if t % RET_CHUNK == 0:
        og, s_new = _retention(zqk, zvg, ret_state, cos_r, sin_r, tt=2 * RET_CHUNK, chunk=RET_CHUNK,
                               chunk_rows=RET_CHUNK)
    else:
        rows = 16
        padt = lambda a: jnp.pad(a, ((0, 0),) * (a.ndim - 2) + ((0, rows - t), (0, 0)))
        og, s_new = _retention(padt(zqk), padt(zvg), ret_state, padt(cos_r), padt(sin_r), tt=rows, chunk=t,
                               chunk_rows=rows)
        og = og[:, :t]
    x2 = _mm_res([og.reshape(m, RET_HEADS * RET_DV)], pk['ret_w_out'], x2, tm)
    moe = _moe(x2, pk['norm_ffn'][1], pk['r_hi'], pk['r_lo'], *pk['moe'], tm=moe_tm, tf=EXPERT_FF // 2)
    x2 = _ple([x2, moe], pk['norm_ple'][1], p[1].reshape(m, PLE_DIM), pk['ple_w'][1], pk['ple_gate_w'][1],
              pk['norm_final'], True, tm)
    return x2.reshape(b, t, D_MODEL), new_k, new_v, new_ki, new_conv, s_new[None]


def kernel(x_prompt, x_sample, cache_k, cache_v, cache_kidx, state_conv, state_ret, page_table, p_prompt, p_sample,
           norm_mix, norm_ffn, norm_ple, norm_final, ab_w_in, ab_conv_w, ab_w_out, ffn_w_gate, ffn_w_up, ffn_w_down,
           ret_w_in, ret_w_out, moe_router, moe_w_gate, moe_w_up, moe_w_down, ple_w, ple_gate_w):
    prm = dict(norm_mix=norm_mix, norm_ffn=norm_ffn, norm_ple=norm_ple, norm_final=norm_final, ab_w_in=ab_w_in,
               ab_conv_w=ab_conv_w, ab_w_out=ab_w_out, ffn_w_gate=ffn_w_gate, ffn_w_up=ffn_w_up, ffn_w_down=ffn_w_down,
               ret_w_in=ret_w_in, ret_w_out=ret_w_out, moe_router=moe_router, moe_w_gate=moe_w_gate,
               moe_w_up=moe_w_up, moe_w_down=moe_w_down, ple_w=ple_w, ple_gate_w=ple_gate_w)
    pk = _pack_params(prm)
    b, t = x_prompt.shape[0], x_prompt.shape[1]
    db, ts = x_sample.shape[0], x_sample.shape[1]
    past_len = page_table.shape[1] * PAGE_SIZE
    dt = x_prompt.dtype

    conv0 = jnp.zeros((b, CONV_WIDTH - 1, CONV_CH), dt)
    ret0 = jnp.zeros((b, RET_HEADS, RET_DK, RET_DV), dt)
    y_p, k_p, v_p, ki_p, cb_p, rs_p = _trunk(x_prompt, p_prompt, jnp.arange(t, dtype=I32), conv0, ret0, pk, None)

    pos_s = past_len + jnp.arange(ts, dtype=I32)
    paged = (cache_k[0], cache_v[0], cache_kidx[0], page_table)
    y_s, k_s, v_s, ki_s, cb_s, rs_s = _trunk(x_sample, p_sample, pos_s, state_conv[0], state_ret[0], pk, paged)
    return (y_p, y_s, k_p, v_p, ki_p, cb_p, rs_p, k_s, v_s, ki_s, cb_s, rs_s)
```

```python
import functools
import math

import jax
import jax.numpy as jnp
import numpy as np
from jax import lax
from jax.experimental import pallas as pl
from jax.experimental.pallas import tpu as pltpu

F32 = jnp.float32
BF16 = jnp.bfloat16
I32 = jnp.int32

D_MODEL = 1024
PAGE_SIZE = 128
CONV_CH = D_MODEL // 2
CONV_WIDTH = 3
ATT_HEADS = 8
KV_HEADS = 2
HEAD_DIM = 64
IDX_HEADS = 4
IDX_DIM = 64
TOPK_MAX = 256
ROPE_THETA = 10000.0
RET_HEADS = 4
RET_DK = D_MODEL // RET_HEADS
RET_DV = 2 * RET_DK
RET_CHUNK = 128
D_FF = 2816
N_EXPERTS = 8
EXPERT_FF = 3584
PLE_DIM = 256
EPS = 1e-6
AB_SPLITS = (CONV_CH, CONV_CH, CONV_CH, ATT_HEADS * HEAD_DIM, KV_HEADS * HEAD_DIM, KV_HEADS * HEAD_DIM,
             IDX_HEADS * IDX_DIM, IDX_DIM, IDX_HEADS)

LANES = 128
VMEM_LIMIT = 48 * 1024 * 1024
INT_MIN = -2 ** 31
NEG_BIG = -1e30
LOG2_E = math.log2(math.e)

C_BG, C_CG, C_HV, C_Q, C_K, C_V, C_QI, C_KIW, C_END = 0, 512, 1024, 1536, 2560, 2688, 2816, 3328, 3456
KEY_CHUNK = 512
VT_ROWS = LANES + 16
TOKEN_TILE = 512
QUERY_BLOCK = 128


def _cparams(sem):
    return pltpu.CompilerParams(dimension_semantics=sem, vmem_limit_bytes=VMEM_LIMIT)


def _rms(x, g):
    return x * lax.rsqrt(jnp.mean(x * x, axis=-1, keepdims=True) + EPS) * g


def _dot(a, b):
    return jnp.dot(a, b, preferred_element_type=F32)


def _dot_nt(a, b):
    return lax.dot_general(a, b, (((1,), (1,)), ((), ())), preferred_element_type=F32)


def _dot_tn(a, b):
    return lax.dot_general(a, b, (((0,), (0,)), ((), ())), preferred_element_type=F32)


def _swap_halves64(x):
    lane = lax.broadcasted_iota(I32, x.shape, 1)
    from_above = pltpu.roll(x, LANES - 32, 1)
    from_below = pltpu.roll(x, 32, 1)
    return jnp.where((lane & 63) < 32, from_above, from_below)


def _l0_in_kernel(x_ref, g_ref, w_ref, wvt_ref, cos_ref, sin_ref, cosb_ref, sinb_ref,
                  bg_ref, u_ref, q_ref, k_ref, v_ref, qi_ref, kiw_ref, kbf_ref, vbf_ref, kiwbf_ref, vt_ref, *, head_major):
    h = _rms(x_ref[...], g_ref[...]).astype(BF16)
    cos, sin = cos_ref[...], sin_ref[...]

    def store_head(ref, hd, val):
        if head_major:
            for r in range(ref.shape[0]):
                ref[r, hd] = val[r * QUERY_BLOCK:(r + 1) * QUERY_BLOCK]
        else:
            ref[:, hd * LANES:(hd + 1) * LANES] = val

    def rot(z, c, s):
        return z * c + _swap_halves64(z) * s

    bg_ref[...] = _dot(h, w_ref[:, C_BG:C_CG])
    u_ref[...] = _dot(h, w_ref[:, C_CG:C_HV]) * _dot(h, w_ref[:, C_HV:C_Q])
    zq = _dot(h, w_ref[:, C_Q:C_K])
    for hd in range(ATT_HEADS):
        sl = slice(hd * LANES, (hd + 1) * LANES)
        store_head(q_ref, hd, (rot(zq[:, sl], cos, sin) * (HEAD_DIM ** -0.5 * LOG2_E)).astype(BF16))
    k = rot(_dot(h, w_ref[:, C_K:C_V]), cos, sin)
    k_ref[...] = k
    kbf_ref[...] = k.astype(BF16)
    v = _dot(h, w_ref[:, C_V:C_QI])
    v_ref[...] = v
    vbf_ref[...] = v.astype(BF16)
    vt_ref[0:LANES, :] = _dot_nt(wvt_ref[...], h).astype(BF16)
    vt_ref[LANES:VT_ROWS, :] = jnp.ones((VT_ROWS - LANES, vt_ref.shape[1]), BF16)
    zqi = _dot(h, w_ref[:, C_QI:C_KIW])
    for hd in range(IDX_HEADS):
        sl = slice(hd * LANES, (hd + 1) * LANES)
        store_head(qi_ref, hd, rot(zqi[:, sl], cos, sin).astype(BF16))
    kiw = rot(_dot(h, w_ref[:, C_KIW:C_END]), cosb_ref[...], sinb_ref[...])
    kiw_ref[...] = kiw
    kiwbf_ref[...] = kiw.astype(BF16)


def _l0_in_proj(x2, g, w, wvt, tabs, tm, n_tab_blocks, head_major):
    m = x2.shape[0]
    row = lambda i: (i, 0)
    const = lambda i: (0, 0)
    tab = lambda i: (i % n_tab_blocks, 0)
    widths = (512, 512, 1024, 128, 128, 512, 128, 128, 128, 128)
    dtypes = (F32, F32, BF16, F32, F32, BF16, F32, BF16, BF16, BF16)
    out_specs = [pl.BlockSpec((tm, wd), row) for wd in widths]
    out_shape = [jax.ShapeDtypeStruct((m, wd), dt) for wd, dt in zip(widths, dtypes)]
    if head_major:
        for n in (2, 5):
            heads = widths[n] // LANES
            out_specs[n] = pl.BlockSpec((tm // QUERY_BLOCK, heads, QUERY_BLOCK, LANES), lambda i: (i, 0, 0, 0))
            out_shape[n] = jax.ShapeDtypeStruct((m // QUERY_BLOCK, heads, QUERY_BLOCK, LANES), BF16)
    return pl.pallas_call(
        functools.partial(_l0_in_kernel, head_major=head_major),
        grid=(m // tm,),
        in_specs=[pl.BlockSpec((tm, D_MODEL), row), pl.BlockSpec((1, D_MODEL), const),
                  pl.BlockSpec((D_MODEL, C_END), const), pl.BlockSpec((LANES, D_MODEL), const)]
        + [pl.BlockSpec((tm, LANES), tab)] * 4,
        out_specs=out_specs + [pl.BlockSpec((VT_ROWS, tm), lambda i: (0, i))],
        out_shape=out_shape + [jax.ShapeDtypeStruct((VT_ROWS, m), BF16)],
        compiler_params=_cparams(("parallel",)),
        name="l0_in_proj",
    )(x2, g, w, wvt, *tabs)


def _score_keys(score):
    bits = pltpu.bitcast(score, I32)
    return bits ^ ((bits >> 31) & jnp.int32(0x7FFFFFFF))


def _key_scores(key):
    return pltpu.bitcast(key ^ ((key >> 31) & jnp.int32(0x7FFFFFFF)), F32)


SEARCH_PLAIN_STEPS = 11
SEARCH_SNAP_INTERP_STEPS = 8
SEARCH_MAX_STEPS = SEARCH_PLAIN_STEPS + SEARCH_SNAP_INTERP_STEPS + 34


NO_TIE_LIMIT = 1e9


def _topk_threshold(probe, vmin, vmax, n_real, topk):
    k = float(topk)
    lo0, hi0 = _score_keys(vmin), _score_keys(vmax) + 1
    take_all = n_real <= k
    zero, one = jnp.zeros_like(vmin), jnp.ones_like(vmin)
    active0 = jnp.where(jnp.logical_not(take_all) & (lo0 + 1 < hi0), 1.0, 0.0)

    def step(snap, st):
        it, lo, hi, c_lo, c_hi, w_lo, w_hi, last, active = st
        act = active > 0.0
        f_lo = (c_lo - (k - 0.5)) * w_lo
        f_hi = ((k - 0.5) - c_hi) * w_hi
        v_lo, v_hi = _key_scores(lo), _key_scores(hi)
        g = _score_keys(v_lo + (v_hi - v_lo) * (f_lo / (f_lo + f_hi)))
        if snap:
            mid = (lo >> 1) + (hi >> 1) + (lo & hi & 1)
            g = jnp.where(it < SEARCH_PLAIN_STEPS + SEARCH_SNAP_INTERP_STEPS, g, mid)
        g = jnp.minimum(jnp.maximum(g, lo + 1), hi - 1)
        c, key_up, key_dn = probe(g, snap)
        hit = act & (c == k)
        up = act & (c > k)
        dn = act & (c < k)
        lo = jnp.where(hit, g, jnp.where(up, key_up if snap else g, lo))
        c_lo = jnp.where(hit | up, c, c_lo)
        hi = jnp.where(dn, key_dn + 1 if snap else g, hi)
        c_hi = jnp.where(dn, c, c_hi)
        w_hi = jnp.where(up, jnp.where(last > 0.0, w_hi * 0.5, one), jnp.where(dn, one, w_hi))
        w_lo = jnp.where(dn, jnp.where(last < 0.0, w_lo * 0.5, one), jnp.where(up, one, w_lo))
        last = jnp.where(up, one, jnp.where(dn, -one, last))
        active = jnp.where(act & jnp.logical_not(hit) & (lo + 1 < hi), 1.0, 0.0)
        return it + 1, lo, hi, c_lo, c_hi, w_lo, w_hi, last, active

    def cond(limit, st):
        return (st[0] < limit) & (jnp.max(st[-1]) > 0.0)

    st = (jnp.int32(0), lo0, hi0, n_real, zero, one, one, zero, active0)
    st = lax.fori_loop(0, SEARCH_PLAIN_STEPS, lambda _, s: step(False, s), st)
    st = lax.while_loop(functools.partial(cond, SEARCH_MAX_STEPS), functools.partial(step, True), st)
    _, lo, _, c_lo, c_hi, _, _, _, _ = st
    thr = jnp.where(take_all, INT_MIN, lo)
    excess = jnp.logical_not(take_all) & (c_lo > k)
    ties = jnp.where(excess, k - c_hi, jnp.where(take_all, 0.0, NO_TIE_LIMIT))
    return thr, ties, jnp.max(jnp.where(excess, 1.0, 0.0)) > 0.0


def _tie_prefix_matrix(lower):
    r = lax.broadcasted_iota(I32, (KEY_CHUNK, KEY_CHUNK), 0)
    c = lax.broadcasted_iota(I32, (KEY_CHUNK, KEY_CHUNK), 1)
    return jnp.where((r >= c) if lower else (r <= c), 1.0, 0.0).astype(BF16)


def _bias_bits(sel):
    return pltpu.bitcast(jnp.where(sel, 0.0, NEG_BIG).astype(F32), I32)


ROW_SLAB = 2048


def _lane_fold(m, op=jnp.add):
    parts = [m[:, c * LANES:(c + 1) * LANES] for c in range(m.shape[1] // LANES)]
    while len(parts) > 1:
        parts = [op(parts[n], parts[n + 1]) for n in range(0, len(parts) - 1, 2)] + (parts[-1:] if len(parts) % 2 else [])
    return parts[0]


def _row_slabs(n_chunks):
    width = n_chunks * KEY_CHUNK
    return [(k0, min(ROW_SLAB, width - k0)) for k0 in range(0, width, ROW_SLAB)]


def _int_reduce(x, take_min, axis):
    red = jnp.min if take_min else jnp.max
    hi = (x >> 16).astype(F32)
    lo = (x & 0xFFFF).astype(F32)
    m_hi = red(hi, axis=axis, keepdims=True)
    m_lo = red(jnp.where(hi == m_hi, lo, 65536.0 if take_min else -1.0), axis=axis, keepdims=True)
    return (m_hi.astype(I32) << 16) | m_lo.astype(I32)


def _row_probe(s_ref, n_chunks, g, snap):
    cnt = up = dn = None
    for k0, size in _row_slabs(n_chunks):
        blk = s_ref[:, k0:k0 + size]
        ge = blk >= g
        part = _lane_fold(jnp.where(ge, 1.0, 0.0))
        cnt = part if cnt is None else cnt + part
        if snap:
            above = _lane_fold(jnp.where(ge, blk, 2 ** 31 - 1), jnp.minimum)
            below = _lane_fold(jnp.where(ge, INT_MIN, blk), jnp.maximum)
            up = above if up is None else jnp.minimum(up, above)
            dn = below if dn is None else jnp.maximum(dn, below)
    c = jnp.sum(cnt, axis=1, keepdims=True)
    if not snap:
        return c, None, None
    return c, _int_reduce(up, True, 1), _int_reduce(dn, False, 1)


def _row_select_to_bias(s_ref, n_chunks, n_real, topk):
    inf = jnp.float32(jnp.inf)
    mn = mx = None
    for k0, size in _row_slabs(n_chunks):
        blk = s_ref[:, k0:k0 + size]
        v = _key_scores(blk)
        real = blk != INT_MIN
        lo_part = _lane_fold(jnp.where(real, v, inf), jnp.minimum)
        hi_part = _lane_fold(jnp.where(real, v, -inf), jnp.maximum)
        mn = lo_part if mn is None else jnp.minimum(mn, lo_part)
        mx = hi_part if mx is None else jnp.maximum(mx, hi_part)
    vmin, vmax = jnp.min(mn, axis=1, keepdims=True), jnp.max(mx, axis=1, keepdims=True)
    thr, ties, any_excess = _topk_threshold(functools.partial(_row_probe, s_ref, n_chunks), vmin, vmax, n_real, topk)

    def plain(_):
        thr_ge = jnp.maximum(thr, INT_MIN + 1)
        for k0, size in _row_slabs(n_chunks):
            s_ref[:, k0:k0 + size] = _bias_bits(s_ref[:, k0:k0 + size] >= thr_ge)
        return 0

    def with_ties(_):
        upper = _tie_prefix_matrix(lower=False)
        seen = jnp.zeros_like(ties)
        for c in range(n_chunks):
            blk = s_ref[:, c * KEY_CHUNK:(c + 1) * KEY_CHUNK]
            tie = jnp.where(blk == thr, 1.0, 0.0)
            rows = tie.shape[0]
            tie16 = jnp.concatenate([tie, tie], axis=0).astype(BF16) if rows % 16 else tie.astype(BF16)
            rank = _dot(tie16, upper)[0:rows] + seen
            s_ref[:, c * KEY_CHUNK:(c + 1) * KEY_CHUNK] = _bias_bits((blk > thr) | ((tie > 0.0) & (rank <= ties)))
            seen = seen + jnp.sum(tie, axis=1, keepdims=True)
        return 0

    lax.cond(any_excess, with_ties, plain, 0)


def _attend_group(qg, segments, bias_ref, rep):
    n = qg.shape[0]

    def step(chunk, base, load_kv, c, carry):
        m, l, acc = carry
        k0 = pl.multiple_of(base + c * chunk, KEY_CHUNK)
        ktc, vtc = load_kv(k0)
        bias = pltpu.bitcast(bias_ref[:, pl.ds(k0, chunk)], F32)
        s = _dot(qg, ktc) + jnp.concatenate([bias] * rep, axis=0)
        m_new = jnp.maximum(m, jnp.max(s, axis=1, keepdims=True))
        alpha = jnp.exp2(m - m_new)
        p = jnp.exp2(s - m_new)
        l = alpha * l + jnp.sum(p, axis=1, keepdims=True)
        acc = alpha * acc + _dot_nt(p.astype(BF16), vtc)
        return m_new, l, acc

    carry = (jnp.full((n, 1), NEG_BIG, F32), jnp.zeros((n, 1), F32), jnp.zeros((n, LANES), F32))
    for n_steps, chunk, base, load_kv in segments:
        carry = lax.fori_loop(0, n_steps, functools.partial(step, chunk, base, load_kv), carry)
    _, l, acc = carry
    return acc / l


def _stack_heads(x, first, count, width=LANES):
    x = x.astype(F32)
    return jnp.concatenate([x[:, (first + hd) * LANES:(first + hd) * LANES + width] for hd in range(count)],
                           axis=0).astype(BF16)


def _attend_all_heads(q, segments, bias_ref, o_ref):
    rows = q.shape[0]
    group = ATT_HEADS // KV_HEADS
    out = _attend_group(_stack_heads(q, 0, ATT_HEADS), segments, bias_ref, ATT_HEADS)
    for hd in range(ATT_HEADS):
        piece = out[hd * rows:(hd + 1) * rows]
        piece = jnp.where(_group_lane_mask(piece.shape, hd // group), piece, 0.0)
        o_ref[0, :, hd * LANES:(hd + 1) * LANES] = piece.astype(o_ref.dtype)


def _indexer_weights(kiw_q):
    return kiw_q[:, IDX_DIM:IDX_DIM + IDX_HEADS] * (IDX_HEADS ** -0.5 * IDX_DIM ** -0.5)


def _group_lane_mask(shape, g):
    lane = lax.broadcasted_iota(I32, shape, 1)
    return (lane >= g * HEAD_DIM) & (lane < (g + 1) * HEAD_DIM)


ATT_CHUNK = 256


def _sublane_fold(m, op=jnp.add):
    parts = [m[r * 8:(r + 1) * 8] for r in range(m.shape[0] // 8)]
    while len(parts) > 1:
        parts = [op(parts[n], parts[n + 1]) for n in range(0, len(parts) - 1, 2)] + (parts[-1:] if len(parts) % 2 else [])
    return parts[0]


def _col_reduce(m, op, reduce_fn):
    return reduce_fn(_sublane_fold(m, op), axis=0, keepdims=True)


def _col_probe(s_ref, n_chunks, g, snap):
    cols = s_ref.shape[1]

    def body(c, carry):
        k0 = pl.multiple_of(c * KEY_CHUNK, KEY_CHUNK)
        blk = s_ref[pl.ds(k0, KEY_CHUNK), :]
        ge = blk >= g
        out = [carry[0] + _sublane_fold(jnp.where(ge, 1.0, 0.0))]
        if snap:
            out.append(jnp.minimum(carry[1], _sublane_fold(jnp.where(ge, blk, 2 ** 31 - 1), jnp.minimum)))
            out.append(jnp.maximum(carry[2], _sublane_fold(jnp.where(ge, INT_MIN, blk), jnp.maximum)))
        return tuple(out)

    init = [jnp.zeros((8, cols), F32)]
    if snap:
        init += [jnp.full((8, cols), 2 ** 31 - 1, I32), jnp.full((8, cols), INT_MIN, I32)]
    res = lax.fori_loop(0, n_chunks, body, tuple(init))
    c = jnp.sum(res[0], axis=0, keepdims=True)
    if not snap:
        return c, None, None
    return c, _int_reduce(res[1], True, 0), _int_reduce(res[2], False, 0)


def _col_select_to_bias(s_ref, n_chunks, n_real, topk, vmin, vmax):
    thr, ties, any_excess = _topk_threshold(functools.partial(_col_probe, s_ref, n_chunks), vmin, vmax, n_real, topk)

    def plain(_):
        thr_ge = jnp.maximum(thr, INT_MIN + 1)

        def body(c, _):
            k0 = pl.multiple_of(c * KEY_CHUNK, KEY_CHUNK)
            s_ref[pl.ds(k0, KEY_CHUNK), :] = _bias_bits(s_ref[pl.ds(k0, KEY_CHUNK), :] >= thr_ge)
            return 0

        return lax.fori_loop(0, n_chunks, body, 0)

    def with_ties(_):
        lower = _tie_prefix_matrix(lower=True)

        def body(c, seen):
            k0 = pl.multiple_of(c * KEY_CHUNK, KEY_CHUNK)
            blk = s_ref[pl.ds(k0, KEY_CHUNK), :]
            tie = jnp.where(blk == thr, 1.0, 0.0)
            rank = _dot(lower, tie.astype(BF16)) + seen
            s_ref[pl.ds(k0, KEY_CHUNK), :] = _bias_bits((blk > thr) | ((tie > 0.0) & (rank <= ties)))
            return seen + jnp.sum(_sublane_fold(tie), axis=0, keepdims=True)

        lax.fori_loop(0, n_chunks, body, jnp.zeros_like(ties))
        return 0

    lax.cond(any_excess, with_ties, plain, 0)


def _dsa_prompt_kernel(q_ref, qi_ref, kiwq_ref, k_ref, vt_ref, kiw_ref, o_ref, s_ref, m_scr, acc_scr,
                       sa_scr, sb_scr, ia_scr, ib_scr, *, qb, topk):
    j = pl.program_id(1)
    n_keys = j * qb + qb
    n_chunks = (n_keys + KEY_CHUNK - 1) // KEY_CHUNK
    qpos = j * qb + lax.broadcasted_iota(I32, (1, qb), 1)
    wt = kiwq_ref[0].T[IDX_DIM:IDX_DIM + IDX_HEADS] * (IDX_HEADS ** -0.5 * IDX_DIM ** -0.5)

    inf = jnp.float32(jnp.inf)

    last_chunk = s_ref.shape[0] // KEY_CHUNK - 1

    def index_dots(c, buf):
        k0 = pl.multiple_of(jnp.minimum(c, last_chunk) * KEY_CHUNK, KEY_CHUNK)
        kc = kiw_ref[0, pl.ds(k0, KEY_CHUNK), :]
        for pr in range(IDX_HEADS // 2):
            buf[pr] = _dot_nt(kc, qi_ref[0, 2 * pr:2 * pr + 2].reshape(2 * qb, LANES))

    def score_chunk(c, buf, carry):
        mn, mx = carry
        k0 = pl.multiple_of(jnp.minimum(c, last_chunk) * KEY_CHUNK, KEY_CHUNK)
        acc = jnp.zeros((KEY_CHUNK, qb), F32)
        for pr in range(IDX_HEADS // 2):
            acc = acc + jnp.maximum(buf[pr, :, 0:qb], 0.0) * wt[2 * pr:2 * pr + 1]
            acc = acc + jnp.maximum(buf[pr, :, qb:2 * qb], 0.0) * wt[2 * pr + 1:2 * pr + 2]
        causal = k0 + lax.broadcasted_iota(I32, (KEY_CHUNK, qb), 0) <= qpos
        s_ref[pl.ds(k0, KEY_CHUNK), :] = jnp.where(causal, _score_keys(acc), INT_MIN)
        mn = jnp.minimum(mn, _sublane_fold(jnp.where(causal, acc, inf), jnp.minimum))
        mx = jnp.maximum(mx, _sublane_fold(jnp.where(causal, acc, -inf), jnp.maximum))
        return mn, mx

    index_dots(0, ia_scr)

    def scores(t, carry):
        index_dots(2 * t + 1, ib_scr)
        carry = score_chunk(2 * t, ia_scr, carry)
        index_dots(2 * t + 2, ia_scr)
        return score_chunk(2 * t + 1, ib_scr, carry)

    mn, mx = lax.fori_loop(0, (n_chunks + 1) // 2, scores,
                           (jnp.full((8, qb), inf, F32), jnp.full((8, qb), -inf, F32)))
    _col_select_to_bias(s_ref, n_chunks, (qpos + 1).astype(F32), topk,
                        jnp.min(mn, axis=0, keepdims=True), jnp.max(mx, axis=0, keepdims=True))

    m_scr[...] = jnp.full(m_scr.shape, NEG_BIG, F32)
    acc_scr[...] = jnp.zeros(acc_scr.shape, F32)

    n_att = n_chunks * (KEY_CHUNK // ATT_CHUNK)

    def logits(step, buf):
        k0 = pl.multiple_of(jnp.minimum(step, n_att - 1) * ATT_CHUNK, ATT_CHUNK)
        kc = k_ref[0, pl.ds(k0, ATT_CHUNK), :]
        bias = pltpu.bitcast(s_ref[pl.ds(k0, ATT_CHUNK), :], F32)
        for pr in range(ATT_HEADS // 2):
            s2 = _dot_nt(kc, q_ref[0, 2 * pr:2 * pr + 2].reshape(2 * qb, LANES))
            buf[2 * pr] = s2[:, :qb] + bias
            buf[2 * pr + 1] = s2[:, qb:] + bias

    def accumulate(step, buf):
        k0 = pl.multiple_of(step * ATT_CHUNK, ATT_CHUNK)
        vtc = vt_ref[:, pl.ds(k0, ATT_CHUNK)]
        for pr in range(ATT_HEADS // 2):
            ps, alphas = [], []
            for hd in (2 * pr, 2 * pr + 1):
                s = buf[hd]
                m_prev = m_scr[hd:hd + 1, :]
                m_new = jnp.maximum(m_prev, _col_reduce(s, jnp.maximum, jnp.max))
                alphas.append(jnp.exp2(m_prev - m_new))
                ps.append(jnp.exp2(s - m_new).astype(BF16))
                m_scr[hd:hd + 1, :] = m_new
            acc_scr[pr] = jnp.concatenate(alphas, axis=1) * acc_scr[pr] + _dot(vtc, jnp.concatenate(ps, axis=1))

    logits(0, sa_scr)

    def attend(c, _):
        logits(2 * c + 1, sb_scr)
        accumulate(2 * c, sa_scr)
        logits(2 * c + 2, sa_scr)
        accumulate(2 * c + 1, sb_scr)
        return 0

    lax.fori_loop(0, n_att // 2, attend, 0)

    group = ATT_HEADS // KV_HEADS
    for hd in range(ATT_HEADS):
        cols = slice((hd % 2) * qb, (hd % 2 + 1) * qb)
        out_t = acc_scr[hd // 2, 0:LANES, cols] / acc_scr[hd // 2, LANES:LANES + 1, cols]
        row = lax.broadcasted_iota(I32, out_t.shape, 0)
        g = hd // group
        out_t = jnp.where((row >= g * HEAD_DIM) & (row < (g + 1) * HEAD_DIM), out_t, 0.0)
        o_ref[0, :, hd * LANES:(hd + 1) * LANES] = out_t.T.astype(BF16)


def _dsa_prompt(q, qi, kiw, kbf, vt, kiwbf):
    b, t, _ = kiw.shape
    qb = QUERY_BLOCK
    nb = t // qb
    topk = min(TOPK_MAX, t // 4)
    qblk = lambda width: pl.BlockSpec((1, qb, width), lambda bi, j: (bi, j, 0))
    hblk = lambda heads: pl.BlockSpec((1, heads, qb, LANES), lambda bi, j: (bi * nb + j, 0, 0, 0))
    full = pl.BlockSpec((1, t, LANES), lambda bi, j: (bi, 0, 0))
    return pl.pallas_call(
        functools.partial(_dsa_prompt_kernel, qb=qb, topk=topk),
        grid=(b, nb),
        in_specs=[hblk(ATT_HEADS), hblk(IDX_HEADS), qblk(LANES), full,
                  pl.BlockSpec((VT_ROWS, t), lambda bi, j: (0, bi)), full],
        out_specs=qblk(ATT_HEADS * LANES),
        out_shape=jax.ShapeDtypeStruct((b, t, ATT_HEADS * LANES), BF16),
        scratch_shapes=[pltpu.VMEM((pl.cdiv(t, KEY_CHUNK) * KEY_CHUNK, qb), I32), pltpu.VMEM((ATT_HEADS, qb), F32),
                        pltpu.VMEM((ATT_HEADS // 2, VT_ROWS, 2 * qb), F32),
                        pltpu.VMEM((ATT_HEADS, ATT_CHUNK, qb), F32), pltpu.VMEM((ATT_HEADS, ATT_CHUNK, qb), F32),
                        pltpu.VMEM((IDX_HEADS // 2, KEY_CHUNK, 2 * qb), F32),
                        pltpu.VMEM((IDX_HEADS // 2, KEY_CHUNK, 2 * qb), F32)],
        compiler_params=_cparams(("parallel", "arbitrary")),
        name="dsa_prompt",
    )(q, qi, kiw, kbf, vt, kiwbf)


def _dsa_sample_kernel(pt_ref, q_ref, qi_ref, kiwq_ref, kn_ref, vn_ref, kiwn_ref, ck_hbm, cv_hbm, cki_hbm,
                       o_ref, kbuf, vbuf, kibuf, s_ref, sems, *, ts, n_pages, topk):
    b = pl.program_id(0)
    past = n_pages * PAGE_SIZE

    streams = ((cki_hbm, kibuf), (ck_hbm, kbuf), (cv_hbm, vbuf))

    def page_copy(p, which):
        src, dst = streams[which]
        cols = pl.ds(pl.multiple_of(p * PAGE_SIZE, PAGE_SIZE), PAGE_SIZE)
        return pltpu.make_async_copy(src.at[pt_ref[b, p]], dst.at[:, cols], sems.at[which])

    def start_page(p, _):
        for which in range(len(streams)):
            page_copy(p, which).start()
        return 0

    lax.fori_loop(0, n_pages, start_page, 0)

    def wait_pages(which):
        def body(p, _):
            page_copy(p, which).wait()
            return 0
        lax.fori_loop(0, n_pages, body, 0)

    n_past_chunks = past // KEY_CHUNK
    n_chunks = n_past_chunks + 1
    qpos = lax.broadcasted_iota(I32, (ts, 1), 0)
    wq = _indexer_weights(kiwq_ref[0])
    qi_stack = _stack_heads(qi_ref[0], 0, IDX_HEADS, IDX_DIM)
    w_stack = jnp.concatenate([wq[:, hd:hd + 1] for hd in range(IDX_HEADS)], axis=0)

    def head_sum(x):
        acc = x[0:ts]
        for hd in range(1, IDX_HEADS):
            acc = acc + x[hd * ts:(hd + 1) * ts]
        return acc

    wait_pages(0)

    step_keys = math.gcd(past, ROW_SLAB)

    def past_scores(c, _):
        k0 = pl.multiple_of(c * step_keys, KEY_CHUNK)
        ktc = kibuf[:, pl.ds(k0, step_keys)].astype(BF16)
        acc = head_sum(jnp.maximum(_dot(qi_stack, ktc), 0.0) * w_stack)
        s_ref[:, pl.ds(k0, step_keys)] = _score_keys(acc + 0.0)
        return 0

    lax.fori_loop(0, past // step_keys, past_scores, 0)
    acc = head_sum(jnp.maximum(_dot(qi_stack, kiwn_ref[0, 0:IDX_DIM, :]), 0.0) * w_stack)
    kpos = lax.broadcasted_iota(I32, (ts, KEY_CHUNK), 1)
    s_ref[:, pl.ds(past, KEY_CHUNK)] = jnp.where(kpos <= qpos, _score_keys(acc + 0.0), INT_MIN)
    _row_select_to_bias(s_ref, n_chunks, (past + 1 + qpos).astype(F32), topk)

    wait_pages(1)
    wait_pages(2)

    def load_past(k0):
        return kbuf[:, pl.ds(k0, step_keys)].astype(BF16), vbuf[:, pl.ds(k0, step_keys)].astype(BF16)

    def load_new(k0):
        return kn_ref[0], vn_ref[0]

    segments = [(past // step_keys, step_keys, 0, load_past), (1, KEY_CHUNK, past, load_new)]
    _attend_all_heads(q_ref[0], segments, s_ref, o_ref)


def _dsa_sample(q, qi, kiw, knt, vnt, kiwnt, cache_kt, cache_vt, cache_kit, page_table):
    b, ts, _ = q.shape
    n_pages = page_table.shape[1]
    past = n_pages * PAGE_SIZE
    topk = min(TOPK_MAX, (past + ts) // 4)
    blk = lambda rows, width: pl.BlockSpec((1, rows, width), lambda bi, pt: (bi, 0, 0))
    hbm = pl.BlockSpec(memory_space=pl.ANY)
    grid_spec = pltpu.PrefetchScalarGridSpec(
        num_scalar_prefetch=1,
        grid=(b,),
        in_specs=[blk(ts, ATT_HEADS * LANES), blk(ts, IDX_HEADS * LANES), blk(ts, LANES),
                  blk(LANES, KEY_CHUNK), blk(LANES, KEY_CHUNK), blk(LANES, KEY_CHUNK), hbm, hbm, hbm],
        out_specs=blk(ts, ATT_HEADS * LANES),
        scratch_shapes=[pltpu.VMEM((LANES, past), F32), pltpu.VMEM((LANES, past), F32),
                        pltpu.VMEM((IDX_DIM, past), F32), pltpu.VMEM((ts, past + KEY_CHUNK), I32),
                        pltpu.SemaphoreType.DMA((3,))],
    )
    return pl.pallas_call(
        functools.partial(_dsa_sample_kernel, ts=ts, n_pages=n_pages, topk=topk),
        grid_spec=grid_spec,
        out_shape=jax.ShapeDtypeStruct((b, ts, ATT_HEADS * LANES), F32),
        compiler_params=_cparams(("arbitrary",)),
        name="dsa_sample",
    )(page_table, q, qi, kiw, knt, vnt, kiwnt, cache_kt, cache_vt, cache_kit)


def _conv_kernel(u_ref, halo_ref, buf_ref, bg_ref, w_ref, ya_ref):
    i = pl.program_id(1)
    u = u_ref[0]
    halo, buf = halo_ref[0], buf_ref[0]
    first = i == 0
    prev1 = jnp.where(first, buf[1:2], halo[7:8])
    prev2 = jnp.where(first, buf[0:1], halo[6:7])
    row = lax.broadcasted_iota(I32, u.shape, 0)
    um1 = jnp.where(row == 0, prev1, pltpu.roll(u, 1, 0))
    um2 = jnp.where(row == 0, prev2, jnp.where(row == 1, prev1, pltpu.roll(u, 2, 0)))
    w = w_ref[...]
    conv = w[0:1] * um2 + w[1:2] * um1 + w[2:3] * u
    ya_ref[0] = (bg_ref[0] * conv).astype(BF16)


def _conv(u, bg, buf, w, tt):
    b, t, c = u.shape
    halo_rows = 8
    tile = pl.BlockSpec((1, tt, c), lambda bi, i: (bi, i, 0))
    halo = pl.BlockSpec((1, halo_rows, c), lambda bi, i: (bi, jnp.maximum(i * (tt // halo_rows) - 1, 0), 0))
    return pl.pallas_call(
        _conv_kernel,
        grid=(b, t // tt),
        in_specs=[tile, halo, pl.BlockSpec((1, CONV_WIDTH - 1, c), lambda bi, i: (bi, 0, 0)), tile,
                  pl.BlockSpec((CONV_WIDTH, c), lambda bi, i: (0, 0))],
        out_specs=tile,
        out_shape=jax.ShapeDtypeStruct((b, t, c), BF16),
        compiler_params=_cparams(("parallel", "parallel")),
        name="short_conv",
    )(u, u, buf, bg, w)


def _mm_res_kernel(*refs, n_in):
    a_refs, w_ref, x_ref, o_ref = refs[:n_in], refs[n_in], refs[n_in + 1], refs[n_in + 2]
    a = jnp.concatenate([r[...] for r in a_refs], axis=1) if n_in > 1 else a_refs[0][...]
    o_ref[...] = x_ref[...] + _dot(a, w_ref[...])


def _mm_res(a_list, w, x, tm):
    m = x.shape[0]
    row = lambda i: (i, 0)
    return pl.pallas_call(
        functools.partial(_mm_res_kernel, n_in=len(a_list)),
        grid=(m // tm,),
        in_specs=[pl.BlockSpec((tm, a.shape[1]), row) for a in a_list]
        + [pl.BlockSpec(w.shape, lambda i: (0, 0)), pl.BlockSpec((tm, D_MODEL), row)],
        out_specs=pl.BlockSpec((tm, D_MODEL), row),
        out_shape=jax.ShapeDtypeStruct((m, D_MODEL), F32),
        compiler_params=_cparams(("parallel",)),
        name="matmul_residual",
    )(*a_list, w, x)


def _ret_in_kernel(x_ref, g_ref, w_ref, qk_ref, vg_ref):
    h = _rms(x_ref[...], g_ref[...]).astype(BF16)
    n_qk = qk_ref.shape[1]
    qk_ref[...] = _dot(h, w_ref[:, :n_qk])
    vg_ref[...] = _dot(h, w_ref[:, n_qk:]).astype(BF16)


def _ret_in_proj(x, g, w, tm):
    m, n = x.shape[0], w.shape[1]
    n_qk = 2 * RET_HEADS * RET_DK
    row = lambda i: (i, 0)
    const = lambda i: (0, 0)
    return pl.pallas_call(
        _ret_in_kernel,
        grid=(m // tm,),
        in_specs=[pl.BlockSpec((tm, D_MODEL), row), pl.BlockSpec((1, D_MODEL), const), pl.BlockSpec((D_MODEL, n), const)],
        out_specs=[pl.BlockSpec((tm, n_qk), row), pl.BlockSpec((tm, n - n_qk), row)],
        out_shape=[jax.ShapeDtypeStruct((m, n_qk), F32), jax.ShapeDtypeStruct((m, n - n_qk), BF16)],
        compiler_params=_cparams(("parallel",)),
        name="retention_in_proj",
    )(x, g, w)


def _ffn_kernel(x_ref, g_ref, wg_ref, wu_ref, wd_ref, o_ref, h_scr, acc):
    f = pl.program_id(1)

    @pl.when(f == 0)
    def _():
        x = x_ref[...]
        h_scr[...] = _rms(x, g_ref[...]).astype(BF16)
        acc[...] = x

    h = h_scr[...]
    a = jax.nn.silu(_dot(h, wg_ref[...])) * _dot(h, wu_ref[...])
    acc[...] += _dot(a.astype(BF16), wd_ref[...])

    @pl.when(f == pl.num_programs(1) - 1)
    def _():
        o_ref[...] = acc[...]


def _ffn(x, g, wg, wu, wd, tm, tf):
    m, ff = x.shape[0], wg.shape[1]
    row = lambda i, f: (i, 0)
    return pl.pallas_call(
        _ffn_kernel,
        grid=(m // tm, ff // tf),
        in_specs=[pl.BlockSpec((tm, D_MODEL), row), pl.BlockSpec((1, D_MODEL), lambda i, f: (0, 0)),
                  pl.BlockSpec((D_MODEL, tf), lambda i, f: (0, f)), pl.BlockSpec((D_MODEL, tf), lambda i, f: (0, f)),
                  pl.BlockSpec((tf, D_MODEL), lambda i, f: (f, 0))],
        out_specs=pl.BlockSpec((tm, D_MODEL), row),
        out_shape=jax.ShapeDtypeStruct((m, D_MODEL), F32),
        scratch_shapes=[pltpu.VMEM((tm, D_MODEL), BF16), pltpu.VMEM((tm, D_MODEL), F32)],
        compiler_params=_cparams(("parallel", "arbitrary")),
        name="dense_swiglu",
    )(x, g, wg, wu, wd)


def _top2_gates(logits):
    lane = lax.broadcasted_iota(I32, logits.shape, 1).astype(F32)
    neg = jnp.float32(-jnp.inf)
    l1 = jnp.where(lane < N_EXPERTS, logits, neg)
    m1 = jnp.max(l1, axis=1, keepdims=True)
    i1 = jnp.min(jnp.where(l1 == m1, lane, float(LANES)), axis=1, keepdims=True)
    l2 = jnp.where(lane == i1, neg, l1)
    m2 = jnp.max(l2, axis=1, keepdims=True)
    i2 = jnp.min(jnp.where(l2 == m2, lane, float(LANES)), axis=1, keepdims=True)
    e = jnp.exp(m2 - m1)
    w1 = 1.0 / (1.0 + e)
    w2 = e / (1.0 + e)
    first, second = lane == i1, lane == i2
    return jnp.where(first, w1, jnp.where(second, w2, 0.0)), jnp.where(first | second, 1.0, 0.0)


MOE_SUB = 144
MOE_TOKEN_TILE = 1024


def _moe_route_kernel(x_ref, g_ref, rhi_ref, rlo_ref, h_ref, gate_ref, posc_ref, posr_ref, cnt_ref):
    tm = x_ref.shape[0]
    hn = _rms(x_ref[...], g_ref[...])
    h_hi = hn.astype(BF16)
    h_lo = (hn - h_hi.astype(F32)).astype(BF16)
    logits = _dot(h_hi, rhi_ref[...]) + (_dot(h_lo, rhi_ref[...]) + _dot(h_hi, rlo_ref[...]))
    gate, routed = _top2_gates(logits)
    h_ref[...] = h_hi
    gate_ref[...] = gate
    earlier = (lax.broadcasted_iota(I32, (tm, tm), 0) > lax.broadcasted_iota(I32, (tm, tm), 1))
    slot = _dot(jnp.where(earlier, 1.0, 0.0).astype(BF16), routed.astype(BF16))
    posc = jnp.where(routed > 0.0, slot, -1.0)
    posc_ref[...] = posc
    posr_ref[0] = posc.T[0:N_EXPERTS]
    cnt_ref[0] = jnp.broadcast_to(jnp.sum(routed, axis=0, keepdims=True), (8, LANES))


def _moe_expert_kernel(cnt_ref, h_ref, gate_ref, posc_ref, posr_ref, wg_ref, wu_ref, wd_ref, o_ref, xg, yacc):
    i, e, f = pl.program_id(0), pl.program_id(1), pl.program_id(2)
    tm = h_ref.shape[0]
    n_sub = (cnt_ref[i * N_EXPERTS + e] + MOE_SUB - 1) // MOE_SUB

    @pl.when((e == 0) & (f == 0))
    def _():
        o_ref[...] = jnp.zeros(o_ref.shape, F32)

    @pl.when(f == 0)
    def _():
        posr = posr_ref[0]
        h = h_ref[...]

        def gather(s, _):
            base = pl.multiple_of(s * MOE_SUB, MOE_SUB)
            slot = (base + lax.broadcasted_iota(I32, (MOE_SUB, tm), 0)).astype(F32)
            onehot = jnp.where(posr == slot, 1.0, 0.0).astype(BF16)
            xg[pl.ds(base, MOE_SUB), :] = _dot(onehot, h).astype(BF16)
            yacc[pl.ds(base, MOE_SUB), :] = jnp.zeros((MOE_SUB, D_MODEL), F32)
            return 0

        lax.fori_loop(0, n_sub, gather, 0)

    def expert(s, _):
        rows = pl.ds(pl.multiple_of(s * MOE_SUB, MOE_SUB), MOE_SUB)
        xs = xg[rows, :]
        a = jax.nn.silu(_dot(xs, wg_ref[0])) * _dot(xs, wu_ref[0])
        yacc[rows, :] += _dot(a.astype(BF16), wd_ref[0])
        return 0

    lax.fori_loop(0, n_sub, expert, 0)

    @pl.when(f == pl.num_programs(2) - 1)
    def _():
        lane = lax.broadcasted_iota(I32, (tm, LANES), 1)
        mine = lane == e
        posc = jnp.sum(jnp.where(mine, posc_ref[...], 0.0), axis=1, keepdims=True)
        gate = jnp.sum(jnp.where(mine, gate_ref[...], 0.0), axis=1, keepdims=True)

        def scatter(s, _):
            base = pl.multiple_of(s * MOE_SUB, MOE_SUB)
            slot = (base + lax.broadcasted_iota(I32, (tm, MOE_SUB), 1)).astype(F32)
            onehot = jnp.where(posc == slot, 1.0, 0.0).astype(BF16)
            o_ref[...] += gate * _dot(onehot, yacc[pl.ds(base, MOE_SUB), :].astype(BF16))
            return 0

        lax.fori_loop(0, n_sub, scatter, 0)


def _moe(x, g, r_hi, r_lo, wg, wu, wd, tm, tf):
    m = x.shape[0]
    nt = m // tm
    row = lambda i: (i, 0)
    const = lambda i: (0, 0)
    h, gate, posc, posr, cnt = pl.pallas_call(
        _moe_route_kernel,
        grid=(nt,),
        in_specs=[pl.BlockSpec((tm, D_MODEL), row), pl.BlockSpec((1, D_MODEL), const),
                  pl.BlockSpec((D_MODEL, LANES), const), pl.BlockSpec((D_MODEL, LANES), const)],
        out_specs=[pl.BlockSpec((tm, D_MODEL), row), pl.BlockSpec((tm, LANES), row), pl.BlockSpec((tm, LANES), row),
                   pl.BlockSpec((1, N_EXPERTS, tm), lambda i: (i, 0, 0)), pl.BlockSpec((1, 8, LANES), lambda i: (i, 0, 0))],
        out_shape=[jax.ShapeDtypeStruct((m, D_MODEL), BF16), jax.ShapeDtypeStruct((m, LANES), F32),
                   jax.ShapeDtypeStruct((m, LANES), F32), jax.ShapeDtypeStruct((nt, N_EXPERTS, tm), F32),
                   jax.ShapeDtypeStruct((nt, 8, LANES), F32)],
        compiler_params=_cparams(("parallel",)),
        name="moe_route",
    )(x, g, r_hi, r_lo)
    counts = cnt[:, 0, :N_EXPERTS].astype(I32).reshape(nt * N_EXPERTS)
    posr = posr.reshape(nt * N_EXPERTS, 1, tm)
    row3 = lambda i, e, f, c: (i, 0)
    slot_rows = pl.cdiv(tm, MOE_SUB) * MOE_SUB
    grid_spec = pltpu.PrefetchScalarGridSpec(
        num_scalar_prefetch=1,
        grid=(nt, N_EXPERTS, EXPERT_FF // tf),
        in_specs=[pl.BlockSpec((tm, D_MODEL), row3),
                  pl.BlockSpec((tm, LANES), row3), pl.BlockSpec((tm, LANES), row3),
                  pl.BlockSpec((1, 1, tm), lambda i, e, f, c: (i * N_EXPERTS + e, 0, 0)),
                  pl.BlockSpec((1, D_MODEL, tf), lambda i, e, f, c: (e, 0, f)),
                  pl.BlockSpec((1, D_MODEL, tf), lambda i, e, f, c: (e, 0, f)),
                  pl.BlockSpec((1, tf, D_MODEL), lambda i, e, f, c: (e, f, 0))],
        out_specs=pl.BlockSpec((tm, D_MODEL), row3),
        scratch_shapes=[pltpu.VMEM((slot_rows, D_MODEL), BF16), pltpu.VMEM((slot_rows, D_MODEL), F32)],
    )
    return pl.pallas_call(
        _moe_expert_kernel,
        grid_spec=grid_spec,
        out_shape=jax.ShapeDtypeStruct((m, D_MODEL), F32),
        compiler_params=_cparams(("parallel", "arbitrary", "arbitrary")),
        name="moe_experts",
    )(counts, h, gate, posc, posr, wg, wu, wd)


def _ple_kernel(*refs, final_norm, n_addends):
    x_refs, (g_ref, p_ref, wp_ref, wgate_ref, gf_ref, o_ref) = refs[:n_addends], refs[n_addends:]
    x = x_refs[0][...]
    for r in x_refs[1:]:
        x = x + r[...]
    hp = _rms(x, g_ref[...]).astype(BF16)
    gate = jax.nn.sigmoid(_dot(hp, wgate_ref[...]))
    y = x + _dot(p_ref[...].astype(BF16), wp_ref[...]) * gate
    if final_norm:
        y = _rms(y, gf_ref[...])
    o_ref[...] = y


def _ple(xs, g, p, wp, wgate, g_final, final_norm, tm):
    m = xs[0].shape[0]
    row = lambda i: (i, 0)
    const = lambda i: (0, 0)
    return pl.pallas_call(
        functools.partial(_ple_kernel, final_norm=final_norm, n_addends=len(xs)),
        grid=(m // tm,),
        in_specs=[pl.BlockSpec((tm, D_MODEL), row)] * len(xs)
        + [pl.BlockSpec((1, D_MODEL), const), pl.BlockSpec((tm, PLE_DIM), row),
           pl.BlockSpec((PLE_DIM, D_MODEL), const), pl.BlockSpec((D_MODEL, D_MODEL), const),
           pl.BlockSpec((1, D_MODEL), const)],
        out_specs=pl.BlockSpec((tm, D_MODEL), row),
        out_shape=jax.ShapeDtypeStruct((m, D_MODEL), F32),
        compiler_params=_cparams(("parallel",)),
        name="per_layer_embedding",
    )(*xs, g, p, wp, wgate, g_final)


def _ret_kernel(q_ref, k_ref, v_ref, gate_ref, cos_ref, sin_ref, s0_ref, o_ref, sout_ref, state, *, chunk, chunk_rows):
    i = pl.program_id(1)

    @pl.when(i == 0)
    def _():
        state[...] = s0_ref[0]

    cos, sin = cos_ref[...], sin_ref[...]
    tt = q_ref.shape[1]
    r = chunk_rows
    ii = lax.broadcasted_iota(I32, (r, r), 0).astype(F32)
    jj = lax.broadcasted_iota(I32, (r, r), 1).astype(F32)
    rel = ii - jj
    icol = lax.broadcasted_iota(I32, (r, 1), 0).astype(F32)
    half = RET_DK // 2

    def rot(ref, hd):
        x1 = ref[0, :, hd * RET_DK:hd * RET_DK + half]
        x2 = ref[0, :, hd * RET_DK + half:(hd + 1) * RET_DK]
        return jnp.concatenate([x1 * cos - x2 * sin, x2 * cos + x1 * sin], axis=1)

    for hd in range(RET_HEADS):
        lg = math.log(1.0 - 2.0 ** (-5.0 - hd))
        d_in = jnp.where(rel >= 0, jnp.exp(lg * jnp.maximum(rel, 0.0)), 0.0)
        d_q = jnp.exp(lg * (icol + 1.0))
        d_k = jnp.exp(lg * (chunk - 1.0 - icol)) * (RET_DK ** -0.5)
        d_c = math.exp(lg * chunk)
        qr = rot(q_ref, hd)
        kr = rot(k_ref, hd)
        vsl = slice(hd * RET_DV, (hd + 1) * RET_DV)
        for c in range(tt // r):
            rows = slice(c * r, (c + 1) * r)
            qc = qr[rows].astype(BF16)
            kc = kr[rows]
            vc = v_ref[0, rows, vsl]
            s_prev = state[hd]
            att = _dot_nt(qc, (kc * (RET_DK ** -0.5)).astype(BF16)) * d_in
            o = _dot(att.astype(BF16), vc) + _dot(qc, s_prev.astype(BF16)) * d_q
            state[hd] = s_prev * d_c + _dot_tn((kc * d_k).astype(BF16), vc)
            mu = jnp.mean(o, axis=-1, keepdims=True)
            var = jnp.mean(jnp.square(o - mu), axis=-1, keepdims=True)
            on = (o - mu) * lax.rsqrt(var + EPS)
            o_ref[0, rows, vsl] = (jax.nn.silu(gate_ref[0, rows, vsl].astype(F32)) * on).astype(BF16)

    @pl.when(i == pl.num_programs(1) - 1)
    def _():
        sout_ref[0] = state[...]


def _retention(zqk, zvg, s0, cos, sin, tt, chunk, chunk_rows):
    b, t, _ = zqk.shape
    hk, hv = RET_HEADS * RET_DK, RET_HEADS * RET_DV
    half = RET_DK // 2
    state_spec = pl.BlockSpec((1, RET_HEADS, RET_DK, RET_DV), lambda bi, i: (bi, 0, 0, 0))
    tab = pl.BlockSpec((tt, half), lambda bi, i: (i, 0))
    return pl.pallas_call(
        functools.partial(_ret_kernel, chunk=chunk, chunk_rows=chunk_rows),
        grid=(b, t // tt),
        in_specs=[pl.BlockSpec((1, tt, hk), lambda bi, i: (bi, i, 0)), pl.BlockSpec((1, tt, hk), lambda bi, i: (bi, i, 1)),
                  pl.BlockSpec((1, tt, hv), lambda bi, i: (bi, i, 0)), pl.BlockSpec((1, tt, hv), lambda bi, i: (bi, i, 1)),
                  tab, tab, state_spec],
        out_specs=[pl.BlockSpec((1, tt, hv), lambda bi, i: (bi, i, 0)), state_spec],
        out_shape=[jax.ShapeDtypeStruct((b, t, hv), BF16), jax.ShapeDtypeStruct(s0.shape, F32)],
        scratch_shapes=[pltpu.VMEM((RET_HEADS, RET_DK, RET_DV), F32)],
        compiler_params=_cparams(("parallel", "arbitrary")),
        name="retention",
    )(zqk, zqk, zvg, zvg, cos, sin, s0)


def _pack_l0_w_in(w):
    offs = np.cumsum((0,) + AB_SPLITS)
    bg, cg, hv, q, k, v, qi, ki, wi = [w[:, offs[n]:offs[n + 1]] for n in range(len(AB_SPLITS))]
    group = ATT_HEADS // KV_HEADS
    q4 = q.reshape(D_MODEL, ATT_HEADS, HEAD_DIM)
    zq = jnp.zeros_like(q4)
    q_pad = jnp.concatenate([jnp.concatenate([q4[:, :group], zq[:, :group]], axis=-1),
                             jnp.concatenate([zq[:, group:], q4[:, group:]], axis=-1)], axis=1).reshape(D_MODEL, -1)
    qi4 = qi.reshape(D_MODEL, IDX_HEADS, IDX_DIM)
    qi_pad = jnp.concatenate([qi4, jnp.zeros_like(qi4)], axis=-1).reshape(D_MODEL, -1)
    kiw = jnp.concatenate([ki, wi, jnp.zeros((D_MODEL, LANES - IDX_DIM - IDX_HEADS), w.dtype)], axis=1)
    return jnp.concatenate([bg, cg, hv, q_pad, k, v, qi_pad, kiw], axis=1).astype(BF16)


def _pack_l0_w_out(w):
    group = ATT_HEADS // KV_HEADS
    wa, wb = w[:CONV_CH], w[CONV_CH:].reshape(ATT_HEADS, HEAD_DIM, D_MODEL)
    zb = jnp.zeros_like(wb)
    wb_pad = jnp.concatenate([jnp.concatenate([wb[:group], zb[:group]], axis=1),
                              jnp.concatenate([zb[group:], wb[group:]], axis=1)], axis=0).reshape(-1, D_MODEL)
    return jnp.concatenate([wa, wb_pad], axis=0).astype(BF16)


def _rope_tables(pos, reps):
    inv = ROPE_THETA ** (-jnp.arange(0, HEAD_DIM, 2, dtype=F32) / HEAD_DIM)
    ang = pos.astype(F32)[:, None] * inv[None, :]
    cos, sin = jnp.cos(ang), jnp.sin(ang)
    cos64 = jnp.concatenate([cos, cos], axis=1)
    sin64 = jnp.concatenate([-sin, sin], axis=1)
    one, zero = jnp.ones_like(cos64), jnp.zeros_like(cos64)
    tabs = (jnp.concatenate([cos64, cos64], 1), jnp.concatenate([sin64, sin64], 1),
            jnp.concatenate([cos64, one], 1), jnp.concatenate([sin64, zero], 1))
    return tuple(jnp.tile(tb, (reps, 1)) for tb in tabs)


def _ret_tables(pos):
    inv = ROPE_THETA ** (-jnp.linspace(0.0, 1.0, RET_DK // 2, dtype=F32))
    ang = pos.astype(F32)[:, None] * inv[None, :]
    return jnp.cos(ang), jnp.sin(ang)


def _pack_params(prm):
    r = prm['moe_router'][0]
    r_pad = jnp.concatenate([r, jnp.zeros((D_MODEL, LANES - N_EXPERTS), F32)], axis=1)
    r_hi = r_pad.astype(BF16)
    bf = lambda a: a.astype(BF16)
    return dict(
        l0_w_in=_pack_l0_w_in(prm['ab_w_in'][0]), l0_w_out=_pack_l0_w_out(prm['ab_w_out'][0]),
        l0_w_vt=bf(prm['ab_w_in'][0][:, sum(AB_SPLITS[:5]):sum(AB_SPLITS[:6])].T),
        conv_w=prm['ab_conv_w'][0],
        ffn=(bf(prm['ffn_w_gate'][0]), bf(prm['ffn_w_up'][0]), bf(prm['ffn_w_down'][0])),
        ret_w_in=bf(prm['ret_w_in'][0]), ret_w_out=bf(prm['ret_w_out'][0]),
        r_hi=r_hi, r_lo=(r_pad - r_hi.astype(F32)).astype(BF16),
        moe=(bf(prm['moe_w_gate'][0]), bf(prm['moe_w_up'][0]), bf(prm['moe_w_down'][0])),
        ple_w=bf(prm['ple_w']), ple_gate_w=bf(prm['ple_gate_w']),
        norm_mix=prm['norm_mix'][:, None, :], norm_ffn=prm['norm_ffn'][:, None, :],
        norm_ple=prm['norm_ple'][:, None, :], norm_final=prm['norm_final'][None, :],
    )


def _trunk(x, p, pos, conv_buf, ret_state, pk, paged):
    b, t, _ = x.shape
    m = b * t
    tm = min(m, TOKEN_TILE)
    moe_tm = min(m, MOE_TOKEN_TILE)
    assert m % tm == 0 and m % moe_tm == 0 and m % min(m, 256) == 0, "token count must divide into whole tiles"
    assert paged is not None or (t % QUERY_BLOCK == 0 and t % tm == 0), "prompt length must divide into whole tiles"
    assert paged is None or (tm % t == 0 and t <= 8), "sample sequences are one sublane tile long at most"
    x2 = x.reshape(m, D_MODEL)

    reps = max(1, tm // t)
    tabs = _rope_tables(pos, reps)
    bg, u, q, k, v, qi, kiw, kbf, vbf, kiwbf, vt = _l0_in_proj(x2, pk['norm_mix'][0], pk['l0_w_in'], pk['l0_w_vt'], tabs,
                                                              tm, tabs[0].shape[0] // tm, head_major=paged is None)
    seq = lambda a: a.reshape(b, t, a.shape[-1])
    if paged is None:
        yb = _dsa_prompt(q, qi, seq(kiw), seq(kbf), vt, seq(kiwbf))
    else:
        cache_k, cache_v, cache_ki, page_table = paged
        padt = lambda a: jnp.swapaxes(jnp.pad(seq(a), ((0, 0), (0, KEY_CHUNK - t), (0, 0))), 1, 2)
        n_pool = cache_k.shape[0]
        pages_t = lambda c: jnp.transpose(c, (0, 2, 3, 1)).reshape(n_pool, LANES, PAGE_SIZE)
        yb = _dsa_sample(seq(q).astype(F32), seq(qi).astype(F32), seq(kiw), padt(kbf), padt(vbf), padt(kiwbf),
                         pages_t(cache_k), pages_t(cache_v), jnp.swapaxes(cache_ki, 1, 2), page_table).astype(BF16)
    u3 = seq(u)
    ya = _conv(u3, seq(bg), conv_buf, pk['conv_w'], tt=min(t, TOKEN_TILE))
    x2 = _mm_res([ya.reshape(m, CONV_CH), yb.reshape(m, ATT_HEADS * LANES)], pk['l0_w_out'], x2, tm)
    x2 = _ffn(x2, pk['norm_ffn'][0], *pk['ffn'], tm=tm, tf=D_FF // 2)
    x2 = _ple([x2], pk['norm_ple'][0], p[0].reshape(m, PLE_DIM), pk['ple_w'][0], pk['ple_gate_w'][0],
              pk['norm_final'], False, tm)
    new_k = k.reshape(1, b, t, KV_HEADS, HEAD_DIM)
    new_v = v.reshape(1, b, t, KV_HEADS, HEAD_DIM)
    new_ki = seq(kiw)[None, :, :, :IDX_DIM]
    new_conv = jnp.concatenate([conv_buf, u3], axis=1)[None, :, -(CONV_WIDTH - 1):]

    zqk, zvg = _ret_in_proj(x2, pk['norm_mix'][1], pk['ret_w_in'], min(m, 256))
    zqk, zvg = zqk.reshape(b, t, -1), zvg.reshape(b, t, -1)
    cos_r, sin_r = _ret_tables(pos)
    if t % RET_CHUNK == 0:
        og, s_new = _retention(zqk, zvg, ret_state, cos_r, sin_r, tt=2 * RET_CHUNK, chunk=RET_CHUNK,
                               chunk_rows=RET_CHUNK)
    else:
        rows = 16
        padt = lambda a: jnp.pad(a, ((0, 0),) * (a.ndim - 2) + ((0, rows - t), (0, 0)))
        og, s_new = _retention(padt(zqk), padt(zvg), ret_state, padt(cos_r), padt(sin_r), tt=rows, chunk=t,
                               chunk_rows=rows)
        og = og[:, :t]
    x2 = _mm_res([og.reshape(m, RET_HEADS * RET_DV)], pk['ret_w_out'], x2, tm)
    moe = _moe(x2, pk['norm_ffn'][1], pk['r_hi'], pk['r_lo'], *pk['moe'], tm=moe_tm, tf=EXPERT_FF // 2)
    x2 = _ple([x2, moe], pk['norm_ple'][1], p[1].reshape(m, PLE_DIM), pk['ple_w'][1], pk['ple_gate_w'][1],
              pk['norm_final'], True, tm)
    return x2.reshape(b, t, D_MODEL), new_k, new_v, new_ki, new_conv, s_new[None]


def kernel(x_prompt, x_sample, cache_k, cache_v, cache_kidx, state_conv, state_ret, page_table, p_prompt, p_sample,
           norm_mix, norm_ffn, norm_ple, norm_final, ab_w_in, ab_conv_w, ab_w_out, ffn_w_gate, ffn_w_up, ffn_w_down,
           ret_w_in, ret_w_out, moe_router, moe_w_gate, moe_w_up, moe_w_down, ple_w, ple_gate_w):
    prm = dict(norm_mix=norm_mix, norm_ffn=norm_ffn, norm_ple=norm_ple, norm_final=norm_final, ab_w_in=ab_w_in,
               ab_conv_w=ab_conv_w, ab_w_out=ab_w_out, ffn_w_gate=ffn_w_gate, ffn_w_up=ffn_w_up, ffn_w_down=ffn_w_down,
               ret_w_in=ret_w_in, ret_w_out=ret_w_out, moe_router=moe_router, moe_w_gate=moe_w_gate,
               moe_w_up=moe_w_up, moe_w_down=moe_w_down, ple_w=ple_w, ple_gate_w=ple_gate_w)
    pk = _pack_params(prm)
    b, t = x_prompt.shape[0], x_prompt.shape[1]
    db, ts = x_sample.shape[0], x_sample.shape[1]
    past_len = page_table.shape[1] * PAGE_SIZE
    dt = x_prompt.dtype

    conv0 = jnp.zeros((b, CONV_WIDTH - 1, CONV_CH), dt)
    ret0 = jnp.zeros((b, RET_HEADS, RET_DK, RET_DV), dt)
    y_p, k_p, v_p, ki_p, cb_p, rs_p = _trunk(x_prompt, p_prompt, jnp.arange(t, dtype=I32), conv0, ret0, pk, None)

    pos_s = past_len + jnp.arange(ts, dtype=I32)
    paged = (cache_k[0], cache_v[0], cache_kidx[0], page_table)
    y_s, k_s, v_s, ki_s, cb_s, rs_s = _trunk(x_sample, p_sample, pos_s, state_conv[0], state_ret[0], pk, paged)
    return (y_p, y_s, k_p, v_p, ki_p, cb_p, rs_p, k_s, v_s, ki_s, cb_s, rs_s)
```

```python
import functools
import math

import jax
import jax.numpy as jnp
import numpy as np
from jax import lax
from jax.experimental import pallas as pl
from jax.experimental.pallas import tpu as pltpu

F32 = jnp.float32
BF16 = jnp.bfloat16
I32 = jnp.int32

D_MODEL = 1024
PAGE_SIZE = 128
CONV_CH = D_MODEL // 2
CONV_WIDTH = 3
ATT_HEADS = 8
KV_HEADS = 2
HEAD_DIM = 64
IDX_HEADS = 4
IDX_DIM = 64
TOPK_MAX = 256
ROPE_THETA = 10000.0
RET_HEADS = 4
RET_DK = D_MODEL // RET_HEADS
RET_DV = 2 * RET_DK
RET_CHUNK = 128
D_FF = 2816
N_EXPERTS = 8
EXPERT_FF = 3584
PLE_DIM = 256
EPS = 1e-6
AB_SPLITS = (CONV_CH, CONV_CH, CONV_CH, ATT_HEADS * HEAD_DIM, KV_HEADS * HEAD_DIM, KV_HEADS * HEAD_DIM,
             IDX_HEADS * IDX_DIM, IDX_DIM, IDX_HEADS)

LANES = 128
VMEM_LIMIT = 48 * 1024 * 1024
INT_MIN = -2 ** 31
NEG_BIG = -1e30
LOG2_E = math.log2(math.e)

C_BG, C_CG, C_HV, C_Q, C_K, C_V, C_QI, C_KIW, C_END = 0, 512, 1024, 1536, 2560, 2688, 2816, 3328, 3456
KEY_CHUNK = 512
VT_ROWS = LANES + 16
TOKEN_TILE = 512
QUERY_BLOCK = 128


def _cparams(sem):
    return pltpu.CompilerParams(dimension_semantics=sem, vmem_limit_bytes=VMEM_LIMIT)


def _rms(x, g):
    return x * lax.rsqrt(jnp.mean(x * x, axis=-1, keepdims=True) + EPS) * g


def _dot(a, b):
    return jnp.dot(a, b, preferred_element_type=F32)


def _dot_nt(a, b):
    return lax.dot_general(a, b, (((1,), (1,)), ((), ())), preferred_element_type=F32)


def _dot_tn(a, b):
    return lax.dot_general(a, b, (((0,), (0,)), ((), ())), preferred_element_type=F32)


def _swap_halves64(x):
    lane = lax.broadcasted_iota(I32, x.shape, 1)
    from_above = pltpu.roll(x, LANES - 32, 1)
    from_below = pltpu.roll(x, 32, 1)
    return jnp.where((lane & 63) < 32, from_above, from_below)


def _l0_in_kernel(x_ref, g_ref, w_ref, wvt_ref, cos_ref, sin_ref, cosb_ref, sinb_ref,
                  bg_ref, u_ref, q_ref, k_ref, v_ref, qi_ref, kiw_ref, kbf_ref, vbf_ref, kiwbf_ref, vt_ref, *, head_major):
    h = _rms(x_ref[...], g_ref[...]).astype(BF16)
    cos, sin = cos_ref[...], sin_ref[...]

    def store_head(ref, hd, val):
        if head_major:
            for r in range(ref.shape[0]):
                ref[r, hd] = val[r * QUERY_BLOCK:(r + 1) * QUERY_BLOCK]
        else:
            ref[:, hd * LANES:(hd + 1) * LANES] = val

    def rot(z, c, s):
        return z * c + _swap_halves64(z) * s

    bg_ref[...] = _dot(h, w_ref[:, C_BG:C_CG])
    u_ref[...] = _dot(h, w_ref[:, C_CG:C_HV]) * _dot(h, w_ref[:, C_HV:C_Q])
    zq = _dot(h, w_ref[:, C_Q:C_K])
    for hd in range(ATT_HEADS):
        sl = slice(hd * LANES, (hd + 1) * LANES)
        store_head(q_ref, hd, (rot(zq[:, sl], cos, sin) * (HEAD_DIM ** -0.5 * LOG2_E)).astype(BF16))
    k = rot(_dot(h, w_ref[:, C_K:C_V]), cos, sin)
    k_ref[...] = k
    kbf_ref[...] = k.astype(BF16)
    v = _dot(h, w_ref[:, C_V:C_QI])
    v_ref[...] = v
    vbf_ref[...] = v.astype(BF16)
    vt_ref[0:LANES, :] = _dot_nt(wvt_ref[...], h).astype(BF16)
    vt_ref[LANES:VT_ROWS, :] = jnp.ones((VT_ROWS - LANES, vt_ref.shape[1]), BF16)
    zqi = _dot(h, w_ref[:, C_QI:C_KIW])
    for hd in range(IDX_HEADS):
        sl = slice(hd * LANES, (hd + 1) * LANES)
        store_head(qi_ref, hd, rot(zqi[:, sl], cos, sin).astype(BF16))
    kiw = rot(_dot(h, w_ref[:, C_KIW:C_END]), cosb_ref[...], sinb_ref[...])
    kiw_ref[...] = kiw
    kiwbf_ref[...] = kiw.astype(BF16)


def _l0_in_proj(x2, g, w, wvt, tabs, tm, n_tab_blocks, head_major):
    m = x2.shape[0]
    row = lambda i: (i, 0)
    const = lambda i: (0, 0)
    tab = lambda i: (i % n_tab_blocks, 0)
    widths = (512, 512, 1024, 128, 128, 512, 128, 128, 128, 128)
    dtypes = (F32, F32, BF16, F32, F32, BF16, F32, BF16, BF16, BF16)
    out_specs = [pl.BlockSpec((tm, wd), row) for wd in widths]
    out_shape = [jax.ShapeDtypeStruct((m, wd), dt) for wd, dt in zip(widths, dtypes)]
    if head_major:
        for n in (2, 5):
            heads = widths[n] // LANES
            out_specs[n] = pl.BlockSpec((tm // QUERY_BLOCK, heads, QUERY_BLOCK, LANES), lambda i: (i, 0, 0, 0))
            out_shape[n] = jax.ShapeDtypeStruct((m // QUERY_BLOCK, heads, QUERY_BLOCK, LANES), BF16)
    return pl.pallas_call(
        functools.partial(_l0_in_kernel, head_major=head_major),
        grid=(m // tm,),
        in_specs=[pl.BlockSpec((tm, D_MODEL), row), pl.BlockSpec((1, D_MODEL), const),
                  pl.BlockSpec((D_MODEL, C_END), const), pl.BlockSpec((LANES, D_MODEL), const)]
        + [pl.BlockSpec((tm, LANES), tab)] * 4,
        out_specs=out_specs + [pl.BlockSpec((VT_ROWS, tm), lambda i: (0, i))],
        out_shape=out_shape + [jax.ShapeDtypeStruct((VT_ROWS, m), BF16)],
        compiler_params=_cparams(("parallel",)),
        name="l0_in_proj",
    )(x2, g, w, wvt, *tabs)


def _score_keys(score):
    bits = pltpu.bitcast(score, I32)
    return bits ^ ((bits >> 31) & jnp.int32(0x7FFFFFFF))


def _key_scores(key):
    return pltpu.bitcast(key ^ ((key >> 31) & jnp.int32(0x7FFFFFFF)), F32)


SEARCH_PLAIN_STEPS = 11
SEARCH_SNAP_INTERP_STEPS = 8
SEARCH_MAX_STEPS = SEARCH_PLAIN_STEPS + SEARCH_SNAP_INTERP_STEPS + 34


NO_TIE_LIMIT = 1e9


def _topk_threshold(probe, vmin, vmax, n_real, topk):
    k = float(topk)
    lo0, hi0 = _score_keys(vmin), _score_keys(vmax) + 1
    take_all = n_real <= k
    zero, one = jnp.zeros_like(vmin), jnp.ones_like(vmin)
    active0 = jnp.where(jnp.logical_not(take_all) & (lo0 + 1 < hi0), 1.0, 0.0)

    def step(snap, st):
        it, lo, hi, c_lo, c_hi, w_lo, w_hi, last, active = st
        act = active > 0.0
        f_lo = (c_lo - (k - 0.5)) * w_lo
        f_hi = ((k - 0.5) - c_hi) * w_hi
        v_lo, v_hi = _key_scores(lo), _key_scores(hi)
        g = _score_keys(v_lo + (v_hi - v_lo) * (f_lo / (f_lo + f_hi)))
        if snap:
            mid = (lo >> 1) + (hi >> 1) + (lo & hi & 1)
            g = jnp.where(it < SEARCH_PLAIN_STEPS + SEARCH_SNAP_INTERP_STEPS, g, mid)
        g = jnp.minimum(jnp.maximum(g, lo + 1), hi - 1)
        c, key_up, key_dn = probe(g, snap)
        hit = act & (c == k)
        up = act & (c > k)
        dn = act & (c < k)
        lo = jnp.where(hit, g, jnp.where(up, key_up if snap else g, lo))
        c_lo = jnp.where(hit | up, c, c_lo)
        hi = jnp.where(dn, key_dn + 1 if snap else g, hi)
        c_hi = jnp.where(dn, c, c_hi)
        w_hi = jnp.where(up, jnp.where(last > 0.0, w_hi * 0.5, one), jnp.where(dn, one, w_hi))
        w_lo = jnp.where(dn, jnp.where(last < 0.0, w_lo * 0.5, one), jnp.where(up, one, w_lo))
        last = jnp.where(up, one, jnp.where(dn, -one, last))
        active = jnp.where(act & jnp.logical_not(hit) & (lo + 1 < hi), 1.0, 0.0)
        return it + 1, lo, hi, c_lo, c_hi, w_lo, w_hi, last, active

    def cond(limit, st):
        return (st[0] < limit) & (jnp.max(st[-1]) > 0.0)

    st = (jnp.int32(0), lo0, hi0, n_real, zero, one, one, zero, active0)
    st = lax.fori_loop(0, SEARCH_PLAIN_STEPS, lambda _, s: step(False, s), st)
    st = lax.while_loop(functools.partial(cond, SEARCH_MAX_STEPS), functools.partial(step, True), st)
    _, lo, _, c_lo, c_hi, _, _, _, _ = st
    thr = jnp.where(take_all, INT_MIN, lo)
    excess = jnp.logical_not(take_all) & (c_lo > k)
    ties = jnp.where(excess, k - c_hi, jnp.where(take_all, 0.0, NO_TIE_LIMIT))
    return thr, ties, jnp.max(jnp.where(excess, 1.0, 0.0)) > 0.0


def _tie_prefix_matrix(lower):
    r = lax.broadcasted_iota(I32, (KEY_CHUNK, KEY_CHUNK), 0)
    c = lax.broadcasted_iota(I32, (KEY_CHUNK, KEY_CHUNK), 1)
    return jnp.where((r >= c) if lower else (r <= c), 1.0, 0.0).astype(BF16)


def _bias_bits(sel):
    return pltpu.bitcast(jnp.where(sel, 0.0, NEG_BIG).astype(F32), I32)


ROW_SLAB = 2048


def _lane_fold(m, op=jnp.add):
    parts = [m[:, c * LANES:(c + 1) * LANES] for c in range(m.shape[1] // LANES)]
    while len(parts) > 1:
        parts = [op(parts[n], parts[n + 1]) for n in range(0, len(parts) - 1, 2)] + (parts[-1:] if len(parts) % 2 else [])
    return parts[0]


def _row_slabs(n_chunks):
    width = n_chunks * KEY_CHUNK
    return [(k0, min(ROW_SLAB, width - k0)) for k0 in range(0, width, ROW_SLAB)]


def _int_reduce(x, take_min, axis):
    red = jnp.min if take_min else jnp.max
    hi = (x >> 16).astype(F32)
    lo = (x & 0xFFFF).astype(F32)
    m_hi = red(hi, axis=axis, keepdims=True)
    m_lo = red(jnp.where(hi == m_hi, lo, 65536.0 if take_min else -1.0), axis=axis, keepdims=True)
    return (m_hi.astype(I32) << 16) | m_lo.astype(I32)


def _row_probe(s_ref, n_chunks, g, snap):
    cnt = up = dn = None
    for k0, size in _row_slabs(n_chunks):
        blk = s_ref[:, k0:k0 + size]
        ge = blk >= g
        part = _lane_fold(jnp.where(ge, 1.0, 0.0))
        cnt = part if cnt is None else cnt + part
        if snap:
            above = _lane_fold(jnp.where(ge, blk, 2 ** 31 - 1), jnp.minimum)
            below = _lane_fold(jnp.where(ge, INT_MIN, blk), jnp.maximum)
            up = above if up is None else jnp.minimum(up, above)
            dn = below if dn is None else jnp.maximum(dn, below)
    c = jnp.sum(cnt, axis=1, keepdims=True)
    if not snap:
        return c, None, None
    return c, _int_reduce(up, True, 1), _int_reduce(dn, False, 1)


def _row_select_to_bias(s_ref, n_chunks, n_real, topk):
    inf = jnp.float32(jnp.inf)
    mn = mx = None
    for k0, size in _row_slabs(n_chunks):
        blk = s_ref[:, k0:k0 + size]
        v = _key_scores(blk)
        real = blk != INT_MIN
        lo_part = _lane_fold(jnp.where(real, v, inf), jnp.minimum)
        hi_part = _lane_fold(jnp.where(real, v, -inf), jnp.maximum)
        mn = lo_part if mn is None else jnp.minimum(mn, lo_part)
        mx = hi_part if mx is None else jnp.maximum(mx, hi_part)
    vmin, vmax = jnp.min(mn, axis=1, keepdims=True), jnp.max(mx, axis=1, keepdims=True)
    thr, ties, any_excess = _topk_threshold(functools.partial(_row_probe, s_ref, n_chunks), vmin, vmax, n_real, topk)

    def plain(_):
        thr_ge = jnp.maximum(thr, INT_MIN + 1)
        for k0, size in _row_slabs(n_chunks):
            s_ref[:, k0:k0 + size] = _bias_bits(s_ref[:, k0:k0 + size] >= thr_ge)
        return 0

    def with_ties(_):
        upper = _tie_prefix_matrix(lower=False)
        seen = jnp.zeros_like(ties)
        for c in range(n_chunks):
            blk = s_ref[:, c * KEY_CHUNK:(c + 1) * KEY_CHUNK]
            tie = jnp.where(blk == thr, 1.0, 0.0)
            rows = tie.shape[0]
            tie16 = jnp.concatenate([tie, tie], axis=0).astype(BF16) if rows % 16 else tie.astype(BF16)
            rank = _dot(tie16, upper)[0:rows] + seen
            s_ref[:, c * KEY_CHUNK:(c + 1) * KEY_CHUNK] = _bias_bits((blk > thr) | ((tie > 0.0) & (rank <= ties)))
            seen = seen + jnp.sum(tie, axis=1, keepdims=True)
        return 0

    lax.cond(any_excess, with_ties, plain, 0)


def _attend_group(qg, segments, bias_ref, rep):
    n = qg.shape[0]

    def step(chunk, base, load_kv, c, carry):
        m, l, acc = carry
        k0 = pl.multiple_of(base + c * chunk, KEY_CHUNK)
        ktc, vtc = load_kv(k0)
        bias = pltpu.bitcast(bias_ref[:, pl.ds(k0, chunk)], F32)
        s = _dot(qg, ktc) + jnp.concatenate([bias] * rep, axis=0)
        m_new = jnp.maximum(m, jnp.max(s, axis=1, keepdims=True))
        alpha = jnp.exp2(m - m_new)
        p = jnp.exp2(s - m_new)
        l = alpha * l + jnp.sum(p, axis=1, keepdims=True)
        acc = alpha * acc + _dot_nt(p.astype(BF16), vtc)
        return m_new, l, acc

    carry = (jnp.full((n, 1), NEG_BIG, F32), jnp.zeros((n, 1), F32), jnp.zeros((n, LANES), F32))
    for n_steps, chunk, base, load_kv in segments:
        carry = lax.fori_loop(0, n_steps, functools.partial(step, chunk, base, load_kv), carry)
    _, l, acc = carry
    return acc / l


def _stack_heads(x, first, count, width=LANES):
    x = x.astype(F32)
    return jnp.concatenate([x[:, (first + hd) * LANES:(first + hd) * LANES + width] for hd in range(count)],
                           axis=0).astype(BF16)


def _attend_all_heads(q, segments, bias_ref, o_ref):
    rows = q.shape[0]
    group = ATT_HEADS // KV_HEADS
    out = _attend_group(_stack_heads(q, 0, ATT_HEADS), segments, bias_ref, ATT_HEADS)
    for hd in range(ATT_HEADS):
        piece = out[hd * rows:(hd + 1) * rows]
        piece = jnp.where(_group_lane_mask(piece.shape, hd // group), piece, 0.0)
        o_ref[0, :, hd * LANES:(hd + 1) * LANES] = piece.astype(o_ref.dtype)


def _indexer_weights(kiw_q):
    return kiw_q[:, IDX_DIM:IDX_DIM + IDX_HEADS] * (IDX_HEADS ** -0.5 * IDX_DIM ** -0.5)


def _group_lane_mask(shape, g):
    lane = lax.broadcasted_iota(I32, shape, 1)
    return (lane >= g * HEAD_DIM) & (lane < (g + 1) * HEAD_DIM)


ATT_CHUNK = 256


def _sublane_fold(m, op=jnp.add):
    parts = [m[r * 8:(r + 1) * 8] for r in range(m.shape[0] // 8)]
    while len(parts) > 1:
        parts = [op(parts[n], parts[n + 1]) for n in range(0, len(parts) - 1, 2)] + (parts[-1:] if len(parts) % 2 else [])
    return parts[0]


def _col_reduce(m, op, reduce_fn):
    return reduce_fn(_sublane_fold(m, op), axis=0, keepdims=True)


def _col_probe(s_ref, n_chunks, g, snap):
    cols = s_ref.shape[1]

    def body(c, carry):
        k0 = pl.multiple_of(c * KEY_CHUNK, KEY_CHUNK)
        blk = s_ref[pl.ds(k0, KEY_CHUNK), :]
        ge = blk >= g
        out = [carry[0] + _sublane_fold(jnp.where(ge, 1.0, 0.0))]
        if snap:
            out.append(jnp.minimum(carry[1], _sublane_fold(jnp.where(ge, blk, 2 ** 31 - 1), jnp.minimum)))
            out.append(jnp.maximum(carry[2], _sublane_fold(jnp.where(ge, INT_MIN, blk), jnp.maximum)))
        return tuple(out)

    init = [jnp.zeros((8, cols), F32)]
    if snap:
        init += [jnp.full((8, cols), 2 ** 31 - 1, I32), jnp.full((8, cols), INT_MIN, I32)]
    res = lax.fori_loop(0, n_chunks, body, tuple(init))
    c = jnp.sum(res[0], axis=0, keepdims=True)
    if not snap:
        return c, None, None
    return c, _int_reduce(res[1], True, 0), _int_reduce(res[2], False, 0)


def _col_select_to_bias(s_ref, n_chunks, n_real, topk, vmin, vmax):
    thr, ties, any_excess = _topk_threshold(functools.partial(_col_probe, s_ref, n_chunks), vmin, vmax, n_real, topk)

    def plain(_):
        thr_ge = jnp.maximum(thr, INT_MIN + 1)

        def body(c, _):
            k0 = pl.multiple_of(c * KEY_CHUNK, KEY_CHUNK)
            s_ref[pl.ds(k0, KEY_CHUNK), :] = _bias_bits(s_ref[pl.ds(k0, KEY_CHUNK), :] >= thr_ge)
            return 0

        return lax.fori_loop(0, n_chunks, body, 0)

    def with_ties(_):
        lower = _tie_prefix_matrix(lower=True)

        def body(c, seen):
            k0 = pl.multiple_of(c * KEY_CHUNK, KEY_CHUNK)
            blk = s_ref[pl.ds(k0, KEY_CHUNK), :]
            tie = jnp.where(blk == thr, 1.0, 0.0)
            rank = _dot(lower, tie.astype(BF16)) + seen
            s_ref[pl.ds(k0, KEY_CHUNK), :] = _bias_bits((blk > thr) | ((tie > 0.0) & (rank <= ties)))
            return seen + jnp.sum(_sublane_fold(tie), axis=0, keepdims=True)

        lax.fori_loop(0, n_chunks, body, jnp.zeros_like(ties))
        return 0

    lax.cond(any_excess, with_ties, plain, 0)


def _dsa_prompt_kernel(q_ref, qi_ref, kiwq_ref, k_ref, vt_ref, kiw_ref, o_ref, s_ref, m_scr, acc_scr,
                       sa_scr, sb_scr, ia_scr, ib_scr, *, qb, topk):
    j = pl.program_id(1)
    n_keys = j * qb + qb
    n_chunks = (n_keys + KEY_CHUNK - 1) // KEY_CHUNK
    qpos = j * qb + lax.broadcasted_iota(I32, (1, qb), 1)
    wt = kiwq_ref[0].T[IDX_DIM:IDX_DIM + IDX_HEADS] * (IDX_HEADS ** -0.5 * IDX_DIM ** -0.5)

    inf = jnp.float32(jnp.inf)

    last_chunk = s_ref.shape[0] // KEY_CHUNK - 1

    def index_dots(c, buf):
        k0 = pl.multiple_of(jnp.minimum(c, last_chunk) * KEY_CHUNK, KEY_CHUNK)
        kc = kiw_ref[0, pl.ds(k0, KEY_CHUNK), :]
        for pr in range(IDX_HEADS // 2):
            buf[pr] = _dot_nt(kc, qi_ref[0, 2 * pr:2 * pr + 2].reshape(2 * qb, LANES))

    def score_chunk(c, buf, carry):
        mn, mx = carry
        k0 = pl.multiple_of(jnp.minimum(c, last_chunk) * KEY_CHUNK, KEY_CHUNK)
        acc = jnp.zeros((KEY_CHUNK, qb), F32)
        for pr in range(IDX_HEADS // 2):
            acc = acc + jnp.maximum(buf[pr, :, 0:qb], 0.0) * wt[2 * pr:2 * pr + 1]
            acc = acc + jnp.maximum(buf[pr, :, qb:2 * qb], 0.0) * wt[2 * pr + 1:2 * pr + 2]
        causal = k0 + lax.broadcasted_iota(I32, (KEY_CHUNK, qb), 0) <= qpos
        s_ref[pl.ds(k0, KEY_CHUNK), :] = jnp.where(causal, _score_keys(acc), INT_MIN)
        mn = jnp.minimum(mn, _sublane_fold(jnp.where(causal, acc, inf), jnp.minimum))
        mx = jnp.maximum(mx, _sublane_fold(jnp.where(causal, acc, -inf), jnp.maximum))
        return mn, mx

    index_dots(0, ia_scr)

    def scores(t, carry):
        index_dots(2 * t + 1, ib_scr)
        carry = score_chunk(2 * t, ia_scr, carry)
        index_dots(2 * t + 2, ia_scr)
        return score_chunk(2 * t + 1, ib_scr, carry)

    mn, mx = lax.fori_loop(0, (n_chunks + 1) // 2, scores,
                           (jnp.full((8, qb), inf, F32), jnp.full((8, qb), -inf, F32)))
    _col_select_to_bias(s_ref, n_chunks, (qpos + 1).astype(F32), topk,
                        jnp.min(mn, axis=0, keepdims=True), jnp.max(mx, axis=0, keepdims=True))

    m_scr[...] = jnp.full(m_scr.shape, NEG_BIG, F32)
    acc_scr[...] = jnp.zeros(acc_scr.shape, F32)

    n_att = n_chunks * (KEY_CHUNK // ATT_CHUNK)

    def logits(step, buf):
        k0 = pl.multiple_of(jnp.minimum(step, n_att - 1) * ATT_CHUNK, ATT_CHUNK)
        kc = k_ref[0, pl.ds(k0, ATT_CHUNK), :]
        bias = pltpu.bitcast(s_ref[pl.ds(k0, ATT_CHUNK), :], F32)
        for pr in range(ATT_HEADS // 2):
            s2 = _dot_nt(kc, q_ref[0, 2 * pr:2 * pr + 2].reshape(2 * qb, LANES))
            buf[2 * pr] = s2[:, :qb] + bias
            buf[2 * pr + 1] = s2[:, qb:] + bias

    def accumulate(step, buf):
        k0 = pl.multiple_of(step * ATT_CHUNK, ATT_CHUNK)
        vtc = vt_ref[:, pl.ds(k0, ATT_CHUNK)]
        for pr in range(ATT_HEADS // 2):
            ps, alphas = [], []
            for hd in (2 * pr, 2 * pr + 1):
                s = buf[hd]
                m_prev = m_scr[hd:hd + 1, :]
                m_new = jnp.maximum(m_prev, _col_reduce(s, jnp.maximum, jnp.max))
                alphas.append(jnp.exp2(m_prev - m_new))
                ps.append(jnp.exp2(s - m_new).astype(BF16))
                m_scr[hd:hd + 1, :] = m_new
            acc_scr[pr] = jnp.concatenate(alphas, axis=1) * acc_scr[pr] + _dot(vtc, jnp.concatenate(ps, axis=1))

    logits(0, sa_scr)

    def attend(c, _):
        logits(2 * c + 1, sb_scr)
        accumulate(2 * c, sa_scr)
        logits(2 * c + 2, sa_scr)
        accumulate(2 * c + 1, sb_scr)
        return 0

    lax.fori_loop(0, n_att // 2, attend, 0)

    group = ATT_HEADS // KV_HEADS
    for hd in range(ATT_HEADS):
        cols = slice((hd % 2) * qb, (hd % 2 + 1) * qb)
        out_t = acc_scr[hd // 2, 0:LANES, cols] / acc_scr[hd // 2, LANES:LANES + 1, cols]
        row = lax.broadcasted_iota(I32, out_t.shape, 0)
        g = hd // group
        out_t = jnp.where((row >= g * HEAD_DIM) & (row < (g + 1) * HEAD_DIM), out_t, 0.0)
        o_ref[0, :, hd * LANES:(hd + 1) * LANES] = out_t.T.astype(BF16)


def _dsa_prompt(q, qi, kiw, kbf, vt, kiwbf):
    b, t, _ = kiw.shape
    qb = QUERY_BLOCK
    nb = t // qb
    topk = min(TOPK_MAX, t // 4)
    qblk = lambda width: pl.BlockSpec((1, qb, width), lambda bi, j: (bi, j, 0))
    hblk = lambda heads: pl.BlockSpec((1, heads, qb, LANES), lambda bi, j: (bi * nb + j, 0, 0, 0))
    full = pl.BlockSpec((1, t, LANES), lambda bi, j: (bi, 0, 0))
    return pl.pallas_call(
        functools.partial(_dsa_prompt_kernel, qb=qb, topk=topk),
        grid=(b, nb),
        in_specs=[hblk(ATT_HEADS), hblk(IDX_HEADS), qblk(LANES), full,
                  pl.BlockSpec((VT_ROWS, t), lambda bi, j: (0, bi)), full],
        out_specs=qblk(ATT_HEADS * LANES),
        out_shape=jax.ShapeDtypeStruct((b, t, ATT_HEADS * LANES), BF16),
        scratch_shapes=[pltpu.VMEM((pl.cdiv(t, KEY_CHUNK) * KEY_CHUNK, qb), I32), pltpu.VMEM((ATT_HEADS, qb), F32),
                        pltpu.VMEM((ATT_HEADS // 2, VT_ROWS, 2 * qb), F32),
                        pltpu.VMEM((ATT_HEADS, ATT_CHUNK, qb), F32), pltpu.VMEM((ATT_HEADS, ATT_CHUNK, qb), F32),
                        pltpu.VMEM((IDX_HEADS // 2, KEY_CHUNK, 2 * qb), F32),
                        pltpu.VMEM((IDX_HEADS // 2, KEY_CHUNK, 2 * qb), F32)],
        compiler_params=_cparams(("parallel", "arbitrary")),
        name="dsa_prompt",
    )(q, qi, kiw, kbf, vt, kiwbf)


def _dsa_sample_kernel(pt_ref, q_ref, qi_ref, kiwq_ref, kn_ref, vn_ref, kiwn_ref, ck_hbm, cv_hbm, cki_hbm,
                       o_ref, kbuf, vbuf, kibuf, s_ref, sems, *, ts, n_pages, topk):
    b = pl.program_id(0)
    past = n_pages * PAGE_SIZE

    streams = ((cki_hbm, kibuf), (ck_hbm, kbuf), (cv_hbm, vbuf))

    def page_copy(p, which):
        src, dst = streams[which]
        cols = pl.ds(pl.multiple_of(p * PAGE_SIZE, PAGE_SIZE), PAGE_SIZE)
        return pltpu.make_async_copy(src.at[pt_ref[b, p]], dst.at[:, cols], sems.at[which])

    for which in range(len(streams)):
        def start_page(p, _, which=which):
            page_copy(p, which).start()
            return 0

        lax.fori_loop(0, n_pages, start_page, 0)

    def wait_pages(which):
        def body(p, _):
            page_copy(p, which).wait()
            return 0
        lax.fori_loop(0, n_pages, body, 0)

    n_past_chunks = past // KEY_CHUNK
    n_chunks = n_past_chunks + 1
    qpos = lax.broadcasted_iota(I32, (ts, 1), 0)
    wq = _indexer_weights(kiwq_ref[0])
    qi_stack = _stack_heads(qi_ref[0], 0, IDX_HEADS, IDX_DIM)
    w_stack = jnp.concatenate([wq[:, hd:hd + 1] for hd in range(IDX_HEADS)], axis=0)

    def head_sum(x):
        acc = x[0:ts]
        for hd in range(1, IDX_HEADS):
            acc = acc + x[hd * ts:(hd + 1) * ts]
        return acc

    wait_pages(0)

    step_keys = math.gcd(past, ROW_SLAB)

    def past_scores(c, _):
        k0 = pl.multiple_of(c * step_keys, KEY_CHUNK)
        ktc = kibuf[:, pl.ds(k0, step_keys)].astype(BF16)
        acc = head_sum(jnp.maximum(_dot(qi_stack, ktc), 0.0) * w_stack)
        s_ref[:, pl.ds(k0, step_keys)] = _score_keys(acc + 0.0)
        return 0

    lax.fori_loop(0, past // step_keys, past_scores, 0)
    acc = head_sum(jnp.maximum(_dot(qi_stack, kiwn_ref[0, 0:IDX_DIM, :]), 0.0) * w_stack)
    kpos = lax.broadcasted_iota(I32, (ts, KEY_CHUNK), 1)
    s_ref[:, pl.ds(past, KEY_CHUNK)] = jnp.where(kpos <= qpos, _score_keys(acc + 0.0), INT_MIN)
    _row_select_to_bias(s_ref, n_chunks, (past + 1 + qpos).astype(F32), topk)

    wait_pages(1)
    wait_pages(2)

    def load_past(k0):
        return kbuf[:, pl.ds(k0, step_keys)].astype(BF16), vbuf[:, pl.ds(k0, step_keys)].astype(BF16)

    def load_new(k0):
        return kn_ref[0], vn_ref[0]

    segments = [(past // step_keys, step_keys, 0, load_past), (1, KEY_CHUNK, past, load_new)]
    _attend_all_heads(q_ref[0], segments, s_ref, o_ref)


def _dsa_sample(q, qi, kiw, knt, vnt, kiwnt, cache_kt, cache_vt, cache_kit, page_table):
    b, ts, _ = q.shape
    n_pages = page_table.shape[1]
    past = n_pages * PAGE_SIZE
    topk = min(TOPK_MAX, (past + ts) // 4)
    blk = lambda rows, width: pl.BlockSpec((1, rows, width), lambda bi, pt: (bi, 0, 0))
    hbm = pl.BlockSpec(memory_space=pl.ANY)
    grid_spec = pltpu.PrefetchScalarGridSpec(
        num_scalar_prefetch=1,
        grid=(b,),
        in_specs=[blk(ts, ATT_HEADS * LANES), blk(ts, IDX_HEADS * LANES), blk(ts, LANES),
                  blk(LANES, KEY_CHUNK), blk(LANES, KEY_CHUNK), blk(LANES, KEY_CHUNK), hbm, hbm, hbm],
        out_specs=blk(ts, ATT_HEADS * LANES),
        scratch_shapes=[pltpu.VMEM((LANES, past), F32), pltpu.VMEM((LANES, past), F32),
                        pltpu.VMEM((IDX_DIM, past), F32), pltpu.VMEM((ts, past + KEY_CHUNK), I32),
                        pltpu.SemaphoreType.DMA((3,))],
    )
    return pl.pallas_call(
        functools.partial(_dsa_sample_kernel, ts=ts, n_pages=n_pages, topk=topk),
        grid_spec=grid_spec,
        out_shape=jax.ShapeDtypeStruct((b, ts, ATT_HEADS * LANES), F32),
        compiler_params=_cparams(("arbitrary",)),
        name="dsa_sample",
    )(page_table, q, qi, kiw, knt, vnt, kiwnt, cache_kt, cache_vt, cache_kit)


def _conv_kernel(u_ref, halo_ref, buf_ref, bg_ref, w_ref, ya_ref):
    i = pl.program_id(1)
    u = u_ref[0]
    halo, buf = halo_ref[0], buf_ref[0]
    first = i == 0
    prev1 = jnp.where(first, buf[1:2], halo[7:8])
    prev2 = jnp.where(first, buf[0:1], halo[6:7])
    row = lax.broadcasted_iota(I32, u.shape, 0)
    um1 = jnp.where(row == 0, prev1, pltpu.roll(u, 1, 0))
    um2 = jnp.where(row == 0, prev2, jnp.where(row == 1, prev1, pltpu.roll(u, 2, 0)))
    w = w_ref[...]
    conv = w[0:1] * um2 + w[1:2] * um1 + w[2:3] * u
    ya_ref[0] = (bg_ref[0] * conv).astype(BF16)


def _conv(u, bg, buf, w, tt):
    b, t, c = u.shape
    halo_rows = 8
    tile = pl.BlockSpec((1, tt, c), lambda bi, i: (bi, i, 0))
    halo = pl.BlockSpec((1, halo_rows, c), lambda bi, i: (bi, jnp.maximum(i * (tt // halo_rows) - 1, 0), 0))
    return pl.pallas_call(
        _conv_kernel,
        grid=(b, t // tt),
        in_specs=[tile, halo, pl.BlockSpec((1, CONV_WIDTH - 1, c), lambda bi, i: (bi, 0, 0)), tile,
                  pl.BlockSpec((CONV_WIDTH, c), lambda bi, i: (0, 0))],
        out_specs=tile,
        out_shape=jax.ShapeDtypeStruct((b, t, c), BF16),
        compiler_params=_cparams(("parallel", "parallel")),
        name="short_conv",
    )(u, u, buf, bg, w)


def _mm_res_kernel(*refs, n_in):
    a_refs, w_ref, x_ref, o_ref = refs[:n_in], refs[n_in], refs[n_in + 1], refs[n_in + 2]
    a = jnp.concatenate([r[...] for r in a_refs], axis=1) if n_in > 1 else a_refs[0][...]
    o_ref[...] = x_ref[...] + _dot(a, w_ref[...])


def _mm_res(a_list, w, x, tm):
    m = x.shape[0]
    row = lambda i: (i, 0)
    return pl.pallas_call(
        functools.partial(_mm_res_kernel, n_in=len(a_list)),
        grid=(m // tm,),
        in_specs=[pl.BlockSpec((tm, a.shape[1]), row) for a in a_list]
        + [pl.BlockSpec(w.shape, lambda i: (0, 0)), pl.BlockSpec((tm, D_MODEL), row)],
        out_specs=pl.BlockSpec((tm, D_MODEL), row),
        out_shape=jax.ShapeDtypeStruct((m, D_MODEL), F32),
        compiler_params=_cparams(("parallel",)),
        name="matmul_residual",
    )(*a_list, w, x)


def _ret_in_kernel(x_ref, g_ref, w_ref, qk_ref, vg_ref):
    h = _rms(x_ref[...], g_ref[...]).astype(BF16)
    n_qk = qk_ref.shape[1]
    qk_ref[...] = _dot(h, w_ref[:, :n_qk])
    vg_ref[...] = _dot(h, w_ref[:, n_qk:]).astype(BF16)


def _ret_in_proj(x, g, w, tm):
    m, n = x.shape[0], w.shape[1]
    n_qk = 2 * RET_HEADS * RET_DK
    row = lambda i: (i, 0)
    const = lambda i: (0, 0)
    return pl.pallas_call(
        _ret_in_kernel,
        grid=(m // tm,),
        in_specs=[pl.BlockSpec((tm, D_MODEL), row), pl.BlockSpec((1, D_MODEL), const), pl.BlockSpec((D_MODEL, n), const)],
        out_specs=[pl.BlockSpec((tm, n_qk), row), pl.BlockSpec((tm, n - n_qk), row)],
        out_shape=[jax.ShapeDtypeStruct((m, n_qk), F32), jax.ShapeDtypeStruct((m, n - n_qk), BF16)],
        compiler_params=_cparams(("parallel",)),
        name="retention_in_proj",
    )(x, g, w)


def _ffn_kernel(x_ref, g_ref, wg_ref, wu_ref, wd_ref, o_ref, h_scr, acc):
    f = pl.program_id(1)

    @pl.when(f == 0)
    def _():
        x = x_ref[...]
        h_scr[...] = _rms(x, g_ref[...]).astype(BF16)
        acc[...] = x

    h = h_scr[...]
    a = jax.nn.silu(_dot(h, wg_ref[...])) * _dot(h, wu_ref[...])
    acc[...] += _dot(a.astype(BF16), wd_ref[...])

    @pl.when(f == pl.num_programs(1) - 1)
    def _():
        o_ref[...] = acc[...]


def _ffn(x, g, wg, wu, wd, tm, tf):
    m, ff = x.shape[0], wg.shape[1]
    row = lambda i, f: (i, 0)
    return pl.pallas_call(
        _ffn_kernel,
        grid=(m // tm, ff // tf),
        in_specs=[pl.BlockSpec((tm, D_MODEL), row), pl.BlockSpec((1, D_MODEL), lambda i, f: (0, 0)),
                  pl.BlockSpec((D_MODEL, tf), lambda i, f: (0, f)), pl.BlockSpec((D_MODEL, tf), lambda i, f: (0, f)),
                  pl.BlockSpec((tf, D_MODEL), lambda i, f: (f, 0))],
        out_specs=pl.BlockSpec((tm, D_MODEL), row),
        out_shape=jax.ShapeDtypeStruct((m, D_MODEL), F32),
        scratch_shapes=[pltpu.VMEM((tm, D_MODEL), BF16), pltpu.VMEM((tm, D_MODEL), F32)],
        compiler_params=_cparams(("parallel", "arbitrary")),
        name="dense_swiglu",
    )(x, g, wg, wu, wd)


def _top2_gates(logits):
    lane = lax.broadcasted_iota(I32, logits.shape, 1).astype(F32)
    neg = jnp.float32(-jnp.inf)
    l1 = jnp.where(lane < N_EXPERTS, logits, neg)
    m1 = jnp.max(l1, axis=1, keepdims=True)
    i1 = jnp.min(jnp.where(l1 == m1, lane, float(LANES)), axis=1, keepdims=True)
    l2 = jnp.where(lane == i1, neg, l1)
    m2 = jnp.max(l2, axis=1, keepdims=True)
    i2 = jnp.min(jnp.where(l2 == m2, lane, float(LANES)), axis=1, keepdims=True)
    e = jnp.exp(m2 - m1)
    w1 = 1.0 / (1.0 + e)
    w2 = e / (1.0 + e)
    first, second = lane == i1, lane == i2
    return jnp.where(first, w1, jnp.where(second, w2, 0.0)), jnp.where(first | second, 1.0, 0.0)


MOE_SUB = 144
MOE_TOKEN_TILE = 1024


def _moe_route_kernel(x_ref, g_ref, rhi_ref, rlo_ref, h_ref, gate_ref, posc_ref, posr_ref, cnt_ref):
    tm = x_ref.shape[0]
    hn = _rms(x_ref[...], g_ref[...])
    h_hi = hn.astype(BF16)
    h_lo = (hn - h_hi.astype(F32)).astype(BF16)
    logits = _dot(h_hi, rhi_ref[...]) + (_dot(h_lo, rhi_ref[...]) + _dot(h_hi, rlo_ref[...]))
    gate, routed = _top2_gates(logits)
    h_ref[...] = h_hi
    gate_ref[...] = gate
    earlier = (lax.broadcasted_iota(I32, (tm, tm), 0) > lax.broadcasted_iota(I32, (tm, tm), 1))
    slot = _dot(jnp.where(earlier, 1.0, 0.0).astype(BF16), routed.astype(BF16))
    posc = jnp.where(routed > 0.0, slot, -1.0)
    posc_ref[...] = posc
    posr_ref[0] = posc.T[0:N_EXPERTS]
    cnt_ref[0] = jnp.broadcast_to(jnp.sum(routed, axis=0, keepdims=True), (8, LANES))


def _moe_expert_kernel(cnt_ref, h_ref, gate_ref, posc_ref, posr_ref, wg_ref, wu_ref, wd_ref, o_ref, xg, yacc):
    i, e, f = pl.program_id(0), pl.program_id(1), pl.program_id(2)
    tm = h_ref.shape[0]
    n_sub = (cnt_ref[i * N_EXPERTS + e] + MOE_SUB - 1) // MOE_SUB

    @pl.when((e == 0) & (f == 0))
    def _():
        o_ref[...] = jnp.zeros(o_ref.shape, F32)

    @pl.when(f == 0)
    def _():
        posr = posr_ref[0]
        h = h_ref[...]

        def gather(s, _):
            base = pl.multiple_of(s * MOE_SUB, MOE_SUB)
            slot = (base + lax.broadcasted_iota(I32, (MOE_SUB, tm), 0)).astype(F32)
            onehot = jnp.where(posr == slot, 1.0, 0.0).astype(BF16)
            xg[pl.ds(base, MOE_SUB), :] = _dot(onehot, h).astype(BF16)
            yacc[pl.ds(base, MOE_SUB), :] = jnp.zeros((MOE_SUB, D_MODEL), F32)
            return 0

        lax.fori_loop(0, n_sub, gather, 0)

    def expert(s, _):
        rows = pl.ds(pl.multiple_of(s * MOE_SUB, MOE_SUB), MOE_SUB)
        xs = xg[rows, :]
        a = jax.nn.silu(_dot(xs, wg_ref[0])) * _dot(xs, wu_ref[0])
        yacc[rows, :] += _dot(a.astype(BF16), wd_ref[0])
        return 0

    lax.fori_loop(0, n_sub, expert, 0)

    @pl.when(f == pl.num_programs(2) - 1)
    def _():
        lane = lax.broadcasted_iota(I32, (tm, LANES), 1)
        mine = lane == e
        posc = jnp.sum(jnp.where(mine, posc_ref[...], 0.0), axis=1, keepdims=True)
        gate = jnp.sum(jnp.where(mine, gate_ref[...], 0.0), axis=1, keepdims=True)

        def scatter(s, _):
            base = pl.multiple_of(s * MOE_SUB, MOE_SUB)
            slot = (base + lax.broadcasted_iota(I32, (tm, MOE_SUB), 1)).astype(F32)
            onehot = jnp.where(posc == slot, 1.0, 0.0).astype(BF16)
            o_ref[...] += gate * _dot(onehot, yacc[pl.ds(base, MOE_SUB), :].astype(BF16))
            return 0

        lax.fori_loop(0, n_sub, scatter, 0)


def _moe(x, g, r_hi, r_lo, wg, wu, wd, tm, tf):
    m = x.shape[0]
    nt = m // tm
    row = lambda i: (i, 0)
    const = lambda i: (0, 0)
    h, gate, posc, posr, cnt = pl.pallas_call(
        _moe_route_kernel,
        grid=(nt,),
        in_specs=[pl.BlockSpec((tm, D_MODEL), row), pl.BlockSpec((1, D_MODEL), const),
                  pl.BlockSpec((D_MODEL, LANES), const), pl.BlockSpec((D_MODEL, LANES), const)],
        out_specs=[pl.BlockSpec((tm, D_MODEL), row), pl.BlockSpec((tm, LANES), row), pl.BlockSpec((tm, LANES), row),
                   pl.BlockSpec((1, N_EXPERTS, tm), lambda i: (i, 0, 0)), pl.BlockSpec((1, 8, LANES), lambda i: (i, 0, 0))],
        out_shape=[jax.ShapeDtypeStruct((m, D_MODEL), BF16), jax.ShapeDtypeStruct((m, LANES), F32),
                   jax.ShapeDtypeStruct((m, LANES), F32), jax.ShapeDtypeStruct((nt, N_EXPERTS, tm), F32),
                   jax.ShapeDtypeStruct((nt, 8, LANES), F32)],
        compiler_params=_cparams(("parallel",)),
        name="moe_route",
    )(x, g, r_hi, r_lo)
    counts = cnt[:, 0, :N_EXPERTS].astype(I32).reshape(nt * N_EXPERTS)
    posr = posr.reshape(nt * N_EXPERTS, 1, tm)
    row3 = lambda i, e, f, c: (i, 0)
    slot_rows = pl.cdiv(tm, MOE_SUB) * MOE_SUB
    grid_spec = pltpu.PrefetchScalarGridSpec(
        num_scalar_prefetch=1,
        grid=(nt, N_EXPERTS, EXPERT_FF // tf),
        in_specs=[pl.BlockSpec((tm, D_MODEL), row3),
                  pl.BlockSpec((tm, LANES), row3), pl.BlockSpec((tm, LANES), row3),
                  pl.BlockSpec((1, 1, tm), lambda i, e, f, c: (i * N_EXPERTS + e, 0, 0)),
                  pl.BlockSpec((1, D_MODEL, tf), lambda i, e, f, c: (e, 0, f)),
                  pl.BlockSpec((1, D_MODEL, tf), lambda i, e, f, c: (e, 0, f)),
                  pl.BlockSpec((1, tf, D_MODEL), lambda i, e, f, c: (e, f, 0))],
        out_specs=pl.BlockSpec((tm, D_MODEL), row3),
        scratch_shapes=[pltpu.VMEM((slot_rows, D_MODEL), BF16), pltpu.VMEM((slot_rows, D_MODEL), F32)],
    )
    return pl.pallas_call(
        _moe_expert_kernel,
        grid_spec=grid_spec,
        out_shape=jax.ShapeDtypeStruct((m, D_MODEL), F32),
        compiler_params=_cparams(("parallel", "arbitrary", "arbitrary")),
        name="moe_experts",
    )(counts, h, gate, posc, posr, wg, wu, wd)


def _ple_kernel(*refs, final_norm, n_addends):
    x_refs, (g_ref, p_ref, wp_ref, wgate_ref, gf_ref, o_ref) = refs[:n_addends], refs[n_addends:]
    x = x_refs[0][...]
    for r in x_refs[1:]:
        x = x + r[...]
    hp = _rms(x, g_ref[...]).astype(BF16)
    gate = jax.nn.sigmoid(_dot(hp, wgate_ref[...]))
    y = x + _dot(p_ref[...].astype(BF16), wp_ref[...]) * gate
    if final_norm:
        y = _rms(y, gf_ref[...])
    o_ref[...] = y


def _ple(xs, g, p, wp, wgate, g_final, final_norm, tm):
    m = xs[0].shape[0]
    row = lambda i: (i, 0)
    const = lambda i: (0, 0)
    return pl.pallas_call(
        functools.partial(_ple_kernel, final_norm=final_norm, n_addends=len(xs)),
        grid=(m // tm,),
        in_specs=[pl.BlockSpec((tm, D_MODEL), row)] * len(xs)
        + [pl.BlockSpec((1, D_MODEL), const), pl.BlockSpec((tm, PLE_DIM), row),
           pl.BlockSpec((PLE_DIM, D_MODEL), const), pl.BlockSpec((D_MODEL, D_MODEL), const),
           pl.BlockSpec((1, D_MODEL), const)],
        out_specs=pl.BlockSpec((tm, D_MODEL), row),
        out_shape=jax.ShapeDtypeStruct((m, D_MODEL), F32),
        compiler_params=_cparams(("parallel",)),
        name="per_layer_embedding",
    )(*xs, g, p, wp, wgate, g_final)


def _ret_kernel(q_ref, k_ref, v_ref, gate_ref, cos_ref, sin_ref, s0_ref, o_ref, sout_ref, state, *, chunk, chunk_rows):
    i = pl.program_id(1)

    @pl.when(i == 0)
    def _():
        state[...] = s0_ref[0]

    cos, sin = cos_ref[...], sin_ref[...]
    tt = q_ref.shape[1]
    r = chunk_rows
    ii = lax.broadcasted_iota(I32, (r, r), 0).astype(F32)
    jj = lax.broadcasted_iota(I32, (r, r), 1).astype(F32)
    rel = ii - jj
    icol = lax.broadcasted_iota(I32, (r, 1), 0).astype(F32)
    half = RET_DK // 2

    def rot(ref, hd):
        x1 = ref[0, :, hd * RET_DK:hd * RET_DK + half]
        x2 = ref[0, :, hd * RET_DK + half:(hd + 1) * RET_DK]
        return jnp.concatenate([x1 * cos - x2 * sin, x2 * cos + x1 * sin], axis=1)

    for hd in range(RET_HEADS):
        lg = math.log(1.0 - 2.0 ** (-5.0 - hd))
        d_in = jnp.where(rel >= 0, jnp.exp(lg * jnp.maximum(rel, 0.0)), 0.0)
        d_q = jnp.exp(lg * (icol + 1.0))
        d_k = jnp.exp(lg * (chunk - 1.0 - icol)) * (RET_DK ** -0.5)
        d_c = math.exp(lg * chunk)
        qr = rot(q_ref, hd)
        kr = rot(k_ref, hd)
        vsl = slice(hd * RET_DV, (hd + 1) * RET_DV)
        for c in range(tt // r):
            rows = slice(c * r, (c + 1) * r)
            qc = qr[rows].astype(BF16)
            kc = kr[rows]
            vc = v_ref[0, rows, vsl]
            s_prev = state[hd]
            att = _dot_nt(qc, (kc * (RET_DK ** -0.5)).astype(BF16)) * d_in
            o = _dot(att.astype(BF16), vc) + _dot(qc, s_prev.astype(BF16)) * d_q
            state[hd] = s_prev * d_c + _dot_tn((kc * d_k).astype(BF16), vc)
            mu = jnp.mean(o, axis=-1, keepdims=True)
            var = jnp.mean(jnp.square(o - mu), axis=-1, keepdims=True)
            on = (o - mu) * lax.rsqrt(var + EPS)
            o_ref[0, rows, vsl] = (jax.nn.silu(gate_ref[0, rows, vsl].astype(F32)) * on).astype(BF16)

    @pl.when(i == pl.num_programs(1) - 1)
    def _():
        sout_ref[0] = state[...]


def _retention(zqk, zvg, s0, cos, sin, tt, chunk, chunk_rows):
    b, t, _ = zqk.shape
    hk, hv = RET_HEADS * RET_DK, RET_HEADS * RET_DV
    half = RET_DK // 2
    state_spec = pl.BlockSpec((1, RET_HEADS, RET_DK, RET_DV), lambda bi, i: (bi, 0, 0, 0))
    tab = pl.BlockSpec((tt, half), lambda bi, i: (i, 0))
    return pl.pallas_call(
        functools.partial(_ret_kernel, chunk=chunk, chunk_rows=chunk_rows),
        grid=(b, t // tt),
        in_specs=[pl.BlockSpec((1, tt, hk), lambda bi, i: (bi, i, 0)), pl.BlockSpec((1, tt, hk), lambda bi, i: (bi, i, 1)),
                  pl.BlockSpec((1, tt, hv), lambda bi, i: (bi, i, 0)), pl.BlockSpec((1, tt, hv), lambda bi, i: (bi, i, 1)),
                  tab, tab, state_spec],
        out_specs=[pl.BlockSpec((1, tt, hv), lambda bi, i: (bi, i, 0)), state_spec],
        out_shape=[jax.ShapeDtypeStruct((b, t, hv), BF16), jax.ShapeDtypeStruct(s0.shape, F32)],
        scratch_shapes=[pltpu.VMEM((RET_HEADS, RET_DK, RET_DV), F32)],
        compiler_params=_cparams(("parallel", "arbitrary")),
        name="retention",
    )(zqk, zqk, zvg, zvg, cos, sin, s0)


def _pack_l0_w_in(w):
    offs = np.cumsum((0,) + AB_SPLITS)
    bg, cg, hv, q, k, v, qi, ki, wi = [w[:, offs[n]:offs[n + 1]] for n in range(len(AB_SPLITS))]
    group = ATT_HEADS // KV_HEADS
    q4 = q.reshape(D_MODEL, ATT_HEADS, HEAD_DIM)
    zq = jnp.zeros_like(q4)
    q_pad = jnp.concatenate([jnp.concatenate([q4[:, :group], zq[:, :group]], axis=-1),
                             jnp.concatenate([zq[:, group:], q4[:, group:]], axis=-1)], axis=1).reshape(D_MODEL, -1)
    qi4 = qi.reshape(D_MODEL, IDX_HEADS, IDX_DIM)
    qi_pad = jnp.concatenate([qi4, jnp.zeros_like(qi4)], axis=-1).reshape(D_MODEL, -1)
    kiw = jnp.concatenate([ki, wi, jnp.zeros((D_MODEL, LANES - IDX_DIM - IDX_HEADS), w.dtype)], axis=1)
    return jnp.concatenate([bg, cg, hv, q_pad, k, v, qi_pad, kiw], axis=1).astype(BF16)


def _pack_l0_w_out(w):
    group = ATT_HEADS // KV_HEADS
    wa, wb = w[:CONV_CH], w[CONV_CH:].reshape(ATT_HEADS, HEAD_DIM, D_MODEL)
    zb = jnp.zeros_like(wb)
    wb_pad = jnp.concatenate([jnp.concatenate([wb[:group], zb[:group]], axis=1),
                              jnp.concatenate([zb[group:], wb[group:]], axis=1)], axis=0).reshape(-1, D_MODEL)
    return jnp.concatenate([wa, wb_pad], axis=0).astype(BF16)


def _rope_tables(pos, reps):
    inv = ROPE_THETA ** (-jnp.arange(0, HEAD_DIM, 2, dtype=F32) / HEAD_DIM)
    ang = pos.astype(F32)[:, None] * inv[None, :]
    cos, sin = jnp.cos(ang), jnp.sin(ang)
    cos64 = jnp.concatenate([cos, cos], axis=1)
    sin64 = jnp.concatenate([-sin, sin], axis=1)
    one, zero = jnp.ones_like(cos64), jnp.zeros_like(cos64)
    tabs = (jnp.concatenate([cos64, cos64], 1), jnp.concatenate([sin64, sin64], 1),
            jnp.concatenate([cos64, one], 1), jnp.concatenate([sin64, zero], 1))
    return tuple(jnp.tile(tb, (reps, 1)) for tb in tabs)


def _ret_tables(pos):
    inv = ROPE_THETA ** (-jnp.linspace(0.0, 1.0, RET_DK // 2, dtype=F32))
    ang = pos.astype(F32)[:, None] * inv[None, :]
    return jnp.cos(ang), jnp.sin(ang)


def _pack_params(prm):
    r = prm['moe_router'][0]
    r_pad = jnp.concatenate([r, jnp.zeros((D_MODEL, LANES - N_EXPERTS), F32)], axis=1)
    r_hi = r_pad.astype(BF16)
    bf = lambda a: a.astype(BF16)
    return dict(
        l0_w_in=_pack_l0_w_in(prm['ab_w_in'][0]), l0_w_out=_pack_l0_w_out(prm['ab_w_out'][0]),
        l0_w_vt=bf(prm['ab_w_in'][0][:, sum(AB_SPLITS[:5]):sum(AB_SPLITS[:6])].T),
        conv_w=prm['ab_conv_w'][0],
        ffn=(bf(prm['ffn_w_gate'][0]), bf(prm['ffn_w_up'][0]), bf(prm['ffn_w_down'][0])),
        ret_w_in=bf(prm['ret_w_in'][0]), ret_w_out=bf(prm['ret_w_out'][0]),
        r_hi=r_hi, r_lo=(r_pad - r_hi.astype(F32)).astype(BF16),
        moe=(bf(prm['moe_w_gate'][0]), bf(prm['moe_w_up'][0]), bf(prm['moe_w_down'][0])),
        ple_w=bf(prm['ple_w']), ple_gate_w=bf(prm['ple_gate_w']),
        norm_mix=prm['norm_mix'][:, None, :], norm_ffn=prm['norm_ffn'][:, None, :],
        norm_ple=prm['norm_ple'][:, None, :], norm_final=prm['norm_final'][None, :],
    )


def _trunk(x, p, pos, conv_buf, ret_state, pk, paged):
    b, t, _ = x.shape
    m = b * t
    tm = min(m, TOKEN_TILE)
    moe_tm = min(m, MOE_TOKEN_TILE)
    assert m % tm == 0 and m % moe_tm == 0 and m % min(m, 256) == 0, "token count must divide into whole tiles"
    assert paged is not None or (t % QUERY_BLOCK == 0 and t % tm == 0), "prompt length must divide into whole tiles"
    assert paged is None or (tm % t == 0 and t <= 8), "sample sequences are one sublane tile long at most"
    x2 = x.reshape(m, D_MODEL)

    reps = max(1, tm // t)
    tabs = _rope_tables(pos, reps)
    bg, u, q, k, v, qi, kiw, kbf, vbf, kiwbf, vt = _l0_in_proj(x2, pk['norm_mix'][0], pk['l0_w_in'], pk['l0_w_vt'], tabs,
                                                              tm, tabs[0].shape[0] // tm, head_major=paged is None)
    seq = lambda a: a.reshape(b, t, a.shape[-1])
    if paged is None:
        yb = _dsa_prompt(q, qi, seq(kiw), seq(kbf), vt, seq(kiwbf))
    else:
        cache_k, cache_v, cache_ki, page_table = paged
        padt = lambda a: jnp.swapaxes(jnp.pad(seq(a), ((0, 0), (0, KEY_CHUNK - t), (0, 0))), 1, 2)
        n_pool = cache_k.shape[0]
        pages_t = lambda c: jnp.transpose(c, (0, 2, 3, 1)).reshape(n_pool, LANES, PAGE_SIZE)
        yb = _dsa_sample(seq(q).astype(F32), seq(qi).astype(F32), seq(kiw), padt(kbf), padt(vbf), padt(kiwbf),
                         pages_t(cache_k), pages_t(cache_v), jnp.swapaxes(cache_ki, 1, 2), page_table).astype(BF16)
    u3 = seq(u)
    ya = _conv(u3, seq(bg), conv_buf, pk['conv_w'], tt=min(t, TOKEN_TILE))
    x2 = _mm_res([ya.reshape(m, CONV_CH), yb.reshape(m, ATT_HEADS * LANES)], pk['l0_w_out'], x2, tm)
    x2 = _ffn(x2, pk['norm_ffn'][0], *pk['ffn'], tm=tm, tf=D_FF // 2)
    x2 = _ple([x2], pk['norm_ple'][0], p[0].reshape(m, PLE_DIM), pk['ple_w'][0], pk['ple_gate_w'][0],
              pk['norm_final'], False, tm)
    new_k = k.reshape(1, b, t, KV_HEADS, HEAD_DIM)
    new_v = v.reshape(1, b, t, KV_HEADS, HEAD_DIM)
    new_ki = seq(kiw)[None, :, :, :IDX_DIM]
    new_conv = jnp.concatenate([conv_buf, u3], axis=1)[None, :, -(CONV_WIDTH - 1):]

    zqk, zvg = _ret_in_proj(x2, pk['norm_mix'][1], pk['ret_w_in'], min(m, 256))
    zqk, zvg = zqk.reshape(b, t, -1), zvg.reshape(b, t, -1)
    cos_r, sin_r = _ret_tables(pos)
    if t % RET_CHUNK == 0:
        og, s_new = _retention(zqk, zvg, ret_state, cos_r, sin_r, tt=2 * RET_CHUNK, chunk=RET_CHUNK,
                               chunk_rows=RET_CHUNK)
    else:
        rows = 16
        padt = lambda a: jnp.pad(a, ((0, 0),) * (a.ndim - 2) + ((0, rows - t), (0, 0)))
        og, s_new = _retention(padt(zqk), padt(zvg), ret_state, padt(cos_r), padt(sin_r), tt=rows, chunk=t,
                               chunk_rows=rows)
        og = og[:, :t]
    x2 = _mm_res([og.reshape(m, RET_HEADS * RET_DV)], pk['ret_w_out'], x2, tm)
    moe = _moe(x2, pk['norm_ffn'][1], pk['r_hi'], pk['r_lo'], *pk['moe'], tm=moe_tm, tf=EXPERT_FF // 2)
    x2 = _ple([x2, moe], pk['norm_ple'][1], p[1].reshape(m, PLE_DIM), pk['ple_w'][1], pk['ple_gate_w'][1],
              pk['norm_final'], True, tm)
    return x2.reshape(b, t, D_MODEL), new_k, new_v, new_ki, new_conv, s_new[None]


def kernel(x_prompt, x_sample, cache_k, cache_v, cache_kidx, state_conv, state_ret, page_table, p_prompt, p_sample,
           norm_mix, norm_ffn, norm_ple, norm_final, ab_w_in, ab_conv_w, ab_w_out, ffn_w_gate, ffn_w_up, ffn_w_down,
           ret_w_in, ret_w_out, moe_router, moe_w_gate, moe_w_up, moe_w_down, ple_w, ple_gate_w):
    prm = dict(norm_mix=norm_mix, norm_ffn=norm_ffn, norm_ple=norm_ple, norm_final=norm_final, ab_w_in=ab_w_in,
               ab_conv_w=ab_conv_w, ab_w_out=ab_w_out, ffn_w_gate=ffn_w_gate, ffn_w_up=ffn_w_up, ffn_w_down=ffn_w_down,
               ret_w_in=ret_w_in, ret_w_out=ret_w_out, moe_router=moe_router, moe_w_gate=moe_w_gate,
               moe_w_up=moe_w_up, moe_w_down=moe_w_down, ple_w=ple_w, ple_gate_w=ple_gate_w)
    pk = _pack_params(prm)
    b, t = x_prompt.shape[0], x_prompt.shape[1]
    db, ts = x_sample.shape[0], x_sample.shape[1]
    past_len = page_table.shape[1] * PAGE_SIZE
    dt = x_prompt.dtype

    conv0 = jnp.zeros((b, CONV_WIDTH - 1, CONV_CH), dt)
    ret0 = jnp.zeros((b, RET_HEADS, RET_DK, RET_DV), dt)
    y_p, k_p, v_p, ki_p, cb_p, rs_p = _trunk(x_prompt, p_prompt, jnp.arange(t, dtype=I32), conv0, ret0, pk, None)

    pos_s = past_len + jnp.arange(ts, dtype=I32)
    paged = (cache_k[0], cache_v[0], cache_kidx[0], page_table)
    y_s, k_s, v_s, ki_s, cb_s, rs_s = _trunk(x_sample, p_sample, pos_s, state_conv[0], state_ret[0], pk, paged)
    return (y_p, y_s, k_p, v_p, ki_p, cb_p, rs_p, k_s, v_s, ki_s, cb_s, rs_s)
```

```python
import functools
import math

import jax
import jax.numpy as jnp
import numpy as np
from jax import lax
from jax.experimental import pallas as pl
from jax.experimental.pallas import tpu as pltpu

F32 = jnp.float32
BF16 = jnp.bfloat16
I32 = jnp.int32

D_MODEL = 1024
PAGE_SIZE = 128
CONV_CH = D_MODEL // 2
CONV_WIDTH = 3
ATT_HEADS = 8
KV_HEADS = 2
HEAD_DIM = 64
IDX_HEADS = 4
IDX_DIM = 64
TOPK_MAX = 256
ROPE_THETA = 10000.0
RET_HEADS = 4
RET_DK = D_MODEL // RET_HEADS
RET_DV = 2 * RET_DK
RET_CHUNK = 128
D_FF = 2816
N_EXPERTS = 8
EXPERT_FF = 3584
PLE_DIM = 256
EPS = 1e-6
AB_SPLITS = (CONV_CH, CONV_CH, CONV_CH, ATT_HEADS * HEAD_DIM, KV_HEADS * HEAD_DIM, KV_HEADS * HEAD_DIM,
             IDX_HEADS * IDX_DIM, IDX_DIM, IDX_HEADS)

LANES = 128
VMEM_LIMIT = 48 * 1024 * 1024
INT_MIN = -2 ** 31
NEG_BIG = -1e30
LOG2_E = math.log2(math.e)

C_BG, C_CG, C_HV, C_Q, C_K, C_V, C_QI, C_KIW, C_END = 0, 512, 1024, 1536, 2560, 2688, 2816, 3328, 3456
KEY_CHUNK = 512
VT_ROWS = LANES + 16
TOKEN_TILE = 512
QUERY_BLOCK = 128


def _cparams(sem):
    return pltpu.CompilerParams(dimension_semantics=sem, vmem_limit_bytes=VMEM_LIMIT)


def _rms(x, g):
    return x * lax.rsqrt(jnp.mean(x * x, axis=-1, keepdims=True) + EPS) * g


def _dot(a, b):
    return jnp.dot(a, b, preferred_element_type=F32)


def _dot_nt(a, b):
    return lax.dot_general(a, b, (((1,), (1,)), ((), ())), preferred_element_type=F32)


def _dot_tn(a, b):
    return lax.dot_general(a, b, (((0,), (0,)), ((), ())), preferred_element_type=F32)


def _swap_halves64(x):
    lane = lax.broadcasted_iota(I32, x.shape, 1)
    from_above = pltpu.roll(x, LANES - 32, 1)
    from_below = pltpu.roll(x, 32, 1)
    return jnp.where((lane & 63) < 32, from_above, from_below)


def _l0_in_kernel(x_ref, g_ref, w_ref, wvt_ref, cos_ref, sin_ref, cosb_ref, sinb_ref,
                  bg_ref, u_ref, q_ref, k_ref, v_ref, qi_ref, kiw_ref, kbf_ref, vbf_ref, kiwbf_ref, vt_ref, *, head_major):
    h = _rms(x_ref[...], g_ref[...]).astype(BF16)
    cos, sin = cos_ref[...], sin_ref[...]

    def store_head(ref, hd, val):
        if head_major:
            for r in range(ref.shape[0]):
                ref[r, hd] = val[r * QUERY_BLOCK:(r + 1) * QUERY_BLOCK]
        else:
            ref[:, hd * LANES:(hd + 1) * LANES] = val

    def rot(z, c, s):
        return z * c + _swap_halves64(z) * s

    bg_ref[...] = _dot(h, w_ref[:, C_BG:C_CG])
    u_ref[...] = _dot(h, w_ref[:, C_CG:C_HV]) * _dot(h, w_ref[:, C_HV:C_Q])
    zq = _dot(h, w_ref[:, C_Q:C_K])
    for hd in range(ATT_HEADS):
        sl = slice(hd * LANES, (hd + 1) * LANES)
        store_head(q_ref, hd, (rot(zq[:, sl], cos, sin) * (HEAD_DIM ** -0.5 * LOG2_E)).astype(BF16))
    k = rot(_dot(h, w_ref[:, C_K:C_V]), cos, sin)
    k_ref[...] = k
    kbf_ref[...] = k.astype(BF16)
    v = _dot(h, w_ref[:, C_V:C_QI])
    v_ref[...] = v
    vbf_ref[...] = v.astype(BF16)
    vt_ref[0:LANES, :] = _dot_nt(wvt_ref[...], h).astype(BF16)
    vt_ref[LANES:VT_ROWS, :] = jnp.ones((VT_ROWS - LANES, vt_ref.shape[1]), BF16)
    zqi = _dot(h, w_ref[:, C_QI:C_KIW])
    for hd in range(IDX_HEADS):
        sl = slice(hd * LANES, (hd + 1) * LANES)
        store_head(qi_ref, hd, rot(zqi[:, sl], cos, sin).astype(BF16))
    kiw = rot(_dot(h, w_ref[:, C_KIW:C_END]), cosb_ref[...], sinb_ref[...])
    kiw_ref[...] = kiw
    kiwbf_ref[...] = kiw.astype(BF16)


def _l0_in_proj(x2, g, w, wvt, tabs, tm, n_tab_blocks, head_major):
    m = x2.shape[0]
    row = lambda i: (i, 0)
    const = lambda i: (0, 0)
    tab = lambda i: (i % n_tab_blocks, 0)
    widths = (512, 512, 1024, 128, 128, 512, 128, 128, 128, 128)
    dtypes = (F32, F32, BF16, F32, F32, BF16, F32, BF16, BF16, BF16)
    out_specs = [pl.BlockSpec((tm, wd), row) for wd in widths]
    out_shape = [jax.ShapeDtypeStruct((m, wd), dt) for wd, dt in zip(widths, dtypes)]
    if head_major:
        for n in (2, 5):
            heads = widths[n] // LANES
            out_specs[n] = pl.BlockSpec((tm // QUERY_BLOCK, heads, QUERY_BLOCK, LANES), lambda i: (i, 0, 0, 0))
            out_shape[n] = jax.ShapeDtypeStruct((m // QUERY_BLOCK, heads, QUERY_BLOCK, LANES), BF16)
    return pl.pallas_call(
        functools.partial(_l0_in_kernel, head_major=head_major),
        grid=(m // tm,),
        in_specs=[pl.BlockSpec((tm, D_MODEL), row), pl.BlockSpec((1, D_MODEL), const),
                  pl.BlockSpec((D_MODEL, C_END), const), pl.BlockSpec((LANES, D_MODEL), const)]
        + [pl.BlockSpec((tm, LANES), tab)] * 4,
        out_specs=out_specs + [pl.BlockSpec((VT_ROWS, tm), lambda i: (0, i))],
        out_shape=out_shape + [jax.ShapeDtypeStruct((VT_ROWS, m), BF16)],
        compiler_params=_cparams(("parallel",)),
        name="l0_in_proj",
    )(x2, g, w, wvt, *tabs)


def _score_keys(score):
    bits = pltpu.bitcast(score, I32)
    return bits ^ ((bits >> 31) & jnp.int32(0x7FFFFFFF))


def _key_scores(key):
    return pltpu.bitcast(key ^ ((key >> 31) & jnp.int32(0x7FFFFFFF)), F32)


SEARCH_PLAIN_STEPS = 11
SEARCH_SNAP_INTERP_STEPS = 8
SEARCH_MAX_STEPS = SEARCH_PLAIN_STEPS + SEARCH_SNAP_INTERP_STEPS + 34


NO_TIE_LIMIT = 1e9


def _topk_threshold(probe, vmin, vmax, n_real, topk):
    k = float(topk)
    lo0, hi0 = _score_keys(vmin), _score_keys(vmax) + 1
    take_all = n_real <= k
    zero, one = jnp.zeros_like(vmin), jnp.ones_like(vmin)
    active0 = jnp.where(jnp.logical_not(take_all) & (lo0 + 1 < hi0), 1.0, 0.0)

    def step(snap, st):
        it, lo, hi, c_lo, c_hi, w_lo, w_hi, last, active = st
        act = active > 0.0
        f_lo = (c_lo - (k - 0.5)) * w_lo
        f_hi = ((k - 0.5) - c_hi) * w_hi
        v_lo, v_hi = _key_scores(lo), _key_scores(hi)
        g = _score_keys(v_lo + (v_hi - v_lo) * (f_lo / (f_lo + f_hi)))
        if snap:
            mid = (lo >> 1) + (hi >> 1) + (lo & hi & 1)
            g = jnp.where(it < SEARCH_PLAIN_STEPS + SEARCH_SNAP_INTERP_STEPS, g, mid)
        g = jnp.minimum(jnp.maximum(g, lo + 1), hi - 1)
        c, key_up, key_dn = probe(g, snap)
        hit = act & (c == k)
        up = act & (c > k)
        dn = act & (c < k)
        lo = jnp.where(hit, g, jnp.where(up, key_up if snap else g, lo))
        c_lo = jnp.where(hit | up, c, c_lo)
        hi = jnp.where(dn, key_dn + 1 if snap else g, hi)
        c_hi = jnp.where(dn, c, c_hi)
        w_hi = jnp.where(up, jnp.where(last > 0.0, w_hi * 0.5, one), jnp.where(dn, one, w_hi))
        w_lo = jnp.where(dn, jnp.where(last < 0.0, w_lo * 0.5, one), jnp.where(up, one, w_lo))
        last = jnp.where(up, one, jnp.where(dn, -one, last))
        active = jnp.where(act & jnp.logical_not(hit) & (lo + 1 < hi), 1.0, 0.0)
        return it + 1, lo, hi, c_lo, c_hi, w_lo, w_hi, last, active

    def cond(limit, st):
        return (st[0] < limit) & (jnp.max(st[-1]) > 0.0)

    st = (jnp.int32(0), lo0, hi0, n_real, zero, one, one, zero, active0)
    st = lax.fori_loop(0, SEARCH_PLAIN_STEPS, lambda _, s: step(False, s), st)
    st = lax.while_loop(functools.partial(cond, SEARCH_MAX_STEPS), functools.partial(step, True), st)
    _, lo, _, c_lo, c_hi, _, _, _, _ = st
    thr = jnp.where(take_all, INT_MIN, lo)
    excess = jnp.logical_not(take_all) & (c_lo > k)
    ties = jnp.where(excess, k - c_hi, jnp.where(take_all, 0.0, NO_TIE_LIMIT))
    return thr, ties, jnp.max(jnp.where(excess, 1.0, 0.0)) > 0.0


def _tie_prefix_matrix(lower):
    r = lax.broadcasted_iota(I32, (KEY_CHUNK, KEY_CHUNK), 0)
    c = lax.broadcasted_iota(I32, (KEY_CHUNK, KEY_CHUNK), 1)
    return jnp.where((r >= c) if lower else (r <= c), 1.0, 0.0).astype(BF16)


def _bias_bits(sel):
    return pltpu.bitcast(jnp.where(sel, 0.0, NEG_BIG).astype(F32), I32)


ROW_SLAB = 2048


def _lane_fold(m, op=jnp.add):
    parts = [m[:, c * LANES:(c + 1) * LANES] for c in range(m.shape[1] // LANES)]
    while len(parts) > 1:
        parts = [op(parts[n], parts[n + 1]) for n in range(0, len(parts) - 1, 2)] + (parts[-1:] if len(parts) % 2 else [])
    return parts[0]


def _row_slabs(n_chunks):
    width = n_chunks * KEY_CHUNK
    return [(k0, min(ROW_SLAB, width - k0)) for k0 in range(0, width, ROW_SLAB)]


def _int_reduce(x, take_min, axis):
    red = jnp.min if take_min else jnp.max
    hi = (x >> 16).astype(F32)
    lo = (x & 0xFFFF).astype(F32)
    m_hi = red(hi, axis=axis, keepdims=True)
    m_lo = red(jnp.where(hi == m_hi, lo, 65536.0 if take_min else -1.0), axis=axis, keepdims=True)
    return (m_hi.astype(I32) << 16) | m_lo.astype(I32)


def _row_probe(s_ref, n_chunks, g, snap):
    cnt = up = dn = None
    for k0, size in _row_slabs(n_chunks):
        blk = s_ref[:, k0:k0 + size]
        ge = blk >= g
        part = _lane_fold(jnp.where(ge, 1.0, 0.0))
        cnt = part if cnt is None else cnt + part
        if snap:
            above = _lane_fold(jnp.where(ge, blk, 2 ** 31 - 1), jnp.minimum)
            below = _lane_fold(jnp.where(ge, INT_MIN, blk), jnp.maximum)
            up = above if up is None else jnp.minimum(up, above)
            dn = below if dn is None else jnp.maximum(dn, below)
    c = jnp.sum(cnt, axis=1, keepdims=True)
    if not snap:
        return c, None, None
    return c, _int_reduce(up, True, 1), _int_reduce(dn, False, 1)


def _row_select_to_bias(s_ref, n_chunks, n_real, topk):
    inf = jnp.float32(jnp.inf)
    mn = mx = None
    for k0, size in _row_slabs(n_chunks):
        blk = s_ref[:, k0:k0 + size]
        v = _key_scores(blk)
        real = blk != INT_MIN
        lo_part = _lane_fold(jnp.where(real, v, inf), jnp.minimum)
        hi_part = _lane_fold(jnp.where(real, v, -inf), jnp.maximum)
        mn = lo_part if mn is None else jnp.minimum(mn, lo_part)
        mx = hi_part if mx is None else jnp.maximum(mx, hi_part)
    vmin, vmax = jnp.min(mn, axis=1, keepdims=True), jnp.max(mx, axis=1, keepdims=True)
    thr, ties, any_excess = _topk_threshold(functools.partial(_row_probe, s_ref, n_chunks), vmin, vmax, n_real, topk)

    def plain(_):
        thr_ge = jnp.maximum(thr, INT_MIN + 1)
        for k0, size in _row_slabs(n_chunks):
            s_ref[:, k0:k0 + size] = _bias_bits(s_ref[:, k0:k0 + size] >= thr_ge)
        return 0

    def with_ties(_):
        upper = _tie_prefix_matrix(lower=False)
        seen = jnp.zeros_like(ties)
        for c in range(n_chunks):
            blk = s_ref[:, c * KEY_CHUNK:(c + 1) * KEY_CHUNK]
            tie = jnp.where(blk == thr, 1.0, 0.0)
            rows = tie.shape[0]
            tie16 = jnp.concatenate([tie, tie], axis=0).astype(BF16) if rows % 16 else tie.astype(BF16)
            rank = _dot(tie16, upper)[0:rows] + seen
            s_ref[:, c * KEY_CHUNK:(c + 1) * KEY_CHUNK] = _bias_bits((blk > thr) | ((tie > 0.0) & (rank <= ties)))
            seen = seen + jnp.sum(tie, axis=1, keepdims=True)
        return 0

    lax.cond(any_excess, with_ties, plain, 0)


def _attend_group(qg, segments, bias_ref, rep):
    n = qg.shape[0]

    def step(chunk, base, load_kv, c, carry):
        m, l, acc = carry
        k0 = pl.multiple_of(base + c * chunk, KEY_CHUNK)
        ktc, vtc = load_kv(k0)
        bias = pltpu.bitcast(bias_ref[:, pl.ds(k0, chunk)], F32)
        s = _dot(qg, ktc) + jnp.concatenate([bias] * rep, axis=0)
        m_new = jnp.maximum(m, jnp.max(s, axis=1, keepdims=True))
        alpha = jnp.exp2(m - m_new)
        p = jnp.exp2(s - m_new)
        l = alpha * l + jnp.sum(p, axis=1, keepdims=True)
        acc = alpha * acc + _dot_nt(p.astype(BF16), vtc)
        return m_new, l, acc

    carry = (jnp.full((n, 1), NEG_BIG, F32), jnp.zeros((n, 1), F32), jnp.zeros((n, LANES), F32))
    for n_steps, chunk, base, load_kv in segments:
        carry = lax.fori_loop(0, n_steps, functools.partial(step, chunk, base, load_kv), carry)
    _, l, acc = carry
    return acc / l


def _stack_heads(x, first, count, width=LANES):
    x = x.astype(F32)
    return jnp.concatenate([x[:, (first + hd) * LANES:(first + hd) * LANES + width] for hd in range(count)],
                           axis=0).astype(BF16)


def _attend_all_heads(q, segments, bias_ref, o_ref):
    rows = q.shape[0]
    group = ATT_HEADS // KV_HEADS
    out = _attend_group(_stack_heads(q, 0, ATT_HEADS), segments, bias_ref, ATT_HEADS)
    for hd in range(ATT_HEADS):
        piece = out[hd * rows:(hd + 1) * rows]
        piece = jnp.where(_group_lane_mask(piece.shape, hd // group), piece, 0.0)
        o_ref[0, :, hd * LANES:(hd + 1) * LANES] = piece.astype(o_ref.dtype)


def _indexer_weights(kiw_q):
    return kiw_q[:, IDX_DIM:IDX_DIM + IDX_HEADS] * (IDX_HEADS ** -0.5 * IDX_DIM ** -0.5)


def _group_lane_mask(shape, g):
    lane = lax.broadcasted_iota(I32, shape, 1)
    return (lane >= g * HEAD_DIM) & (lane < (g + 1) * HEAD_DIM)


ATT_CHUNK = 256


def _sublane_fold(m, op=jnp.add):
    parts = [m[r * 8:(r + 1) * 8] for r in range(m.shape[0] // 8)]
    while len(parts) > 1:
        parts = [op(parts[n], parts[n + 1]) for n in range(0, len(parts) - 1, 2)] + (parts[-1:] if len(parts) % 2 else [])
    return parts[0]


def _col_reduce(m, op, reduce_fn):
    return reduce_fn(_sublane_fold(m, op), axis=0, keepdims=True)


def _col_probe(s_ref, n_chunks, g, snap):
    cols = s_ref.shape[1]

    def body(c, carry):
        k0 = pl.multiple_of(c * KEY_CHUNK, KEY_CHUNK)
        blk = s_ref[pl.ds(k0, KEY_CHUNK), :]
        ge = blk >= g
        out = [carry[0] + _sublane_fold(jnp.where(ge, 1.0, 0.0))]
        if snap:
            out.append(jnp.minimum(carry[1], _sublane_fold(jnp.where(ge, blk, 2 ** 31 - 1), jnp.minimum)))
            out.append(jnp.maximum(carry[2], _sublane_fold(jnp.where(ge, INT_MIN, blk), jnp.maximum)))
        return tuple(out)

    init = [jnp.zeros((8, cols), F32)]
    if snap:
        init += [jnp.full((8, cols), 2 ** 31 - 1, I32), jnp.full((8, cols), INT_MIN, I32)]
    res = lax.fori_loop(0, n_chunks, body, tuple(init))
    c = jnp.sum(res[0], axis=0, keepdims=True)
    if not snap:
        return c, None, None
    return c, _int_reduce(res[1], True, 0), _int_reduce(res[2], False, 0)


def _col_select_to_bias(s_ref, n_chunks, n_real, topk, vmin, vmax):
    thr, ties, any_excess = _topk_threshold(functools.partial(_col_probe, s_ref, n_chunks), vmin, vmax, n_real, topk)

    def plain(_):
        thr_ge = jnp.maximum(thr, INT_MIN + 1)

        def body(c, _):
            k0 = pl.multiple_of(c * KEY_CHUNK, KEY_CHUNK)
            s_ref[pl.ds(k0, KEY_CHUNK), :] = _bias_bits(s_ref[pl.ds(k0, KEY_CHUNK), :] >= thr_ge)
            return 0

        return lax.fori_loop(0, n_chunks, body, 0)

    def with_ties(_):
        lower = _tie_prefix_matrix(lower=True)

        def body(c, seen):
            k0 = pl.multiple_of(c * KEY_CHUNK, KEY_CHUNK)
            blk = s_ref[pl.ds(k0, KEY_CHUNK), :]
            tie = jnp.where(blk == thr, 1.0, 0.0)
            rank = _dot(lower, tie.astype(BF16)) + seen
            s_ref[pl.ds(k0, KEY_CHUNK), :] = _bias_bits((blk > thr) | ((tie > 0.0) & (rank <= ties)))
            return seen + jnp.sum(_sublane_fold(tie), axis=0, keepdims=True)

        lax.fori_loop(0, n_chunks, body, jnp.zeros_like(ties))
        return 0

    lax.cond(any_excess, with_ties, plain, 0)


def _dsa_prompt_kernel(q_ref, qi_ref, kiwq_ref, k_ref, vt_ref, kiw_ref, o_ref, s_ref, m_scr, acc_scr,
                       sa_scr, sb_scr, ia_scr, ib_scr, *, qb, topk):
    j = pl.program_id(1)
    n_keys = j * qb + qb
    n_chunks = (n_keys + KEY_CHUNK - 1) // KEY_CHUNK
    qpos = j * qb + lax.broadcasted_iota(I32, (1, qb), 1)
    wt = kiwq_ref[0].T[IDX_DIM:IDX_DIM + IDX_HEADS] * (IDX_HEADS ** -0.5 * IDX_DIM ** -0.5)

    inf = jnp.float32(jnp.inf)

    last_chunk = s_ref.shape[0] // KEY_CHUNK - 1

    def index_dots(c, buf):
        k0 = pl.multiple_of(jnp.minimum(c, last_chunk) * KEY_CHUNK, KEY_CHUNK)
        kc = kiw_ref[0, pl.ds(k0, KEY_CHUNK), :]
        for pr in range(IDX_HEADS // 2):
            buf[pr] = _dot_nt(kc, qi_ref[0, 2 * pr:2 * pr + 2].reshape(2 * qb, LANES))

    def score_chunk(c, buf, carry):
        mn, mx = carry
        k0 = pl.multiple_of(jnp.minimum(c, last_chunk) * KEY_CHUNK, KEY_CHUNK)
        acc = jnp.zeros((KEY_CHUNK, qb), F32)
        for pr in range(IDX_HEADS // 2):
            acc = acc + jnp.maximum(buf[pr, :, 0:qb], 0.0) * wt[2 * pr:2 * pr + 1]
            acc = acc + jnp.maximum(buf[pr, :, qb:2 * qb], 0.0) * wt[2 * pr + 1:2 * pr + 2]
        causal = k0 + lax.broadcasted_iota(I32, (KEY_CHUNK, qb), 0) <= qpos
        s_ref[pl.ds(k0, KEY_CHUNK), :] = jnp.where(causal, _score_keys(acc), INT_MIN)
        mn = jnp.minimum(mn, _sublane_fold(jnp.where(causal, acc, inf), jnp.minimum))
        mx = jnp.maximum(mx, _sublane_fold(jnp.where(causal, acc, -inf), jnp.maximum))
        return mn, mx

    index_dots(0, ia_scr)

    def scores(t, carry):
        index_dots(2 * t + 1, ib_scr)
        carry = score_chunk(2 * t, ia_scr, carry)
        index_dots(2 * t + 2, ia_scr)
        return score_chunk(2 * t + 1, ib_scr, carry)

    mn, mx = lax.fori_loop(0, (n_chunks + 1) // 2, scores,
                           (jnp.full((8, qb), inf, F32), jnp.full((8, qb), -inf, F32)))
    _col_select_to_bias(s_ref, n_chunks, (qpos + 1).astype(F32), topk,
                        jnp.min(mn, axis=0, keepdims=True), jnp.max(mx, axis=0, keepdims=True))

    m_scr[...] = jnp.full(m_scr.shape, NEG_BIG, F32)
    acc_scr[...] = jnp.zeros(acc_scr.shape, F32)

    n_att = n_chunks * (KEY_CHUNK // ATT_CHUNK)

    def logits(step, buf):
        k0 = pl.multiple_of(jnp.minimum(step, n_att - 1) * ATT_CHUNK, ATT_CHUNK)
        kc = k_ref[0, pl.ds(k0, ATT_CHUNK), :]
        bias = pltpu.bitcast(s_ref[pl.ds(k0, ATT_CHUNK), :], F32)
        for pr in range(ATT_HEADS // 2):
            s2 = _dot_nt(kc, q_ref[0, 2 * pr:2 * pr + 2].reshape(2 * qb, LANES))
            buf[2 * pr] = s2[:, :qb] + bias
            buf[2 * pr + 1] = s2[:, qb:] + bias

    def accumulate(step, buf):
        k0 = pl.multiple_of(step * ATT_CHUNK, ATT_CHUNK)
        vtc = vt_ref[:, pl.ds(k0, ATT_CHUNK)]
        for pr in range(ATT_HEADS // 2):
            ps, alphas = [], []
            for hd in (2 * pr, 2 * pr + 1):
                s = buf[hd]
                m_prev = m_scr[hd:hd + 1, :]
                m_new = jnp.maximum(m_prev, _col_reduce(s, jnp.maximum, jnp.max))
                alphas.append(jnp.exp2(m_prev - m_new))
                ps.append(jnp.exp2(s - m_new).astype(BF16))
                m_scr[hd:hd + 1, :] = m_new
            acc_scr[pr] = jnp.concatenate(alphas, axis=1) * acc_scr[pr] + _dot(vtc, jnp.concatenate(ps, axis=1))

    logits(0, sa_scr)

    def attend(c, _):
        logits(2 * c + 1, sb_scr)
        accumulate(2 * c, sa_scr)
        logits(2 * c + 2, sa_scr)
        accumulate(2 * c + 1, sb_scr)
        return 0

    lax.fori_loop(0, n_att // 2, attend, 0)

    group = ATT_HEADS // KV_HEADS
    for hd in range(ATT_HEADS):
        cols = slice((hd % 2) * qb, (hd % 2 + 1) * qb)
        out_t = acc_scr[hd // 2, 0:LANES, cols] / acc_scr[hd // 2, LANES:LANES + 1, cols]
        row = lax.broadcasted_iota(I32, out_t.shape, 0)
        g = hd // group
        out_t = jnp.where((row >= g * HEAD_DIM) & (row < (g + 1) * HEAD_DIM), out_t, 0.0)
        o_ref[0, :, hd * LANES:(hd + 1) * LANES] = out_t.T.astype(BF16)


def _dsa_prompt(q, qi, kiw, kbf, vt, kiwbf):
    b, t, _ = kiw.shape
    qb = QUERY_BLOCK
    nb = t // qb
    topk = min(TOPK_MAX, t // 4)
    qblk = lambda width: pl.BlockSpec((1, qb, width), lambda bi, j: (bi, j, 0))
    hblk = lambda heads: pl.BlockSpec((1, heads, qb, LANES), lambda bi, j: (bi * nb + j, 0, 0, 0))
    full = pl.BlockSpec((1, t, LANES), lambda bi, j: (bi, 0, 0))
    return pl.pallas_call(
        functools.partial(_dsa_prompt_kernel, qb=qb, topk=topk),
        grid=(b, nb),
        in_specs=[hblk(ATT_HEADS), hblk(IDX_HEADS), qblk(LANES), full,
                  pl.BlockSpec((VT_ROWS, t), lambda bi, j: (0, bi)), full],
        out_specs=qblk(ATT_HEADS * LANES),
        out_shape=jax.ShapeDtypeStruct((b, t, ATT_HEADS * LANES), BF16),
        scratch_shapes=[pltpu.VMEM((pl.cdiv(t, KEY_CHUNK) * KEY_CHUNK, qb), I32), pltpu.VMEM((ATT_HEADS, qb), F32),
                        pltpu.VMEM((ATT_HEADS // 2, VT_ROWS, 2 * qb), F32),
                        pltpu.VMEM((ATT_HEADS, ATT_CHUNK, qb), F32), pltpu.VMEM((ATT_HEADS, ATT_CHUNK, qb), F32),
                        pltpu.VMEM((IDX_HEADS // 2, KEY_CHUNK, 2 * qb), F32),
                        pltpu.VMEM((IDX_HEADS // 2, KEY_CHUNK, 2 * qb), F32)],
        compiler_params=_cparams(("parallel", "arbitrary")),
        name="dsa_prompt",
    )(q, qi, kiw, kbf, vt, kiwbf)


def _dsa_sample_kernel(pt_ref, q_ref, qi_ref, kiwq_ref, kn_ref, vn_ref, kiwn_ref, ck_hbm, cv_hbm, cki_hbm,
                       o_ref, kbuf, vbuf, kibuf, s_ref, sems, *, ts, n_pages, topk):
    b = pl.program_id(0)
    past = n_pages * PAGE_SIZE
    slot = b % 2

    def page_copy(seq, p, which):
        cols = pl.ds(pl.multiple_of(p * PAGE_SIZE, PAGE_SIZE), PAGE_SIZE)
        src, dst = ((cki_hbm, kibuf.at[0]), (cki_hbm, kibuf.at[1]), (ck_hbm, kbuf), (cv_hbm, vbuf))[which]
        return pltpu.make_async_copy(src.at[pt_ref[seq, p]], dst.at[:, cols], sems.at[which])

    def for_pages(fn):
        lax.fori_loop(0, n_pages, lambda p, _: (fn(p), 0)[1], 0)

    def per_slot(s, fn):
        for which in range(2):
            pl.when(s == which)(functools.partial(fn, which))

    @pl.when(b == 0)
    def _():
        for_pages(lambda p: page_copy(0, p, 0).start())

    for which in (2, 3):
        for_pages(lambda p, which=which: page_copy(b, p, which).start())

    def wait_pages(which):
        for_pages(lambda p: page_copy(b, p, which).wait())

    n_past_chunks = past // KEY_CHUNK
    n_chunks = n_past_chunks + 1
    qpos = lax.broadcasted_iota(I32, (ts, 1), 0)
    wq = _indexer_weights(kiwq_ref[0])
    qi_stack = _stack_heads(qi_ref[0], 0, IDX_HEADS, IDX_DIM)
    w_stack = jnp.concatenate([wq[:, hd:hd + 1] for hd in range(IDX_HEADS)], axis=0)

    def head_sum(x):
        acc = x[0:ts]
        for hd in range(1, IDX_HEADS):
            acc = acc + x[hd * ts:(hd + 1) * ts]
        return acc

    per_slot(slot, wait_pages)

    step_keys = math.gcd(past, ROW_SLAB)

    def past_scores(c, _):
        k0 = pl.multiple_of(c * step_keys, KEY_CHUNK)
        ktc = kibuf[slot, :, pl.ds(k0, step_keys)].astype(BF16)
        acc = head_sum(jnp.maximum(_dot(qi_stack, ktc), 0.0) * w_stack)
        s_ref[:, pl.ds(k0, step_keys)] = _score_keys(acc + 0.0)
        return 0

    lax.fori_loop(0, past // step_keys, past_scores, 0)
    acc = head_sum(jnp.maximum(_dot(qi_stack, kiwn_ref[0, 0:IDX_DIM, :]), 0.0) * w_stack)
    kpos = lax.broadcasted_iota(I32, (ts, KEY_CHUNK), 1)
    s_ref[:, pl.ds(past, KEY_CHUNK)] = jnp.where(kpos <= qpos, _score_keys(acc + 0.0), INT_MIN)
    _row_select_to_bias(s_ref, n_chunks, (past + 1 + qpos).astype(F32), topk)

    wait_pages(2)
    wait_pages(3)

    @pl.when(b + 1 < pl.num_programs(0))
    def _():
        per_slot(1 - slot, lambda which: for_pages(lambda p: page_copy(b + 1, p, which).start()))

    def load_past(k0):
        return kbuf[:, pl.ds(k0, step_keys)].astype(BF16), vbuf[:, pl.ds(k0, step_keys)].astype(BF16)

    def load_new(k0):
        return kn_ref[0], vn_ref[0]

    segments = [(past // step_keys, step_keys, 0, load_past), (1, KEY_CHUNK, past, load_new)]
    _attend_all_heads(q_ref[0], segments, s_ref, o_ref)


def _dsa_sample(q, qi, kiw, knt, vnt, kiwnt, cache_kt, cache_vt, cache_kit, page_table):
    b, ts, _ = q.shape
    n_pages = page_table.shape[1]
    past = n_pages * PAGE_SIZE
    topk = min(TOPK_MAX, (past + ts) // 4)
    blk = lambda rows, width: pl.BlockSpec((1, rows, width), lambda bi, pt: (bi, 0, 0))
    hbm = pl.BlockSpec(memory_space=pl.ANY)
    grid_spec = pltpu.PrefetchScalarGridSpec(
        num_scalar_prefetch=1,
        grid=(b,),
        in_specs=[blk(ts, ATT_HEADS * LANES), blk(ts, IDX_HEADS * LANES), blk(ts, LANES),
                  blk(LANES, KEY_CHUNK), blk(LANES, KEY_CHUNK), blk(LANES, KEY_CHUNK), hbm, hbm, hbm],
        out_specs=blk(ts, ATT_HEADS * LANES),
        scratch_shapes=[pltpu.VMEM((LANES, past), F32), pltpu.VMEM((LANES, past), F32),
                        pltpu.VMEM((2, IDX_DIM, past), F32), pltpu.VMEM((ts, past + KEY_CHUNK), I32),
                        pltpu.SemaphoreType.DMA((4,))],
    )
    return pl.pallas_call(
        functools.partial(_dsa_sample_kernel, ts=ts, n_pages=n_pages, topk=topk),
        grid_spec=grid_spec,
        out_shape=jax.ShapeDtypeStruct((b, ts, ATT_HEADS * LANES), F32),
        compiler_params=_cparams(("arbitrary",)),
        name="dsa_sample",
    )(page_table, q, qi, kiw, knt, vnt, kiwnt, cache_kt, cache_vt, cache_kit)


def _conv_kernel(u_ref, halo_ref, buf_ref, bg_ref, w_ref, ya_ref):
    i = pl.program_id(1)
    u = u_ref[0]
    halo, buf = halo_ref[0], buf_ref[0]
    first = i == 0
    prev1 = jnp.where(first, buf[1:2], halo[7:8])
    prev2 = jnp.where(first, buf[0:1], halo[6:7])
    row = lax.broadcasted_iota(I32, u.shape, 0)
    um1 = jnp.where(row == 0, prev1, pltpu.roll(u, 1, 0))
    um2 = jnp.where(row == 0, prev2, jnp.where(row == 1, prev1, pltpu.roll(u, 2, 0)))
    w = w_ref[...]
    conv = w[0:1] * um2 + w[1:2] * um1 + w[2:3] * u
    ya_ref[0] = (bg_ref[0] * conv).astype(BF16)


def _conv(u, bg, buf, w, tt):
    b, t, c = u.shape
    halo_rows = 8
    tile = pl.BlockSpec((1, tt, c), lambda bi, i: (bi, i, 0))
    halo = pl.BlockSpec((1, halo_rows, c), lambda bi, i: (bi, jnp.maximum(i * (tt // halo_rows) - 1, 0), 0))
    return pl.pallas_call(
        _conv_kernel,
        grid=(b, t // tt),
        in_specs=[tile, halo, pl.BlockSpec((1, CONV_WIDTH - 1, c), lambda bi, i: (bi, 0, 0)), tile,
                  pl.BlockSpec((CONV_WIDTH, c), lambda bi, i: (0, 0))],
        out_specs=tile,
        out_shape=jax.ShapeDtypeStruct((b, t, c), BF16),
        compiler_params=_cparams(("parallel", "parallel")),
        name="short_conv",
    )(u, u, buf, bg, w)


def _mm_res_kernel(*refs, n_in):
    a_refs, w_ref, x_ref, o_ref = refs[:n_in], refs[n_in], refs[n_in + 1], refs[n_in + 2]
    a = jnp.concatenate([r[...] for r in a_refs], axis=1) if n_in > 1 else a_refs[0][...]
    o_ref[...] = x_ref[...] + _dot(a, w_ref[...])


def _mm_res(a_list, w, x, tm):
    m = x.shape[0]
    row = lambda i: (i, 0)
    return pl.pallas_call(
        functools.partial(_mm_res_kernel, n_in=len(a_list)),
        grid=(m // tm,),
        in_specs=[pl.BlockSpec((tm, a.shape[1]), row) for a in a_list]
        + [pl.BlockSpec(w.shape, lambda i: (0, 0)), pl.BlockSpec((tm, D_MODEL), row)],
        out_specs=pl.BlockSpec((tm, D_MODEL), row),
        out_shape=jax.ShapeDtypeStruct((m, D_MODEL), F32),
        compiler_params=_cparams(("parallel",)),
        name="matmul_residual",
    )(*a_list, w, x)


def _ret_in_kernel(x_ref, g_ref, w_ref, qk_ref, vg_ref):
    h = _rms(x_ref[...], g_ref[...]).astype(BF16)
    n_qk = qk_ref.shape[1]
    qk_ref[...] = _dot(h, w_ref[:, :n_qk])
    vg_ref[...] = _dot(h, w_ref[:, n_qk:]).astype(BF16)


def _ret_in_proj(x, g, w, tm):
    m, n = x.shape[0], w.shape[1]
    n_qk = 2 * RET_HEADS * RET_DK
    row = lambda i: (i, 0)
    const = lambda i: (0, 0)
    return pl.pallas_call(
        _ret_in_kernel,
        grid=(m // tm,),
        in_specs=[pl.BlockSpec((tm, D_MODEL), row), pl.BlockSpec((1, D_MODEL), const), pl.BlockSpec((D_MODEL, n), const)],
        out_specs=[pl.BlockSpec((tm, n_qk), row), pl.BlockSpec((tm, n - n_qk), row)],
        out_shape=[jax.ShapeDtypeStruct((m, n_qk), F32), jax.ShapeDtypeStruct((m, n - n_qk), BF16)],
        compiler_params=_cparams(("parallel",)),
        name="retention_in_proj",
    )(x, g, w)


def _ffn_kernel(x_ref, g_ref, wg_ref, wu_ref, wd_ref, o_ref, h_scr, acc):
    f = pl.program_id(1)

    @pl.when(f == 0)
    def _():
        x = x_ref[...]
        h_scr[...] = _rms(x, g_ref[...]).astype(BF16)
        acc[...] = x

    h = h_scr[...]
    a = jax.nn.silu(_dot(h, wg_ref[...])) * _dot(h, wu_ref[...])
    acc[...] += _dot(a.astype(BF16), wd_ref[...])

    @pl.when(f == pl.num_programs(1) - 1)
    def _():
        o_ref[...] = acc[...]


def _ffn(x, g, wg, wu, wd, tm, tf):
    m, ff = x.shape[0], wg.shape[1]
    row = lambda i, f: (i, 0)
    return pl.pallas_call(
        _ffn_kernel,
        grid=(m // tm, ff // tf),
        in_specs=[pl.BlockSpec((tm, D_MODEL), row), pl.BlockSpec((1, D_MODEL), lambda i, f: (0, 0)),
                  pl.BlockSpec((D_MODEL, tf), lambda i, f: (0, f)), pl.BlockSpec((D_MODEL, tf), lambda i, f: (0, f)),
                  pl.BlockSpec((tf, D_MODEL), lambda i, f: (f, 0))],
        out_specs=pl.BlockSpec((tm, D_MODEL), row),
        out_shape=jax.ShapeDtypeStruct((m, D_MODEL), F32),
        scratch_shapes=[pltpu.VMEM((tm, D_MODEL), BF16), pltpu.VMEM((tm, D_MODEL), F32)],
        compiler_params=_cparams(("parallel", "arbitrary")),
        name="dense_swiglu",
    )(x, g, wg, wu, wd)


def _top2_gates(logits):
    lane = lax.broadcasted_iota(I32, logits.shape, 1).astype(F32)
    neg = jnp.float32(-jnp.inf)
    l1 = jnp.where(lane < N_EXPERTS, logits, neg)
    m1 = jnp.max(l1, axis=1, keepdims=True)
    i1 = jnp.min(jnp.where(l1 == m1, lane, float(LANES)), axis=1, keepdims=True)
    l2 = jnp.where(lane == i1, neg, l1)
    m2 = jnp.max(l2, axis=1, keepdims=True)
    i2 = jnp.min(jnp.where(l2 == m2, lane, float(LANES)), axis=1, keepdims=True)
    e = jnp.exp(m2 - m1)
    w1 = 1.0 / (1.0 + e)
    w2 = e / (1.0 + e)
    first, second = lane == i1, lane == i2
    return jnp.where(first, w1, jnp.where(second, w2, 0.0)), jnp.where(first | second, 1.0, 0.0)


MOE_SUB = 144
MOE_TOKEN_TILE = 1024


def _moe_route_kernel(x_ref, g_ref, rhi_ref, rlo_ref, h_ref, gate_ref, posc_ref, posr_ref, cnt_ref):
    tm = x_ref.shape[0]
    hn = _rms(x_ref[...], g_ref[...])
    h_hi = hn.astype(BF16)
    h_lo = (hn - h_hi.astype(F32)).astype(BF16)
    logits = _dot(h_hi, rhi_ref[...]) + (_dot(h_lo, rhi_ref[...]) + _dot(h_hi, rlo_ref[...]))
    gate, routed = _top2_gates(logits)
    h_ref[...] = h_hi
    gate_ref[...] = gate
    earlier = (lax.broadcasted_iota(I32, (tm, tm), 0) > lax.broadcasted_iota(I32, (tm, tm), 1))
    slot = _dot(jnp.where(earlier, 1.0, 0.0).astype(BF16), routed.astype(BF16))
    posc = jnp.where(routed > 0.0, slot, -1.0)
    posc_ref[...] = posc
    posr_ref[0] = posc.T[0:N_EXPERTS]
    cnt_ref[0] = jnp.broadcast_to(jnp.sum(routed, axis=0, keepdims=True), (8, LANES))


def _moe_expert_kernel(cnt_ref, h_ref, gate_ref, posc_ref, posr_ref, wg_ref, wu_ref, wd_ref, o_ref, xg, yacc):
    i, e, f = pl.program_id(0), pl.program_id(1), pl.program_id(2)
    tm = h_ref.shape[0]
    n_sub = (cnt_ref[i * N_EXPERTS + e] + MOE_SUB - 1) // MOE_SUB

    @pl.when((e == 0) & (f == 0))
    def _():
        o_ref[...] = jnp.zeros(o_ref.shape, F32)

    @pl.when(f == 0)
    def _():
        posr = posr_ref[0]
        h = h_ref[...]

        def gather(s, _):
            base = pl.multiple_of(s * MOE_SUB, MOE_SUB)
            slot = (base + lax.broadcasted_iota(I32, (MOE_SUB, tm), 0)).astype(F32)
            onehot = jnp.where(posr == slot, 1.0, 0.0).astype(BF16)
            xg[pl.ds(base, MOE_SUB), :] = _dot(onehot, h).astype(BF16)
            yacc[pl.ds(base, MOE_SUB), :] = jnp.zeros((MOE_SUB, D_MODEL), F32)
            return 0

        lax.fori_loop(0, n_sub, gather, 0)

    def expert(s, _):
        rows = pl.ds(pl.multiple_of(s * MOE_SUB, MOE_SUB), MOE_SUB)
        xs = xg[rows, :]
        a = jax.nn.silu(_dot(xs, wg_ref[0])) * _dot(xs, wu_ref[0])
        yacc[rows, :] += _dot(a.astype(BF16), wd_ref[0])
        return 0

    lax.fori_loop(0, n_sub, expert, 0)

    @pl.when(f == pl.num_programs(2) - 1)
    def _():
        lane = lax.broadcasted_iota(I32, (tm, LANES), 1)
        mine = lane == e
        posc = jnp.sum(jnp.where(mine, posc_ref[...], 0.0), axis=1, keepdims=True)
        gate = jnp.sum(jnp.where(mine, gate_ref[...], 0.0), axis=1, keepdims=True)

        def scatter(s, _):
            base = pl.multiple_of(s * MOE_SUB, MOE_SUB)
            slot = (base + lax.broadcasted_iota(I32, (tm, MOE_SUB), 1)).astype(F32)
            onehot = jnp.where(posc == slot, 1.0, 0.0).astype(BF16)
            o_ref[...] += gate * _dot(onehot, yacc[pl.ds(base, MOE_SUB), :].astype(BF16))
            return 0

        lax.fori_loop(0, n_sub, scatter, 0)


def _moe(x, g, r_hi, r_lo, wg, wu, wd, tm, tf):
    m = x.shape[0]
    nt = m // tm
    row = lambda i: (i, 0)
    const = lambda i: (0, 0)
    h, gate, posc, posr, cnt = pl.pallas_call(
        _moe_route_kernel,
        grid=(nt,),
        in_specs=[pl.BlockSpec((tm, D_MODEL), row), pl.BlockSpec((1, D_MODEL), const),
                  pl.BlockSpec((D_MODEL, LANES), const), pl.BlockSpec((D_MODEL, LANES), const)],
        out_specs=[pl.BlockSpec((tm, D_MODEL), row), pl.BlockSpec((tm, LANES), row), pl.BlockSpec((tm, LANES), row),
                   pl.BlockSpec((1, N_EXPERTS, tm), lambda i: (i, 0, 0)), pl.BlockSpec((1, 8, LANES), lambda i: (i, 0, 0))],
        out_shape=[jax.ShapeDtypeStruct((m, D_MODEL), BF16), jax.ShapeDtypeStruct((m, LANES), F32),
                   jax.ShapeDtypeStruct((m, LANES), F32), jax.ShapeDtypeStruct((nt, N_EXPERTS, tm), F32),
                   jax.ShapeDtypeStruct((nt, 8, LANES), F32)],
        compiler_params=_cparams(("parallel",)),
        name="moe_route",
    )(x, g, r_hi, r_lo)
    counts = cnt[:, 0, :N_EXPERTS].astype(I32).reshape(nt * N_EXPERTS)
    posr = posr.reshape(nt * N_EXPERTS, 1, tm)
    row3 = lambda i, e, f, c: (i, 0)
    slot_rows = pl.cdiv(tm, MOE_SUB) * MOE_SUB
    grid_spec = pltpu.PrefetchScalarGridSpec(
        num_scalar_prefetch=1,
        grid=(nt, N_EXPERTS, EXPERT_FF // tf),
        in_specs=[pl.BlockSpec((tm, D_MODEL), row3),
                  pl.BlockSpec((tm, LANES), row3), pl.BlockSpec((tm, LANES), row3),
                  pl.BlockSpec((1, 1, tm), lambda i, e, f, c: (i * N_EXPERTS + e, 0, 0)),
                  pl.BlockSpec((1, D_MODEL, tf), lambda i, e, f, c: (e, 0, f)),
                  pl.BlockSpec((1, D_MODEL, tf), lambda i, e, f, c: (e, 0, f)),
                  pl.BlockSpec((1, tf, D_MODEL), lambda i, e, f, c: (e, f, 0))],
        out_specs=pl.BlockSpec((tm, D_MODEL), row3),
        scratch_shapes=[pltpu.VMEM((slot_rows, D_MODEL), BF16), pltpu.VMEM((slot_rows, D_MODEL), F32)],
    )
    return pl.pallas_call(
        _moe_expert_kernel,
        grid_spec=grid_spec,
        out_shape=jax.ShapeDtypeStruct((m, D_MODEL), F32),
        compiler_params=_cparams(("parallel", "arbitrary", "arbitrary")),
        name="moe_experts",
    )(counts, h, gate, posc, posr, wg, wu, wd)


def _ple_kernel(*refs, final_norm, n_addends):
    x_refs, (g_ref, p_ref, wp_ref, wgate_ref, gf_ref, o_ref) = refs[:n_addends], refs[n_addends:]
    x = x_refs[0][...]
    for r in x_refs[1:]:
        x = x + r[...]
    hp = _rms(x, g_ref[...]).astype(BF16)
    gate = jax.nn.sigmoid(_dot(hp, wgate_ref[...]))
    y = x + _dot(p_ref[...].astype(BF16), wp_ref[...]) * gate
    if final_norm:
        y = _rms(y, gf_ref[...])
    o_ref[...] = y


def _ple(xs, g, p, wp, wgate, g_final, final_norm, tm):
    m = xs[0].shape[0]
    row = lambda i: (i, 0)
    const = lambda i: (0, 0)
    return pl.pallas_call(
        functools.partial(_ple_kernel, final_norm=final_norm, n_addends=len(xs)),
        grid=(m // tm,),
        in_specs=[pl.BlockSpec((tm, D_MODEL), row)] * len(xs)
        + [pl.BlockSpec((1, D_MODEL), const), pl.BlockSpec((tm, PLE_DIM), row),
           pl.BlockSpec((PLE_DIM, D_MODEL), const), pl.BlockSpec((D_MODEL, D_MODEL), const),
           pl.BlockSpec((1, D_MODEL), const)],
        out_specs=pl.BlockSpec((tm, D_MODEL), row),
        out_shape=jax.ShapeDtypeStruct((m, D_MODEL), F32),
        compiler_params=_cparams(("parallel",)),
        name="per_layer_embedding",
    )(*xs, g, p, wp, wgate, g_final)


def _ret_kernel(q_ref, k_ref, v_ref, gate_ref, cos_ref, sin_ref, s0_ref, o_ref, sout_ref, state, *, chunk, chunk_rows):
    i = pl.program_id(1)

    @pl.when(i == 0)
    def _():
        state[...] = s0_ref[0]

    cos, sin = cos_ref[...], sin_ref[...]
    tt = q_ref.shape[1]
    r = chunk_rows
    ii = lax.broadcasted_iota(I32, (r, r), 0).astype(F32)
    jj = lax.broadcasted_iota(I32, (r, r), 1).astype(F32)
    rel = ii - jj
    icol = lax.broadcasted_iota(I32, (r, 1), 0).astype(F32)
    half = RET_DK // 2

    def rot(ref, hd):
        x1 = ref[0, :, hd * RET_DK:hd * RET_DK + half]
        x2 = ref[0, :, hd * RET_DK + half:(hd + 1) * RET_DK]
        return jnp.concatenate([x1 * cos - x2 * sin, x2 * cos + x1 * sin], axis=1)

    for hd in range(RET_HEADS):
        lg = math.log(1.0 - 2.0 ** (-5.0 - hd))
        d_in = jnp.where(rel >= 0, jnp.exp(lg * jnp.maximum(rel, 0.0)), 0.0)
        d_q = jnp.exp(lg * (icol + 1.0))
        d_k = jnp.exp(lg * (chunk - 1.0 - icol)) * (RET_DK ** -0.5)
        d_c = math.exp(lg * chunk)
        qr = rot(q_ref, hd)
        kr = rot(k_ref, hd)
        vsl = slice(hd * RET_DV, (hd + 1) * RET_DV)
        for c in range(tt // r):
            rows = slice(c * r, (c + 1) * r)
            qc = qr[rows].astype(BF16)
            kc = kr[rows]
            vc = v_ref[0, rows, vsl]
            s_prev = state[hd]
            att = _dot_nt(qc, (kc * (RET_DK ** -0.5)).astype(BF16)) * d_in
            o = _dot(att.astype(BF16), vc) + _dot(qc, s_prev.astype(BF16)) * d_q
            state[hd] = s_prev * d_c + _dot_tn((kc * d_k).astype(BF16), vc)
            mu = jnp.mean(o, axis=-1, keepdims=True)
            var = jnp.mean(jnp.square(o - mu), axis=-1, keepdims=True)
            on = (o - mu) * lax.rsqrt(var + EPS)
            o_ref[0, rows, vsl] = (jax.nn.silu(gate_ref[0, rows, vsl].astype(F32)) * on).astype(BF16)

    @pl.when(i == pl.num_programs(1) - 1)
    def _():
        sout_ref[0] = state[...]


def _retention(zqk, zvg, s0, cos, sin, tt, chunk, chunk_rows):
    b, t, _ = zqk.shape
    hk, hv = RET_HEADS * RET_DK, RET_HEADS * RET_DV
    half = RET_DK // 2
    state_spec = pl.BlockSpec((1, RET_HEADS, RET_DK, RET_DV), lambda bi, i: (bi, 0, 0, 0))
    tab = pl.BlockSpec((tt, half), lambda bi, i: (i, 0))
    return pl.pallas_call(
        functools.partial(_ret_kernel, chunk=chunk, chunk_rows=chunk_rows),
        grid=(b, t // tt),
        in_specs=[pl.BlockSpec((1, tt, hk), lambda bi, i: (bi, i, 0)), pl.BlockSpec((1, tt, hk), lambda bi, i: (bi, i, 1)),
                  pl.BlockSpec((1, tt, hv), lambda bi, i: (bi, i, 0)), pl.BlockSpec((1, tt, hv), lambda bi, i: (bi, i, 1)),
                  tab, tab, state_spec],
        out_specs=[pl.BlockSpec((1, tt, hv), lambda bi, i: (bi, i, 0)), state_spec],
        out_shape=[jax.ShapeDtypeStruct((b, t, hv), BF16), jax.ShapeDtypeStruct(s0.shape, F32)],
        scratch_shapes=[pltpu.VMEM((RET_HEADS, RET_DK, RET_DV), F32)],
        compiler_params=_cparams(("parallel", "arbitrary")),
        name="retention",
    )(zqk, zqk, zvg, zvg, cos, sin, s0)


def _pack_l0_w_in(w):
    offs = np.cumsum((0,) + AB_SPLITS)
    bg, cg, hv, q, k, v, qi, ki, wi = [w[:, offs[n]:offs[n + 1]] for n in range(len(AB_SPLITS))]
    group = ATT_HEADS // KV_HEADS
    q4 = q.reshape(D_MODEL, ATT_HEADS, HEAD_DIM)
    zq = jnp.zeros_like(q4)
    q_pad = jnp.concatenate([jnp.concatenate([q4[:, :group], zq[:, :group]], axis=-1),
                             jnp.concatenate([zq[:, group:], q4[:, group:]], axis=-1)], axis=1).reshape(D_MODEL, -1)
    qi4 = qi.reshape(D_MODEL, IDX_HEADS, IDX_DIM)
    qi_pad = jnp.concatenate([qi4, jnp.zeros_like(qi4)], axis=-1).reshape(D_MODEL, -1)
    kiw = jnp.concatenate([ki, wi, jnp.zeros((D_MODEL, LANES - IDX_DIM - IDX_HEADS), w.dtype)], axis=1)
    return jnp.concatenate([bg, cg, hv, q_pad, k, v, qi_pad, kiw], axis=1).astype(BF16)


def _pack_l0_w_out(w):
    group = ATT_HEADS // KV_HEADS
    wa, wb = w[:CONV_CH], w[CONV_CH:].reshape(ATT_HEADS, HEAD_DIM, D_MODEL)
    zb = jnp.zeros_like(wb)
    wb_pad = jnp.concatenate([jnp.concatenate([wb[:group], zb[:group]], axis=1),
                              jnp.concatenate([zb[group:], wb[group:]], axis=1)], axis=0).reshape(-1, D_MODEL)
    return jnp.concatenate([wa, wb_pad], axis=0).astype(BF16)


def _rope_tables(pos, reps):
    inv = ROPE_THETA ** (-jnp.arange(0, HEAD_DIM, 2, dtype=F32) / HEAD_DIM)
    ang = pos.astype(F32)[:, None] * inv[None, :]
    cos, sin = jnp.cos(ang), jnp.sin(ang)
    cos64 = jnp.concatenate([cos, cos], axis=1)
    sin64 = jnp.concatenate([-sin, sin], axis=1)
    one, zero = jnp.ones_like(cos64), jnp.zeros_like(cos64)
    tabs = (jnp.concatenate([cos64, cos64], 1), jnp.concatenate([sin64, sin64], 1),
            jnp.concatenate([cos64, one], 1), jnp.concatenate([sin64, zero], 1))
    return tuple(jnp.tile(tb, (reps, 1)) for tb in tabs)


def _ret_tables(pos):
    inv = ROPE_THETA ** (-jnp.linspace(0.0, 1.0, RET_DK // 2, dtype=F32))
    ang = pos.astype(F32)[:, None] * inv[None, :]
    return jnp.cos(ang), jnp.sin(ang)


def _pack_params(prm):
    r = prm['moe_router'][0]
    r_pad = jnp.concatenate([r, jnp.zeros((D_MODEL, LANES - N_EXPERTS), F32)], axis=1)
    r_hi = r_pad.astype(BF16)
    bf = lambda a: a.astype(BF16)
    return dict(
        l0_w_in=_pack_l0_w_in(prm['ab_w_in'][0]), l0_w_out=_pack_l0_w_out(prm['ab_w_out'][0]),
        l0_w_vt=bf(prm['ab_w_in'][0][:, sum(AB_SPLITS[:5]):sum(AB_SPLITS[:6])].T),
        conv_w=prm['ab_conv_w'][0],
        ffn=(bf(prm['ffn_w_gate'][0]), bf(prm['ffn_w_up'][0]), bf(prm['ffn_w_down'][0])),
        ret_w_in=bf(prm['ret_w_in'][0]), ret_w_out=bf(prm['ret_w_out'][0]),
        r_hi=r_hi, r_lo=(r_pad - r_hi.astype(F32)).astype(BF16),
        moe=(bf(prm['moe_w_gate'][0]), bf(prm['moe_w_up'][0]), bf(prm['moe_w_down'][0])),
        ple_w=bf(prm['ple_w']), ple_gate_w=bf(prm['ple_gate_w']),
        norm_mix=prm['norm_mix'][:, None, :], norm_ffn=prm['norm_ffn'][:, None, :],
        norm_ple=prm['norm_ple'][:, None, :], norm_final=prm['norm_final'][None, :],
    )


def _trunk(x, p, pos, conv_buf, ret_state, pk, paged):
    b, t, _ = x.shape
    m = b * t
    tm = min(m, TOKEN_TILE)
    moe_tm = min(m, MOE_TOKEN_TILE)
    assert m % tm == 0 and m % moe_tm == 0 and m % min(m, 256) == 0, "token count must divide into whole tiles"
    assert paged is not None or (t % QUERY_BLOCK == 0 and t % tm == 0), "prompt length must divide into whole tiles"
    assert paged is None or (tm % t == 0 and t <= 8), "sample sequences are one sublane tile long at most"
    x2 = x.reshape(m, D_MODEL)

    reps = max(1, tm // t)
    tabs = _rope_tables(pos, reps)
    bg, u, q, k, v, qi, kiw, kbf, vbf, kiwbf, vt = _l0_in_proj(x2, pk['norm_mix'][0], pk['l0_w_in'], pk['l0_w_vt'], tabs,
                                                              tm, tabs[0].shape[0] // tm, head_major=paged is None)
    seq = lambda a: a.reshape(b, t, a.shape[-1])
    if paged is None:
        yb = _dsa_prompt(q, qi, seq(kiw), seq(kbf), vt, seq(kiwbf))
    else:
        cache_k, cache_v, cache_ki, page_table = paged
        padt = lambda a: jnp.swapaxes(jnp.pad(seq(a), ((0, 0), (0, KEY_CHUNK - t), (0, 0))), 1, 2)
        n_pool = cache_k.shape[0]
        pages_t = lambda c: jnp.transpose(c, (0, 2, 3, 1)).reshape(n_pool, LANES, PAGE_SIZE)
        yb = _dsa_sample(seq(q).astype(F32), seq(qi).astype(F32), seq(kiw), padt(kbf), padt(vbf), padt(kiwbf),
                         pages_t(cache_k), pages_t(cache_v), jnp.swapaxes(cache_ki, 1, 2), page_table).astype(BF16)
    u3 = seq(u)
    ya = _conv(u3, seq(bg), conv_buf, pk['conv_w'], tt=min(t, TOKEN_TILE))
    x2 = _mm_res([ya.reshape(m, CONV_CH), yb.reshape(m, ATT_HEADS * LANES)], pk['l0_w_out'], x2, tm)
    x2 = _ffn(x2, pk['norm_ffn'][0], *pk['ffn'], tm=tm, tf=D_FF // 2)
    x2 = _ple([x2], pk['norm_ple'][0], p[0].reshape(m, PLE_DIM), pk['ple_w'][0], pk['ple_gate_w'][0],
              pk['norm_final'], False, tm)
    new_k = k.reshape(1, b, t, KV_HEADS, HEAD_DIM)
    new_v = v.reshape(1, b, t, KV_HEADS, HEAD_DIM)
    new_ki = seq(kiw)[None, :, :, :IDX_DIM]
    new_conv = jnp.concatenate([conv_buf, u3], axis=1)[None, :, -(CONV_WIDTH - 1):]

    zqk, zvg = _ret_in_proj(x2, pk['norm_mix'][1], pk['ret_w_in'], min(m, 256))
    zqk, zvg = zqk.reshape(b, t, -1), zvg.reshape(b, t, -1)
    cos_r, sin_r = _ret_tables(pos)
    if t % RET_CHUNK == 0:
        og, s_new = _retention(zqk, zvg, ret_state, cos_r, sin_r, tt=2 * RET_CHUNK, chunk=RET_CHUNK,
                               chunk_rows=RET_CHUNK)
    else:
        rows = 16
        padt = lambda a: jnp.pad(a, ((0, 0),) * (a.ndim - 2) + ((0, rows - t), (0, 0)))
        og, s_new = _retention(padt(zqk), padt(zvg), ret_state, padt(cos_r), padt(sin_r), tt=rows, chunk=t,
                               chunk_rows=rows)
        og = og[:, :t]
    x2 = _mm_res([og.reshape(m, RET_HEADS * RET_DV)], pk['ret_w_out'], x2, tm)
    moe = _moe(x2, pk['norm_ffn'][1], pk['r_hi'], pk['r_lo'], *pk['moe'], tm=moe_tm, tf=EXPERT_FF // 2)
    x2 = _ple([x2, moe], pk['norm_ple'][1], p[1].reshape(m, PLE_DIM), pk['ple_w'][1], pk['ple_gate_w'][1],
              pk['norm_final'], True, tm)
    return x2.reshape(b, t, D_MODEL), new_k, new_v, new_ki, new_conv, s_new[None]


def kernel(x_prompt, x_sample, cache_k, cache_v, cache_kidx, state_conv, state_ret, page_table, p_prompt, p_sample,
           norm_mix, norm_ffn, norm_ple, norm_final, ab_w_in, ab_conv_w, ab_w_out, ffn_w_gate, ffn_w_up, ffn_w_down,
           ret_w_in, ret_w_out, moe_router, moe_w_gate, moe_w_up, moe_w_down, ple_w, ple_gate_w):
    prm = dict(norm_mix=norm_mix, norm_ffn=norm_ffn, norm_ple=norm_ple, norm_final=norm_final, ab_w_in=ab_w_in,
               ab_conv_w=ab_conv_w, ab_w_out=ab_w_out, ffn_w_gate=ffn_w_gate, ffn_w_up=ffn_w_up, ffn_w_down=ffn_w_down,
               ret_w_in=ret_w_in, ret_w_out=ret_w_out, moe_router=moe_router, moe_w_gate=moe_w_gate,
               moe_w_up=moe_w_up, moe_w_down=moe_w_down, ple_w=ple_w, ple_gate_w=ple_gate_w)
    pk = _pack_params(prm)
    b, t = x_prompt.shape[0], x_prompt.shape[1]
    db, ts = x_sample.shape[0], x_sample.shape[1]
    past_len = page_table.shape[1] * PAGE_SIZE
    dt = x_prompt.dtype

    conv0 = jnp.zeros((b, CONV_WIDTH - 1, CONV_CH), dt)
    ret0 = jnp.zeros((b, RET_HEADS, RET_DK, RET_DV), dt)
    y_p, k_p, v_p, ki_p, cb_p, rs_p = _trunk(x_prompt, p_prompt, jnp.arange(t, dtype=I32), conv0, ret0, pk, None)

    pos_s = past_len + jnp.arange(ts, dtype=I32)
    paged = (cache_k[0], cache_v[0], cache_kidx[0], page_table)
    y_s, k_s, v_s, ki_s, cb_s, rs_s = _trunk(x_sample, p_sample, pos_s, state_conv[0], state_ret[0], pk, paged)
    return (y_p, y_s, k_p, v_p, ki_p, cb_p, rs_p, k_s, v_s, ki_s, cb_s, rs_s)
```
